```python
import math
import jax, jax.numpy as jnp
from jax import lax
import numpy as np

D_MODEL = 1024
BATCH = 8
SEQ = 2048
DEPTH = 1
DEC_BATCH = 32
DEC_SEQ = 4
PAST_LEN = 16384
PAGE_SIZE = 128

HEAD_DIM = 64
D_A = D_MODEL // 2
N_HEADS_A = D_A // HEAD_DIM
DILATED_PATTERNS = ((128, 1), (512, 4), (2048, 16))
MAX_WINDOW = max(w for w, _ in DILATED_PATTERNS)
BAND_BLOCK = 128
POOL_WINDOWS = (2, 4, 8, 16)
N_POOL_GROUPS = len(POOL_WINDOWS)
D_B = D_MODEL // 4
POOL_GROUP_DIM = D_B // N_POOL_GROUPS
POOL_BUF = max(POOL_WINDOWS) - 1
D_C = D_MODEL // 4
N_HEADS_C = D_C // HEAD_DIM
N_MEM = 256
D_MIX = D_A + D_B + D_C
D_IN = 3 * D_A + D_B + D_C
SPLITS = (D_A, 2 * D_A, 3 * D_A, 3 * D_A + D_B)
N_EXPERT_GROUPS = 4
EXPERTS_PER_GROUP = 8
TOP_K_INNER = 2
D_EXPERT = D_MODEL // 4
DN_ALPHA = (2.0 * DEPTH) ** 0.25
DN_BETA = (8.0 * DEPTH) ** -0.25
LN_EPS = 1e-5
ATTN_SCALE = HEAD_DIM ** -0.5

kernel_name = "hymba_dilated_pool_hmoe_step"


def alibi_slopes(n):
    return 2.0 ** (-8.0 * jnp.arange(1, n + 1, dtype=jnp.float32) / n)


def layer_norm(x, g, b):
    xf = x.astype(jnp.float32)
    mu = jnp.mean(xf, -1, keepdims=True)
    var = jnp.mean(jnp.square(xf - mu), -1, keepdims=True)
    return ((xf - mu) * lax.rsqrt(var + LN_EPS) * g.astype(jnp.float32) + b.astype(jnp.float32)).astype(x.dtype)


def _heads(t, h):
    return t.reshape(t.shape[:-1] + (h, HEAD_DIM))


def _to_strided(t, d):
    B, S = t.shape[:2]
    rest = t.shape[2:]
    t = jnp.swapaxes(t.reshape((B, S // d, d) + rest), 1, 2)
    return t.reshape((B * d, S // d) + rest)


def _from_strided(t, B, d):
    L = t.shape[1]
    rest = t.shape[2:]
    t = jnp.swapaxes(t.reshape((B, d, L) + rest), 1, 2)
    return t.reshape((B, L * d) + rest)


def _dilated_band_prompt(q, k, v, window, dilation, slopes):
    B = q.shape[0]
    n_steps = window // dilation
    blk = BAND_BLOCK
    qs, ks, vs = _to_strided(q, dilation), _to_strided(k, dilation), _to_strided(v, dilation)
    N, L, H, hd = qs.shape
    nb = -(-L // blk)
    Lp = nb * blk
    qb = jnp.pad(qs, ((0, 0), (0, Lp - L), (0, 0), (0, 0))).reshape(N, nb, blk, H, hd)

    def band(t):
        t = jnp.pad(t, ((0, 0), (blk, Lp - L), (0, 0), (0, 0))).reshape(N, nb + 1, blk, H, hd)
        return jnp.concatenate([t[:, :-1], t[:, 1:]], axis=2)

    kb, vb = band(ks), band(vs)
    s = jnp.einsum('nbqhd,nbkhd->nbhqk', qb, kb).astype(jnp.float32) * ATTN_SCALE
    steps = jnp.arange(blk)[:, None] + blk - jnp.arange(2 * blk)[None, :]
    key_pos = jnp.arange(nb)[:, None] * blk - blk + jnp.arange(2 * blk)[None, :]
    valid = ((steps >= 0) & (steps <= n_steps))[None] & (key_pos >= 0)[:, None, :]
    dist = (steps * dilation).astype(jnp.float32)
    s = s - slopes[:, None, None] * dist[None]
    s = jnp.where(valid[None, :, None], s, -jnp.inf)
    m = jnp.max(s, -1, keepdims=True)
    p = jnp.exp(s - m)
    den = jnp.sum(p, -1)
    o = jnp.einsum('nbhqk,nbkhd->nbqhd', p, vb.astype(jnp.float32)) / jnp.swapaxes(den, 2, 3)[..., None]
    lse = jnp.swapaxes(m[..., 0] + jnp.log(den), 2, 3)
    o = o.reshape(N, Lp, H, hd)[:, :L]
    lse = lse.reshape(N, Lp, H)[:, :L]
    return _from_strided(o, B, dilation), _from_strided(lse, B, dilation)


def _dilated_sample(q, kc, vc, wb, slopes):
    T = q.shape[1]
    outs, lses = [], []
    for window, dil in DILATED_PATTERNS:
        n_steps = window // dil
        i = jnp.arange(n_steps + 1)
        idx = wb + jnp.arange(T)[:, None] - i[None, :] * dil
        valid = idx >= 0
        idxc = jnp.maximum(idx, 0)
        kg, vg = kc[:, idxc], vc[:, idxc]
        s = jnp.einsum('bthd,btnhd->bhtn', q, kg).astype(jnp.float32) * ATTN_SCALE
        s = s - slopes[:, None, None] * (i * dil).astype(jnp.float32)[None, None, :]
        s = jnp.where(valid[None, None], s, -jnp.inf)
        m = jnp.max(s, -1, keepdims=True)
        p = jnp.exp(s - m)
        den = jnp.sum(p, -1)
        o = jnp.einsum('bhtn,btnhd->bthd', p, vg.astype(jnp.float32)) / jnp.swapaxes(den, 1, 2)[..., None]
        outs.append(o)
        lses.append(jnp.swapaxes(m[..., 0] + jnp.log(den), 1, 2))
    return _mix_dilations(outs, lses)


def _mix_dilations(outs, lses):
    w = jax.nn.softmax(jnp.stack(lses), axis=0)
    return jnp.sum(w[..., None] * jnp.stack(outs), axis=0)


def _multi_pool(u, pos, w_pool, pool_scale):
    N, L, _ = u.shape
    uf = u.astype(jnp.float32)
    c = jnp.cumsum(uf, axis=1)
    pooled = []
    for g, w in enumerate(POOL_WINDOWS):
        cg = c[..., g * POOL_GROUP_DIM:(g + 1) * POOL_GROUP_DIM]
        lag = jnp.pad(cg, ((0, 0), (w, 0), (0, 0)))[:, :L]
        cnt = jnp.minimum(w, pos + 1).astype(jnp.float32)
        pooled.append((cg - lag) / cnt[None, :, None])
    diff = jnp.stack(pooled, axis=2) - uf.reshape(N, L, N_POOL_GROUPS, POOL_GROUP_DIM)
    out = jnp.einsum('nlgc,gce->nlge', diff, w_pool.astype(jnp.float32)).reshape(N, L, D_B)
    return out * pool_scale.astype(jnp.float32)


def _mem_attention(q, km, vm):
    s = jnp.einsum('nlhd,nmhd->nhlm', q, km).astype(jnp.float32) * ATTN_SCALE
    p = jax.nn.softmax(s, axis=-1)
    return jnp.einsum('nhlm,nmhd->nlhd', p, vm.astype(jnp.float32))


def _merge(x, attn, pool, memo, w_o):
    lead = x.shape[:2]
    cat = jnp.concatenate([attn.reshape(lead + (D_A,)).astype(x.dtype), pool.astype(x.dtype),
                           memo.reshape(lead + (D_C,)).astype(x.dtype)], axis=-1)
    return cat @ w_o


def _mixer_prompt(x, mem, w_in, w_mem_kv, w_pool, pool_scale, w_o):
    B, S, _ = x.shape
    slopes = alibi_slopes(N_HEADS_A)
    q_a, k_a, v_a, u_b, q_c = jnp.split(x @ w_in, SPLITS, axis=-1)
    q_a, k_a, v_a = _heads(q_a, N_HEADS_A), _heads(k_a, N_HEADS_A), _heads(v_a, N_HEADS_A)
    outs, lses = [], []
    for window, dil in DILATED_PATTERNS:
        o, l = _dilated_band_prompt(q_a, k_a, v_a, window, dil, slopes)
        outs.append(o)
        lses.append(l)
    attn = _mix_dilations(outs, lses)
    pool = _multi_pool(u_b, jnp.arange(S), w_pool, pool_scale)
    km, vm = jnp.split(mem @ w_mem_kv, 2, axis=-1)
    km, vm = _heads(km, N_HEADS_C), _heads(vm, N_HEADS_C)
    memo = _mem_attention(_heads(q_c, N_HEADS_C), km, vm)
    y = _merge(x, attn, pool, memo, w_o)
    wbp = min(MAX_WINDOW, S)
    return y, k_a[:, S - wbp:], v_a[:, S - wbp:], u_b[:, S - POOL_BUF:], km, vm


def _mixer_sample(x, win_k, win_v, pool_buf, mem_k, mem_v, w_in, w_pool, pool_scale, w_o):
    T = x.shape[1]
    wb = win_k.shape[1]
    pb = pool_buf.shape[1]
    slopes = alibi_slopes(N_HEADS_A)
    q_a, k_a, v_a, u_b, q_c = jnp.split(x @ w_in, SPLITS, axis=-1)
    q_a, k_a, v_a = _heads(q_a, N_HEADS_A), _heads(k_a, N_HEADS_A), _heads(v_a, N_HEADS_A)
    kc = jnp.concatenate([win_k.astype(k_a.dtype), k_a], axis=1)
    vc = jnp.concatenate([win_v.astype(v_a.dtype), v_a], axis=1)
    attn = _dilated_sample(q_a, kc, vc, wb, slopes)
    useq = jnp.concatenate([pool_buf.astype(u_b.dtype), u_b], axis=1)
    pos = PAST_LEN - pb + jnp.arange(pb + T)
    pool = _multi_pool(useq, pos, w_pool, pool_scale)[:, pb:]
    memo = _mem_attention(_heads(q_c, N_HEADS_C), mem_k, mem_v)
    y = _merge(x, attn, pool, memo, w_o)
    return y, kc[:, T:], vc[:, T:], useq[:, T:]


def _hier_moe(x, w_r1, b_r1, w_r2, b_r2, w_gate, w_up, w_down):
    p1 = jax.nn.softmax((x @ w_r1).astype(jnp.float32) + b_r1.astype(jnp.float32), axis=-1)
    v1, g_sel = lax.top_k(p1, 1)
    l2 = jnp.einsum('nd,gde->nge', x, w_r2).astype(jnp.float32) + b_r2.astype(jnp.float32)
    l2_sel = jnp.take_along_axis(l2, g_sel[:, :, None], axis=1)[:, 0]
    v2, e_idx = lax.top_k(l2_sel, TOP_K_INNER)
    w2 = jax.nn.softmax(v2, axis=-1)
    inner = jnp.sum(w2[..., None] * jax.nn.one_hot(e_idx, EXPERTS_PER_GROUP, dtype=jnp.float32), axis=1)
    gate = v1[:, 0, None, None] * jax.nn.one_hot(g_sel[:, 0], N_EXPERT_GROUPS, dtype=jnp.float32)[:, :, None] * inner[:, None, :]
    y = jnp.zeros(x.shape, jnp.float32)
    for g in range(N_EXPERT_GROUPS):
        hg = jnp.einsum('nd,edf->nef', x, w_gate[g])
        hu = jnp.einsum('nd,edf->nef', x, w_up[g])
        a = jax.nn.silu(hg) * hu * gate[:, g, :, None].astype(x.dtype)
        y = y + jnp.einsum('nef,efd->nd', a, w_down[g]).astype(jnp.float32)
    return y.astype(x.dtype)


def _post_block(x, mix, ln1_g, ln1_b, ln2_g, ln2_b, w_r1, b_r1, w_r2, b_r2, w_gate, w_up, w_down):
    h1 = layer_norm(DN_ALPHA * x + mix, ln1_g, ln1_b)
    f = _hier_moe(h1.reshape(-1, h1.shape[-1]), w_r1, b_r1, w_r2, b_r2, w_gate, w_up, w_down).reshape(h1.shape)
    return layer_norm(DN_ALPHA * h1 + f, ln2_g, ln2_b)


def setup_inputs(seed: int = 0) -> dict:
    key = jax.random.key(seed)
    ks = jax.random.split(key, 24)
    wb = min(MAX_WINDOW, PAST_LEN)
    nrm = jax.random.normal
    f32 = jnp.float32
    return {
        "x_prompt": nrm(ks[0], (BATCH, SEQ, D_MODEL), f32),
        "x_sample": nrm(ks[1], (DEC_BATCH, DEC_SEQ, D_MODEL), f32),
        "cache_win_k": nrm(ks[2], (DEPTH, DEC_BATCH, wb, N_HEADS_A, HEAD_DIM), f32),
        "cache_win_v": nrm(ks[3], (DEPTH, DEC_BATCH, wb, N_HEADS_A, HEAD_DIM), f32),
        "state_pool": nrm(ks[4], (DEPTH, DEC_BATCH, POOL_BUF, D_B), f32),
        "cache_mem_k": nrm(ks[5], (DEPTH, DEC_BATCH, N_MEM, N_HEADS_C, HEAD_DIM), f32),
        "cache_mem_v": nrm(ks[6], (DEPTH, DEC_BATCH, N_MEM, N_HEADS_C, HEAD_DIM), f32),
        "mem_prompt": nrm(ks[7], (BATCH, N_MEM, D_MODEL), f32),
        "w_in": nrm(ks[8], (DEPTH, D_MODEL, D_IN), f32) * D_MODEL ** -0.5,
        "w_mem_kv": nrm(ks[9], (DEPTH, D_MODEL, 2 * D_C), f32) * D_MODEL ** -0.5,
        "w_pool": nrm(ks[10], (DEPTH, N_POOL_GROUPS, POOL_GROUP_DIM, POOL_GROUP_DIM), f32) * POOL_GROUP_DIM ** -0.5,
        "pool_scale": 1.0 + 0.1 * nrm(ks[11], (DEPTH, D_B), f32),
        "w_o": nrm(ks[12], (DEPTH, D_MIX, D_MODEL), f32) * (D_MIX ** -0.5 * DN_BETA),
        "ln1_g": 1.0 + 0.05 * nrm(ks[13], (DEPTH, D_MODEL), f32),
        "ln1_b": 0.02 * nrm(ks[14], (DEPTH, D_MODEL), f32),
        "ln2_g": 1.0 + 0.05 * nrm(ks[15], (DEPTH, D_MODEL), f32),
        "ln2_b": 0.02 * nrm(ks[16], (DEPTH, D_MODEL), f32),
        "w_r1": nrm(ks[17], (DEPTH, D_MODEL, N_EXPERT_GROUPS), f32) * D_MODEL ** -0.5,
        "b_r1": 0.01 * nrm(ks[18], (DEPTH, N_EXPERT_GROUPS), f32),
        "w_r2": nrm(ks[19], (DEPTH, N_EXPERT_GROUPS, D_MODEL, EXPERTS_PER_GROUP), f32) * D_MODEL ** -0.5,
        "b_r2": 0.01 * nrm(ks[20], (DEPTH, N_EXPERT_GROUPS, EXPERTS_PER_GROUP), f32),
        "w_gate": nrm(ks[21], (DEPTH, N_EXPERT_GROUPS, EXPERTS_PER_GROUP, D_MODEL, D_EXPERT), f32) * D_MODEL ** -0.5,
        "w_up": nrm(ks[22], (DEPTH, N_EXPERT_GROUPS, EXPERTS_PER_GROUP, D_MODEL, D_EXPERT), f32) * D_MODEL ** -0.5,
        "w_down": nrm(ks[23], (DEPTH, N_EXPERT_GROUPS, EXPERTS_PER_GROUP, D_EXPERT, D_MODEL), f32) * (D_EXPERT ** -0.5 * DN_BETA),
    }


def reference(x_prompt, x_sample, cache_win_k, cache_win_v, state_pool, cache_mem_k, cache_mem_v, mem_prompt,
              w_in, w_mem_kv, w_pool, pool_scale, w_o, ln1_g, ln1_b, ln2_g, ln2_b,
              w_r1, b_r1, w_r2, b_r2, w_gate, w_up, w_down):
    h_p, h_s = x_prompt, x_sample
    wk_p, wv_p, pl_p, mk_p, mv_p = [], [], [], [], []
    wk_s, wv_s, pl_s = [], [], []
    for l in range(DEPTH):
        mix_p, k_new, v_new, u_new, km, vm = _mixer_prompt(h_p, mem_prompt, w_in[l], w_mem_kv[l], w_pool[l], pool_scale[l], w_o[l])
        h_p = _post_block(h_p, mix_p, ln1_g[l], ln1_b[l], ln2_g[l], ln2_b[l], w_r1[l], b_r1[l], w_r2[l], b_r2[l], w_gate[l], w_up[l], w_down[l])
        wk_p.append(k_new); wv_p.append(v_new); pl_p.append(u_new); mk_p.append(km); mv_p.append(vm)
        mix_s, kbuf, vbuf, ubuf = _mixer_sample(h_s, cache_win_k[l], cache_win_v[l], state_pool[l], cache_mem_k[l], cache_mem_v[l], w_in[l], w_pool[l], pool_scale[l], w_o[l])
        h_s = _post_block(h_s, mix_s, ln1_g[l], ln1_b[l], ln2_g[l], ln2_b[l], w_r1[l], b_r1[l], w_r2[l], b_r2[l], w_gate[l], w_up[l], w_down[l])
        wk_s.append(kbuf); wv_s.append(vbuf); pl_s.append(ubuf)
    return (h_p, h_s, jnp.stack(wk_p), jnp.stack(wv_p), jnp.stack(pl_p), jnp.stack(mk_p), jnp.stack(mv_p),
            jnp.stack(wk_s), jnp.stack(wv_s), jnp.stack(pl_s))
```

```python
import functools
import math

import jax
import jax.numpy as jnp
from jax import lax
from jax.experimental import pallas as pl
from jax.experimental.pallas import tpu as pltpu

F32 = jnp.float32
BF16 = jnp.bfloat16
HIGHEST = lax.Precision.HIGHEST
NEG_INF = float("-inf")

HEAD_DIM = 64
DILATED_PATTERNS = ((128, 1), (512, 4), (2048, 16))
BAND_BLOCK = 128
POOL_WINDOWS = (2, 4, 8, 16)
N_EXPERT_GROUPS = 4
EXPERTS_PER_GROUP = 8
N_EXPERTS = N_EXPERT_GROUPS * EXPERTS_PER_GROUP
PAST_LEN = 16384
LN_EPS = 1e-5
ATTN_SCALE = HEAD_DIM ** -0.5

LANES = 128
SUBLANES = 8
VMEM_LIMIT = 56 * 1024 * 1024

ROUTER_OFF = N_EXPERT_GROUPS


def _cparams(sem):
    return pltpu.CompilerParams(dimension_semantics=sem, vmem_limit_bytes=VMEM_LIMIT)


def _dot(a, b, precision=None):
    return jnp.dot(a, b, preferred_element_type=F32, precision=precision)


def _dot_nt(a, b, precision=None):
    return lax.dot_general(a, b, (((1,), (1,)), ((), ())), preferred_element_type=F32, precision=precision)


def _memproj_body(mem_ref, wkT_ref, wv_ref, wvT_ref, kmT_ref, vm_ref, vmT_ref):
    m = mem_ref[0].astype(BF16)
    kmT_ref[0] = _dot_nt(wkT_ref[...], m)
    vm_ref[0] = _dot(m, wv_ref[...])
    vmT_ref[0] = _dot_nt(wvT_ref[...], m)


def _memproj(mem, wkT, wv, wvT):
    B, M, D = mem.shape
    C = wv.shape[1]
    full = lambda shape: pl.BlockSpec(shape, lambda b: (0,) * len(shape))
    return pl.pallas_call(
        _memproj_body,
        grid=(B,),
        in_specs=[pl.BlockSpec((1, M, D), lambda b: (b, 0, 0)), full((C, D)), full((D, C)), full((C, D))],
        out_specs=[pl.BlockSpec((1, C, M), lambda b: (b, 0, 0)),
                   pl.BlockSpec((1, M, C), lambda b: (b, 0, 0)),
                   pl.BlockSpec((1, C, M), lambda b: (b, 0, 0))],
        out_shape=[jax.ShapeDtypeStruct((B, C, M), F32), jax.ShapeDtypeStruct((B, M, C), F32),
                   jax.ShapeDtypeStruct((B, C, M), F32)],
        compiler_params=_cparams(("arbitrary",)),
        name="memproj",
    )(mem, wkT, wv, wvT)


def _proj_body(d_a, d_b, x_ref, w_ref, wkvT_ref, q_ref, k_ref, v_ref, u_ref, qc_ref, kT_ref, vT_ref):
    xb = x_ref[0].astype(BF16)
    y = _dot(xb, w_ref[...])
    q_ref[0] = y[:, 0:d_a]
    k_ref[0] = y[:, d_a:2 * d_a]
    v_ref[0] = y[:, 2 * d_a:3 * d_a]
    u_ref[0] = y[:, 3 * d_a:3 * d_a + d_b]
    qc_ref[0] = y[:, 3 * d_a + d_b:]
    yT = _dot_nt(wkvT_ref[...], xb)
    kT_ref[0] = yT[0:d_a]
    vT_ref[0] = yT[d_a:]


def _proj(x, w_bf, wkvT_bf, d_a, d_b, d_c, ts):
    B, S, D = x.shape
    d_in = w_bf.shape[1]
    row = lambda n: pl.BlockSpec((1, ts, n), lambda b, i: (b, i, 0))
    col = lambda n: pl.BlockSpec((1, n, ts), lambda b, i: (b, 0, i))
    return pl.pallas_call(
        functools.partial(_proj_body, d_a, d_b),
        grid=(B, S // ts),
        in_specs=[row(D), pl.BlockSpec((D, d_in), lambda b, i: (0, 0)),
                  pl.BlockSpec((2 * d_a, D), lambda b, i: (0, 0))],
        out_specs=[row(d_a), row(d_a), row(d_a), row(d_b), row(d_c), col(d_a), col(d_a)],
        out_shape=[jax.ShapeDtypeStruct((B, S, d_a), F32)] * 3
        + [jax.ShapeDtypeStruct((B, S, d_b), F32), jax.ShapeDtypeStruct((B, S, d_c), F32)]
        + [jax.ShapeDtypeStruct((B, d_a, S), F32)] * 2,
        compiler_params=_cparams(("arbitrary", "arbitrary")),
        name="proj",
    )(x, w_bf, wkvT_bf)


def _attn_body(S, slopes_ref, q_ref, k_ref, v_ref, o_ref, opat_ref, lpat_ref):
    hp = pl.program_id(1)
    blk = BAND_BLOCK
    lane = lax.broadcasted_iota(jnp.int32, (blk, LANES), 1)
    head0 = lane < HEAD_DIM

    for pi, (window, dil) in enumerate(DILATED_PATTERNS):
        n_steps = window // dil
        L = S // dil
        nblk = L // blk
        has_prev = nblk > 1
        nk = 2 * blk if has_prev else blk
        qi = lax.broadcasted_iota(jnp.int32, (blk, nk), 0)
        kj = lax.broadcasted_iota(jnp.int32, (blk, nk), 1)
        steps = qi + (blk if has_prev else 0) - kj
        valid = (steps >= 0) & (steps <= n_steps)
        dist = (steps * dil).astype(F32)
        biases = []
        for hh in range(2):
            slope = slopes_ref[2 * hp + hh]
            biases.append(jnp.where(valid, -slope * dist, NEG_INF))
        prev_cols = kj < blk

        def block(idx, carry, dil=dil, nblk=nblk, has_prev=has_prev, biases=biases, prev_cols=prev_cols, pi=pi):
            r = idx // nblk
            j = idx % nblk
            start = r + blk * dil * j
            rows = lambda s0: pl.ds(s0, blk, stride=dil) if dil > 1 else pl.ds(pl.multiple_of(s0, blk), blk)
            qb = q_ref[0, rows(start), :] * ATTN_SCALE
            kc = k_ref[0, rows(start), :]
            vc = v_ref[0, rows(start), :]
            if has_prev:
                pstart = jnp.maximum(start - blk * dil, r)
                kb = jnp.concatenate([k_ref[0, rows(pstart), :], kc], axis=0)
                vb = jnp.concatenate([v_ref[0, rows(pstart), :], vc], axis=0)
            else:
                kb, vb = kc, vc
            kb = kb.astype(BF16)
            vb = vb.astype(BF16)
            outs, lses = [], []
            for hh in range(2):
                qm = jnp.where(head0 if hh == 0 else ~head0, qb, 0.0).astype(BF16)
                s = _dot_nt(qm, kb) + biases[hh]
                if has_prev:
                    s = jnp.where(prev_cols & (j == 0), NEG_INF, s)
                m = jnp.max(s, axis=1, keepdims=True)
                p = jnp.exp(s - m)
                den = jnp.sum(p, axis=1, keepdims=True)
                o = _dot(p.astype(BF16), vb) / den
                outs.append(o)
                lses.append(m + jnp.log(den))
            o_pair = jnp.where(head0, outs[0], outs[1])
            l_pair = jnp.where(head0, lses[0], lses[1])
            opat_ref[pi, rows(start), :] = o_pair
            lpat_ref[pi, rows(start), :] = l_pair
            return carry

        lax.fori_loop(0, dil * nblk, block, 0)

    chunk = 256

    def mix(c, carry):
        rows = pl.ds(pl.multiple_of(c * chunk, chunk), chunk)
        l0, l1, l2 = lpat_ref[0, rows, :], lpat_ref[1, rows, :], lpat_ref[2, rows, :]
        mx = jnp.maximum(jnp.maximum(l0, l1), l2)
        w0, w1, w2 = jnp.exp(l0 - mx), jnp.exp(l1 - mx), jnp.exp(l2 - mx)
        num = w0 * opat_ref[0, rows, :] + w1 * opat_ref[1, rows, :] + w2 * opat_ref[2, rows, :]
        o_ref[0, rows, :] = num / (w0 + w1 + w2)
        return carry

    lax.fori_loop(0, S // chunk, mix, 0)


def _attn(q, k, v, slopes):
    B, S, d_a = q.shape
    assert S % (BAND_BLOCK * max(d for _, d in DILATED_PATTERNS)) == 0
    npair = d_a // LANES
    spec = pl.BlockSpec((1, S, LANES), lambda b, h, *_: (b, 0, h))
    grid_spec = pltpu.PrefetchScalarGridSpec(
        num_scalar_prefetch=0,
        grid=(B, npair),
        in_specs=[pl.BlockSpec(memory_space=pltpu.SMEM), spec, spec, spec],
        out_specs=spec,
        scratch_shapes=[pltpu.VMEM((len(DILATED_PATTERNS), S, LANES), F32),
                        pltpu.VMEM((len(DILATED_PATTERNS), S, LANES), F32)],
    )
    return pl.pallas_call(
        functools.partial(_attn_body, S),
        grid_spec=grid_spec,
        out_shape=jax.ShapeDtypeStruct((B, S, d_a), F32),
        compiler_params=_cparams(("arbitrary", "arbitrary")),
        name="dilated_attn",
    )(slopes, q, k, v)


def _layer_norm(z, g, b):
    mu = jnp.mean(z, axis=-1, keepdims=True)
    zc = z - mu
    var = jnp.mean(zc * zc, axis=-1, keepdims=True)
    return zc * lax.rsqrt(var + LN_EPS) * g + b


def _route(logits):
    n = logits.shape[0]
    lane = lax.broadcasted_iota(jnp.int32, (n, LANES), 1)
    lane_f = lane.astype(F32)
    big = float(LANES)
    is_outer = lane < N_EXPERT_GROUPS
    l1 = jnp.where(is_outer, logits, NEG_INF)
    m1 = jnp.max(l1, axis=1, keepdims=True)
    g_sel = jnp.min(jnp.where(l1 == m1, lane_f, big), axis=1, keepdims=True)
    v1 = 1.0 / jnp.sum(jnp.exp(l1 - m1), axis=1, keepdims=True)
    lo = ROUTER_OFF + g_sel * EXPERTS_PER_GROUP
    in_group = (lane_f >= lo) & (lane_f < lo + EXPERTS_PER_GROUP)
    l2 = jnp.where(in_group, logits, NEG_INF)
    ma = jnp.max(l2, axis=1, keepdims=True)
    ia = jnp.min(jnp.where(l2 == ma, lane_f, big), axis=1, keepdims=True)
    l2b = jnp.where(lane_f == ia, NEG_INF, l2)
    mb = jnp.max(l2b, axis=1, keepdims=True)
    ib = jnp.min(jnp.where(l2b == mb, lane_f, big), axis=1, keepdims=True)
    eb = jnp.exp(mb - ma)
    wa = 1.0 / (1.0 + eb)
    wb = eb / (1.0 + eb)
    return ia - ROUTER_OFF, ib - ROUTER_OFF, v1 * wa, v1 * wb


def _rank_and_meta(e0, e1, g0, g1, carry):
    n = e0.shape[0]
    lane_f = lax.broadcasted_iota(jnp.int32, (n, LANES), 1).astype(F32)
    oh0 = (lane_f == e0).astype(F32)
    oh1 = (lane_f == e1).astype(F32)
    both = oh0 + oh1
    ti = lax.broadcasted_iota(jnp.int32, (n, n), 0)
    tj = lax.broadcasted_iota(jnp.int32, (n, n), 1)
    tri = (tj < ti).astype(BF16)
    before = _dot(tri, both.astype(BF16)) + carry
    r0 = jnp.sum(before * oh0, axis=1, keepdims=True)
    r1 = jnp.sum(before * oh1, axis=1, keepdims=True)
    new_carry = carry + jnp.sum(both, axis=0, keepdims=True)
    lane = lax.broadcasted_iota(jnp.int32, (n, LANES), 1)
    meta = jnp.zeros((n, LANES), F32)
    for i, val in enumerate((e0, e1, g0, g1, r0, r1)):
        meta = jnp.where(lane == i, val, meta)
    return meta, new_carry


def _lane_group(shape, width):
    lane = lax.broadcasted_iota(jnp.int32, shape, len(shape) - 1)
    grp = jnp.zeros(shape, jnp.int32)
    for g in range(1, shape[-1] // width):
        grp = grp + (lane >= g * width).astype(jnp.int32)
    return grp


def _select_by_group(grp, vals):
    out = vals[-1]
    for g in range(len(vals) - 2, -1, -1):
        out = jnp.where(grp == g, vals[g], out)
    return out


def _store_token_rows(ref, val):
    n, d = val.shape
    nchunk = d // LANES
    for c in range(nchunk):
        ref[pl.ds(c, n, stride=nchunk), :] = val[:, c * LANES:(c + 1) * LANES]


def _load_token_rows(ref, n, nchunk):
    return jnp.concatenate([ref[pl.ds(c, n, stride=nchunk), :] for c in range(nchunk)], axis=1)


HALO = 16


def _mix_body(tm, d_b, alpha, x_ref, at_ref, u_ref, uh_ref, qc_ref, kmT_ref, vm_ref, wo_ref, wp_ref, ps_ref,
              g1_ref, b1_ref, wr_ref, br_ref, h1_ref, meta_ref, cnt_ref, carry_ref):
    i = pl.program_id(1)

    @pl.when((pl.program_id(0) == 0) & (i == 0))
    def _():
        carry_ref[...] = jnp.zeros_like(carry_ref)

    u = u_ref[0]
    halo = jnp.where(i == 0, 0.0, uh_ref[0])
    ext = jnp.concatenate([halo, halo, u], axis=0)
    s2 = ext + pltpu.roll(ext, 1, 0)
    s4 = s2 + pltpu.roll(s2, 2, 0)
    s8 = s4 + pltpu.roll(s4, 4, 0)
    s16 = s8 + pltpu.roll(s8, 8, 0)
    grp = _lane_group((tm, d_b), d_b // len(POOL_WINDOWS))
    sums = [s[2 * HALO:] for s in (s2, s4, s8, s16)]
    win = _select_by_group(grp, sums)
    wlen = _select_by_group(grp, [jnp.int32(w) for w in POOL_WINDOWS])
    pos = i * tm + lax.broadcasted_iota(jnp.int32, (tm, d_b), 0)
    cnt = jnp.minimum(wlen, pos + 1).astype(F32)
    diff = win / cnt - u
    pool = _dot(diff.astype(BF16), wp_ref[...]) * ps_ref[...]

    qc = qc_ref[0] * ATTN_SCALE
    d_c = qc.shape[1]
    kmT = kmT_ref[0].astype(BF16)
    vm = vm_ref[0].astype(BF16)
    hl = _lane_group((tm, d_c), HEAD_DIM)
    memo = jnp.zeros((tm, d_c), F32)
    for h in range(d_c // HEAD_DIM):
        qm = jnp.where(hl == h, qc, 0.0).astype(BF16)
        s = _dot(qm, kmT)
        p = jnp.exp(s - jnp.max(s, axis=1, keepdims=True))
        den = jnp.sum(p, axis=1, keepdims=True)
        o = _dot(p.astype(BF16), vm) / den
        memo = jnp.where(hl == h, o, memo)

    cat = jnp.concatenate([at_ref[0].astype(BF16), pool.astype(BF16), memo.astype(BF16)], axis=1)
    mixv = _dot(cat, wo_ref[...])
    h1 = _layer_norm(alpha * x_ref[0] + mixv, g1_ref[...], b1_ref[...])
    _store_token_rows(h1_ref, h1)

    logits = _dot(h1, wr_ref[...], precision=HIGHEST) + br_ref[...]
    e0, e1, g0, g1 = _route(logits)
    meta, new_carry = _rank_and_meta(e0, e1, g0, g1, carry_ref[0:1, :])
    meta_ref[...] = meta
    carry_ref[...] = jnp.broadcast_to(new_carry, carry_ref.shape)
    cnt_ref[...] = carry_ref[...]


def _mix(x, attn, u, qc, kmT, vm, wo_bf, wp_bd_bf, pool_scale, g1, b1, wr, br, alpha, tm):
    B, S, D = x.shape
    d_a, d_b, d_c = attn.shape[2], u.shape[2], qc.shape[2]
    n_mem = vm.shape[1]
    nt = S // tm
    row = lambda n: pl.BlockSpec((1, tm, n), lambda b, i: (b, i, 0))
    full = lambda shape: pl.BlockSpec(shape, lambda b, i: (0,) * len(shape))
    per_b = lambda shape: pl.BlockSpec((1,) + shape, lambda b, i: (b, 0, 0))
    halo_spec = pl.BlockSpec((1, HALO, d_b), lambda b, i: (b, jnp.maximum(i * (tm // HALO) - 1, 0), 0))
    nchunk = D // LANES
    return pl.pallas_call(
        functools.partial(_mix_body, tm, d_b, alpha),
        grid=(B, nt),
        in_specs=[row(D), row(d_a), row(d_b), halo_spec, row(d_c), per_b((d_c, n_mem)), per_b((n_mem, d_c)),
                  full((D, D)), full((d_b, d_b)), full((1, d_b)), full((1, D)), full((1, D)),
                  full((D, LANES)), full((1, LANES))],
        out_specs=[pl.BlockSpec((tm * nchunk, LANES), lambda b, i: (b * nt + i, 0)),
                   pl.BlockSpec((tm, LANES), lambda b, i: (b * nt + i, 0)),
                   pl.BlockSpec((SUBLANES, LANES), lambda b, i: (0, 0))],
        out_shape=[jax.ShapeDtypeStruct((B * S * nchunk, LANES), F32),
                   jax.ShapeDtypeStruct((B * S, LANES), F32),
                   jax.ShapeDtypeStruct((SUBLANES, LANES), F32)],
        scratch_shapes=[pltpu.VMEM((SUBLANES, LANES), F32)],
        compiler_params=_cparams(("arbitrary", "arbitrary")),
        name="mix_ln1_router",
    )(x, attn, u, u, qc, kmT, vm, wo_bf, wp_bd_bf, pool_scale, g1, b1, wr, br)


def _columns_to_lanes(cols, rows):
    lane = lax.broadcasted_iota(jnp.int32, (rows, LANES), 1)
    tile = jnp.zeros((rows, LANES), F32)
    for t, c in enumerate(cols):
        tile = jnp.where(lane == t, c, tile)
    return tile


def _decode_body(T, wb, slopes_ref, x_ref, wqkvT_ref, wqcT_ref, ck_ref, cv_ref, mk_ref, mv_ref,
                 nk_ref, nv_ref, at_ref, mo_ref, qkv_ref, qc_ref):
    H = ck_ref.shape[1]
    d_a = H * HEAD_DIM
    xs = x_ref[0]
    qkv_ref[...] = _dot_nt(wqkvT_ref[...], xs, precision=HIGHEST)
    qc_ref[...] = _dot_nt(wqcT_ref[...], xs, precision=HIGHEST)

    jpos = lax.broadcasted_iota(jnp.int32, (1, wb), 1)
    tnew = lax.broadcasted_iota(jnp.int32, (1, T), 1)
    mult_c, dist_c, mult_n, dist_n = [], [], [], []
    for t in range(T):
        dc = wb + t - jpos
        dn = t - tnew
        mc = jnp.zeros((1, wb), F32)
        mn = jnp.zeros((1, T), F32)
        for window, dil in DILATED_PATTERNS:
            assert dil & (dil - 1) == 0
            mc = mc + (((dc & (dil - 1)) == 0) & (dc <= window) & (dc >= 0)).astype(F32)
            mn = mn + (((dn & (dil - 1)) == 0) & (dn <= window) & (dn >= 0)).astype(F32)
        mult_c.append(mc)
        mult_n.append(mn)
        dist_c.append(dc.astype(F32))
        dist_n.append(jnp.maximum(dn, 0).astype(F32))

    lane_t = lax.broadcasted_iota(jnp.int32, (HEAD_DIM, LANES), 1)
    last = wb - LANES

    def head(h, carry):
        slope = slopes_ref[h]
        kTc = ck_ref[0, h]
        vTc = cv_ref[0, h]
        r0 = pl.multiple_of(h * HEAD_DIM, HEAD_DIM)
        qT = qkv_ref[pl.ds(r0, HEAD_DIM), :]
        kTn = qkv_ref[pl.ds(d_a + r0, HEAD_DIM), :]
        vTn = qkv_ref[pl.ds(2 * d_a + r0, HEAD_DIM), :]
        cols = []
        for t in range(T):
            qcol = qT[:, t:t + 1] * ATTN_SCALE
            sc = jnp.sum(kTc * qcol, axis=0, keepdims=True) - slope * dist_c[t]
            sn = jnp.sum(kTn * qcol, axis=0, keepdims=True) - slope * dist_n[t]
            sc = jnp.where(mult_c[t] > 0, sc, NEG_INF)
            sn = jnp.where(mult_n[t] > 0, sn, NEG_INF)
            m = jnp.maximum(jnp.max(sc, axis=1, keepdims=True), jnp.max(sn, axis=1, keepdims=True))
            pc = mult_c[t] * jnp.exp(sc - m)
            pn = mult_n[t] * jnp.exp(sn - m)
            den = jnp.sum(pc, axis=1, keepdims=True) + jnp.sum(pn, axis=1, keepdims=True)
            o = jnp.sum(vTc * pc, axis=1, keepdims=True) + jnp.sum(vTn * pn, axis=1, keepdims=True)
            cols.append(o / den)
        at_ref[0, pl.ds(r0, HEAD_DIM), :] = _columns_to_lanes(cols, HEAD_DIM)

        rk = pltpu.roll(kTc, wb - T, 1)
        rv = pltpu.roll(vTc, wb - T, 1)
        nk_ref[0, h] = rk
        nv_ref[0, h] = rv
        newk = jnp.zeros((HEAD_DIM, LANES), F32)
        newv = jnp.zeros((HEAD_DIM, LANES), F32)
        for t in range(T):
            newk = jnp.where(lane_t == LANES - T + t, kTn[:, t:t + 1], newk)
            newv = jnp.where(lane_t == LANES - T + t, vTn[:, t:t + 1], newv)
        nk_ref[0, h, :, last:] = jnp.where(lane_t >= LANES - T, newk, rk[:, last:])
        nv_ref[0, h, :, last:] = jnp.where(lane_t >= LANES - T, newv, rv[:, last:])
        return carry

    lax.fori_loop(0, H, head, 0)

    Hc = mk_ref.shape[1]
    for h in range(Hc):
        kT = mk_ref[0, h]
        vT = mv_ref[0, h]
        r0 = h * HEAD_DIM
        qT = qc_ref[r0:r0 + HEAD_DIM, :]
        cols = []
        for t in range(T):
            qcol = qT[:, t:t + 1] * ATTN_SCALE
            s = jnp.sum(kT * qcol, axis=0, keepdims=True)
            p = jnp.exp(s - jnp.max(s, axis=1, keepdims=True))
            den = jnp.sum(p, axis=1, keepdims=True)
            cols.append(jnp.sum(vT * p, axis=1, keepdims=True) / den)
        mo_ref[0, r0:r0 + HEAD_DIM, :] = _columns_to_lanes(cols, HEAD_DIM)


def _decode(xs, wqkvT, wqcT, ck, cv, mk, mv, slopes):
    DB, T, D = xs.shape
    _, H, hd, wb = ck.shape
    _, Hc, _, n_mem = mk.shape
    d_a, d_c = H * hd, Hc * hd
    assert wb >= max(w for w, _ in DILATED_PATTERNS) and wb % LANES == 0 and T <= LANES
    cache = pl.BlockSpec((1, H, hd, wb), lambda b, *_: (b, 0, 0, 0))
    memc = pl.BlockSpec((1, Hc, hd, n_mem), lambda b, *_: (b, 0, 0, 0))
    grid_spec = pltpu.PrefetchScalarGridSpec(
        num_scalar_prefetch=0,
        grid=(DB,),
        in_specs=[pl.BlockSpec(memory_space=pltpu.SMEM),
                  pl.BlockSpec((1, T, D), lambda b, *_: (b, 0, 0)),
                  pl.BlockSpec((3 * d_a, D), lambda b, *_: (0, 0)),
                  pl.BlockSpec((d_c, D), lambda b, *_: (0, 0)),
                  cache, cache, memc, memc],
        out_specs=[cache, cache,
                   pl.BlockSpec((1, d_a, LANES), lambda b, *_: (b, 0, 0)),
                   pl.BlockSpec((1, d_c, LANES), lambda b, *_: (b, 0, 0))],
        scratch_shapes=[pltpu.VMEM((3 * d_a, T), F32), pltpu.VMEM((d_c, T), F32)],
    )
    nk, nv, at, mo = pl.pallas_call(
        functools.partial(_decode_body, T, wb),
        grid_spec=grid_spec,
        out_shape=[jax.ShapeDtypeStruct(ck.shape, F32), jax.ShapeDtypeStruct(cv.shape, F32),
                   jax.ShapeDtypeStruct((DB, d_a, LANES), F32), jax.ShapeDtypeStruct((DB, d_c, LANES), F32)],
        compiler_params=_cparams(("arbitrary",)),
        name="decode_attn_cache",
    )(slopes, xs, wqkvT, wqcT, ck, cv, mk, mv)
    return nk, nv, at[:, :, :T], mo[:, :, :T]


def _smix_body(T, alpha, pos0, x_ref, at_ref, mo_ref, st_ref, wu_ref, wo_ref, wp_ref, ps_ref, g1_ref, b1_ref,
               wr_ref, br_ref, cin_ref, h1_ref, meta_ref, cnt_ref, pool_ref):
    n = x_ref.shape[0]
    db = n // T
    pb = st_ref.shape[0]
    d_b = st_ref.shape[2]
    x = x_ref[...]
    u_new = _dot(x, wu_ref[...], precision=HIGHEST)
    seq = [st_ref[j] for j in range(pb)] + [u_new[t * db:(t + 1) * db] for t in range(T)]
    for j in range(pb):
        pool_ref[j] = seq[j + T]
    grp = _lane_group((db, d_b), d_b // len(POOL_WINDOWS))
    diffs = []
    for t in range(T):
        j = pb + t
        per_w = []
        for w in POOL_WINDOWS:
            acc = seq[j]
            for back in range(1, w):
                if j - back >= 0:
                    acc = acc + seq[j - back]
            per_w.append(acc / float(min(w, pos0 + j + 1)))
        diffs.append(_select_by_group(grp, per_w) - seq[j])
    diff = jnp.concatenate(diffs, axis=0)
    pool = _dot(diff, wp_ref[...], precision=HIGHEST) * ps_ref[...]
    cat = jnp.concatenate([at_ref[...], pool, mo_ref[...]], axis=1)
    mixv = _dot(cat, wo_ref[...], precision=HIGHEST)
    h1 = _layer_norm(alpha * x + mixv, g1_ref[...], b1_ref[...])
    _store_token_rows(h1_ref, h1)
    logits = _dot(h1, wr_ref[...], precision=HIGHEST) + br_ref[...]
    e0, e1, g0, g1 = _route(logits)
    meta, new_carry = _rank_and_meta(e0, e1, g0, g1, cin_ref[0:1, :])
    meta_ref[...] = meta
    cnt_ref[...] = jnp.broadcast_to(new_carry, cnt_ref.shape)


def _smix(x_tb, attn_tb, memo_tb, state, wu, wo, wp_bd, pool_scale, g1, b1, wr, br, counts_in, alpha, T):
    n, D = x_tb.shape
    pb, db, d_b = state.shape
    nchunk = D // LANES
    vm = pl.BlockSpec(memory_space=pltpu.VMEM)
    return pl.pallas_call(
        functools.partial(_smix_body, T, alpha, PAST_LEN - pb),
        in_specs=[vm] * 13,
        out_specs=[vm] * 4,
        out_shape=[jax.ShapeDtypeStruct((n * nchunk, LANES), F32), jax.ShapeDtypeStruct((n, LANES), F32),
                   jax.ShapeDtypeStruct((SUBLANES, LANES), F32), jax.ShapeDtypeStruct((pb, db, d_b), F32)],
        compiler_params=pltpu.CompilerParams(vmem_limit_bytes=VMEM_LIMIT),
        name="decode_mix_ln1_router",
    )(x_tb, attn_tb, memo_tb, state, wu, wo, wp_bd, pool_scale, g1, b1, wr, br, counts_in)


def _dispatch_body(ch, n_p, tmx, max_tiles, valid_ref, pos_ref, hp_ref, hs_ref, zero_ref, xs_ref, sem):
    g = pl.program_id(0)
    base = g * ch

    def pad_copies(i, fn):
        npad = tmx - valid_ref[i]
        off = i * tmx + valid_ref[i]
        bit = tmx
        while bit >= 1:
            take = (npad & bit) != 0

            @pl.when(take)
            def _(off=off, bit=bit):
                fn(pltpu.make_async_copy(zero_ref.at[pl.ds(0, bit)], xs_ref.at[pl.ds(off, bit)], sem))

            off = off + jnp.where(take, bit, 0)
            bit //= 2

    @pl.when(g == 0)
    def _():
        def fill(i, carry):
            pad_copies(i, lambda c: c.start())
            return carry

        lax.fori_loop(0, max_tiles, fill, 0)

        def drain(i, carry):
            pad_copies(i, lambda c: c.wait())
            return carry

        lax.fori_loop(0, max_tiles, drain, 0)

    def start(t, carry):
        tok = base + t
        for k in range(2):
            dst = xs_ref.at[pos_ref[0, 0, 2 * t + k]]

            @pl.when(tok < n_p)
            def _():
                pltpu.make_async_copy(hp_ref.at[tok], dst, sem).start()

            @pl.when(tok >= n_p)
            def _():
                pltpu.make_async_copy(hs_ref.at[tok - n_p], dst, sem).start()
        return carry

    lax.fori_loop(0, ch, start, 0)

    def wait(t, carry):
        for k in range(2):
            pltpu.make_async_copy(hp_ref.at[0], xs_ref.at[0], sem).wait()
        return carry

    lax.fori_loop(0, ch, wait, 0)


def _dispatch(tile_valid, pos, h1p, h1s, tmx, max_tiles, ch):
    n_p, n_s = h1p.shape[0], h1s.shape[0]
    n = n_p + n_s
    assert n % ch == 0 and tmx & (tmx - 1) == 0
    pos3 = pos.reshape(n // ch, 1, 2 * ch)
    zeros = jnp.zeros((tmx,) + h1p.shape[1:], F32)
    anyspace = pl.BlockSpec(memory_space=pl.ANY)
    grid_spec = pltpu.PrefetchScalarGridSpec(
        num_scalar_prefetch=1,
        grid=(n // ch,),
        in_specs=[pl.BlockSpec((1, 1, 2 * ch), lambda g, tv: (g, 0, 0), memory_space=pltpu.SMEM),
                  anyspace, anyspace, anyspace],
        out_specs=anyspace,
        scratch_shapes=[pltpu.SemaphoreType.DMA(())],
    )
    return pl.pallas_call(
        functools.partial(_dispatch_body, ch, n_p, tmx, max_tiles),
        grid_spec=grid_spec,
        out_shape=jax.ShapeDtypeStruct((max_tiles * tmx,) + h1p.shape[1:], F32),
        compiler_params=_cparams(("arbitrary",)),
        name="expert_dispatch",
    )(tile_valid, pos3, h1p, h1s, zeros)


def _expert_body(tmx, nchunk, te_ref, nt_ref, xs_ref, wg_ref, wu_ref, wd_ref, ys_ref):
    i = pl.program_id(0)

    @pl.when(i < nt_ref[0])
    def _():
        x = _load_token_rows(xs_ref, tmx, nchunk).astype(BF16)
        hg = _dot(x, wg_ref[0].astype(BF16))
        hu = _dot(x, wu_ref[0].astype(BF16))
        a = (hg * jax.nn.sigmoid(hg) * hu).astype(BF16)
        y = _dot(a, wd_ref[0].astype(BF16))
        _store_token_rows(ys_ref, y)

    @pl.when(i >= nt_ref[0])
    def _():
        ys_ref[...] = jnp.zeros_like(ys_ref)


def _experts(tile_expert, n_tiles, xs_flat, wg, wu, wd, tmx, max_tiles):
    E, D, F = wg.shape
    nchunk = D // LANES
    grid_spec = pltpu.PrefetchScalarGridSpec(
        num_scalar_prefetch=2,
        grid=(max_tiles,),
        in_specs=[pl.BlockSpec((tmx * nchunk, LANES), lambda i, te, nt: (jnp.minimum(i, nt[0] - 1), 0)),
                  pl.BlockSpec((1, D, F), lambda i, te, nt: (te[i], 0, 0)),
                  pl.BlockSpec((1, D, F), lambda i, te, nt: (te[i], 0, 0)),
                  pl.BlockSpec((1, F, D), lambda i, te, nt: (te[i], 0, 0))],
        out_specs=pl.BlockSpec((tmx * nchunk, LANES), lambda i, te, nt: (i, 0)),
    )
    return pl.pallas_call(
        functools.partial(_expert_body, tmx, nchunk),
        grid_spec=grid_spec,
        out_shape=jax.ShapeDtypeStruct(xs_flat.shape, F32),
        compiler_params=_cparams(("arbitrary",)),
        name="expert_swiglu",
    )(tile_expert, n_tiles, xs_flat, wg, wu, wd)


def _combine_body(tm, nchunk, alpha, pos_ref, h1_ref, meta_ref, ys_ref, g2_ref, b2_ref, o_ref, buf_ref, sem):
    def row_copy(src_row, k, t):
        dst = buf_ref.at[k, pl.ds(pl.multiple_of(t * nchunk, nchunk), nchunk)]
        return pltpu.make_async_copy(ys_ref.at[src_row], dst, sem)

    def start(t, carry):
        for k in range(2):
            row_copy(pos_ref[0, 0, 2 * t + k], k, t).start()
        return carry

    lax.fori_loop(0, tm, start, 0)

    def wait(t, carry):
        for k in range(2):
            row_copy(0, k, t).wait()
        return carry

    lax.fori_loop(0, tm, wait, 0)

    h1 = _load_token_rows(h1_ref, tm, nchunk)
    y0 = _load_token_rows(buf_ref.at[0], tm, nchunk)
    y1 = _load_token_rows(buf_ref.at[1], tm, nchunk)
    meta = meta_ref[...]
    lane = lax.broadcasted_iota(jnp.int32, meta.shape, 1)
    gate0 = jnp.sum(jnp.where(lane == 2, meta, 0.0), axis=1, keepdims=True)
    gate1 = jnp.sum(jnp.where(lane == 3, meta, 0.0), axis=1, keepdims=True)
    f = gate0 * y0 + gate1 * y1
    o_ref[...] = _layer_norm(alpha * h1 + f, g2_ref[...], b2_ref[...])


def _combine(pos, tok0, n, h1_flat, meta, ys, g2, b2, alpha, tm):
    D = g2.shape[1]
    nchunk = D // LANES
    assert n % tm == 0 and tok0 % tm == 0
    pos3 = pos.reshape(-1, 1, 2 * tm)
    off = tok0 // tm
    return pl.pallas_call(
        functools.partial(_combine_body, tm, nchunk, alpha),
        grid=(n // tm,),
        in_specs=[pl.BlockSpec((1, 1, 2 * tm), lambda i: (i + off, 0, 0), memory_space=pltpu.SMEM),
                  pl.BlockSpec((tm * nchunk, LANES), lambda i: (i, 0)),
                  pl.BlockSpec((tm, LANES), lambda i: (i, 0)),
                  pl.BlockSpec(memory_space=pl.ANY),
                  pl.BlockSpec((1, D), lambda i: (0, 0)), pl.BlockSpec((1, D), lambda i: (0, 0))],
        out_specs=pl.BlockSpec((tm, D), lambda i: (i, 0)),
        out_shape=jax.ShapeDtypeStruct((n, D), F32),
        scratch_shapes=[pltpu.VMEM((2, tm * nchunk, LANES), F32), pltpu.SemaphoreType.DMA(())],
        compiler_params=_cparams(("arbitrary",)),
        name="combine_ln2",
    )(pos3, h1_flat, meta, ys, g2, b2)


def _block_diag(w):
    g, a, b = w.shape
    eye = jnp.eye(g, dtype=w.dtype)
    return (eye[:, None, :, None] * w[:, :, None, :]).reshape(g * a, g * b)


def _layer(h_p, h_s, win_k, win_v, pool_st, mem_k, mem_v, mem_prompt,
           w_in, w_mem_kv, w_pool, pool_scale, w_o, ln1_g, ln1_b, ln2_g, ln2_b,
           w_r1, b_r1, w_r2, b_r2, w_gate, w_up, w_down, alpha):
    B, S, D = h_p.shape
    DB, T, _ = h_s.shape
    H = win_k.shape[2]
    Hc = mem_k.shape[2]
    d_a, d_c = H * HEAD_DIM, Hc * HEAD_DIM
    d_b = pool_st.shape[2]
    nchunk = D // LANES
    slopes = 2.0 ** (-8.0 * jnp.arange(1, H + 1, dtype=F32) / H)

    w_in_bf = w_in.astype(BF16)
    w_inT = w_in.T
    wkvT_bf = w_inT[d_a:3 * d_a].astype(BF16)
    wp_bd = _block_diag(w_pool)
    ps = pool_scale.reshape(1, d_b)
    g1, b1 = ln1_g.reshape(1, D), ln1_b.reshape(1, D)
    g2, b2 = ln2_g.reshape(1, D), ln2_b.reshape(1, D)
    n_r = N_EXPERT_GROUPS + N_EXPERTS
    wr = jnp.concatenate([w_r1, jnp.transpose(w_r2, (1, 0, 2)).reshape(D, N_EXPERTS),
                          jnp.zeros((D, LANES - n_r), F32)], axis=1)
    br = jnp.concatenate([b_r1, b_r2.reshape(-1), jnp.zeros((LANES - n_r,), F32)]).reshape(1, LANES)

    w_memT = w_mem_kv.T
    kmT, vm, vmT = _memproj(mem_prompt, w_memT[:d_c].astype(BF16), w_mem_kv[:, d_c:].astype(BF16),
                            w_memT[d_c:].astype(BF16))
    q, k, v, u, qc, kT, vT = _proj(h_p, w_in_bf, wkvT_bf, d_a, d_b, d_c, ts=512)
    attn = _attn(q, k, v, slopes)
    h1p, meta_p, cnt_p = _mix(h_p, attn, u, qc, kmT, vm, w_o.astype(BF16), wp_bd.astype(BF16), ps, g1, b1, wr, br,
                              alpha, tm=256)

    ck = jnp.transpose(win_k, (0, 2, 3, 1))
    cv = jnp.transpose(win_v, (0, 2, 3, 1))
    mk = jnp.transpose(mem_k, (0, 2, 3, 1))
    mv = jnp.transpose(mem_v, (0, 2, 3, 1))
    nk, nv, attn_sT, memo_sT = _decode(h_s, w_inT[:3 * d_a], w_inT[3 * d_a + d_b:], ck, cv, mk, mv, slopes)
    to_tb = lambda a: jnp.transpose(a, (2, 0, 1)).reshape(T * DB, a.shape[1])
    x_tb = jnp.transpose(h_s, (1, 0, 2)).reshape(T * DB, D)
    state = jnp.transpose(pool_st, (1, 0, 2))
    h1s, meta_s, cnt_all, new_pool = _smix(x_tb, to_tb(attn_sT), to_tb(memo_sT), state,
                                           w_in[:, 3 * d_a:3 * d_a + d_b], w_o, wp_bd, ps, g1, b1, wr, br,
                                           cnt_p, alpha, T)

    tmx = 256
    n_p, n_s = B * S, DB * T
    n = n_p + n_s
    counts = cnt_all[0, :N_EXPERTS].astype(jnp.int32)
    padded = (counts + tmx - 1) // tmx * tmx
    seg_end = jnp.cumsum(padded)
    seg_off = seg_end - padded
    meta = jnp.concatenate([meta_p, meta_s], axis=0)
    e_ids = meta[:, 0:2].astype(jnp.int32)
    pos = (seg_off[e_ids] + meta[:, 4:6].astype(jnp.int32)).reshape(-1)
    max_tiles = (2 * n) // tmx + N_EXPERTS
    n_tiles = (seg_end[-1] // tmx).astype(jnp.int32).reshape(1)
    tile_row0 = jnp.arange(max_tiles, dtype=jnp.int32) * tmx
    tile_expert = jnp.minimum(jnp.searchsorted(seg_end, tile_row0, side="right"), N_EXPERTS - 1).astype(jnp.int32)
    tile_valid = jnp.clip(seg_off[tile_expert] + counts[tile_expert] - tile_row0, 0, tmx).astype(jnp.int32)

    tok_tile = math.gcd(math.gcd(n_p, n_s), 128)
    xs = _dispatch(tile_valid, pos, h1p.reshape(n_p, nchunk, LANES), h1s.reshape(n_s, nchunk, LANES), tmx,
                   max_tiles, ch=tok_tile)
    ys = _experts(tile_expert, n_tiles, xs.reshape(-1, LANES), w_gate.reshape(N_EXPERTS, D, -1),
                  w_up.reshape(N_EXPERTS, D, -1), w_down.reshape(N_EXPERTS, -1, D), tmx, max_tiles)
    ys3 = ys.reshape(-1, nchunk, LANES)
    y_p = _combine(pos, 0, n_p, h1p, meta_p, ys3, g2, b2, alpha, tm=tok_tile)
    y_s = _combine(pos, n_p, n_s, h1s, meta_s, ys3, g2, b2, alpha, tm=tok_tile)

    y_p = y_p.reshape(B, S, D)
    y_s = jnp.transpose(y_s.reshape(T, DB, D), (1, 0, 2))
    heads = lambda a, h: jnp.transpose(a.reshape(a.shape[0], h, HEAD_DIM, a.shape[2]), (0, 3, 1, 2))
    wbp = min(max(w for w, _ in DILATED_PATTERNS), S)
    new_wk_p = heads(kT, H)[:, S - wbp:]
    new_wv_p = heads(vT, H)[:, S - wbp:]
    pb = pool_st.shape[1]
    new_pool_p = u[:, S - pb:]
    new_mk_p = heads(kmT, Hc)
    new_mv_p = heads(vmT, Hc)
    new_wk_s = jnp.transpose(nk, (0, 3, 1, 2))
    new_wv_s = jnp.transpose(nv, (0, 3, 1, 2))
    new_pool_s = jnp.transpose(new_pool, (1, 0, 2))
    return (y_p, y_s, new_wk_p, new_wv_p, new_pool_p, new_mk_p, new_mv_p, new_wk_s, new_wv_s, new_pool_s)


def kernel(x_prompt, x_sample, cache_win_k, cache_win_v, state_pool, cache_mem_k, cache_mem_v, mem_prompt, w_in, w_mem_kv, w_pool, pool_scale, w_o, ln1_g, ln1_b, ln2_g, ln2_b, w_r1, b_r1, w_r2, b_r2, w_gate, w_up, w_down):
    depth = w_in.shape[0]
    alpha = (2.0 * depth) ** 0.25
    h_p, h_s = x_prompt, x_sample
    outs = [[] for _ in range(8)]
    for l in range(depth):
        res = _layer(h_p, h_s, cache_win_k[l], cache_win_v[l], state_pool[l], cache_mem_k[l], cache_mem_v[l],
                     mem_prompt, w_in[l], w_mem_kv[l], w_pool[l], pool_scale[l], w_o[l], ln1_g[l], ln1_b[l],
                     ln2_g[l], ln2_b[l], w_r1[l], b_r1[l], w_r2[l], b_r2[l], w_gate[l], w_up[l], w_down[l], alpha)
        h_p, h_s = res[0], res[1]
        for lst, val in zip(outs, res[2:]):
            lst.append(val)
    return (h_p, h_s) + tuple(jnp.stack(o) for o in outs)
```

```python
import functools
import math

import jax
import jax.numpy as jnp
from jax import lax
from jax.experimental import pallas as pl
from jax.experimental.pallas import tpu as pltpu

F32 = jnp.float32
BF16 = jnp.bfloat16
HIGHEST = lax.Precision.HIGHEST
NEG_INF = float("-inf")

HEAD_DIM = 64
DILATED_PATTERNS = ((128, 1), (512, 4), (2048, 16))
BAND_BLOCK = 128
POOL_WINDOWS = (2, 4, 8, 16)
N_EXPERT_GROUPS = 4
EXPERTS_PER_GROUP = 8
N_EXPERTS = N_EXPERT_GROUPS * EXPERTS_PER_GROUP
PAST_LEN = 16384
LN_EPS = 1e-5
ATTN_SCALE = HEAD_DIM ** -0.5

LANES = 128
SUBLANES = 8
VMEM_LIMIT = 56 * 1024 * 1024

ROUTER_OFF = N_EXPERT_GROUPS
ATTN_UNROLL = 4
DMA_UNROLL = 8


def _cparams(sem):
    return pltpu.CompilerParams(dimension_semantics=sem, vmem_limit_bytes=VMEM_LIMIT)


def _dot(a, b, precision=None):
    return jnp.dot(a, b, preferred_element_type=F32, precision=precision)


def _dot_nt(a, b, precision=None):
    return lax.dot_general(a, b, (((1,), (1,)), ((), ())), preferred_element_type=F32, precision=precision)


def _memproj_body(mem_ref, wkT_ref, wv_ref, wvT_ref, kmT_ref, vm_ref, vmT_ref):
    m = mem_ref[0].astype(BF16)
    kmT_ref[0] = _dot_nt(wkT_ref[...], m)
    vm_ref[0] = _dot(m, wv_ref[...])
    vmT_ref[0] = _dot_nt(wvT_ref[...], m)


def _memproj(mem, wkT, wv, wvT):
    B, M, D = mem.shape
    C = wv.shape[1]
    full = lambda shape: pl.BlockSpec(shape, lambda b: (0,) * len(shape))
    return pl.pallas_call(
        _memproj_body,
        grid=(B,),
        in_specs=[pl.BlockSpec((1, M, D), lambda b: (b, 0, 0)), full((C, D)), full((D, C)), full((C, D))],
        out_specs=[pl.BlockSpec((1, C, M), lambda b: (b, 0, 0)),
                   pl.BlockSpec((1, M, C), lambda b: (b, 0, 0)),
                   pl.BlockSpec((1, C, M), lambda b: (b, 0, 0))],
        out_shape=[jax.ShapeDtypeStruct((B, C, M), F32), jax.ShapeDtypeStruct((B, M, C), F32),
                   jax.ShapeDtypeStruct((B, C, M), F32)],
        compiler_params=_cparams(("arbitrary",)),
        name="memproj",
    )(mem, wkT, wv, wvT)


def _proj_body(d_a, d_b, x_ref, w_ref, wkvT_ref, q_ref, k_ref, v_ref, u_ref, qc_ref, kT_ref, vT_ref):
    xb = x_ref[0].astype(BF16)
    y = _dot(xb, w_ref[...])
    q_ref[0] = y[:, 0:d_a]
    k_ref[0] = y[:, d_a:2 * d_a]
    v_ref[0] = y[:, 2 * d_a:3 * d_a]
    u_ref[0] = y[:, 3 * d_a:3 * d_a + d_b]
    qc_ref[0] = y[:, 3 * d_a + d_b:]
    yT = _dot_nt(wkvT_ref[...], xb)
    kT_ref[0] = yT[0:d_a]
    vT_ref[0] = yT[d_a:]


def _proj(x, w_bf, wkvT_bf, d_a, d_b, d_c, ts):
    B, S, D = x.shape
    d_in = w_bf.shape[1]
    row = lambda n: pl.BlockSpec((1, ts, n), lambda b, i: (b, i, 0))
    col = lambda n: pl.BlockSpec((1, n, ts), lambda b, i: (b, 0, i))
    return pl.pallas_call(
        functools.partial(_proj_body, d_a, d_b),
        grid=(B, S // ts),
        in_specs=[row(D), pl.BlockSpec((D, d_in), lambda b, i: (0, 0)),
                  pl.BlockSpec((2 * d_a, D), lambda b, i: (0, 0))],
        out_specs=[row(d_a), row(d_a), row(d_a), row(d_b), row(d_c), col(d_a), col(d_a)],
        out_shape=[jax.ShapeDtypeStruct((B, S, d_a), F32)] * 3
        + [jax.ShapeDtypeStruct((B, S, d_b), F32), jax.ShapeDtypeStruct((B, S, d_c), F32)]
        + [jax.ShapeDtypeStruct((B, d_a, S), F32)] * 2,
        compiler_params=_cparams(("arbitrary", "arbitrary")),
        name="proj",
    )(x, w_bf, wkvT_bf)


def _attn_body(S, slopes_ref, q_ref, k_ref, v_ref, o_ref, opat_ref, lpat_ref):
    hp = pl.program_id(1)
    blk = BAND_BLOCK
    lane = lax.broadcasted_iota(jnp.int32, (blk, LANES), 1)
    head0 = lane < HEAD_DIM

    for pi, (window, dil) in enumerate(DILATED_PATTERNS):
        n_steps = window // dil
        L = S // dil
        nblk = L // blk
        has_prev = nblk > 1
        nk = 2 * blk if has_prev else blk
        qi = lax.broadcasted_iota(jnp.int32, (blk, nk), 0)
        kj = lax.broadcasted_iota(jnp.int32, (blk, nk), 1)
        steps = qi + (blk if has_prev else 0) - kj
        valid = (steps >= 0) & (steps <= n_steps)
        dist = (steps * dil).astype(F32)
        biases = []
        for hh in range(2):
            slope = slopes_ref[2 * hp + hh]
            biases.append(jnp.where(valid, -slope * dist, NEG_INF))
        prev_cols = kj < blk

        def block(idx, carry, dil=dil, nblk=nblk, has_prev=has_prev, biases=biases, prev_cols=prev_cols, pi=pi):
            r = idx // nblk
            j = idx % nblk
            start = r + blk * dil * j
            rows = lambda s0: pl.ds(s0, blk, stride=dil) if dil > 1 else pl.ds(pl.multiple_of(s0, blk), blk)
            qb = q_ref[0, rows(start), :] * ATTN_SCALE
            kc = k_ref[0, rows(start), :]
            vc = v_ref[0, rows(start), :]
            if has_prev:
                pstart = jnp.maximum(start - blk * dil, r)
                kb = jnp.concatenate([k_ref[0, rows(pstart), :], kc], axis=0)
                vb = jnp.concatenate([v_ref[0, rows(pstart), :], vc], axis=0)
            else:
                kb, vb = kc, vc
            kb = kb.astype(BF16)
            vb = vb.astype(BF16)
            outs, lses = [], []
            for hh in range(2):
                qm = jnp.where(head0 if hh == 0 else ~head0, qb, 0.0).astype(BF16)
                s = _dot_nt(qm, kb) + biases[hh]
                if has_prev:
                    s = jnp.where(prev_cols & (j == 0), NEG_INF, s)
                m = jnp.max(s, axis=1, keepdims=True)
                p = jnp.exp(s - m)
                den = jnp.sum(p, axis=1, keepdims=True)
                o = _dot(p.astype(BF16), vb) / den
                outs.append(o)
                lses.append(m + jnp.log(den))
            o_pair = jnp.where(head0, outs[0], outs[1])
            l_pair = jnp.where(head0, lses[0], lses[1])
            opat_ref[pi, rows(start), :] = o_pair
            lpat_ref[pi, rows(start), :] = l_pair
            return carry

        lax.fori_loop(0, dil * nblk, block, 0, unroll=ATTN_UNROLL)

    chunk = 256

    def mix(c, carry):
        rows = pl.ds(pl.multiple_of(c * chunk, chunk), chunk)
        l0, l1, l2 = lpat_ref[0, rows, :], lpat_ref[1, rows, :], lpat_ref[2, rows, :]
        mx = jnp.maximum(jnp.maximum(l0, l1), l2)
        w0, w1, w2 = jnp.exp(l0 - mx), jnp.exp(l1 - mx), jnp.exp(l2 - mx)
        num = w0 * opat_ref[0, rows, :] + w1 * opat_ref[1, rows, :] + w2 * opat_ref[2, rows, :]
        o_ref[0, rows, :] = num / (w0 + w1 + w2)
        return carry

    lax.fori_loop(0, S // chunk, mix, 0)


def _attn(q, k, v, slopes):
    B, S, d_a = q.shape
    assert S % (BAND_BLOCK * max(d for _, d in DILATED_PATTERNS)) == 0
    npair = d_a // LANES
    spec = pl.BlockSpec((1, S, LANES), lambda b, h, *_: (b, 0, h))
    grid_spec = pltpu.PrefetchScalarGridSpec(
        num_scalar_prefetch=0,
        grid=(B, npair),
        in_specs=[pl.BlockSpec(memory_space=pltpu.SMEM), spec, spec, spec],
        out_specs=spec,
        scratch_shapes=[pltpu.VMEM((len(DILATED_PATTERNS), S, LANES), F32),
                        pltpu.VMEM((len(DILATED_PATTERNS), S, LANES), F32)],
    )
    return pl.pallas_call(
        functools.partial(_attn_body, S),
        grid_spec=grid_spec,
        out_shape=jax.ShapeDtypeStruct((B, S, d_a), F32),
        compiler_params=_cparams(("arbitrary", "arbitrary")),
        name="dilated_attn",
    )(slopes, q, k, v)


def _layer_norm(z, g, b):
    mu = jnp.mean(z, axis=-1, keepdims=True)
    zc = z - mu
    var = jnp.mean(zc * zc, axis=-1, keepdims=True)
    return zc * lax.rsqrt(var + LN_EPS) * g + b


def _route(logits):
    n = logits.shape[0]
    lane = lax.broadcasted_iota(jnp.int32, (n, LANES), 1)
    lane_f = lane.astype(F32)
    big = float(LANES)
    is_outer = lane < N_EXPERT_GROUPS
    l1 = jnp.where(is_outer, logits, NEG_INF)
    m1 = jnp.max(l1, axis=1, keepdims=True)
    g_sel = jnp.min(jnp.where(l1 == m1, lane_f, big), axis=1, keepdims=True)
    v1 = 1.0 / jnp.sum(jnp.exp(l1 - m1), axis=1, keepdims=True)
    lo = ROUTER_OFF + g_sel * EXPERTS_PER_GROUP
    in_group = (lane_f >= lo) & (lane_f < lo + EXPERTS_PER_GROUP)
    l2 = jnp.where(in_group, logits, NEG_INF)
    ma = jnp.max(l2, axis=1, keepdims=True)
    ia = jnp.min(jnp.where(l2 == ma, lane_f, big), axis=1, keepdims=True)
    l2b = jnp.where(lane_f == ia, NEG_INF, l2)
    mb = jnp.max(l2b, axis=1, keepdims=True)
    ib = jnp.min(jnp.where(l2b == mb, lane_f, big), axis=1, keepdims=True)
    eb = jnp.exp(mb - ma)
    wa = 1.0 / (1.0 + eb)
    wb = eb / (1.0 + eb)
    return ia - ROUTER_OFF, ib - ROUTER_OFF, v1 * wa, v1 * wb


def _rank_and_meta(e0, e1, g0, g1, carry):
    n = e0.shape[0]
    lane_f = lax.broadcasted_iota(jnp.int32, (n, LANES), 1).astype(F32)
    oh0 = (lane_f == e0).astype(F32)
    oh1 = (lane_f == e1).astype(F32)
    both = oh0 + oh1
    ti = lax.broadcasted_iota(jnp.int32, (n, n), 0)
    tj = lax.broadcasted_iota(jnp.int32, (n, n), 1)
    tri = (tj < ti).astype(BF16)
    before = _dot(tri, both.astype(BF16)) + carry
    r0 = jnp.sum(before * oh0, axis=1, keepdims=True)
    r1 = jnp.sum(before * oh1, axis=1, keepdims=True)
    new_carry = carry + jnp.sum(both, axis=0, keepdims=True)
    lane = lax.broadcasted_iota(jnp.int32, (n, LANES), 1)
    meta = jnp.zeros((n, LANES), F32)
    for i, val in enumerate((e0, e1, g0, g1, r0, r1)):
        meta = jnp.where(lane == i, val, meta)
    return meta, new_carry


def _lane_group(shape, width):
    lane = lax.broadcasted_iota(jnp.int32, shape, len(shape) - 1)
    grp = jnp.zeros(shape, jnp.int32)
    for g in range(1, shape[-1] // width):
        grp = grp + (lane >= g * width).astype(jnp.int32)
    return grp


def _select_by_group(grp, vals):
    out = vals[-1]
    for g in range(len(vals) - 2, -1, -1):
        out = jnp.where(grp == g, vals[g], out)
    return out


def _store_token_rows(ref, val):
    n, d = val.shape
    nchunk = d // LANES
    for c in range(nchunk):
        ref[pl.ds(c, n, stride=nchunk), :] = val[:, c * LANES:(c + 1) * LANES]


def _load_token_rows(ref, n, nchunk):
    return jnp.concatenate([ref[pl.ds(c, n, stride=nchunk), :] for c in range(nchunk)], axis=1)


HALO = 16


def _mix_body(tm, d_b, alpha, x_ref, at_ref, u_ref, uh_ref, qc_ref, kmT_ref, vm_ref, wo_ref, wp_ref, ps_ref,
              g1_ref, b1_ref, wr_ref, br_ref, h1_ref, meta_ref, cnt_ref, carry_ref):
    i = pl.program_id(1)

    @pl.when((pl.program_id(0) == 0) & (i == 0))
    def _():
        carry_ref[...] = jnp.zeros_like(carry_ref)

    u = u_ref[0]
    halo = jnp.where(i == 0, 0.0, uh_ref[0])
    ext = jnp.concatenate([halo, halo, u], axis=0)
    s2 = ext + pltpu.roll(ext, 1, 0)
    s4 = s2 + pltpu.roll(s2, 2, 0)
    s8 = s4 + pltpu.roll(s4, 4, 0)
    s16 = s8 + pltpu.roll(s8, 8, 0)
    grp = _lane_group((tm, d_b), d_b // len(POOL_WINDOWS))
    sums = [s[2 * HALO:] for s in (s2, s4, s8, s16)]
    win = _select_by_group(grp, sums)
    wlen = _select_by_group(grp, [jnp.int32(w) for w in POOL_WINDOWS])
    pos = i * tm + lax.broadcasted_iota(jnp.int32, (tm, d_b), 0)
    cnt = jnp.minimum(wlen, pos + 1).astype(F32)
    diff = win / cnt - u
    pool = _dot(diff.astype(BF16), wp_ref[...]) * ps_ref[...]

    qc = qc_ref[0] * ATTN_SCALE
    d_c = qc.shape[1]
    kmT = kmT_ref[0].astype(BF16)
    vm = vm_ref[0].astype(BF16)
    hl = _lane_group((tm, d_c), HEAD_DIM)
    memo = jnp.zeros((tm, d_c), F32)
    for h in range(d_c // HEAD_DIM):
        qm = jnp.where(hl == h, qc, 0.0).astype(BF16)
        s = _dot(qm, kmT)
        p = jnp.exp(s - jnp.max(s, axis=1, keepdims=True))
        den = jnp.sum(p, axis=1, keepdims=True)
        o = _dot(p.astype(BF16), vm) / den
        memo = jnp.where(hl == h, o, memo)

    cat = jnp.concatenate([at_ref[0].astype(BF16), pool.astype(BF16), memo.astype(BF16)], axis=1)
    mixv = _dot(cat, wo_ref[...])
    h1 = _layer_norm(alpha * x_ref[0] + mixv, g1_ref[...], b1_ref[...])
    _store_token_rows(h1_ref, h1)

    logits = _dot(h1, wr_ref[...], precision=HIGHEST) + br_ref[...]
    e0, e1, g0, g1 = _route(logits)
    meta, new_carry = _rank_and_meta(e0, e1, g0, g1, carry_ref[0:1, :])
    meta_ref[...] = meta
    carry_ref[...] = jnp.broadcast_to(new_carry, carry_ref.shape)
    cnt_ref[...] = carry_ref[...]


def _mix(x, attn, u, qc, kmT, vm, wo_bf, wp_bd_bf, pool_scale, g1, b1, wr, br, alpha, tm):
    B, S, D = x.shape
    d_a, d_b, d_c = attn.shape[2], u.shape[2], qc.shape[2]
    n_mem = vm.shape[1]
    nt = S // tm
    row = lambda n: pl.BlockSpec((1, tm, n), lambda b, i: (b, i, 0))
    full = lambda shape: pl.BlockSpec(shape, lambda b, i: (0,) * len(shape))
    per_b = lambda shape: pl.BlockSpec((1,) + shape, lambda b, i: (b, 0, 0))
    halo_spec = pl.BlockSpec((1, HALO, d_b), lambda b, i: (b, jnp.maximum(i * (tm // HALO) - 1, 0), 0))
    nchunk = D // LANES
    return pl.pallas_call(
        functools.partial(_mix_body, tm, d_b, alpha),
        grid=(B, nt),
        in_specs=[row(D), row(d_a), row(d_b), halo_spec, row(d_c), per_b((d_c, n_mem)), per_b((n_mem, d_c)),
                  full((D, D)), full((d_b, d_b)), full((1, d_b)), full((1, D)), full((1, D)),
                  full((D, LANES)), full((1, LANES))],
        out_specs=[pl.BlockSpec((tm * nchunk, LANES), lambda b, i: (b * nt + i, 0)),
                   pl.BlockSpec((tm, LANES), lambda b, i: (b * nt + i, 0)),
                   pl.BlockSpec((SUBLANES, LANES), lambda b, i: (0, 0))],
        out_shape=[jax.ShapeDtypeStruct((B * S * nchunk, LANES), F32),
                   jax.ShapeDtypeStruct((B * S, LANES), F32),
                   jax.ShapeDtypeStruct((SUBLANES, LANES), F32)],
        scratch_shapes=[pltpu.VMEM((SUBLANES, LANES), F32)],
        compiler_params=_cparams(("arbitrary", "arbitrary")),
        name="mix_ln1_router",
    )(x, attn, u, u, qc, kmT, vm, wo_bf, wp_bd_bf, pool_scale, g1, b1, wr, br)


def _columns_to_lanes(cols, rows):
    lane = lax.broadcasted_iota(jnp.int32, (rows, LANES), 1)
    tile = jnp.zeros((rows, LANES), F32)
    for t, c in enumerate(cols):
        tile = jnp.where(lane == t, c, tile)
    return tile


def _sproj_body(x_ref, wT_ref, o_ref):
    o_ref[...] = _dot_nt(wT_ref[...], x_ref[...], precision=HIGHEST)


def _sproj(x, wT):
    vm = pl.BlockSpec(memory_space=pltpu.VMEM)
    return pl.pallas_call(
        _sproj_body, in_specs=[vm, vm], out_specs=vm,
        out_shape=jax.ShapeDtypeStruct((wT.shape[0], x.shape[0]), F32),
        compiler_params=pltpu.CompilerParams(vmem_limit_bytes=VMEM_LIMIT),
        name="decode_proj",
    )(x, wT)


def _decode_body(T, wb, slopes_ref, pT_ref, ck_ref, cv_ref, mk_ref, mv_ref,
                 nk_ref, nv_ref, at_ref, mo_ref, qkv_ref):
    H = ck_ref.shape[1]
    d_a = H * HEAD_DIM
    ntok = pT_ref.shape[1]
    shift = (ntok - T * pl.program_id(0)) % ntok
    qkv_ref[...] = pltpu.roll(pT_ref[...], shift, 1)

    jpos = lax.broadcasted_iota(jnp.int32, (1, wb), 1)
    tnew = lax.broadcasted_iota(jnp.int32, (1, T), 1)
    mult_c, dist_c, mult_n, dist_n = [], [], [], []
    for t in range(T):
        dc = wb + t - jpos
        dn = t - tnew
        mc = jnp.zeros((1, wb), F32)
        mn = jnp.zeros((1, T), F32)
        for window, dil in DILATED_PATTERNS:
            assert dil & (dil - 1) == 0
            mc = mc + (((dc & (dil - 1)) == 0) & (dc <= window) & (dc >= 0)).astype(F32)
            mn = mn + (((dn & (dil - 1)) == 0) & (dn <= window) & (dn >= 0)).astype(F32)
        mult_c.append(mc)
        mult_n.append(mn)
        dist_c.append(dc.astype(F32))
        dist_n.append(jnp.maximum(dn, 0).astype(F32))

    lane_t = lax.broadcasted_iota(jnp.int32, (HEAD_DIM, LANES), 1)
    last = wb - LANES

    def head(h, carry):
        slope = slopes_ref[h]
        kTc = ck_ref[0, h]
        vTc = cv_ref[0, h]
        r0 = pl.multiple_of(h * HEAD_DIM, HEAD_DIM)
        qT = qkv_ref[pl.ds(r0, HEAD_DIM), :][:, 0:T]
        kTn = qkv_ref[pl.ds(d_a + r0, HEAD_DIM), :][:, 0:T]
        vTn = qkv_ref[pl.ds(2 * d_a + r0, HEAD_DIM), :][:, 0:T]
        cols = []
        for t in range(T):
            qcol = qT[:, t:t + 1] * ATTN_SCALE
            sc = jnp.sum(kTc * qcol, axis=0, keepdims=True) - slope * dist_c[t]
            sn = jnp.sum(kTn * qcol, axis=0, keepdims=True) - slope * dist_n[t]
            sc = jnp.where(mult_c[t] > 0, sc, NEG_INF)
            sn = jnp.where(mult_n[t] > 0, sn, NEG_INF)
            m = jnp.maximum(jnp.max(sc, axis=1, keepdims=True), jnp.max(sn, axis=1, keepdims=True))
            pc = mult_c[t] * jnp.exp(sc - m)
            pn = mult_n[t] * jnp.exp(sn - m)
            den = jnp.sum(pc, axis=1, keepdims=True) + jnp.sum(pn, axis=1, keepdims=True)
            o = jnp.sum(vTc * pc, axis=1, keepdims=True) + jnp.sum(vTn * pn, axis=1, keepdims=True)
            cols.append(o / den)
        at_ref[0, pl.ds(r0, HEAD_DIM), :] = _columns_to_lanes(cols, HEAD_DIM)

        rk = pltpu.roll(kTc, wb - T, 1)
        rv = pltpu.roll(vTc, wb - T, 1)
        nk_ref[0, h] = rk
        nv_ref[0, h] = rv
        newk = jnp.zeros((HEAD_DIM, LANES), F32)
        newv = jnp.zeros((HEAD_DIM, LANES), F32)
        for t in range(T):
            newk = jnp.where(lane_t == LANES - T + t, kTn[:, t:t + 1], newk)
            newv = jnp.where(lane_t == LANES - T + t, vTn[:, t:t + 1], newv)
        nk_ref[0, h, :, last:] = jnp.where(lane_t >= LANES - T, newk, rk[:, last:])
        nv_ref[0, h, :, last:] = jnp.where(lane_t >= LANES - T, newv, rv[:, last:])
        return carry

    lax.fori_loop(0, H, head, 0, unroll=2)

    Hc = mk_ref.shape[1]
    for h in range(Hc):
        kT = mk_ref[0, h]
        vT = mv_ref[0, h]
        r0 = h * HEAD_DIM
        qT = qkv_ref[3 * d_a + r0:3 * d_a + r0 + HEAD_DIM, :][:, 0:T]
        cols = []
        for t in range(T):
            qcol = qT[:, t:t + 1] * ATTN_SCALE
            s = jnp.sum(kT * qcol, axis=0, keepdims=True)
            p = jnp.exp(s - jnp.max(s, axis=1, keepdims=True))
            den = jnp.sum(p, axis=1, keepdims=True)
            cols.append(jnp.sum(vT * p, axis=1, keepdims=True) / den)
        mo_ref[0, r0:r0 + HEAD_DIM, :] = _columns_to_lanes(cols, HEAD_DIM)


def _decode(projT, T, ck, cv, mk, mv, slopes):
    DB, H, hd, wb = ck.shape
    _, Hc, _, n_mem = mk.shape
    d_a, d_c = H * hd, Hc * hd
    assert wb >= max(w for w, _ in DILATED_PATTERNS) and wb % LANES == 0 and T <= LANES
    assert projT.shape == (3 * d_a + d_c, DB * T)
    cache = pl.BlockSpec((1, H, hd, wb), lambda b, *_: (b, 0, 0, 0))
    memc = pl.BlockSpec((1, Hc, hd, n_mem), lambda b, *_: (b, 0, 0, 0))
    grid_spec = pltpu.PrefetchScalarGridSpec(
        num_scalar_prefetch=0,
        grid=(DB,),
        in_specs=[pl.BlockSpec(memory_space=pltpu.SMEM),
                  pl.BlockSpec(projT.shape, lambda b, *_: (0, 0)),
                  cache, cache, memc, memc],
        out_specs=[cache, cache,
                   pl.BlockSpec((1, d_a, LANES), lambda b, *_: (b, 0, 0)),
                   pl.BlockSpec((1, d_c, LANES), lambda b, *_: (b, 0, 0))],
        scratch_shapes=[pltpu.VMEM(projT.shape, F32)],
    )
    nk, nv, at, mo = pl.pallas_call(
        functools.partial(_decode_body, T, wb),
        grid_spec=grid_spec,
        out_shape=[jax.ShapeDtypeStruct(ck.shape, F32), jax.ShapeDtypeStruct(cv.shape, F32),
                   jax.ShapeDtypeStruct((DB, d_a, LANES), F32), jax.ShapeDtypeStruct((DB, d_c, LANES), F32)],
        compiler_params=_cparams(("arbitrary",)),
        name="decode_attn_cache",
    )(slopes, projT, ck, cv, mk, mv)
    return nk, nv, at[:, :, :T], mo[:, :, :T]


def _smix_body(T, alpha, pos0, x_ref, at_ref, mo_ref, st_ref, wu_ref, wo_ref, wp_ref, ps_ref, g1_ref, b1_ref,
               wr_ref, br_ref, cin_ref, h1_ref, meta_ref, cnt_ref, pool_ref):
    n = x_ref.shape[0]
    db = n // T
    pb = st_ref.shape[0]
    d_b = st_ref.shape[2]
    x = x_ref[...]
    u_new = _dot(x, wu_ref[...], precision=HIGHEST)
    seq = [st_ref[j] for j in range(pb)] + [u_new[t * db:(t + 1) * db] for t in range(T)]
    for j in range(pb):
        pool_ref[j] = seq[j + T]
    grp = _lane_group((db, d_b), d_b // len(POOL_WINDOWS))
    diffs = []
    for t in range(T):
        j = pb + t
        per_w = []
        for w in POOL_WINDOWS:
            acc = seq[j]
            for back in range(1, w):
                if j - back >= 0:
                    acc = acc + seq[j - back]
            per_w.append(acc / float(min(w, pos0 + j + 1)))
        diffs.append(_select_by_group(grp, per_w) - seq[j])
    diff = jnp.concatenate(diffs, axis=0)
    pool = _dot(diff, wp_ref[...], precision=HIGHEST) * ps_ref[...]
    cat = jnp.concatenate([at_ref[...], pool, mo_ref[...]], axis=1)
    mixv = _dot(cat, wo_ref[...], precision=HIGHEST)
    h1 = _layer_norm(alpha * x + mixv, g1_ref[...], b1_ref[...])
    _store_token_rows(h1_ref, h1)
    logits = _dot(h1, wr_ref[...], precision=HIGHEST) + br_ref[...]
    e0, e1, g0, g1 = _route(logits)
    meta, new_carry = _rank_and_meta(e0, e1, g0, g1, cin_ref[0:1, :])
    meta_ref[...] = meta
    cnt_ref[...] = jnp.broadcast_to(new_carry, cnt_ref.shape)


def _smix(x_tb, attn_tb, memo_tb, state, wu, wo, wp_bd, pool_scale, g1, b1, wr, br, counts_in, alpha, T):
    n, D = x_tb.shape
    pb, db, d_b = state.shape
    nchunk = D // LANES
    vm = pl.BlockSpec(memory_space=pltpu.VMEM)
    return pl.pallas_call(
        functools.partial(_smix_body, T, alpha, PAST_LEN - pb),
        in_specs=[vm] * 13,
        out_specs=[vm] * 4,
        out_shape=[jax.ShapeDtypeStruct((n * nchunk, LANES), F32), jax.ShapeDtypeStruct((n, LANES), F32),
                   jax.ShapeDtypeStruct((SUBLANES, LANES), F32), jax.ShapeDtypeStruct((pb, db, d_b), F32)],
        compiler_params=pltpu.CompilerParams(vmem_limit_bytes=VMEM_LIMIT),
        name="decode_mix_ln1_router",
    )(x_tb, attn_tb, memo_tb, state, wu, wo, wp_bd, pool_scale, g1, b1, wr, br, counts_in)


def _dispatch_body(ch, n_s, nchunk, tmx, max_tiles, valid_ref, pos_ref, spos_ref, h_ref, hs_ref, zero_ref,
                   xs_ref, sem):
    g = pl.program_id(0)

    def pad_copies(i, fn):
        npad = tmx - valid_ref[i]
        off = i * tmx + valid_ref[i]
        bit = tmx
        while bit >= 1:
            take = (npad & bit) != 0

            @pl.when(take)
            def _(off=off, bit=bit):
                fn(pltpu.make_async_copy(zero_ref.at[pl.ds(0, bit)], xs_ref.at[pl.ds(off, bit)], sem))

            off = off + jnp.where(take, bit, 0)
            bit //= 2

    def scatter(src_ref, slots_ref, count):
        def row_copy(t, slot):
            src = src_ref.at[pl.ds(pl.multiple_of(t * nchunk, nchunk), nchunk)]
            return pltpu.make_async_copy(src, xs_ref.at[slot], sem)

        def start(t, carry):
            for k in range(2):
                row_copy(t, slots_ref[0, 0, 2 * t + k]).start()
            return carry

        lax.fori_loop(0, count, start, 0, unroll=DMA_UNROLL)

        def wait(t, carry):
            for k in range(2):
                row_copy(t, 0).wait()
            return carry

        lax.fori_loop(0, count, wait, 0, unroll=DMA_UNROLL)

    @pl.when(g == 0)
    def _():
        def fill(i, carry):
            pad_copies(i, lambda c: c.start())
            return carry

        lax.fori_loop(0, max_tiles, fill, 0)

        def drain(i, carry):
            pad_copies(i, lambda c: c.wait())
            return carry

        lax.fori_loop(0, max_tiles, drain, 0)
        scatter(hs_ref, spos_ref, n_s)

    scatter(h_ref, pos_ref, ch)


def _dispatch(tile_valid, pos_p, pos_s, h1p, h1s, tmx, max_tiles, nchunk, ch):
    n_p, n_s = h1p.shape[0] // nchunk, h1s.shape[0] // nchunk
    assert n_p % ch == 0 and n_s % DMA_UNROLL == 0 and ch % DMA_UNROLL == 0 and tmx & (tmx - 1) == 0
    zeros = jnp.zeros((tmx, nchunk, LANES), F32)
    grid_spec = pltpu.PrefetchScalarGridSpec(
        num_scalar_prefetch=1,
        grid=(n_p // ch,),
        in_specs=[pl.BlockSpec((1, 1, 2 * ch), lambda g, tv: (g, 0, 0), memory_space=pltpu.SMEM),
                  pl.BlockSpec((1, 1, 2 * n_s), lambda g, tv: (0, 0, 0), memory_space=pltpu.SMEM),
                  pl.BlockSpec((ch * nchunk, LANES), lambda g, tv: (g, 0)),
                  pl.BlockSpec((n_s * nchunk, LANES), lambda g, tv: (0, 0)),
                  pl.BlockSpec(memory_space=pl.ANY)],
        out_specs=pl.BlockSpec(memory_space=pl.ANY),
        scratch_shapes=[pltpu.SemaphoreType.DMA(())],
    )
    return pl.pallas_call(
        functools.partial(_dispatch_body, ch, n_s, nchunk, tmx, max_tiles),
        grid_spec=grid_spec,
        out_shape=jax.ShapeDtypeStruct((max_tiles * tmx, nchunk, LANES), F32),
        compiler_params=_cparams(("arbitrary",)),
        name="expert_dispatch",
    )(tile_valid, pos_p.reshape(n_p // ch, 1, 2 * ch), pos_s.reshape(1, 1, 2 * n_s), h1p, h1s, zeros)


def _expert_body(tmx, nchunk, te_ref, nt_ref, xs_ref, wg_ref, wu_ref, wd_ref, ys_ref):
    i = pl.program_id(0)

    @pl.when(i < nt_ref[0])
    def _():
        x = _load_token_rows(xs_ref, tmx, nchunk).astype(BF16)
        hg = _dot(x, wg_ref[0].astype(BF16))
        hu = _dot(x, wu_ref[0].astype(BF16))
        a = (hg * jax.nn.sigmoid(hg) * hu).astype(BF16)
        y = _dot(a, wd_ref[0].astype(BF16))
        _store_token_rows(ys_ref, y)

    @pl.when(i >= nt_ref[0])
    def _():
        ys_ref[...] = jnp.zeros_like(ys_ref)


def _experts(tile_expert, n_tiles, xs_flat, wg, wu, wd, tmx, max_tiles):
    E, D, F = wg.shape
    nchunk = D // LANES
    grid_spec = pltpu.PrefetchScalarGridSpec(
        num_scalar_prefetch=2,
        grid=(max_tiles,),
        in_specs=[pl.BlockSpec((tmx * nchunk, LANES), lambda i, te, nt: (jnp.minimum(i, nt[0] - 1), 0)),
                  pl.BlockSpec((1, D, F), lambda i, te, nt: (te[i], 0, 0)),
                  pl.BlockSpec((1, D, F), lambda i, te, nt: (te[i], 0, 0)),
                  pl.BlockSpec((1, F, D), lambda i, te, nt: (te[i], 0, 0))],
        out_specs=pl.BlockSpec((tmx * nchunk, LANES), lambda i, te, nt: (i, 0)),
    )
    return pl.pallas_call(
        functools.partial(_expert_body, tmx, nchunk),
        grid_spec=grid_spec,
        out_shape=jax.ShapeDtypeStruct(xs_flat.shape, F32),
        compiler_params=_cparams(("arbitrary",)),
        name="expert_swiglu",
    )(tile_expert, n_tiles, xs_flat, wg, wu, wd)


def _combine_body(tm, nchunk, alpha, pos_ref, h1_ref, meta_ref, ys_ref, g2_ref, b2_ref, o_ref, buf_ref, sem):
    def row_copy(src_row, k, t):
        dst = buf_ref.at[k, pl.ds(pl.multiple_of(t * nchunk, nchunk), nchunk)]
        return pltpu.make_async_copy(ys_ref.at[src_row], dst, sem)

    def start(t, carry):
        for k in range(2):
            row_copy(pos_ref[0, 0, 2 * t + k], k, t).start()
        return carry

    lax.fori_loop(0, tm, start, 0)

    def wait(t, carry):
        for k in range(2):
            row_copy(0, k, t).wait()
        return carry

    lax.fori_loop(0, tm, wait, 0)

    h1 = _load_token_rows(h1_ref, tm, nchunk)
    y0 = _load_token_rows(buf_ref.at[0], tm, nchunk)
    y1 = _load_token_rows(buf_ref.at[1], tm, nchunk)
    meta = meta_ref[...]
    lane = lax.broadcasted_iota(jnp.int32, meta.shape, 1)
    gate0 = jnp.sum(jnp.where(lane == 2, meta, 0.0), axis=1, keepdims=True)
    gate1 = jnp.sum(jnp.where(lane == 3, meta, 0.0), axis=1, keepdims=True)
    f = gate0 * y0 + gate1 * y1
    o_ref[...] = _layer_norm(alpha * h1 + f, g2_ref[...], b2_ref[...])


def _combine(pos, tok0, n, h1_flat, meta, ys, g2, b2, alpha, tm):
    D = g2.shape[1]
    nchunk = D // LANES
    assert n % tm == 0 and tok0 % tm == 0
    pos3 = pos.reshape(-1, 1, 2 * tm)
    off = tok0 // tm
    return pl.pallas_call(
        functools.partial(_combine_body, tm, nchunk, alpha),
        grid=(n // tm,),
        in_specs=[pl.BlockSpec((1, 1, 2 * tm), lambda i: (i + off, 0, 0), memory_space=pltpu.SMEM),
                  pl.BlockSpec((tm * nchunk, LANES), lambda i: (i, 0)),
                  pl.BlockSpec((tm, LANES), lambda i: (i, 0)),
                  pl.BlockSpec(memory_space=pl.ANY),
                  pl.BlockSpec((1, D), lambda i: (0, 0)), pl.BlockSpec((1, D), lambda i: (0, 0))],
        out_specs=pl.BlockSpec((tm, D), lambda i: (i, 0)),
        out_shape=jax.ShapeDtypeStruct((n, D), F32),
        scratch_shapes=[pltpu.VMEM((2, tm * nchunk, LANES), F32), pltpu.SemaphoreType.DMA(())],
        compiler_params=_cparams(("arbitrary",)),
        name="combine_ln2",
    )(pos3, h1_flat, meta, ys, g2, b2)


def _block_diag(w):
    g, a, b = w.shape
    eye = jnp.eye(g, dtype=w.dtype)
    return (eye[:, None, :, None] * w[:, :, None, :]).reshape(g * a, g * b)


def _layer(h_p, h_s, win_k, win_v, pool_st, mem_k, mem_v, mem_prompt,
           w_in, w_mem_kv, w_pool, pool_scale, w_o, ln1_g, ln1_b, ln2_g, ln2_b,
           w_r1, b_r1, w_r2, b_r2, w_gate, w_up, w_down, alpha):
    B, S, D = h_p.shape
    DB, T, _ = h_s.shape
    H = win_k.shape[2]
    Hc = mem_k.shape[2]
    d_a, d_c = H * HEAD_DIM, Hc * HEAD_DIM
    d_b = pool_st.shape[2]
    nchunk = D // LANES
    slopes = 2.0 ** (-8.0 * jnp.arange(1, H + 1, dtype=F32) / H)

    w_in_bf = w_in.astype(BF16)
    w_inT = w_in.T
    wkvT_bf = w_inT[d_a:3 * d_a].astype(BF16)
    wp_bd = _block_diag(w_pool)
    ps = pool_scale.reshape(1, d_b)
    g1, b1 = ln1_g.reshape(1, D), ln1_b.reshape(1, D)
    g2, b2 = ln2_g.reshape(1, D), ln2_b.reshape(1, D)
    n_r = N_EXPERT_GROUPS + N_EXPERTS
    wr = jnp.concatenate([w_r1, jnp.transpose(w_r2, (1, 0, 2)).reshape(D, N_EXPERTS),
                          jnp.zeros((D, LANES - n_r), F32)], axis=1)
    br = jnp.concatenate([b_r1, b_r2.reshape(-1), jnp.zeros((LANES - n_r,), F32)]).reshape(1, LANES)

    w_memT = w_mem_kv.T
    kmT, vm, vmT = _memproj(mem_prompt, w_memT[:d_c].astype(BF16), w_mem_kv[:, d_c:].astype(BF16),
                            w_memT[d_c:].astype(BF16))
    q, k, v, u, qc, kT, vT = _proj(h_p, w_in_bf, wkvT_bf, d_a, d_b, d_c, ts=512)
    attn = _attn(q, k, v, slopes)
    h1p, meta_p, cnt_p = _mix(h_p, attn, u, qc, kmT, vm, w_o.astype(BF16), wp_bd.astype(BF16), ps, g1, b1, wr, br,
                              alpha, tm=256)

    ck = jnp.transpose(win_k, (0, 2, 3, 1))
    cv = jnp.transpose(win_v, (0, 2, 3, 1))
    mk = jnp.transpose(mem_k, (0, 2, 3, 1))
    mv = jnp.transpose(mem_v, (0, 2, 3, 1))
    w_qkvqcT = jnp.concatenate([w_inT[:3 * d_a], w_inT[3 * d_a + d_b:]], axis=0)
    projT = _sproj(h_s.reshape(DB * T, D), w_qkvqcT)
    nk, nv, attn_sT, memo_sT = _decode(projT, T, ck, cv, mk, mv, slopes)
    to_tb = lambda a: jnp.transpose(a, (2, 0, 1)).reshape(T * DB, a.shape[1])
    x_tb = jnp.transpose(h_s, (1, 0, 2)).reshape(T * DB, D)
    state = jnp.transpose(pool_st, (1, 0, 2))
    h1s, meta_s, cnt_all, new_pool = _smix(x_tb, to_tb(attn_sT), to_tb(memo_sT), state,
                                           w_in[:, 3 * d_a:3 * d_a + d_b], w_o, wp_bd, ps, g1, b1, wr, br,
                                           cnt_p, alpha, T)

    tmx = 256
    n_p, n_s = B * S, DB * T
    n = n_p + n_s
    counts = cnt_all[0, :N_EXPERTS].astype(jnp.int32)
    padded = (counts + tmx - 1) // tmx * tmx
    seg_end = jnp.cumsum(padded)
    seg_off = seg_end - padded
    meta = jnp.concatenate([meta_p, meta_s], axis=0)
    e_ids = meta[:, 0:2].astype(jnp.int32)
    pos = (seg_off[e_ids] + meta[:, 4:6].astype(jnp.int32)).reshape(-1)
    max_tiles = (2 * n) // tmx + N_EXPERTS
    n_tiles = (seg_end[-1] // tmx).astype(jnp.int32).reshape(1)
    tile_row0 = jnp.arange(max_tiles, dtype=jnp.int32) * tmx
    tile_expert = jnp.sum((tile_row0[:, None] >= seg_end[None, :]).astype(jnp.int32), axis=1)
    tile_expert = jnp.minimum(tile_expert, N_EXPERTS - 1)
    tile_valid = jnp.clip(seg_off[tile_expert] + counts[tile_expert] - tile_row0, 0, tmx).astype(jnp.int32)

    tok_tile = math.gcd(math.gcd(n_p, n_s), 128)
    xs = _dispatch(tile_valid, pos[:2 * n_p], pos[2 * n_p:], h1p, h1s, tmx, max_tiles, nchunk,
                   ch=math.gcd(n_p, 512))
    ys = _experts(tile_expert, n_tiles, xs.reshape(-1, LANES), w_gate.reshape(N_EXPERTS, D, -1),
                  w_up.reshape(N_EXPERTS, D, -1), w_down.reshape(N_EXPERTS, -1, D), tmx, max_tiles)
    ys3 = ys.reshape(-1, nchunk, LANES)
    y_p = _combine(pos, 0, n_p, h1p, meta_p, ys3, g2, b2, alpha, tm=tok_tile)
    y_s = _combine(pos, n_p, n_s, h1s, meta_s, ys3, g2, b2, alpha, tm=tok_tile)

    y_p = y_p.reshape(B, S, D)
    y_s = jnp.transpose(y_s.reshape(T, DB, D), (1, 0, 2))
    heads = lambda a, h: jnp.transpose(a.reshape(a.shape[0], h, HEAD_DIM, a.shape[2]), (0, 3, 1, 2))
    wbp = min(max(w for w, _ in DILATED_PATTERNS), S)
    new_wk_p = heads(kT, H)[:, S - wbp:]
    new_wv_p = heads(vT, H)[:, S - wbp:]
    pb = pool_st.shape[1]
    new_pool_p = u[:, S - pb:]
    new_mk_p = heads(kmT, Hc)
    new_mv_p = heads(vmT, Hc)
    new_wk_s = jnp.transpose(nk, (0, 3, 1, 2))
    new_wv_s = jnp.transpose(nv, (0, 3, 1, 2))
    new_pool_s = jnp.transpose(new_pool, (1, 0, 2))
    return (y_p, y_s, new_wk_p, new_wv_p, new_pool_p, new_mk_p, new_mv_p, new_wk_s, new_wv_s, new_pool_s)


def kernel(x_prompt, x_sample, cache_win_k, cache_win_v, state_pool, cache_mem_k, cache_mem_v, mem_prompt, w_in, w_mem_kv, w_pool, pool_scale, w_o, ln1_g, ln1_b, ln2_g, ln2_b, w_r1, b_r1, w_r2, b_r2, w_gate, w_up, w_down):
    depth = w_in.shape[0]
    alpha = (2.0 * depth) ** 0.25
    h_p, h_s = x_prompt, x_sample
    outs = [[] for _ in range(8)]
    for l in range(depth):
        res = _layer(h_p, h_s, cache_win_k[l], cache_win_v[l], state_pool[l], cache_mem_k[l], cache_mem_v[l],
                     mem_prompt, w_in[l], w_mem_kv[l], w_pool[l], pool_scale[l], w_o[l], ln1_g[l], ln1_b[l],
                     ln2_g[l], ln2_b[l], w_r1[l], b_r1[l], w_r2[l], b_r2[l], w_gate[l], w_up[l], w_down[l], alpha)
        h_p, h_s = res[0], res[1]
        for lst, val in zip(outs, res[2:]):
            lst.append(val)
    return (h_p, h_s) + tuple(jnp.stack(o) for o in outs)
```

```python
import functools
import math

import jax
import jax.numpy as jnp
from jax import lax
from jax.experimental import pallas as pl
from jax.experimental.pallas import tpu as pltpu

F32 = jnp.float32
BF16 = jnp.bfloat16
HIGHEST = lax.Precision.HIGHEST
NEG_INF = float("-inf")

HEAD_DIM = 64
DILATED_PATTERNS = ((128, 1), (512, 4), (2048, 16))
BAND_BLOCK = 128
RES = max(d for _, d in DILATED_PATTERNS)
POOL_WINDOWS = (2, 4, 8, 16)
N_EXPERT_GROUPS = 4
EXPERTS_PER_GROUP = 8
N_EXPERTS = N_EXPERT_GROUPS * EXPERTS_PER_GROUP
PAST_LEN = 16384
LN_EPS = 1e-5
ATTN_SCALE = HEAD_DIM ** -0.5

LANES = 128
SUBLANES = 8
VMEM_LIMIT = 56 * 1024 * 1024

ROUTER_OFF = N_EXPERT_GROUPS
ATTN_UNROLL = 4
DMA_UNROLL = 8


def _cparams(sem):
    return pltpu.CompilerParams(dimension_semantics=sem, vmem_limit_bytes=VMEM_LIMIT)


def _dot(a, b, precision=None):
    return jnp.dot(a, b, preferred_element_type=F32, precision=precision)


def _dot_nt(a, b, precision=None):
    return lax.dot_general(a, b, (((1,), (1,)), ((), ())), preferred_element_type=F32, precision=precision)


def _memproj_body(mem_ref, wkT_ref, wv_ref, wvT_ref, kmT_ref, vm_ref, vmT_ref):
    m = mem_ref[0].astype(BF16)
    kmT_ref[0] = _dot_nt(wkT_ref[...], m)
    vm_ref[0] = _dot(m, wv_ref[...])
    vmT_ref[0] = _dot_nt(wvT_ref[...], m)


def _memproj(mem, wkT, wv, wvT):
    B, M, D = mem.shape
    C = wv.shape[1]
    full = lambda shape: pl.BlockSpec(shape, lambda b: (0,) * len(shape))
    return pl.pallas_call(
        _memproj_body,
        grid=(B,),
        in_specs=[pl.BlockSpec((1, M, D), lambda b: (b, 0, 0)), full((C, D)), full((D, C)), full((C, D))],
        out_specs=[pl.BlockSpec((1, C, M), lambda b: (b, 0, 0)),
                   pl.BlockSpec((1, M, C), lambda b: (b, 0, 0)),
                   pl.BlockSpec((1, C, M), lambda b: (b, 0, 0))],
        out_shape=[jax.ShapeDtypeStruct((B, C, M), F32), jax.ShapeDtypeStruct((B, M, C), F32),
                   jax.ShapeDtypeStruct((B, C, M), F32)],
        compiler_params=_cparams(("arbitrary",)),
        name="memproj",
    )(mem, wkT, wv, wvT)


def _proj_body(d_a, d_b, x_ref, perm_ref, w_ref, wkvT_ref, q_ref, k_ref, v_ref, u_ref, qc_ref, kT_ref, vT_ref):
    xb = x_ref[0].astype(BF16)
    ts = xb.shape[0]
    xp = _dot(perm_ref[...], xb).astype(BF16)
    y = _dot(xp, w_ref[:, 0:3 * d_a])
    rows = ts // RES
    for r in range(RES):
        q_ref[0, r] = y[r * rows:(r + 1) * rows, 0:d_a]
        k_ref[0, r] = y[r * rows:(r + 1) * rows, d_a:2 * d_a]
        v_ref[0, r] = y[r * rows:(r + 1) * rows, 2 * d_a:3 * d_a]
    y2 = _dot(xb, w_ref[:, 3 * d_a:])
    u_ref[0] = y2[:, 0:d_b]
    qc_ref[0] = y2[:, d_b:]
    yT = _dot_nt(wkvT_ref[...], xb)
    kT_ref[0] = yT[0:d_a]
    vT_ref[0] = yT[d_a:]


def _residue_major_perm(n):
    dst = jnp.arange(n)
    src = RES * (dst % (n // RES)) + dst // (n // RES)
    return (src[:, None] == jnp.arange(n)[None, :]).astype(BF16)


def _proj(x, w_bf, wkvT_bf, d_a, d_b, d_c, ts):
    B, S, D = x.shape
    d_in = w_bf.shape[1]
    assert ts % (RES * SUBLANES) == 0 and S % ts == 0
    row = lambda n: pl.BlockSpec((1, ts, n), lambda b, i: (b, i, 0))
    col = lambda n: pl.BlockSpec((1, n, ts), lambda b, i: (b, 0, i))
    grp = pl.BlockSpec((1, RES, ts // RES, d_a), lambda b, i: (b, 0, i, 0))
    outs = pl.pallas_call(
        functools.partial(_proj_body, d_a, d_b),
        grid=(B, S // ts),
        in_specs=[row(D), pl.BlockSpec((ts, ts), lambda b, i: (0, 0)), pl.BlockSpec((D, d_in), lambda b, i: (0, 0)),
                  pl.BlockSpec((2 * d_a, D), lambda b, i: (0, 0))],
        out_specs=[grp, grp, grp, row(d_b), row(d_c), col(d_a), col(d_a)],
        out_shape=[jax.ShapeDtypeStruct((B, RES, S // RES, d_a), F32)] * 3
        + [jax.ShapeDtypeStruct((B, S, d_b), F32), jax.ShapeDtypeStruct((B, S, d_c), F32)]
        + [jax.ShapeDtypeStruct((B, d_a, S), F32)] * 2,
        compiler_params=_cparams(("arbitrary", "arbitrary")),
        name="proj",
    )(x, _residue_major_perm(ts), w_bf, wkvT_bf)
    return [o.reshape(B, S, d_a) for o in outs[:3]] + list(outs[3:])


def _attn_body(S, slopes_ref, q_ref, k_ref, v_ref, o_ref, opat_ref, lpat_ref):
    hp = pl.program_id(1)
    blk = BAND_BLOCK
    lane = lax.broadcasted_iota(jnp.int32, (blk, LANES), 1)
    head0 = lane < HEAD_DIM

    for pi, (window, dil) in enumerate(DILATED_PATTERNS):
        n_steps = window // dil
        ngrp = RES // dil
        cs = blk // ngrp
        cs_bits = cs.bit_length() - 1
        assert ngrp * dil == RES and cs * ngrp == blk and cs % SUBLANES == 0 and S == RES * blk
        has_prev = ngrp > 1
        nk = 2 * blk if has_prev else blk
        nat = lambda a, ngrp=ngrp, cs=cs, cs_bits=cs_bits: (a & (cs - 1)) * ngrp + (a >> cs_bits)
        qi = lax.broadcasted_iota(jnp.int32, (blk, nk), 0)
        kj = lax.broadcasted_iota(jnp.int32, (blk, nk), 1)
        if has_prev:
            steps = (nat(qi) + blk) - (nat(kj & (blk - 1)) + (kj & blk))
        else:
            steps = qi - kj
        valid = (steps >= 0) & (steps <= n_steps)
        dist = (steps * dil).astype(F32)
        biases = []
        for hh in range(2):
            slope = slopes_ref[2 * hp + hh]
            biases.append(jnp.where(valid, -slope * dist, NEG_INF))
        prev_cols = kj < blk

        def block(idx, carry, dil=dil, ngrp=ngrp, cs=cs, has_prev=has_prev, biases=biases, prev_cols=prev_cols,
                  pi=pi):
            r = idx // ngrp
            j = idx % ngrp

            def chunk(c, jj):
                return pl.ds(pl.multiple_of((r + dil * c) * blk + cs * jj, cs), cs)

            def load(ref, jj):
                return jnp.concatenate([ref[0, chunk(c, jj), :] for c in range(ngrp)], axis=0)

            qb = load(q_ref, j) * ATTN_SCALE
            kb = load(k_ref, j)
            vb = load(v_ref, j)
            if has_prev:
                jp = jnp.maximum(j - 1, 0)
                kb = jnp.concatenate([load(k_ref, jp), kb], axis=0)
                vb = jnp.concatenate([load(v_ref, jp), vb], axis=0)
            kb = kb.astype(BF16)
            vb = vb.astype(BF16)
            outs, lses = [], []
            for hh in range(2):
                qm = jnp.where(head0 if hh == 0 else ~head0, qb, 0.0).astype(BF16)
                s = _dot_nt(qm, kb) + biases[hh]
                if has_prev:
                    s = jnp.where(prev_cols & (j == 0), NEG_INF, s)
                m = jnp.max(s, axis=1, keepdims=True)
                p = jnp.exp(s - m)
                den = jnp.sum(p, axis=1, keepdims=True)
                o = _dot(p.astype(BF16), vb) / den
                outs.append(o)
                lses.append(m + jnp.log(den))
            o_pair = jnp.where(head0, outs[0], outs[1])
            l_pair = jnp.where(head0, lses[0], lses[1])
            for c in range(ngrp):
                opat_ref[pi, chunk(c, j), :] = o_pair[c * cs:(c + 1) * cs]
                lpat_ref[pi, chunk(c, j), :] = l_pair[c * cs:(c + 1) * cs]
            return carry

        lax.fori_loop(0, RES, block, 0, unroll=ATTN_UNROLL)

    chunk = 256

    def mix(c, carry):
        rows = pl.ds(pl.multiple_of(c * chunk, chunk), chunk)
        l0, l1, l2 = lpat_ref[0, rows, :], lpat_ref[1, rows, :], lpat_ref[2, rows, :]
        mx = jnp.maximum(jnp.maximum(l0, l1), l2)
        w0, w1, w2 = jnp.exp(l0 - mx), jnp.exp(l1 - mx), jnp.exp(l2 - mx)
        num = w0 * opat_ref[0, rows, :] + w1 * opat_ref[1, rows, :] + w2 * opat_ref[2, rows, :]
        o_ref[0, rows, :] = (num / (w0 + w1 + w2)).astype(o_ref.dtype)
        return carry

    lax.fori_loop(0, S // chunk, mix, 0)


def _attn(q, k, v, slopes):
    B, S, d_a = q.shape
    assert S % (BAND_BLOCK * max(d for _, d in DILATED_PATTERNS)) == 0
    npair = d_a // LANES
    spec = pl.BlockSpec((1, S, LANES), lambda b, h, *_: (b, 0, h))
    grid_spec = pltpu.PrefetchScalarGridSpec(
        num_scalar_prefetch=0,
        grid=(B, npair),
        in_specs=[pl.BlockSpec(memory_space=pltpu.SMEM), spec, spec, spec],
        out_specs=spec,
        scratch_shapes=[pltpu.VMEM((len(DILATED_PATTERNS), S, LANES), F32),
                        pltpu.VMEM((len(DILATED_PATTERNS), S, LANES), F32)],
    )
    return pl.pallas_call(
        functools.partial(_attn_body, S),
        grid_spec=grid_spec,
        out_shape=jax.ShapeDtypeStruct((B, S, d_a), BF16),
        compiler_params=_cparams(("arbitrary", "arbitrary")),
        name="dilated_attn",
    )(slopes, q, k, v)


def _layer_norm(z, g, b):
    mu = jnp.mean(z, axis=-1, keepdims=True)
    zc = z - mu
    var = jnp.mean(zc * zc, axis=-1, keepdims=True)
    return zc * lax.rsqrt(var + LN_EPS) * g + b


def _route(logits):
    n = logits.shape[0]
    lane = lax.broadcasted_iota(jnp.int32, (n, LANES), 1)
    lane_f = lane.astype(F32)
    big = float(LANES)
    is_outer = lane < N_EXPERT_GROUPS
    l1 = jnp.where(is_outer, logits, NEG_INF)
    m1 = jnp.max(l1, axis=1, keepdims=True)
    g_sel = jnp.min(jnp.where(l1 == m1, lane_f, big), axis=1, keepdims=True)
    v1 = 1.0 / jnp.sum(jnp.exp(l1 - m1), axis=1, keepdims=True)
    lo = ROUTER_OFF + g_sel * EXPERTS_PER_GROUP
    in_group = (lane_f >= lo) & (lane_f < lo + EXPERTS_PER_GROUP)
    l2 = jnp.where(in_group, logits, NEG_INF)
    ma = jnp.max(l2, axis=1, keepdims=True)
    ia = jnp.min(jnp.where(l2 == ma, lane_f, big), axis=1, keepdims=True)
    l2b = jnp.where(lane_f == ia, NEG_INF, l2)
    mb = jnp.max(l2b, axis=1, keepdims=True)
    ib = jnp.min(jnp.where(l2b == mb, lane_f, big), axis=1, keepdims=True)
    eb = jnp.exp(mb - ma)
    wa = 1.0 / (1.0 + eb)
    wb = eb / (1.0 + eb)
    return ia - ROUTER_OFF, ib - ROUTER_OFF, v1 * wa, v1 * wb


def _rank_and_meta(e0, e1, g0, g1, carry):
    n = e0.shape[0]
    lane_f = lax.broadcasted_iota(jnp.int32, (n, LANES), 1).astype(F32)
    oh0 = (lane_f == e0).astype(F32)
    oh1 = (lane_f == e1).astype(F32)
    both = oh0 + oh1
    ti = lax.broadcasted_iota(jnp.int32, (n, n), 0)
    tj = lax.broadcasted_iota(jnp.int32, (n, n), 1)
    tri = (tj < ti).astype(BF16)
    before = _dot(tri, both.astype(BF16)) + carry
    r0 = jnp.sum(before * oh0, axis=1, keepdims=True)
    r1 = jnp.sum(before * oh1, axis=1, keepdims=True)
    new_carry = carry + jnp.sum(both, axis=0, keepdims=True)
    lane = lax.broadcasted_iota(jnp.int32, (n, LANES), 1)
    meta = jnp.zeros((n, LANES), F32)
    for i, val in enumerate((e0, e1, g0, g1, r0, r1)):
        meta = jnp.where(lane == i, val, meta)
    return meta, new_carry


def _lane_group(shape, width):
    lane = lax.broadcasted_iota(jnp.int32, shape, len(shape) - 1)
    grp = jnp.zeros(shape, jnp.int32)
    for g in range(1, shape[-1] // width):
        grp = grp + (lane >= g * width).astype(jnp.int32)
    return grp


def _select_by_group(grp, vals):
    out = vals[-1]
    for g in range(len(vals) - 2, -1, -1):
        out = jnp.where(grp == g, vals[g], out)
    return out


def _store_token_rows(ref, val):
    n, d = val.shape
    nchunk = d // LANES
    for c in range(nchunk):
        ref[pl.ds(c, n, stride=nchunk), :] = val[:, c * LANES:(c + 1) * LANES]


def _load_token_rows(ref, n, nchunk):
    return jnp.concatenate([ref[pl.ds(c, n, stride=nchunk), :] for c in range(nchunk)], axis=1)


HALO = 16


def _mix_body(tm, nt, n_tiles, d_b, alpha, x_ref, at_ref, unperm_ref, u_ref, uh_ref, qc_ref, kmT_ref, vm_ref,
              wo_ref, wp_ref, ps_ref, g1_ref, b1_ref, wr_ref, br_ref, cin_ref, hs_ref,
              h1_ref, meta_ref, cnt_ref, carry_ref):
    g = pl.program_id(0)

    @pl.when(g == 0)
    def _():
        carry_ref[...] = cin_ref[...]

    @pl.when(g == n_tiles)
    def _():
        ns_rows = hs_ref.shape[0]
        h1_ref[0:ns_rows, :] = hs_ref[...]
        h1_ref[ns_rows:, :] = jnp.zeros((h1_ref.shape[0] - ns_rows, LANES), F32)

    @pl.when(g < n_tiles)
    def _():
        _mix_tile(tm, g % nt, d_b, alpha, x_ref, at_ref, unperm_ref, u_ref, uh_ref, qc_ref, kmT_ref, vm_ref, wo_ref,
                  wp_ref, ps_ref, g1_ref, b1_ref, wr_ref, br_ref, h1_ref, meta_ref, cnt_ref, carry_ref)


def _mix_tile(tm, i, d_b, alpha, x_ref, at_ref, unperm_ref, u_ref, uh_ref, qc_ref, kmT_ref, vm_ref, wo_ref, wp_ref,
              ps_ref, g1_ref, b1_ref, wr_ref, br_ref, h1_ref, meta_ref, cnt_ref, carry_ref):
    u = u_ref[0]
    halo = jnp.where(i == 0, 0.0, uh_ref[0])
    ext = jnp.concatenate([halo, halo, u], axis=0)
    s2 = ext + pltpu.roll(ext, 1, 0)
    s4 = s2 + pltpu.roll(s2, 2, 0)
    s8 = s4 + pltpu.roll(s4, 4, 0)
    s16 = s8 + pltpu.roll(s8, 8, 0)
    grp = _lane_group((tm, d_b), d_b // len(POOL_WINDOWS))
    sums = [s[2 * HALO:] for s in (s2, s4, s8, s16)]
    win = _select_by_group(grp, sums)
    wlen = _select_by_group(grp, [jnp.int32(w) for w in POOL_WINDOWS])
    pos = i * tm + lax.broadcasted_iota(jnp.int32, (tm, d_b), 0)
    cnt = jnp.minimum(wlen, pos + 1).astype(F32)
    diff = win / cnt - u
    pool = _dot(diff.astype(BF16), wp_ref[...]) * ps_ref[...]

    qc = qc_ref[0] * ATTN_SCALE
    d_c = qc.shape[1]
    kmT = kmT_ref[0].astype(BF16)
    vm = vm_ref[0].astype(BF16)
    hl = _lane_group((tm, d_c), HEAD_DIM)
    memo = jnp.zeros((tm, d_c), F32)
    for h in range(d_c // HEAD_DIM):
        qm = jnp.where(hl == h, qc, 0.0).astype(BF16)
        s = _dot(qm, kmT)
        p = jnp.exp(s - jnp.max(s, axis=1, keepdims=True))
        den = jnp.sum(p, axis=1, keepdims=True)
        o = _dot(p.astype(BF16), vm) / den
        memo = jnp.where(hl == h, o, memo)

    at_rl = jnp.concatenate([at_ref[0, r] for r in range(RES)], axis=0)
    attn = _dot(unperm_ref[...], at_rl).astype(BF16)
    cat = jnp.concatenate([attn, pool.astype(BF16), memo.astype(BF16)], axis=1)
    mixv = _dot(cat, wo_ref[...])
    h1 = _layer_norm(alpha * x_ref[0] + mixv, g1_ref[...], b1_ref[...])
    _store_token_rows(h1_ref, h1)

    logits = _dot(h1, wr_ref[...], precision=HIGHEST) + br_ref[...]
    e0, e1, g0, g1 = _route(logits)
    meta, new_carry = _rank_and_meta(e0, e1, g0, g1, carry_ref[0:1, :])
    meta_ref[...] = meta
    carry_ref[...] = jnp.broadcast_to(new_carry, carry_ref.shape)
    cnt_ref[...] = carry_ref[...]


def _mix(x, attn_rm, u, qc, kmT, vm, wo_bf, wp_bd_bf, pool_scale, g1, b1, wr, br, counts_in, h1s, alpha, tm):
    B, S, D = x.shape
    d_a, d_b, d_c = attn_rm.shape[2], u.shape[2], qc.shape[2]
    n_mem = vm.shape[1]
    nt = S // tm
    n_tiles = B * nt
    nchunk = D // LANES
    assert tm % (RES * SUBLANES * 2) == 0 and h1s.shape[0] < tm * nchunk
    last = n_tiles - 1
    bi = lambda g: (jnp.minimum(g, last) // nt, jnp.minimum(g, last) % nt)
    row = lambda n: pl.BlockSpec((1, tm, n), lambda g: bi(g) + (0,))
    full = lambda shape: pl.BlockSpec(shape, lambda g: (0,) * len(shape))
    per_b = lambda shape: pl.BlockSpec((1,) + shape, lambda g: (bi(g)[0], 0, 0))
    halo_spec = pl.BlockSpec((1, HALO, d_b), lambda g: (bi(g)[0], jnp.maximum(bi(g)[1] * (tm // HALO) - 1, 0), 0))
    at_spec = pl.BlockSpec((1, RES, tm // RES, d_a), lambda g: (bi(g)[0], 0, bi(g)[1], 0))
    dst = jnp.arange(tm)
    unperm = ((dst % RES) * (tm // RES) + dst // RES)[:, None] == jnp.arange(tm)[None, :]
    return pl.pallas_call(
        functools.partial(_mix_body, tm, nt, n_tiles, d_b, alpha),
        grid=(n_tiles + 1,),
        in_specs=[row(D), at_spec, full((tm, tm)), row(d_b), halo_spec, row(d_c), per_b((d_c, n_mem)),
                  per_b((n_mem, d_c)), full((D, D)), full((d_b, d_b)), full((1, d_b)), full((1, D)), full((1, D)),
                  full((D, LANES)), full((1, LANES)), full((SUBLANES, LANES)), full(h1s.shape)],
        out_specs=[pl.BlockSpec((tm * nchunk, LANES), lambda g: (g, 0)),
                   pl.BlockSpec((tm, LANES), lambda g: (jnp.minimum(g, last), 0)),
                   pl.BlockSpec((SUBLANES, LANES), lambda g: (0, 0))],
        out_shape=[jax.ShapeDtypeStruct(((n_tiles + 1) * tm * nchunk, LANES), F32),
                   jax.ShapeDtypeStruct((B * S, LANES), F32),
                   jax.ShapeDtypeStruct((SUBLANES, LANES), F32)],
        scratch_shapes=[pltpu.VMEM((SUBLANES, LANES), F32)],
        compiler_params=_cparams(("arbitrary",)),
        name="mix_ln1_router",
    )(x, attn_rm.reshape(B, RES, S // RES, d_a), unperm.astype(BF16), u, u, qc, kmT, vm, wo_bf, wp_bd_bf,
      pool_scale, g1, b1, wr, br, counts_in, h1s)


def _columns_to_lanes(cols, rows):
    lane = lax.broadcasted_iota(jnp.int32, (rows, LANES), 1)
    tile = jnp.zeros((rows, LANES), F32)
    for t, c in enumerate(cols):
        tile = jnp.where(lane == t, c, tile)
    return tile


def _sproj_body(x_ref, wT_ref, o_ref):
    o_ref[...] = _dot_nt(wT_ref[...], x_ref[...], precision=HIGHEST)


def _sproj(x, wT):
    vm = pl.BlockSpec(memory_space=pltpu.VMEM)
    return pl.pallas_call(
        _sproj_body, in_specs=[vm, vm], out_specs=vm,
        out_shape=jax.ShapeDtypeStruct((wT.shape[0], x.shape[0]), F32),
        compiler_params=pltpu.CompilerParams(vmem_limit_bytes=VMEM_LIMIT),
        name="decode_proj",
    )(x, wT)


def _decode_body(T, wb, slopes_ref, pT_ref, ck_ref, cv_ref, mk_ref, mv_ref,
                 nk_ref, nv_ref, at_ref, mo_ref, qkv_ref):
    H = ck_ref.shape[1]
    d_a = H * HEAD_DIM
    ntok = pT_ref.shape[1]
    shift = (ntok - T * pl.program_id(0)) % ntok
    qkv_ref[...] = pltpu.roll(pT_ref[...], shift, 1)

    jpos = lax.broadcasted_iota(jnp.int32, (1, wb), 1)
    tnew = lax.broadcasted_iota(jnp.int32, (1, T), 1)
    mult_c, dist_c, mult_n, dist_n = [], [], [], []
    for t in range(T):
        dc = wb + t - jpos
        dn = t - tnew
        mc = jnp.zeros((1, wb), F32)
        mn = jnp.zeros((1, T), F32)
        for window, dil in DILATED_PATTERNS:
            assert dil & (dil - 1) == 0
            mc = mc + (((dc & (dil - 1)) == 0) & (dc <= window) & (dc >= 0)).astype(F32)
            mn = mn + (((dn & (dil - 1)) == 0) & (dn <= window) & (dn >= 0)).astype(F32)
        mult_c.append(mc)
        mult_n.append(mn)
        dist_c.append(dc.astype(F32))
        dist_n.append(jnp.maximum(dn, 0).astype(F32))

    lane_t = lax.broadcasted_iota(jnp.int32, (HEAD_DIM, LANES), 1)
    last = wb - LANES

    def head(h, carry):
        slope = slopes_ref[h]
        kTc = ck_ref[0, h]
        vTc = cv_ref[0, h]
        r0 = pl.multiple_of(h * HEAD_DIM, HEAD_DIM)
        qT = qkv_ref[pl.ds(r0, HEAD_DIM), :][:, 0:T]
        kTn = qkv_ref[pl.ds(d_a + r0, HEAD_DIM), :][:, 0:T]
        vTn = qkv_ref[pl.ds(2 * d_a + r0, HEAD_DIM), :][:, 0:T]
        cols = []
        for t in range(T):
            qcol = qT[:, t:t + 1] * ATTN_SCALE
            sc = jnp.sum(kTc * qcol, axis=0, keepdims=True) - slope * dist_c[t]
            sn = jnp.sum(kTn * qcol, axis=0, keepdims=True) - slope * dist_n[t]
            sc = jnp.where(mult_c[t] > 0, sc, NEG_INF)
            sn = jnp.where(mult_n[t] > 0, sn, NEG_INF)
            m = jnp.maximum(jnp.max(sc, axis=1, keepdims=True), jnp.max(sn, axis=1, keepdims=True))
            pc = mult_c[t] * jnp.exp(sc - m)
            pn = mult_n[t] * jnp.exp(sn - m)
            den = jnp.sum(pc, axis=1, keepdims=True) + jnp.sum(pn, axis=1, keepdims=True)
            o = jnp.sum(vTc * pc, axis=1, keepdims=True) + jnp.sum(vTn * pn, axis=1, keepdims=True)
            cols.append(o / den)
        at_ref[0, pl.ds(r0, HEAD_DIM), :] = _columns_to_lanes(cols, HEAD_DIM)

        rk = pltpu.roll(kTc, wb - T, 1)
        rv = pltpu.roll(vTc, wb - T, 1)
        nk_ref[0, h] = rk
        nv_ref[0, h] = rv
        newk = jnp.zeros((HEAD_DIM, LANES), F32)
        newv = jnp.zeros((HEAD_DIM, LANES), F32)
        for t in range(T):
            newk = jnp.where(lane_t == LANES - T + t, kTn[:, t:t + 1], newk)
            newv = jnp.where(lane_t == LANES - T + t, vTn[:, t:t + 1], newv)
        nk_ref[0, h, :, last:] = jnp.where(lane_t >= LANES - T, newk, rk[:, last:])
        nv_ref[0, h, :, last:] = jnp.where(lane_t >= LANES - T, newv, rv[:, last:])
        return carry

    lax.fori_loop(0, H, head, 0, unroll=2)

    Hc = mk_ref.shape[1]
    for h in range(Hc):
        kT = mk_ref[0, h]
        vT = mv_ref[0, h]
        r0 = h * HEAD_DIM
        qT = qkv_ref[3 * d_a + r0:3 * d_a + r0 + HEAD_DIM, :][:, 0:T]
        cols = []
        for t in range(T):
            qcol = qT[:, t:t + 1] * ATTN_SCALE
            s = jnp.sum(kT * qcol, axis=0, keepdims=True)
            p = jnp.exp(s - jnp.max(s, axis=1, keepdims=True))
            den = jnp.sum(p, axis=1, keepdims=True)
            cols.append(jnp.sum(vT * p, axis=1, keepdims=True) / den)
        mo_ref[0, r0:r0 + HEAD_DIM, :] = _columns_to_lanes(cols, HEAD_DIM)


def _decode(projT, T, ck, cv, mk, mv, slopes):
    DB, H, hd, wb = ck.shape
    _, Hc, _, n_mem = mk.shape
    d_a, d_c = H * hd, Hc * hd
    assert wb >= max(w for w, _ in DILATED_PATTERNS) and wb % LANES == 0 and T <= LANES
    assert projT.shape == (3 * d_a + d_c, DB * T)
    cache = pl.BlockSpec((1, H, hd, wb), lambda b, *_: (b, 0, 0, 0))
    memc = pl.BlockSpec((1, Hc, hd, n_mem), lambda b, *_: (b, 0, 0, 0))
    grid_spec = pltpu.PrefetchScalarGridSpec(
        num_scalar_prefetch=0,
        grid=(DB,),
        in_specs=[pl.BlockSpec(memory_space=pltpu.SMEM),
                  pl.BlockSpec(projT.shape, lambda b, *_: (0, 0)),
                  cache, cache, memc, memc],
        out_specs=[cache, cache,
                   pl.BlockSpec((1, d_a, LANES), lambda b, *_: (b, 0, 0)),
                   pl.BlockSpec((1, d_c, LANES), lambda b, *_: (b, 0, 0))],
        scratch_shapes=[pltpu.VMEM(projT.shape, F32)],
    )
    nk, nv, at, mo = pl.pallas_call(
        functools.partial(_decode_body, T, wb),
        grid_spec=grid_spec,
        out_shape=[jax.ShapeDtypeStruct(ck.shape, F32), jax.ShapeDtypeStruct(cv.shape, F32),
                   jax.ShapeDtypeStruct((DB, d_a, LANES), F32), jax.ShapeDtypeStruct((DB, d_c, LANES), F32)],
        compiler_params=_cparams(("arbitrary",)),
        name="decode_attn_cache",
    )(slopes, projT, ck, cv, mk, mv)
    return nk, nv, at[:, :, :T], mo[:, :, :T]


def _smix_body(T, alpha, pos0, x_ref, at_ref, mo_ref, st_ref, wu_ref, wo_ref, wp_ref, ps_ref, g1_ref, b1_ref,
               wr_ref, br_ref, cin_ref, h1_ref, meta_ref, cnt_ref, pool_ref):
    n = x_ref.shape[0]
    db = n // T
    pb = st_ref.shape[0]
    d_b = st_ref.shape[2]
    x = x_ref[...]
    u_new = _dot(x, wu_ref[...], precision=HIGHEST)
    seq = [st_ref[j] for j in range(pb)] + [u_new[t * db:(t + 1) * db] for t in range(T)]
    for j in range(pb):
        pool_ref[j] = seq[j + T]
    grp = _lane_group((db, d_b), d_b // len(POOL_WINDOWS))
    diffs = []
    for t in range(T):
        j = pb + t
        per_w = []
        for w in POOL_WINDOWS:
            acc = seq[j]
            for back in range(1, w):
                if j - back >= 0:
                    acc = acc + seq[j - back]
            per_w.append(acc / float(min(w, pos0 + j + 1)))
        diffs.append(_select_by_group(grp, per_w) - seq[j])
    diff = jnp.concatenate(diffs, axis=0)
    pool = _dot(diff, wp_ref[...], precision=HIGHEST) * ps_ref[...]
    cat = jnp.concatenate([at_ref[...], pool, mo_ref[...]], axis=1)
    mixv = _dot(cat, wo_ref[...], precision=HIGHEST)
    h1 = _layer_norm(alpha * x + mixv, g1_ref[...], b1_ref[...])
    _store_token_rows(h1_ref, h1)
    logits = _dot(h1, wr_ref[...], precision=HIGHEST) + br_ref[...]
    e0, e1, g0, g1 = _route(logits)
    meta, new_carry = _rank_and_meta(e0, e1, g0, g1, cin_ref[0:1, :])
    meta_ref[...] = meta
    cnt_ref[...] = jnp.broadcast_to(new_carry, cnt_ref.shape)


def _smix(x_tb, attn_tb, memo_tb, state, wu, wo, wp_bd, pool_scale, g1, b1, wr, br, counts_in, alpha, T):
    n, D = x_tb.shape
    pb, db, d_b = state.shape
    nchunk = D // LANES
    vm = pl.BlockSpec(memory_space=pltpu.VMEM)
    return pl.pallas_call(
        functools.partial(_smix_body, T, alpha, PAST_LEN - pb),
        in_specs=[vm] * 13,
        out_specs=[vm] * 4,
        out_shape=[jax.ShapeDtypeStruct((n * nchunk, LANES), F32), jax.ShapeDtypeStruct((n, LANES), F32),
                   jax.ShapeDtypeStruct((SUBLANES, LANES), F32), jax.ShapeDtypeStruct((pb, db, d_b), F32)],
        compiler_params=pltpu.CompilerParams(vmem_limit_bytes=VMEM_LIMIT),
        name="decode_mix_ln1_router",
    )(x_tb, attn_tb, memo_tb, state, wu, wo, wp_bd, pool_scale, g1, b1, wr, br, counts_in)


def _slot_map_body(blk, pos_ref, fill_ref, inv_ref, sem):
    g = pl.program_id(0)

    @pl.when(g == 0)
    def _():
        c = pltpu.make_async_copy(fill_ref, inv_ref, sem)
        c.start()
        c.wait()

    base = g * blk

    def body(i, carry):
        inv_ref[pos_ref[0, 0, i]] = (base + i) >> 1
        return carry

    lax.fori_loop(0, blk, body, 0, unroll=DMA_UNROLL)


def _slot_map(pos, n_slots, zero_token, blk):
    n2 = pos.shape[0]
    assert n2 % blk == 0 and n_slots % 1024 == 0
    return pl.pallas_call(
        functools.partial(_slot_map_body, blk),
        grid=(n2 // blk,),
        in_specs=[pl.BlockSpec((1, 1, blk), lambda g: (g, 0, 0), memory_space=pltpu.SMEM),
                  pl.BlockSpec(memory_space=pl.ANY)],
        out_specs=pl.BlockSpec(memory_space=pltpu.SMEM),
        out_shape=jax.ShapeDtypeStruct((n_slots,), jnp.int32),
        scratch_shapes=[pltpu.SemaphoreType.DMA(())],
        compiler_params=_cparams(("arbitrary",)),
        name="slot_map",
    )(pos.reshape(n2 // blk, 1, blk), jnp.full((n_slots,), zero_token, jnp.int32))


def _expert_body(tmx, nchunk, te_ref, nt_ref, cur_ref, nxt_ref, h_ref, wg_ref, wu_ref, wd_ref, ys_ref, buf_ref,
                 sems):
    i = pl.program_id(0)
    n_tiles = nt_ref[0]

    def gather(idx_ref, slot, fn):
        def body(j, carry):
            dst = buf_ref.at[slot, pl.ds(pl.multiple_of(j * nchunk, nchunk), nchunk)]
            fn(pltpu.make_async_copy(h_ref.at[idx_ref[0, 0, j]], dst, sems.at[slot]))
            return carry

        lax.fori_loop(0, tmx, body, 0, unroll=DMA_UNROLL)

    @pl.when(i == 0)
    def _():
        gather(cur_ref, 0, lambda c: c.start())

    @pl.when(i < n_tiles)
    def _():
        slot = i % 2
        gather(cur_ref, slot, lambda c: c.wait())

        @pl.when(i + 1 < n_tiles)
        def _():
            gather(nxt_ref, 1 - slot, lambda c: c.start())

        x = _load_token_rows(buf_ref.at[slot], tmx, nchunk).astype(BF16)
        hg = _dot(x, wg_ref[0].astype(BF16))
        hu = _dot(x, wu_ref[0].astype(BF16))
        a = (hg * jax.nn.sigmoid(hg) * hu).astype(BF16)
        y = _dot(a, wd_ref[0].astype(BF16))
        _store_token_rows(ys_ref, y)

    @pl.when(i >= n_tiles)
    def _():
        ys_ref[...] = jnp.zeros_like(ys_ref)


def _experts(tile_expert, n_tiles, inv, h_rows, wg, wu, wd, tmx, max_tiles):
    E, D, F = wg.shape
    nchunk = D // LANES
    inv3 = inv.reshape(-1, 1, tmx)
    last_blk = inv3.shape[0] - 1
    assert inv3.shape[0] >= max_tiles
    grid_spec = pltpu.PrefetchScalarGridSpec(
        num_scalar_prefetch=2,
        grid=(max_tiles,),
        in_specs=[pl.BlockSpec((1, 1, tmx), lambda i, te, nt: (i, 0, 0), memory_space=pltpu.SMEM),
                  pl.BlockSpec((1, 1, tmx), lambda i, te, nt: (jnp.minimum(i + 1, last_blk), 0, 0),
                               memory_space=pltpu.SMEM),
                  pl.BlockSpec(memory_space=pl.ANY),
                  pl.BlockSpec((1, D, F), lambda i, te, nt: (te[i], 0, 0)),
                  pl.BlockSpec((1, D, F), lambda i, te, nt: (te[i], 0, 0)),
                  pl.BlockSpec((1, F, D), lambda i, te, nt: (te[i], 0, 0))],
        out_specs=pl.BlockSpec((tmx * nchunk, LANES), lambda i, te, nt: (i, 0)),
        scratch_shapes=[pltpu.VMEM((2, tmx * nchunk, LANES), F32), pltpu.SemaphoreType.DMA((2,))],
    )
    return pl.pallas_call(
        functools.partial(_expert_body, tmx, nchunk),
        grid_spec=grid_spec,
        out_shape=jax.ShapeDtypeStruct((max_tiles * tmx * nchunk, LANES), F32),
        compiler_params=_cparams(("arbitrary",)),
        name="expert_swiglu",
    )(tile_expert, n_tiles, inv3, inv3, h_rows, wg, wu, wd)


def _combine_body(tm, nchunk, alpha, pos_ref, h1_ref, meta_ref, ys_ref, g2_ref, b2_ref, o_ref, buf_ref, sem):
    def row_copy(src_row, k, t):
        dst = buf_ref.at[k, pl.ds(pl.multiple_of(t * nchunk, nchunk), nchunk)]
        return pltpu.make_async_copy(ys_ref.at[src_row], dst, sem)

    def start(t, carry):
        for k in range(2):
            row_copy(pos_ref[0, 0, 2 * t + k], k, t).start()
        return carry

    lax.fori_loop(0, tm, start, 0)

    def wait(t, carry):
        for k in range(2):
            row_copy(0, k, t).wait()
        return carry

    lax.fori_loop(0, tm, wait, 0)

    h1 = _load_token_rows(h1_ref, tm, nchunk)
    y0 = _load_token_rows(buf_ref.at[0], tm, nchunk)
    y1 = _load_token_rows(buf_ref.at[1], tm, nchunk)
    meta = meta_ref[...]
    lane = lax.broadcasted_iota(jnp.int32, meta.shape, 1)
    gate0 = jnp.sum(jnp.where(lane == 2, meta, 0.0), axis=1, keepdims=True)
    gate1 = jnp.sum(jnp.where(lane == 3, meta, 0.0), axis=1, keepdims=True)
    f = gate0 * y0 + gate1 * y1
    o_ref[...] = _layer_norm(alpha * h1 + f, g2_ref[...], b2_ref[...])


def _combine(pos, tok0, n, h1_flat, meta, ys, g2, b2, alpha, tm):
    D = g2.shape[1]
    nchunk = D // LANES
    assert n % tm == 0 and tok0 % tm == 0
    pos3 = pos.reshape(-1, 1, 2 * tm)
    off = tok0 // tm
    return pl.pallas_call(
        functools.partial(_combine_body, tm, nchunk, alpha),
        grid=(n // tm,),
        in_specs=[pl.BlockSpec((1, 1, 2 * tm), lambda i: (i + off, 0, 0), memory_space=pltpu.SMEM),
                  pl.BlockSpec((tm * nchunk, LANES), lambda i: (i + off, 0)),
                  pl.BlockSpec((tm, LANES), lambda i: (i, 0)),
                  pl.BlockSpec(memory_space=pl.ANY),
                  pl.BlockSpec((1, D), lambda i: (0, 0)), pl.BlockSpec((1, D), lambda i: (0, 0))],
        out_specs=pl.BlockSpec((tm, D), lambda i: (i, 0)),
        out_shape=jax.ShapeDtypeStruct((n, D), F32),
        scratch_shapes=[pltpu.VMEM((2, tm * nchunk, LANES), F32), pltpu.SemaphoreType.DMA(())],
        compiler_params=_cparams(("arbitrary",)),
        name="combine_ln2",
    )(pos3, h1_flat, meta, ys, g2, b2)


def _block_diag(w):
    g, a, b = w.shape
    eye = jnp.eye(g, dtype=w.dtype)
    return (eye[:, None, :, None] * w[:, :, None, :]).reshape(g * a, g * b)


def _layer(h_p, h_s, win_k, win_v, pool_st, mem_k, mem_v, mem_prompt,
           w_in, w_mem_kv, w_pool, pool_scale, w_o, ln1_g, ln1_b, ln2_g, ln2_b,
           w_r1, b_r1, w_r2, b_r2, w_gate, w_up, w_down, alpha):
    B, S, D = h_p.shape
    DB, T, _ = h_s.shape
    H = win_k.shape[2]
    Hc = mem_k.shape[2]
    d_a, d_c = H * HEAD_DIM, Hc * HEAD_DIM
    d_b = pool_st.shape[2]
    nchunk = D // LANES
    slopes = 2.0 ** (-8.0 * jnp.arange(1, H + 1, dtype=F32) / H)

    w_in_bf = w_in.astype(BF16)
    w_inT = w_in.T
    wkvT_bf = w_inT[d_a:3 * d_a].astype(BF16)
    wp_bd = _block_diag(w_pool)
    ps = pool_scale.reshape(1, d_b)
    g1, b1 = ln1_g.reshape(1, D), ln1_b.reshape(1, D)
    g2, b2 = ln2_g.reshape(1, D), ln2_b.reshape(1, D)
    n_r = N_EXPERT_GROUPS + N_EXPERTS
    wr = jnp.concatenate([w_r1, jnp.transpose(w_r2, (1, 0, 2)).reshape(D, N_EXPERTS),
                          jnp.zeros((D, LANES - n_r), F32)], axis=1)
    br = jnp.concatenate([b_r1, b_r2.reshape(-1), jnp.zeros((LANES - n_r,), F32)]).reshape(1, LANES)

    ck = jnp.transpose(win_k, (0, 2, 3, 1))
    cv = jnp.transpose(win_v, (0, 2, 3, 1))
    mk = jnp.transpose(mem_k, (0, 2, 3, 1))
    mv = jnp.transpose(mem_v, (0, 2, 3, 1))
    w_qkvqcT = jnp.concatenate([w_inT[:3 * d_a], w_inT[3 * d_a + d_b:]], axis=0)
    projT = _sproj(h_s.reshape(DB * T, D), w_qkvqcT)
    nk, nv, attn_sT, memo_sT = _decode(projT, T, ck, cv, mk, mv, slopes)
    to_tb = lambda a: jnp.transpose(a, (2, 0, 1)).reshape(T * DB, a.shape[1])
    x_tb = jnp.transpose(h_s, (1, 0, 2)).reshape(T * DB, D)
    state = jnp.transpose(pool_st, (1, 0, 2))
    h1s, meta_s, cnt_s, new_pool = _smix(x_tb, to_tb(attn_sT), to_tb(memo_sT), state,
                                         w_in[:, 3 * d_a:3 * d_a + d_b], w_o, wp_bd, ps, g1, b1, wr, br,
                                         jnp.zeros((SUBLANES, LANES), F32), alpha, T)

    w_memT = w_mem_kv.T
    kmT, vm, vmT = _memproj(mem_prompt, w_memT[:d_c].astype(BF16), w_mem_kv[:, d_c:].astype(BF16),
                            w_memT[d_c:].astype(BF16))
    q, k, v, u, qc, kT, vT = _proj(h_p, w_in_bf, wkvT_bf, d_a, d_b, d_c, ts=512)
    attn = _attn(q, k, v, slopes)
    h_rows, meta_p, cnt_all = _mix(h_p, attn, u, qc, kmT, vm, w_o.astype(BF16), wp_bd.astype(BF16), ps, g1, b1,
                                   wr, br, cnt_s, h1s, alpha, tm=256)

    tmx = 256
    n_p, n_s = B * S, DB * T
    n = n_p + n_s
    counts = cnt_all[0, :N_EXPERTS].astype(jnp.int32)
    padded = (counts + tmx - 1) // tmx * tmx
    seg_end = jnp.cumsum(padded)
    seg_off = seg_end - padded
    meta = jnp.concatenate([meta_p, meta_s], axis=0)
    e_ids = meta[:, 0:2].astype(jnp.int32)
    pos = (seg_off[e_ids] + meta[:, 4:6].astype(jnp.int32)).reshape(-1)
    max_tiles = (2 * n) // tmx + N_EXPERTS
    n_tiles = (seg_end[-1] // tmx).astype(jnp.int32).reshape(1)
    tile_row0 = jnp.arange(max_tiles, dtype=jnp.int32) * tmx
    tile_expert = jnp.sum((tile_row0[:, None] >= seg_end[None, :]).astype(jnp.int32), axis=1)
    tile_expert = jnp.minimum(tile_expert, N_EXPERTS - 1)

    tok_tile = math.gcd(math.gcd(n_p, n_s), 128)
    n_slots = -(-(max_tiles * tmx) // 1024) * 1024
    inv = _slot_map(pos, n_slots, n, blk=math.gcd(2 * n, 2048))
    ys = _experts(tile_expert, n_tiles, inv, h_rows.reshape(-1, nchunk, LANES), w_gate.reshape(N_EXPERTS, D, -1),
                  w_up.reshape(N_EXPERTS, D, -1), w_down.reshape(N_EXPERTS, -1, D), tmx, max_tiles)
    ys3 = ys.reshape(-1, nchunk, LANES)
    y_p = _combine(pos, 0, n_p, h_rows, meta_p, ys3, g2, b2, alpha, tm=tok_tile)
    y_s = _combine(pos, n_p, n_s, h_rows, meta_s, ys3, g2, b2, alpha, tm=tok_tile)

    y_p = y_p.reshape(B, S, D)
    y_s = jnp.transpose(y_s.reshape(T, DB, D), (1, 0, 2))
    heads = lambda a, h: jnp.transpose(a.reshape(a.shape[0], h, HEAD_DIM, a.shape[2]), (0, 3, 1, 2))
    wbp = min(max(w for w, _ in DILATED_PATTERNS), S)
    new_wk_p = heads(kT, H)[:, S - wbp:]
    new_wv_p = heads(vT, H)[:, S - wbp:]
    pb = pool_st.shape[1]
    new_pool_p = u[:, S - pb:]
    new_mk_p = heads(kmT, Hc)
    new_mv_p = heads(vmT, Hc)
    new_wk_s = jnp.transpose(nk, (0, 3, 1, 2))
    new_wv_s = jnp.transpose(nv, (0, 3, 1, 2))
    new_pool_s = jnp.transpose(new_pool, (1, 0, 2))
    return (y_p, y_s, new_wk_p, new_wv_p, new_pool_p, new_mk_p, new_mv_p, new_wk_s, new_wv_s, new_pool_s)


def kernel(x_prompt, x_sample, cache_win_k, cache_win_v, state_pool, cache_mem_k, cache_mem_v, mem_prompt, w_in, w_mem_kv, w_pool, pool_scale, w_o, ln1_g, ln1_b, ln2_g, ln2_b, w_r1, b_r1, w_r2, b_r2, w_gate, w_up, w_down):
    depth = w_in.shape[0]
    alpha = (2.0 * depth) ** 0.25
    h_p, h_s = x_prompt, x_sample
    outs = [[] for _ in range(8)]
    for l in range(depth):
        res = _layer(h_p, h_s, cache_win_k[l], cache_win_v[l], state_pool[l], cache_mem_k[l], cache_mem_v[l],
                     mem_prompt, w_in[l], w_mem_kv[l], w_pool[l], pool_scale[l], w_o[l], ln1_g[l], ln1_b[l],
                     ln2_g[l], ln2_b[l], w_r1[l], b_r1[l], w_r2[l], b_r2[l], w_gate[l], w_up[l], w_down[l], alpha)
        h_p, h_s = res[0], res[1]
        for lst, val in zip(outs, res[2:]):
            lst.append(val)
    return (h_p, h_s) + tuple(jnp.stack(o) for o in outs)
```

```python
import functools
import math

import jax
import jax.numpy as jnp
from jax import lax
from jax.experimental import pallas as pl
from jax.experimental.pallas import tpu as pltpu

F32 = jnp.float32
BF16 = jnp.bfloat16
HIGHEST = lax.Precision.HIGHEST
NEG_INF = float("-inf")

HEAD_DIM = 64
DILATED_PATTERNS = ((128, 1), (512, 4), (2048, 16))
BAND_BLOCK = 128
RES = max(d for _, d in DILATED_PATTERNS)
POOL_WINDOWS = (2, 4, 8, 16)
N_EXPERT_GROUPS = 4
EXPERTS_PER_GROUP = 8
N_EXPERTS = N_EXPERT_GROUPS * EXPERTS_PER_GROUP
PAST_LEN = 16384
LN_EPS = 1e-5
ATTN_SCALE = HEAD_DIM ** -0.5

LANES = 128
SUBLANES = 8
VMEM_LIMIT = 56 * 1024 * 1024

ROUTER_OFF = N_EXPERT_GROUPS
ATTN_BATCH = 8
DMA_UNROLL = 8


def _cparams(sem):
    return pltpu.CompilerParams(dimension_semantics=sem, vmem_limit_bytes=VMEM_LIMIT)


def _dot(a, b, precision=None):
    return jnp.dot(a, b, preferred_element_type=F32, precision=precision)


def _dot_nt(a, b, precision=None):
    return lax.dot_general(a, b, (((1,), (1,)), ((), ())), preferred_element_type=F32, precision=precision)


def _memproj_body(mem_ref, wkT_ref, wv_ref, wvT_ref, kmT_ref, vm_ref, vmT_ref):
    m = mem_ref[0].astype(BF16)
    kmT_ref[0] = _dot_nt(wkT_ref[...], m)
    vm_ref[0] = _dot(m, wv_ref[...])
    vmT_ref[0] = _dot_nt(wvT_ref[...], m)


def _memproj(mem, wkT, wv, wvT):
    B, M, D = mem.shape
    C = wv.shape[1]
    full = lambda shape: pl.BlockSpec(shape, lambda b: (0,) * len(shape))
    return pl.pallas_call(
        _memproj_body,
        grid=(B,),
        in_specs=[pl.BlockSpec((1, M, D), lambda b: (b, 0, 0)), full((C, D)), full((D, C)), full((C, D))],
        out_specs=[pl.BlockSpec((1, C, M), lambda b: (b, 0, 0)),
                   pl.BlockSpec((1, M, C), lambda b: (b, 0, 0)),
                   pl.BlockSpec((1, C, M), lambda b: (b, 0, 0))],
        out_shape=[jax.ShapeDtypeStruct((B, C, M), F32), jax.ShapeDtypeStruct((B, M, C), F32),
                   jax.ShapeDtypeStruct((B, C, M), F32)],
        compiler_params=_cparams(("arbitrary",)),
        name="memproj",
    )(mem, wkT, wv, wvT)


def _proj_body(d_a, d_b, x_ref, perm_ref, w_ref, wkvT_ref, q_ref, k_ref, v_ref, u_ref, qc_ref, kT_ref, vT_ref):
    xb = x_ref[0].astype(BF16)
    ts = xb.shape[0]
    xp = _dot(perm_ref[...], xb).astype(BF16)
    y = _dot(xp, w_ref[:, 0:3 * d_a])
    rows = ts // RES
    for r in range(RES):
        q_ref[0, r] = y[r * rows:(r + 1) * rows, 0:d_a]
        k_ref[0, r] = y[r * rows:(r + 1) * rows, d_a:2 * d_a]
        v_ref[0, r] = y[r * rows:(r + 1) * rows, 2 * d_a:3 * d_a]
    y2 = _dot(xb, w_ref[:, 3 * d_a:])
    u_ref[0] = y2[:, 0:d_b]
    qc_ref[0] = y2[:, d_b:]
    yT = _dot_nt(wkvT_ref[...], xb)
    kT_ref[0] = yT[0:d_a]
    vT_ref[0] = yT[d_a:]


def _residue_major_perm(n):
    dst = jnp.arange(n)
    src = RES * (dst % (n // RES)) + dst // (n // RES)
    return (src[:, None] == jnp.arange(n)[None, :]).astype(BF16)


def _proj(x, w_bf, wkvT_bf, d_a, d_b, d_c, ts):
    B, S, D = x.shape
    d_in = w_bf.shape[1]
    assert ts % (RES * SUBLANES) == 0 and S % ts == 0
    row = lambda n: pl.BlockSpec((1, ts, n), lambda b, i: (b, i, 0))
    col = lambda n: pl.BlockSpec((1, n, ts), lambda b, i: (b, 0, i))
    grp = pl.BlockSpec((1, RES, ts // RES, d_a), lambda b, i: (b, 0, i, 0))
    outs = pl.pallas_call(
        functools.partial(_proj_body, d_a, d_b),
        grid=(B, S // ts),
        in_specs=[row(D), pl.BlockSpec((ts, ts), lambda b, i: (0, 0)), pl.BlockSpec((D, d_in), lambda b, i: (0, 0)),
                  pl.BlockSpec((2 * d_a, D), lambda b, i: (0, 0))],
        out_specs=[grp, grp, grp, row(d_b), row(d_c), col(d_a), col(d_a)],
        out_shape=[jax.ShapeDtypeStruct((B, RES, S // RES, d_a), F32)] * 3
        + [jax.ShapeDtypeStruct((B, S, d_b), F32), jax.ShapeDtypeStruct((B, S, d_c), F32)]
        + [jax.ShapeDtypeStruct((B, d_a, S), F32)] * 2,
        compiler_params=_cparams(("arbitrary", "arbitrary")),
        name="proj",
    )(x, _residue_major_perm(ts), w_bf, wkvT_bf)
    return [o.reshape(B, S, d_a) for o in outs[:3]] + list(outs[3:])


def _attn_body(S, slopes_ref, q_ref, k_ref, v_ref, o_ref, opat_ref, lpat_ref):
    hp = pl.program_id(1)
    blk = BAND_BLOCK
    lane = lax.broadcasted_iota(jnp.int32, (blk, LANES), 1)
    head0 = lane < HEAD_DIM

    for pi, (window, dil) in enumerate(DILATED_PATTERNS):
        n_steps = window // dil
        ngrp = RES // dil
        cs = blk // ngrp
        cs_bits = cs.bit_length() - 1
        assert ngrp * dil == RES and cs * ngrp == blk and cs % SUBLANES == 0 and S == RES * blk
        has_prev = ngrp > 1
        nk = 2 * blk if has_prev else blk
        nat = lambda a, ngrp=ngrp, cs=cs, cs_bits=cs_bits: (a & (cs - 1)) * ngrp + (a >> cs_bits)
        qi = lax.broadcasted_iota(jnp.int32, (blk, nk), 0)
        kj = lax.broadcasted_iota(jnp.int32, (blk, nk), 1)
        if has_prev:
            steps = (nat(qi) + blk) - (nat(kj & (blk - 1)) + (kj & blk))
        else:
            steps = qi - kj
        valid = (steps >= 0) & (steps <= n_steps)
        dist = (steps * dil).astype(F32)
        biases = []
        for hh in range(2):
            slope = slopes_ref[2 * hp + hh]
            biases.append(jnp.where(valid, -slope * dist, NEG_INF))
        prev_cols = kj < blk

        ones_k = jnp.ones((ATTN_BATCH, nk, LANES), BF16)

        def blocks(it, carry, dil=dil, ngrp=ngrp, cs=cs, has_prev=has_prev, biases=biases, prev_cols=prev_cols,
                   pi=pi, nk=nk, ones_k=ones_k):
            def chunk(idx, c, back):
                r = idx // ngrp
                j = jnp.maximum(idx % ngrp - back, 0)
                return pl.ds(pl.multiple_of((r + dil * c) * blk + cs * j, cs), cs)

            def load(ref, idx, back=0):
                return jnp.concatenate([ref[0, chunk(idx, c, back), :] for c in range(ngrp)], axis=0)

            qs, ks, vs, firsts = [], [], [], []
            for b in range(ATTN_BATCH):
                idx = it * ATTN_BATCH + b
                qs.append(load(q_ref, idx))
                if has_prev:
                    ks.append(jnp.concatenate([load(k_ref, idx, 1), load(k_ref, idx)], axis=0))
                    vs.append(jnp.concatenate([load(v_ref, idx, 1), load(v_ref, idx)], axis=0))
                    firsts.append(jnp.where(prev_cols & (idx % ngrp == 0), NEG_INF, 0.0))
                else:
                    ks.append(load(k_ref, idx))
                    vs.append(load(v_ref, idx))
            q3 = jnp.stack(qs) * ATTN_SCALE
            k3 = jnp.stack(ks).astype(BF16)
            v3 = jnp.concatenate([jnp.stack(vs).astype(BF16), ones_k], axis=2)
            outs, lses = [], []
            for hh in range(2):
                qm = jnp.where(head0 if hh == 0 else ~head0, q3, 0.0).astype(BF16)
                s = jnp.einsum("bqd,bkd->bqk", qm, k3, preferred_element_type=F32) + biases[hh]
                if has_prev:
                    s = s + jnp.stack(firsts)
                m = jnp.max(s, axis=2, keepdims=True)
                p = jnp.exp(s - m).astype(BF16)
                o = jnp.einsum("bqk,bkd->bqd", p, v3, preferred_element_type=F32)
                den = o[:, :, LANES:]
                outs.append(o[:, :, :LANES] / den)
                lses.append(m + jnp.log(den))
            o_pair = jnp.where(head0, outs[0], outs[1])
            l_pair = jnp.where(head0, lses[0], lses[1])
            for b in range(ATTN_BATCH):
                idx = it * ATTN_BATCH + b
                for c in range(ngrp):
                    opat_ref[pi, chunk(idx, c, 0), :] = o_pair[b, c * cs:(c + 1) * cs]
                    lpat_ref[pi, chunk(idx, c, 0), :] = l_pair[b, c * cs:(c + 1) * cs]
            return carry

        lax.fori_loop(0, RES // ATTN_BATCH, blocks, 0)

    chunk = 256

    def mix(c, carry):
        rows = pl.ds(pl.multiple_of(c * chunk, chunk), chunk)
        l0, l1, l2 = lpat_ref[0, rows, :], lpat_ref[1, rows, :], lpat_ref[2, rows, :]
        mx = jnp.maximum(jnp.maximum(l0, l1), l2)
        w0, w1, w2 = jnp.exp(l0 - mx), jnp.exp(l1 - mx), jnp.exp(l2 - mx)
        num = w0 * opat_ref[0, rows, :] + w1 * opat_ref[1, rows, :] + w2 * opat_ref[2, rows, :]
        o_ref[0, rows, :] = (num / (w0 + w1 + w2)).astype(o_ref.dtype)
        return carry

    lax.fori_loop(0, S // chunk, mix, 0)


def _attn(q, k, v, slopes):
    B, S, d_a = q.shape
    assert S % (BAND_BLOCK * max(d for _, d in DILATED_PATTERNS)) == 0
    npair = d_a // LANES
    spec = pl.BlockSpec((1, S, LANES), lambda b, h, *_: (b, 0, h))
    grid_spec = pltpu.PrefetchScalarGridSpec(
        num_scalar_prefetch=0,
        grid=(B, npair),
        in_specs=[pl.BlockSpec(memory_space=pltpu.SMEM), spec, spec, spec],
        out_specs=spec,
        scratch_shapes=[pltpu.VMEM((len(DILATED_PATTERNS), S, LANES), F32),
                        pltpu.VMEM((len(DILATED_PATTERNS), S, LANES), F32)],
    )
    return pl.pallas_call(
        functools.partial(_attn_body, S),
        grid_spec=grid_spec,
        out_shape=jax.ShapeDtypeStruct((B, S, d_a), BF16),
        compiler_params=_cparams(("arbitrary", "arbitrary")),
        name="dilated_attn",
    )(slopes, q, k, v)


def _layer_norm(z, g, b):
    mu = jnp.mean(z, axis=-1, keepdims=True)
    zc = z - mu
    var = jnp.mean(zc * zc, axis=-1, keepdims=True)
    return zc * lax.rsqrt(var + LN_EPS) * g + b


def _route(logits):
    n = logits.shape[0]
    lane = lax.broadcasted_iota(jnp.int32, (n, LANES), 1)
    lane_f = lane.astype(F32)
    big = float(LANES)
    is_outer = lane < N_EXPERT_GROUPS
    l1 = jnp.where(is_outer, logits, NEG_INF)
    m1 = jnp.max(l1, axis=1, keepdims=True)
    g_sel = jnp.min(jnp.where(l1 == m1, lane_f, big), axis=1, keepdims=True)
    v1 = 1.0 / jnp.sum(jnp.exp(l1 - m1), axis=1, keepdims=True)
    lo = ROUTER_OFF + g_sel * EXPERTS_PER_GROUP
    in_group = (lane_f >= lo) & (lane_f < lo + EXPERTS_PER_GROUP)
    l2 = jnp.where(in_group, logits, NEG_INF)
    ma = jnp.max(l2, axis=1, keepdims=True)
    ia = jnp.min(jnp.where(l2 == ma, lane_f, big), axis=1, keepdims=True)
    l2b = jnp.where(lane_f == ia, NEG_INF, l2)
    mb = jnp.max(l2b, axis=1, keepdims=True)
    ib = jnp.min(jnp.where(l2b == mb, lane_f, big), axis=1, keepdims=True)
    eb = jnp.exp(mb - ma)
    wa = 1.0 / (1.0 + eb)
    wb = eb / (1.0 + eb)
    return ia - ROUTER_OFF, ib - ROUTER_OFF, v1 * wa, v1 * wb


def _rank_and_meta(e0, e1, g0, g1, carry):
    n = e0.shape[0]
    lane_f = lax.broadcasted_iota(jnp.int32, (n, LANES), 1).astype(F32)
    oh0 = (lane_f == e0).astype(F32)
    oh1 = (lane_f == e1).astype(F32)
    both = oh0 + oh1
    ti = lax.broadcasted_iota(jnp.int32, (n, n), 0)
    tj = lax.broadcasted_iota(jnp.int32, (n, n), 1)
    tri = (tj < ti).astype(BF16)
    before = _dot(tri, both.astype(BF16)) + carry
    r0 = jnp.sum(before * oh0, axis=1, keepdims=True)
    r1 = jnp.sum(before * oh1, axis=1, keepdims=True)
    new_carry = carry + jnp.sum(both, axis=0, keepdims=True)
    lane = lax.broadcasted_iota(jnp.int32, (n, LANES), 1)
    meta = jnp.zeros((n, LANES), F32)
    for i, val in enumerate((e0, e1, g0, g1, r0, r1)):
        meta = jnp.where(lane == i, val, meta)
    return meta, new_carry


def _lane_group(shape, width):
    lane = lax.broadcasted_iota(jnp.int32, shape, len(shape) - 1)
    grp = jnp.zeros(shape, jnp.int32)
    for g in range(1, shape[-1] // width):
        grp = grp + (lane >= g * width).astype(jnp.int32)
    return grp


def _select_by_group(grp, vals):
    out = vals[-1]
    for g in range(len(vals) - 2, -1, -1):
        out = jnp.where(grp == g, vals[g], out)
    return out


def _store_token_rows(ref, val):
    n, d = val.shape
    nchunk = d // LANES
    for c in range(nchunk):
        ref[pl.ds(c, n, stride=nchunk), :] = val[:, c * LANES:(c + 1) * LANES]


def _load_token_rows(ref, n, nchunk):
    return jnp.concatenate([ref[pl.ds(c, n, stride=nchunk), :] for c in range(nchunk)], axis=1)


HALO = 16


def _mix_body(tm, nt, n_tiles, d_b, alpha, x_ref, at_ref, unperm_ref, u_ref, uh_ref, qc_ref, kmT_ref, vm_ref,
              wo_ref, wp_ref, ps_ref, g1_ref, b1_ref, wr_ref, br_ref, cin_ref, hs_ref,
              h1_ref, meta_ref, cnt_ref, carry_ref):
    g = pl.program_id(0)

    @pl.when(g == 0)
    def _():
        carry_ref[...] = cin_ref[...]

    @pl.when(g == n_tiles)
    def _():
        ns_rows = hs_ref.shape[0]
        h1_ref[0:ns_rows, :] = hs_ref[...]
        h1_ref[ns_rows:, :] = jnp.zeros((h1_ref.shape[0] - ns_rows, LANES), F32)

    @pl.when(g < n_tiles)
    def _():
        _mix_tile(tm, g % nt, d_b, alpha, x_ref, at_ref, unperm_ref, u_ref, uh_ref, qc_ref, kmT_ref, vm_ref, wo_ref,
                  wp_ref, ps_ref, g1_ref, b1_ref, wr_ref, br_ref, h1_ref, meta_ref, cnt_ref, carry_ref)


def _mix_tile(tm, i, d_b, alpha, x_ref, at_ref, unperm_ref, u_ref, uh_ref, qc_ref, kmT_ref, vm_ref, wo_ref, wp_ref,
              ps_ref, g1_ref, b1_ref, wr_ref, br_ref, h1_ref, meta_ref, cnt_ref, carry_ref):
    u = u_ref[0]
    halo = jnp.where(i == 0, 0.0, uh_ref[0])
    ext = jnp.concatenate([halo, halo, u], axis=0)
    s2 = ext + pltpu.roll(ext, 1, 0)
    s4 = s2 + pltpu.roll(s2, 2, 0)
    s8 = s4 + pltpu.roll(s4, 4, 0)
    s16 = s8 + pltpu.roll(s8, 8, 0)
    grp = _lane_group((tm, d_b), d_b // len(POOL_WINDOWS))
    sums = [s[2 * HALO:] for s in (s2, s4, s8, s16)]
    win = _select_by_group(grp, sums)
    wlen = _select_by_group(grp, [jnp.int32(w) for w in POOL_WINDOWS])
    pos = i * tm + lax.broadcasted_iota(jnp.int32, (tm, d_b), 0)
    cnt = jnp.minimum(wlen, pos + 1).astype(F32)
    diff = win / cnt - u
    pool = _dot(diff.astype(BF16), wp_ref[...]) * ps_ref[...]

    qc = qc_ref[0] * ATTN_SCALE
    d_c = qc.shape[1]
    kmT = kmT_ref[0].astype(BF16)
    vm = vm_ref[0].astype(BF16)
    hl = _lane_group((tm, d_c), HEAD_DIM)
    memo = jnp.zeros((tm, d_c), F32)
    for h in range(d_c // HEAD_DIM):
        qm = jnp.where(hl == h, qc, 0.0).astype(BF16)
        s = _dot(qm, kmT)
        p = jnp.exp(s - jnp.max(s, axis=1, keepdims=True))
        den = jnp.sum(p, axis=1, keepdims=True)
        o = _dot(p.astype(BF16), vm) / den
        memo = jnp.where(hl == h, o, memo)

    at_rl = jnp.concatenate([at_ref[0, r] for r in range(RES)], axis=0)
    attn = _dot(unperm_ref[...], at_rl).astype(BF16)
    cat = jnp.concatenate([attn, pool.astype(BF16), memo.astype(BF16)], axis=1)
    mixv = _dot(cat, wo_ref[...])
    h1 = _layer_norm(alpha * x_ref[0] + mixv, g1_ref[...], b1_ref[...])
    _store_token_rows(h1_ref, h1)

    logits = _dot(h1, wr_ref[...], precision=HIGHEST) + br_ref[...]
    e0, e1, g0, g1 = _route(logits)
    meta, new_carry = _rank_and_meta(e0, e1, g0, g1, carry_ref[0:1, :])
    meta_ref[...] = meta
    carry_ref[...] = jnp.broadcast_to(new_carry, carry_ref.shape)
    cnt_ref[...] = carry_ref[...]


def _mix(x, attn_rm, u, qc, kmT, vm, wo_bf, wp_bd_bf, pool_scale, g1, b1, wr, br, counts_in, h1s, alpha, tm):
    B, S, D = x.shape
    d_a, d_b, d_c = attn_rm.shape[2], u.shape[2], qc.shape[2]
    n_mem = vm.shape[1]
    nt = S // tm
    n_tiles = B * nt
    nchunk = D // LANES
    assert tm % (RES * SUBLANES * 2) == 0 and h1s.shape[0] < tm * nchunk
    last = n_tiles - 1
    bi = lambda g: (jnp.minimum(g, last) // nt, jnp.minimum(g, last) % nt)
    row = lambda n: pl.BlockSpec((1, tm, n), lambda g: bi(g) + (0,))
    full = lambda shape: pl.BlockSpec(shape, lambda g: (0,) * len(shape))
    per_b = lambda shape: pl.BlockSpec((1,) + shape, lambda g: (bi(g)[0], 0, 0))
    halo_spec = pl.BlockSpec((1, HALO, d_b), lambda g: (bi(g)[0], jnp.maximum(bi(g)[1] * (tm // HALO) - 1, 0), 0))
    at_spec = pl.BlockSpec((1, RES, tm // RES, d_a), lambda g: (bi(g)[0], 0, bi(g)[1], 0))
    dst = jnp.arange(tm)
    unperm = ((dst % RES) * (tm // RES) + dst // RES)[:, None] == jnp.arange(tm)[None, :]
    return pl.pallas_call(
        functools.partial(_mix_body, tm, nt, n_tiles, d_b, alpha),
        grid=(n_tiles + 1,),
        in_specs=[row(D), at_spec, full((tm, tm)), row(d_b), halo_spec, row(d_c), per_b((d_c, n_mem)),
                  per_b((n_mem, d_c)), full((D, D)), full((d_b, d_b)), full((1, d_b)), full((1, D)), full((1, D)),
                  full((D, LANES)), full((1, LANES)), full((SUBLANES, LANES)), full(h1s.shape)],
        out_specs=[pl.BlockSpec((tm * nchunk, LANES), lambda g: (g, 0)),
                   pl.BlockSpec((tm, LANES), lambda g: (jnp.minimum(g, last), 0)),
                   pl.BlockSpec((SUBLANES, LANES), lambda g: (0, 0))],
        out_shape=[jax.ShapeDtypeStruct(((n_tiles + 1) * tm * nchunk, LANES), F32),
                   jax.ShapeDtypeStruct((B * S, LANES), F32),
                   jax.ShapeDtypeStruct((SUBLANES, LANES), F32)],
        scratch_shapes=[pltpu.VMEM((SUBLANES, LANES), F32)],
        compiler_params=_cparams(("arbitrary",)),
        name="mix_ln1_router",
    )(x, attn_rm.reshape(B, RES, S // RES, d_a), unperm.astype(BF16), u, u, qc, kmT, vm, wo_bf, wp_bd_bf,
      pool_scale, g1, b1, wr, br, counts_in, h1s)


def _columns_to_lanes(cols, rows):
    lane = lax.broadcasted_iota(jnp.int32, (rows, LANES), 1)
    tile = jnp.zeros((rows, LANES), F32)
    for t, c in enumerate(cols):
        tile = jnp.where(lane == t, c, tile)
    return tile


def _sproj_body(x_ref, wT_ref, o_ref):
    o_ref[...] = _dot_nt(wT_ref[...], x_ref[...], precision=HIGHEST)


def _sproj(x, wT):
    vm = pl.BlockSpec(memory_space=pltpu.VMEM)
    return pl.pallas_call(
        _sproj_body, in_specs=[vm, vm], out_specs=vm,
        out_shape=jax.ShapeDtypeStruct((wT.shape[0], x.shape[0]), F32),
        compiler_params=pltpu.CompilerParams(vmem_limit_bytes=VMEM_LIMIT),
        name="decode_proj",
    )(x, wT)


def _decode_body(T, wb, slopes_ref, pT_ref, ck_ref, cv_ref, mk_ref, mv_ref,
                 nk_ref, nv_ref, at_ref, mo_ref, qkv_ref):
    H = ck_ref.shape[1]
    d_a = H * HEAD_DIM
    ntok = pT_ref.shape[1]
    shift = (ntok - T * pl.program_id(0)) % ntok
    qkv_ref[...] = pltpu.roll(pT_ref[...], shift, 1)

    def multiplicity(dist):
        mult = jnp.zeros(dist.shape, F32)
        for window, dil in DILATED_PATTERNS:
            assert dil & (dil - 1) == 0
            mult = mult + (((dist & (dil - 1)) == 0) & (dist <= window) & (dist >= 0)).astype(F32)
        return mult

    OWN = min(d for _, d in DILATED_PATTERNS if d > 1)
    far = wb - LANES
    assert T <= OWN and far % OWN == 0 and wb % OWN == 0
    assert all(w <= LANES for w, d in DILATED_PATTERNS if d == 1) and all(d % OWN == 0 for _, d in DILATED_PATTERNS if d > 1)
    lane_far = lax.broadcasted_iota(jnp.int32, (T, far), 1)
    t_far = lax.broadcasted_iota(jnp.int32, (T, far), 0)
    owned = (lane_far & (OWN - 1)) == t_far
    dist_far = wb + t_far - lane_far
    mult_far = jnp.where(owned, multiplicity(dist_far), 0.0)
    dist_far = dist_far.astype(F32)
    t_near = lax.broadcasted_iota(jnp.int32, (T, LANES), 0)
    dist_near = wb + t_near - (far + lax.broadcasted_iota(jnp.int32, (T, LANES), 1))
    mult_near = multiplicity(dist_near)
    dist_near = dist_near.astype(F32)
    dist_new = lax.broadcasted_iota(jnp.int32, (T, T), 0) - lax.broadcasted_iota(jnp.int32, (T, T), 1)
    mult_new = multiplicity(dist_new)
    dist_new = jnp.maximum(dist_new, 0).astype(F32)

    lane_t = lax.broadcasted_iota(jnp.int32, (HEAD_DIM, LANES), 1)
    own_t = lane_t & (OWN - 1)
    last = wb - LANES
    n_far = far // LANES

    def head(h, carry):
        slope = slopes_ref[h]
        kTc = ck_ref[0, h]
        vTc = cv_ref[0, h]
        r0 = pl.multiple_of(h * HEAD_DIM, HEAD_DIM)
        qT = qkv_ref[pl.ds(r0, HEAD_DIM), :][:, 0:T] * ATTN_SCALE
        kTn = qkv_ref[pl.ds(d_a + r0, HEAD_DIM), :][:, 0:T]
        vTn = qkv_ref[pl.ds(2 * d_a + r0, HEAD_DIM), :][:, 0:T]
        tile = lambda a, c: a[:, c * LANES:(c + 1) * LANES]

        qb = [jnp.broadcast_to(qT[:, t:t + 1], (HEAD_DIM, LANES)) for t in range(T)]
        qpat = qb[T - 1]
        for t in range(T - 1):
            qpat = jnp.where(own_t == t, qb[t], qpat)
        s_far = jnp.concatenate([jnp.sum(tile(kTc, c) * qpat, axis=0, keepdims=True) for c in range(n_far)], axis=1)
        s_far = jnp.where(mult_far > 0, s_far - slope * dist_far, NEG_INF)
        k_near = tile(kTc, n_far)
        s_near = jnp.concatenate([jnp.sum(k_near * qb[t], axis=0, keepdims=True) for t in range(T)], axis=0)
        s_near = jnp.where(mult_near > 0, s_near - slope * dist_near, NEG_INF)
        s_new = jnp.concatenate([jnp.sum(kTn * qT[:, t:t + 1], axis=0, keepdims=True) for t in range(T)], axis=0)
        s_new = jnp.where(mult_new > 0, s_new - slope * dist_new, NEG_INF)
        rmax = lambda a: jnp.max(a, axis=1, keepdims=True)
        m = jnp.maximum(jnp.maximum(rmax(s_far), rmax(s_near)), rmax(s_new))
        p_far = mult_far * jnp.exp(s_far - m)
        p_near = mult_near * jnp.exp(s_near - m)
        p_new = mult_new * jnp.exp(s_new - m)
        rsum = lambda a: jnp.sum(a, axis=1, keepdims=True)
        den = rsum(p_far) + rsum(p_near) + rsum(p_new)

        p_comb = jnp.sum(p_far, axis=0, keepdims=True)
        acc = tile(vTc, 0) * tile(p_comb, 0)
        for c in range(1, n_far):
            acc = acc + tile(vTc, c) * tile(p_comb, c)
        v_near = tile(vTc, n_far)
        cols = []
        for t in range(T):
            o = (jnp.sum(jnp.where(own_t == t, acc, 0.0) + v_near * p_near[t:t + 1], axis=1, keepdims=True)
                 + jnp.sum(vTn * p_new[t:t + 1], axis=1, keepdims=True))
            cols.append(o / den[t:t + 1])
        at_ref[0, pl.ds(r0, HEAD_DIM), :] = _columns_to_lanes(cols, HEAD_DIM)

        rk = pltpu.roll(kTc, wb - T, 1)
        rv = pltpu.roll(vTc, wb - T, 1)
        nk_ref[0, h] = rk
        nv_ref[0, h] = rv
        newk = jnp.zeros((HEAD_DIM, LANES), F32)
        newv = jnp.zeros((HEAD_DIM, LANES), F32)
        for t in range(T):
            newk = jnp.where(lane_t == LANES - T + t, kTn[:, t:t + 1], newk)
            newv = jnp.where(lane_t == LANES - T + t, vTn[:, t:t + 1], newv)
        nk_ref[0, h, :, last:] = jnp.where(lane_t >= LANES - T, newk, rk[:, last:])
        nv_ref[0, h, :, last:] = jnp.where(lane_t >= LANES - T, newv, rv[:, last:])
        return carry

    lax.fori_loop(0, H, head, 0, unroll=2)

    Hc = mk_ref.shape[1]
    rows = []
    for h in range(Hc):
        kT = mk_ref[0, h]
        r0 = h * HEAD_DIM
        qT = qkv_ref[3 * d_a + r0:3 * d_a + r0 + HEAD_DIM, :][:, 0:T] * ATTN_SCALE
        rows += [jnp.sum(kT * qT[:, t:t + 1], axis=0, keepdims=True) for t in range(T)]
    s = jnp.concatenate(rows, axis=0)
    p = jnp.exp(s - jnp.max(s, axis=1, keepdims=True))
    p = p / jnp.sum(p, axis=1, keepdims=True)
    for h in range(Hc):
        vT = mv_ref[0, h]
        cols = [jnp.sum(vT * p[h * T + t:h * T + t + 1], axis=1, keepdims=True) for t in range(T)]
        mo_ref[0, h * HEAD_DIM:(h + 1) * HEAD_DIM, :] = _columns_to_lanes(cols, HEAD_DIM)


def _decode(projT, T, ck, cv, mk, mv, slopes):
    DB, H, hd, wb = ck.shape
    _, Hc, _, n_mem = mk.shape
    d_a, d_c = H * hd, Hc * hd
    assert wb >= max(w for w, _ in DILATED_PATTERNS) and wb % LANES == 0 and T <= LANES
    assert projT.shape == (3 * d_a + d_c, DB * T)
    cache = pl.BlockSpec((1, H, hd, wb), lambda b, *_: (b, 0, 0, 0))
    memc = pl.BlockSpec((1, Hc, hd, n_mem), lambda b, *_: (b, 0, 0, 0))
    grid_spec = pltpu.PrefetchScalarGridSpec(
        num_scalar_prefetch=0,
        grid=(DB,),
        in_specs=[pl.BlockSpec(memory_space=pltpu.SMEM),
                  pl.BlockSpec(projT.shape, lambda b, *_: (0, 0)),
                  cache, cache, memc, memc],
        out_specs=[cache, cache,
                   pl.BlockSpec((1, d_a, LANES), lambda b, *_: (b, 0, 0)),
                   pl.BlockSpec((1, d_c, LANES), lambda b, *_: (b, 0, 0))],
        scratch_shapes=[pltpu.VMEM(projT.shape, F32)],
    )
    nk, nv, at, mo = pl.pallas_call(
        functools.partial(_decode_body, T, wb),
        grid_spec=grid_spec,
        out_shape=[jax.ShapeDtypeStruct(ck.shape, F32), jax.ShapeDtypeStruct(cv.shape, F32),
                   jax.ShapeDtypeStruct((DB, d_a, LANES), F32), jax.ShapeDtypeStruct((DB, d_c, LANES), F32)],
        compiler_params=_cparams(("arbitrary",)),
        name="decode_attn_cache",
    )(slopes, projT, ck, cv, mk, mv)
    return nk, nv, at[:, :, :T], mo[:, :, :T]


def _smix_body(T, alpha, pos0, x_ref, at_ref, mo_ref, st_ref, wu_ref, wo_ref, wp_ref, ps_ref, g1_ref, b1_ref,
               wr_ref, br_ref, cin_ref, h1_ref, meta_ref, cnt_ref, pool_ref):
    n = x_ref.shape[0]
    db = n // T
    pb = st_ref.shape[0]
    d_b = st_ref.shape[2]
    x = x_ref[...]
    u_new = _dot(x, wu_ref[...], precision=HIGHEST)
    seq = [st_ref[j] for j in range(pb)] + [u_new[t * db:(t + 1) * db] for t in range(T)]
    for j in range(pb):
        pool_ref[j] = seq[j + T]
    grp = _lane_group((db, d_b), d_b // len(POOL_WINDOWS))
    diffs = []
    for t in range(T):
        j = pb + t
        per_w = []
        for w in POOL_WINDOWS:
            acc = seq[j]
            for back in range(1, w):
                if j - back >= 0:
                    acc = acc + seq[j - back]
            per_w.append(acc / float(min(w, pos0 + j + 1)))
        diffs.append(_select_by_group(grp, per_w) - seq[j])
    diff = jnp.concatenate(diffs, axis=0)
    pool = _dot(diff, wp_ref[...], precision=HIGHEST) * ps_ref[...]
    cat = jnp.concatenate([at_ref[...], pool, mo_ref[...]], axis=1)
    mixv = _dot(cat, wo_ref[...], precision=HIGHEST)
    h1 = _layer_norm(alpha * x + mixv, g1_ref[...], b1_ref[...])
    _store_token_rows(h1_ref, h1)
    logits = _dot(h1, wr_ref[...], precision=HIGHEST) + br_ref[...]
    e0, e1, g0, g1 = _route(logits)
    meta, new_carry = _rank_and_meta(e0, e1, g0, g1, cin_ref[0:1, :])
    meta_ref[...] = meta
    cnt_ref[...] = jnp.broadcast_to(new_carry, cnt_ref.shape)


def _smix(x_tb, attn_tb, memo_tb, state, wu, wo, wp_bd, pool_scale, g1, b1, wr, br, counts_in, alpha, T):
    n, D = x_tb.shape
    pb, db, d_b = state.shape
    nchunk = D // LANES
    vm = pl.BlockSpec(memory_space=pltpu.VMEM)
    return pl.pallas_call(
        functools.partial(_smix_body, T, alpha, PAST_LEN - pb),
        in_specs=[vm] * 13,
        out_specs=[vm] * 4,
        out_shape=[jax.ShapeDtypeStruct((n * nchunk, LANES), F32), jax.ShapeDtypeStruct((n, LANES), F32),
                   jax.ShapeDtypeStruct((SUBLANES, LANES), F32), jax.ShapeDtypeStruct((pb, db, d_b), F32)],
        compiler_params=pltpu.CompilerParams(vmem_limit_bytes=VMEM_LIMIT),
        name="decode_mix_ln1_router",
    )(x_tb, attn_tb, memo_tb, state, wu, wo, wp_bd, pool_scale, g1, b1, wr, br, counts_in)


def _slot_map_body(blk, pos_ref, fill_ref, inv_ref, sem):
    g = pl.program_id(0)

    @pl.when(g == 0)
    def _():
        c = pltpu.make_async_copy(fill_ref, inv_ref, sem)
        c.start()
        c.wait()

    base = g * blk

    def body(i, carry):
        inv_ref[pos_ref[0, 0, i]] = (base + i) >> 1
        return carry

    lax.fori_loop(0, blk, body, 0, unroll=DMA_UNROLL)


def _slot_map(pos, n_slots, zero_token, blk):
    n2 = pos.shape[0]
    assert n2 % blk == 0 and n_slots % 1024 == 0
    return pl.pallas_call(
        functools.partial(_slot_map_body, blk),
        grid=(n2 // blk,),
        in_specs=[pl.BlockSpec((1, 1, blk), lambda g: (g, 0, 0), memory_space=pltpu.SMEM),
                  pl.BlockSpec(memory_space=pl.ANY)],
        out_specs=pl.BlockSpec(memory_space=pltpu.SMEM),
        out_shape=jax.ShapeDtypeStruct((n_slots,), jnp.int32),
        scratch_shapes=[pltpu.SemaphoreType.DMA(())],
        compiler_params=_cparams(("arbitrary",)),
        name="slot_map",
    )(pos.reshape(n2 // blk, 1, blk), jnp.full((n_slots,), zero_token, jnp.int32))


def _expert_body(tmx, nchunk, te_ref, nt_ref, tv_ref, cur_ref, nxt_ref, h_ref, wg_ref, wu_ref, wd_ref, ys_ref,
                 buf_ref, sems):
    i = pl.program_id(0)
    n_tiles = nt_ref[0]

    def gather(idx_ref, slot, tile, wait):
        groups = (tv_ref[tile] + DMA_UNROLL - 1) // DMA_UNROLL

        def body(g, carry):
            for u in range(DMA_UNROLL):
                j = g * DMA_UNROLL + u
                dst = buf_ref.at[slot, pl.ds(pl.multiple_of(j * nchunk, nchunk), nchunk)]
                copy = pltpu.make_async_copy(h_ref.at[idx_ref[0, 0, j]], dst, sems.at[slot])
                if wait:
                    copy.wait()
                else:
                    copy.start(priority=u % 2)
            return carry

        lax.fori_loop(0, groups, body, 0)

    @pl.when(i == 0)
    def _():
        buf_ref[...] = jnp.zeros_like(buf_ref)
        gather(cur_ref, 0, 0, False)

    @pl.when(i < n_tiles)
    def _():
        slot = i % 2
        gather(cur_ref, slot, i, True)

        @pl.when(i + 1 < n_tiles)
        def _():
            gather(nxt_ref, 1 - slot, i + 1, False)

        x = _load_token_rows(buf_ref.at[slot], tmx, nchunk).astype(BF16)
        hg = _dot(x, wg_ref[0].astype(BF16))
        hu = _dot(x, wu_ref[0].astype(BF16))
        a = (hg * jax.nn.sigmoid(hg) * hu).astype(BF16)
        y = _dot(a, wd_ref[0].astype(BF16))
        _store_token_rows(ys_ref, y)

    @pl.when(i >= n_tiles)
    def _():
        ys_ref[...] = jnp.zeros_like(ys_ref)


def _experts(tile_expert, n_tiles, tile_valid, inv, h_rows, wg, wu, wd, tmx, max_tiles):
    E, D, F = wg.shape
    nchunk = D // LANES
    inv3 = inv.reshape(-1, 1, tmx)
    last_blk = inv3.shape[0] - 1
    assert inv3.shape[0] >= max_tiles and tmx % DMA_UNROLL == 0
    grid_spec = pltpu.PrefetchScalarGridSpec(
        num_scalar_prefetch=3,
        grid=(max_tiles,),
        in_specs=[pl.BlockSpec((1, 1, tmx), lambda i, *_: (i, 0, 0), memory_space=pltpu.SMEM),
                  pl.BlockSpec((1, 1, tmx), lambda i, *_: (jnp.minimum(i + 1, last_blk), 0, 0),
                               memory_space=pltpu.SMEM),
                  pl.BlockSpec(memory_space=pl.ANY),
                  pl.BlockSpec((1, D, F), lambda i, te, *_: (te[i], 0, 0)),
                  pl.BlockSpec((1, D, F), lambda i, te, *_: (te[i], 0, 0)),
                  pl.BlockSpec((1, F, D), lambda i, te, *_: (te[i], 0, 0))],
        out_specs=pl.BlockSpec((tmx * nchunk, LANES), lambda i, *_: (i, 0)),
        scratch_shapes=[pltpu.VMEM((2, tmx * nchunk, LANES), F32), pltpu.SemaphoreType.DMA((2,))],
    )
    return pl.pallas_call(
        functools.partial(_expert_body, tmx, nchunk),
        grid_spec=grid_spec,
        out_shape=jax.ShapeDtypeStruct((max_tiles * tmx * nchunk, LANES), F32),
        compiler_params=_cparams(("arbitrary",)),
        name="expert_swiglu",
    )(tile_expert, n_tiles, tile_valid, inv3, inv3, h_rows, wg, wu, wd)


def _combine_body(tm, nchunk, alpha, pos_ref, h1_ref, meta_ref, ys_ref, g2_ref, b2_ref, o_ref, buf_ref, sem):
    def row_copy(src_row, k, t):
        dst = buf_ref.at[k, pl.ds(pl.multiple_of(t * nchunk, nchunk), nchunk)]
        return pltpu.make_async_copy(ys_ref.at[src_row], dst, sem)

    def start(t, carry):
        for k in range(2):
            row_copy(pos_ref[0, 0, 2 * t + k], k, t).start(priority=k)
        return carry

    lax.fori_loop(0, tm, start, 0)

    def wait(t, carry):
        for k in range(2):
            row_copy(0, k, t).wait()
        return carry

    lax.fori_loop(0, tm, wait, 0)

    h1 = _load_token_rows(h1_ref, tm, nchunk)
    y0 = _load_token_rows(buf_ref.at[0], tm, nchunk)
    y1 = _load_token_rows(buf_ref.at[1], tm, nchunk)
    meta = meta_ref[...]
    lane = lax.broadcasted_iota(jnp.int32, meta.shape, 1)
    gate0 = jnp.sum(jnp.where(lane == 2, meta, 0.0), axis=1, keepdims=True)
    gate1 = jnp.sum(jnp.where(lane == 3, meta, 0.0), axis=1, keepdims=True)
    f = gate0 * y0 + gate1 * y1
    o_ref[...] = _layer_norm(alpha * h1 + f, g2_ref[...], b2_ref[...])


def _combine(pos, tok0, n, h1_flat, meta, ys, g2, b2, alpha, tm):
    D = g2.shape[1]
    nchunk = D // LANES
    assert n % tm == 0 and tok0 % tm == 0
    pos3 = pos.reshape(-1, 1, 2 * tm)
    off = tok0 // tm
    return pl.pallas_call(
        functools.partial(_combine_body, tm, nchunk, alpha),
        grid=(n // tm,),
        in_specs=[pl.BlockSpec((1, 1, 2 * tm), lambda i: (i + off, 0, 0), memory_space=pltpu.SMEM),
                  pl.BlockSpec((tm * nchunk, LANES), lambda i: (i + off, 0)),
                  pl.BlockSpec((tm, LANES), lambda i: (i, 0)),
                  pl.BlockSpec(memory_space=pl.ANY),
                  pl.BlockSpec((1, D), lambda i: (0, 0)), pl.BlockSpec((1, D), lambda i: (0, 0))],
        out_specs=pl.BlockSpec((tm, D), lambda i: (i, 0)),
        out_shape=jax.ShapeDtypeStruct((n, D), F32),
        scratch_shapes=[pltpu.VMEM((2, tm * nchunk, LANES), F32), pltpu.SemaphoreType.DMA(())],
        compiler_params=_cparams(("arbitrary",)),
        name="combine_ln2",
    )(pos3, h1_flat, meta, ys, g2, b2)


def _block_diag(w):
    g, a, b = w.shape
    eye = jnp.eye(g, dtype=w.dtype)
    return (eye[:, None, :, None] * w[:, :, None, :]).reshape(g * a, g * b)


def _layer(h_p, h_s, win_k, win_v, pool_st, mem_k, mem_v, mem_prompt,
           w_in, w_mem_kv, w_pool, pool_scale, w_o, ln1_g, ln1_b, ln2_g, ln2_b,
           w_r1, b_r1, w_r2, b_r2, w_gate, w_up, w_down, alpha):
    B, S, D = h_p.shape
    DB, T, _ = h_s.shape
    H = win_k.shape[2]
    Hc = mem_k.shape[2]
    d_a, d_c = H * HEAD_DIM, Hc * HEAD_DIM
    d_b = pool_st.shape[2]
    nchunk = D // LANES
    slopes = 2.0 ** (-8.0 * jnp.arange(1, H + 1, dtype=F32) / H)

    w_in_bf = w_in.astype(BF16)
    w_inT = w_in.T
    wkvT_bf = w_inT[d_a:3 * d_a].astype(BF16)
    wp_bd = _block_diag(w_pool)
    ps = pool_scale.reshape(1, d_b)
    g1, b1 = ln1_g.reshape(1, D), ln1_b.reshape(1, D)
    g2, b2 = ln2_g.reshape(1, D), ln2_b.reshape(1, D)
    n_r = N_EXPERT_GROUPS + N_EXPERTS
    wr = jnp.concatenate([w_r1, jnp.transpose(w_r2, (1, 0, 2)).reshape(D, N_EXPERTS),
                          jnp.zeros((D, LANES - n_r), F32)], axis=1)
    br = jnp.concatenate([b_r1, b_r2.reshape(-1), jnp.zeros((LANES - n_r,), F32)]).reshape(1, LANES)

    ck = jnp.transpose(win_k, (0, 2, 3, 1))
    cv = jnp.transpose(win_v, (0, 2, 3, 1))
    mk = jnp.transpose(mem_k, (0, 2, 3, 1))
    mv = jnp.transpose(mem_v, (0, 2, 3, 1))
    w_qkvqcT = jnp.concatenate([w_inT[:3 * d_a], w_inT[3 * d_a + d_b:]], axis=0)
    projT = _sproj(h_s.reshape(DB * T, D), w_qkvqcT)
    nk, nv, attn_sT, memo_sT = _decode(projT, T, ck, cv, mk, mv, slopes)
    to_tb = lambda a: jnp.transpose(a, (2, 0, 1)).reshape(T * DB, a.shape[1])
    x_tb = jnp.transpose(h_s, (1, 0, 2)).reshape(T * DB, D)
    state = jnp.transpose(pool_st, (1, 0, 2))
    h1s, meta_s, cnt_s, new_pool = _smix(x_tb, to_tb(attn_sT), to_tb(memo_sT), state,
                                         w_in[:, 3 * d_a:3 * d_a + d_b], w_o, wp_bd, ps, g1, b1, wr, br,
                                         jnp.zeros((SUBLANES, LANES), F32), alpha, T)

    w_memT = w_mem_kv.T
    kmT, vm, vmT = _memproj(mem_prompt, w_memT[:d_c].astype(BF16), w_mem_kv[:, d_c:].astype(BF16),
                            w_memT[d_c:].astype(BF16))
    q, k, v, u, qc, kT, vT = _proj(h_p, w_in_bf, wkvT_bf, d_a, d_b, d_c, ts=512)
    attn = _attn(q, k, v, slopes)
    h_rows, meta_p, cnt_all = _mix(h_p, attn, u, qc, kmT, vm, w_o.astype(BF16), wp_bd.astype(BF16), ps, g1, b1,
                                   wr, br, cnt_s, h1s, alpha, tm=256)

    tmx = 256
    n_p, n_s = B * S, DB * T
    n = n_p + n_s
    counts = cnt_all[0, :N_EXPERTS].astype(jnp.int32)
    padded = (counts + tmx - 1) // tmx * tmx
    seg_end = jnp.cumsum(padded)
    seg_off = seg_end - padded
    meta = jnp.concatenate([meta_p, meta_s], axis=0)
    e_ids = meta[:, 0:2].astype(jnp.int32)
    pos = (seg_off[e_ids] + meta[:, 4:6].astype(jnp.int32)).reshape(-1)
    max_tiles = (2 * n) // tmx + N_EXPERTS
    n_tiles = (seg_end[-1] // tmx).astype(jnp.int32).reshape(1)
    tile_row0 = jnp.arange(max_tiles, dtype=jnp.int32) * tmx
    tile_expert = jnp.sum((tile_row0[:, None] >= seg_end[None, :]).astype(jnp.int32), axis=1)
    tile_expert = jnp.minimum(tile_expert, N_EXPERTS - 1)
    tile_valid = jnp.clip(seg_off[tile_expert] + counts[tile_expert] - tile_row0, 0, tmx).astype(jnp.int32)

    tok_tile = math.gcd(math.gcd(n_p, n_s), 128)
    n_slots = -(-(max_tiles * tmx) // 1024) * 1024
    inv = _slot_map(pos, n_slots, n, blk=math.gcd(2 * n, 2048))
    ys = _experts(tile_expert, n_tiles, tile_valid, inv, h_rows.reshape(-1, nchunk, LANES),
                  w_gate.reshape(N_EXPERTS, D, -1),
                  w_up.reshape(N_EXPERTS, D, -1), w_down.reshape(N_EXPERTS, -1, D), tmx, max_tiles)
    ys3 = ys.reshape(-1, nchunk, LANES)
    y_p = _combine(pos, 0, n_p, h_rows, meta_p, ys3, g2, b2, alpha, tm=tok_tile)
    y_s = _combine(pos, n_p, n_s, h_rows, meta_s, ys3, g2, b2, alpha, tm=tok_tile)

    y_p = y_p.reshape(B, S, D)
    y_s = jnp.transpose(y_s.reshape(T, DB, D), (1, 0, 2))
    heads = lambda a, h: jnp.transpose(a.reshape(a.shape[0], h, HEAD_DIM, a.shape[2]), (0, 3, 1, 2))
    wbp = min(max(w for w, _ in DILATED_PATTERNS), S)
    new_wk_p = heads(kT, H)[:, S - wbp:]
    new_wv_p = heads(vT, H)[:, S - wbp:]
    pb = pool_st.shape[1]
    new_pool_p = u[:, S - pb:]
    new_mk_p = heads(kmT, Hc)
    new_mv_p = heads(vmT, Hc)
    new_wk_s = jnp.transpose(nk, (0, 3, 1, 2))
    new_wv_s = jnp.transpose(nv, (0, 3, 1, 2))
    new_pool_s = jnp.transpose(new_pool, (1, 0, 2))
    return (y_p, y_s, new_wk_p, new_wv_p, new_pool_p, new_mk_p, new_mv_p, new_wk_s, new_wv_s, new_pool_s)


def kernel(x_prompt, x_sample, cache_win_k, cache_win_v, state_pool, cache_mem_k, cache_mem_v, mem_prompt, w_in, w_mem_kv, w_pool, pool_scale, w_o, ln1_g, ln1_b, ln2_g, ln2_b, w_r1, b_r1, w_r2, b_r2, w_gate, w_up, w_down):
    depth = w_in.shape[0]
    alpha = (2.0 * depth) ** 0.25
    h_p, h_s = x_prompt, x_sample
    outs = [[] for _ in range(8)]
    for l in range(depth):
        res = _layer(h_p, h_s, cache_win_k[l], cache_win_v[l], state_pool[l], cache_mem_k[l], cache_mem_v[l],
                     mem_prompt, w_in[l], w_mem_kv[l], w_pool[l], pool_scale[l], w_o[l], ln1_g[l], ln1_b[l],
                     ln2_g[l], ln2_b[l], w_r1[l], b_r1[l], w_r2[l], b_r2[l], w_gate[l], w_up[l], w_down[l], alpha)
        h_p, h_s = res[0], res[1]
        for lst, val in zip(outs, res[2:]):
            lst.append(val)
    return (h_p, h_s) + tuple(jnp.stack(o) for o in outs)
```

```python
import functools
import math

import jax
import jax.numpy as jnp
from jax import lax
from jax.experimental import pallas as pl
from jax.experimental.pallas import tpu as pltpu

F32 = jnp.float32
BF16 = jnp.bfloat16
HIGHEST = lax.Precision.HIGHEST
NEG_INF = float("-inf")

HEAD_DIM = 64
DILATED_PATTERNS = ((128, 1), (512, 4), (2048, 16))
BAND_BLOCK = 128
RES = max(d for _, d in DILATED_PATTERNS)
POOL_WINDOWS = (2, 4, 8, 16)
N_EXPERT_GROUPS = 4
EXPERTS_PER_GROUP = 8
N_EXPERTS = N_EXPERT_GROUPS * EXPERTS_PER_GROUP
PAST_LEN = 16384
LN_EPS = 1e-5
ATTN_SCALE = HEAD_DIM ** -0.5

LANES = 128
SUBLANES = 8
VMEM_LIMIT = 56 * 1024 * 1024

ROUTER_OFF = N_EXPERT_GROUPS
ATTN_BATCH = 8
DMA_UNROLL = 8


def _cparams(sem):
    return pltpu.CompilerParams(dimension_semantics=sem, vmem_limit_bytes=VMEM_LIMIT)


def _dot(a, b, precision=None):
    return jnp.dot(a, b, preferred_element_type=F32, precision=precision)


def _dot_nt(a, b, precision=None):
    return lax.dot_general(a, b, (((1,), (1,)), ((), ())), preferred_element_type=F32, precision=precision)


def _memproj_body(mem_ref, wkT_ref, wv_ref, wvT_ref, kmT_ref, vm_ref, vmT_ref):
    m = mem_ref[0].astype(BF16)
    kmT_ref[0] = _dot_nt(wkT_ref[...], m)
    vm_ref[0] = _dot(m, wv_ref[...])
    vmT_ref[0] = _dot_nt(wvT_ref[...], m)


def _memproj(mem, wkT, wv, wvT):
    B, M, D = mem.shape
    C = wv.shape[1]
    full = lambda shape: pl.BlockSpec(shape, lambda b: (0,) * len(shape))
    return pl.pallas_call(
        _memproj_body,
        grid=(B,),
        in_specs=[pl.BlockSpec((1, M, D), lambda b: (b, 0, 0)), full((C, D)), full((D, C)), full((C, D))],
        out_specs=[pl.BlockSpec((1, C, M), lambda b: (b, 0, 0)),
                   pl.BlockSpec((1, M, C), lambda b: (b, 0, 0)),
                   pl.BlockSpec((1, C, M), lambda b: (b, 0, 0))],
        out_shape=[jax.ShapeDtypeStruct((B, C, M), F32), jax.ShapeDtypeStruct((B, M, C), F32),
                   jax.ShapeDtypeStruct((B, C, M), F32)],
        compiler_params=_cparams(("arbitrary",)),
        name="memproj",
    )(mem, wkT, wv, wvT)


def _proj_body(d_a, d_b, x_ref, perm_ref, w_ref, wkvT_ref, q_ref, k_ref, v_ref, u_ref, qc_ref, kT_ref, vT_ref):
    xb = x_ref[0].astype(BF16)
    ts = xb.shape[0]
    xp = _dot(perm_ref[...], xb).astype(BF16)
    y = _dot(xp, w_ref[:, 0:3 * d_a])
    rows = ts // RES
    for r in range(RES):
        q_ref[0, r] = y[r * rows:(r + 1) * rows, 0:d_a]
        k_ref[0, r] = y[r * rows:(r + 1) * rows, d_a:2 * d_a]
        v_ref[0, r] = y[r * rows:(r + 1) * rows, 2 * d_a:3 * d_a]
    y2 = _dot(xb, w_ref[:, 3 * d_a:])
    u_ref[0] = y2[:, 0:d_b]
    qc_ref[0] = y2[:, d_b:]
    yT = _dot_nt(wkvT_ref[...], xb)
    kT_ref[0] = yT[0:d_a]
    vT_ref[0] = yT[d_a:]


def _residue_major_perm(n):
    dst = jnp.arange(n)
    src = RES * (dst % (n // RES)) + dst // (n // RES)
    return (src[:, None] == jnp.arange(n)[None, :]).astype(BF16)


def _proj(x, w_bf, wkvT_bf, d_a, d_b, d_c, ts):
    B, S, D = x.shape
    d_in = w_bf.shape[1]
    assert ts % (RES * SUBLANES) == 0 and S % ts == 0
    row = lambda n: pl.BlockSpec((1, ts, n), lambda b, i: (b, i, 0))
    col = lambda n: pl.BlockSpec((1, n, ts), lambda b, i: (b, 0, i))
    grp = pl.BlockSpec((1, RES, ts // RES, d_a), lambda b, i: (b, 0, i, 0))
    outs = pl.pallas_call(
        functools.partial(_proj_body, d_a, d_b),
        grid=(B, S // ts),
        in_specs=[row(D), pl.BlockSpec((ts, ts), lambda b, i: (0, 0)), pl.BlockSpec((D, d_in), lambda b, i: (0, 0)),
                  pl.BlockSpec((2 * d_a, D), lambda b, i: (0, 0))],
        out_specs=[grp, grp, grp, row(d_b), row(d_c), col(d_a), col(d_a)],
        out_shape=[jax.ShapeDtypeStruct((B, RES, S // RES, d_a), F32)] * 3
        + [jax.ShapeDtypeStruct((B, S, d_b), F32), jax.ShapeDtypeStruct((B, S, d_c), F32)]
        + [jax.ShapeDtypeStruct((B, d_a, S), F32)] * 2,
        compiler_params=_cparams(("arbitrary", "arbitrary")),
        name="proj",
    )(x, _residue_major_perm(ts), w_bf, wkvT_bf)
    return [o.reshape(B, S, d_a) for o in outs[:3]] + list(outs[3:])


def _attn_body(S, slopes_ref, q_ref, k_ref, v_ref, o_ref, opat_ref, lpat_ref):
    hp = pl.program_id(1)
    blk = BAND_BLOCK
    lane = lax.broadcasted_iota(jnp.int32, (blk, LANES), 1)
    head0 = lane < HEAD_DIM

    for pi, (window, dil) in enumerate(DILATED_PATTERNS):
        n_steps = window // dil
        ngrp = RES // dil
        cs = blk // ngrp
        cs_bits = cs.bit_length() - 1
        assert ngrp * dil == RES and cs * ngrp == blk and cs % SUBLANES == 0 and S == RES * blk
        has_prev = ngrp > 1
        nk = 2 * blk if has_prev else blk
        nat = lambda a, ngrp=ngrp, cs=cs, cs_bits=cs_bits: (a & (cs - 1)) * ngrp + (a >> cs_bits)
        qi = lax.broadcasted_iota(jnp.int32, (blk, nk), 0)
        kj = lax.broadcasted_iota(jnp.int32, (blk, nk), 1)
        if has_prev:
            steps = (nat(qi) + blk) - (nat(kj & (blk - 1)) + (kj & blk))
        else:
            steps = qi - kj
        valid = (steps >= 0) & (steps <= n_steps)
        dist = (steps * dil).astype(F32)
        biases = []
        for hh in range(2):
            slope = slopes_ref[2 * hp + hh]
            biases.append(jnp.where(valid, -slope * dist, NEG_INF))
        prev_cols = kj < blk

        ones_k = jnp.ones((ATTN_BATCH, nk, LANES), BF16)

        def blocks(it, carry, dil=dil, ngrp=ngrp, cs=cs, has_prev=has_prev, biases=biases, prev_cols=prev_cols,
                   pi=pi, nk=nk, ones_k=ones_k):
            def chunk(idx, c, back):
                r = idx // ngrp
                j = jnp.maximum(idx % ngrp - back, 0)
                return pl.ds(pl.multiple_of((r + dil * c) * blk + cs * j, cs), cs)

            def load(ref, idx, back=0):
                return jnp.concatenate([ref[0, chunk(idx, c, back), :] for c in range(ngrp)], axis=0)

            qs, ks, vs, firsts = [], [], [], []
            for b in range(ATTN_BATCH):
                idx = it * ATTN_BATCH + b
                qs.append(load(q_ref, idx))
                if has_prev:
                    ks.append(jnp.concatenate([load(k_ref, idx, 1), load(k_ref, idx)], axis=0))
                    vs.append(jnp.concatenate([load(v_ref, idx, 1), load(v_ref, idx)], axis=0))
                    firsts.append(jnp.where(prev_cols & (idx % ngrp == 0), NEG_INF, 0.0))
                else:
                    ks.append(load(k_ref, idx))
                    vs.append(load(v_ref, idx))
            q3 = jnp.stack(qs) * ATTN_SCALE
            k3 = jnp.stack(ks).astype(BF16)
            v3 = jnp.concatenate([jnp.stack(vs).astype(BF16), ones_k], axis=2)
            outs, lses = [], []
            for hh in range(2):
                qm = jnp.where(head0 if hh == 0 else ~head0, q3, 0.0).astype(BF16)
                s = jnp.einsum("bqd,bkd->bqk", qm, k3, preferred_element_type=F32) + biases[hh]
                if has_prev:
                    s = s + jnp.stack(firsts)
                m = jnp.max(s, axis=2, keepdims=True)
                p = jnp.exp(s - m).astype(BF16)
                o = jnp.einsum("bqk,bkd->bqd", p, v3, preferred_element_type=F32)
                den = o[:, :, LANES:]
                outs.append(o[:, :, :LANES] / den)
                lses.append(m + jnp.log(den))
            o_pair = jnp.where(head0, outs[0], outs[1])
            l_pair = jnp.where(head0, lses[0], lses[1])
            for b in range(ATTN_BATCH):
                idx = it * ATTN_BATCH + b
                for c in range(ngrp):
                    opat_ref[pi, chunk(idx, c, 0), :] = o_pair[b, c * cs:(c + 1) * cs]
                    lpat_ref[pi, chunk(idx, c, 0), :] = l_pair[b, c * cs:(c + 1) * cs]
            return carry

        lax.fori_loop(0, RES // ATTN_BATCH, blocks, 0)

    chunk = 256

    def mix(c, carry):
        rows = pl.ds(pl.multiple_of(c * chunk, chunk), chunk)
        l0, l1, l2 = lpat_ref[0, rows, :], lpat_ref[1, rows, :], lpat_ref[2, rows, :]
        mx = jnp.maximum(jnp.maximum(l0, l1), l2)
        w0, w1, w2 = jnp.exp(l0 - mx), jnp.exp(l1 - mx), jnp.exp(l2 - mx)
        num = w0 * opat_ref[0, rows, :] + w1 * opat_ref[1, rows, :] + w2 * opat_ref[2, rows, :]
        o_ref[0, rows, :] = (num / (w0 + w1 + w2)).astype(o_ref.dtype)
        return carry

    lax.fori_loop(0, S // chunk, mix, 0)


def _attn(q, k, v, slopes):
    B, S, d_a = q.shape
    assert S % (BAND_BLOCK * max(d for _, d in DILATED_PATTERNS)) == 0
    npair = d_a // LANES
    spec = pl.BlockSpec((1, S, LANES), lambda b, h, *_: (b, 0, h))
    grid_spec = pltpu.PrefetchScalarGridSpec(
        num_scalar_prefetch=0,
        grid=(B, npair),
        in_specs=[pl.BlockSpec(memory_space=pltpu.SMEM), spec, spec, spec],
        out_specs=spec,
        scratch_shapes=[pltpu.VMEM((len(DILATED_PATTERNS), S, LANES), F32),
                        pltpu.VMEM((len(DILATED_PATTERNS), S, LANES), F32)],
    )
    return pl.pallas_call(
        functools.partial(_attn_body, S),
        grid_spec=grid_spec,
        out_shape=jax.ShapeDtypeStruct((B, S, d_a), BF16),
        compiler_params=_cparams(("arbitrary", "arbitrary")),
        name="dilated_attn",
    )(slopes, q, k, v)


def _layer_norm(z, g, b):
    mu = jnp.mean(z, axis=-1, keepdims=True)
    zc = z - mu
    var = jnp.mean(zc * zc, axis=-1, keepdims=True)
    return zc * lax.rsqrt(var + LN_EPS) * g + b


def _route(logits):
    n = logits.shape[0]
    lane = lax.broadcasted_iota(jnp.int32, (n, LANES), 1)
    lane_f = lane.astype(F32)
    big = float(LANES)
    is_outer = lane < N_EXPERT_GROUPS
    l1 = jnp.where(is_outer, logits, NEG_INF)
    m1 = jnp.max(l1, axis=1, keepdims=True)
    g_sel = jnp.min(jnp.where(l1 == m1, lane_f, big), axis=1, keepdims=True)
    v1 = 1.0 / jnp.sum(jnp.exp(l1 - m1), axis=1, keepdims=True)
    lo = ROUTER_OFF + g_sel * EXPERTS_PER_GROUP
    in_group = (lane_f >= lo) & (lane_f < lo + EXPERTS_PER_GROUP)
    l2 = jnp.where(in_group, logits, NEG_INF)
    ma = jnp.max(l2, axis=1, keepdims=True)
    ia = jnp.min(jnp.where(l2 == ma, lane_f, big), axis=1, keepdims=True)
    l2b = jnp.where(lane_f == ia, NEG_INF, l2)
    mb = jnp.max(l2b, axis=1, keepdims=True)
    ib = jnp.min(jnp.where(l2b == mb, lane_f, big), axis=1, keepdims=True)
    eb = jnp.exp(mb - ma)
    wa = 1.0 / (1.0 + eb)
    wb = eb / (1.0 + eb)
    return ia - ROUTER_OFF, ib - ROUTER_OFF, v1 * wa, v1 * wb


def _rank_and_meta(e0, e1, g0, g1, carry):
    n = e0.shape[0]
    lane_f = lax.broadcasted_iota(jnp.int32, (n, LANES), 1).astype(F32)
    oh0 = (lane_f == e0).astype(F32)
    oh1 = (lane_f == e1).astype(F32)
    both = oh0 + oh1
    ti = lax.broadcasted_iota(jnp.int32, (n, n), 0)
    tj = lax.broadcasted_iota(jnp.int32, (n, n), 1)
    tri = (tj < ti).astype(BF16)
    before = _dot(tri, both.astype(BF16)) + carry
    r0 = jnp.sum(before * oh0, axis=1, keepdims=True)
    r1 = jnp.sum(before * oh1, axis=1, keepdims=True)
    new_carry = carry + jnp.sum(both, axis=0, keepdims=True)
    lane = lax.broadcasted_iota(jnp.int32, (n, LANES), 1)
    meta = jnp.zeros((n, LANES), F32)
    for i, val in enumerate((e0, e1, g0, g1, r0, r1)):
        meta = jnp.where(lane == i, val, meta)
    return meta, new_carry


def _lane_group(shape, width):
    lane = lax.broadcasted_iota(jnp.int32, shape, len(shape) - 1)
    grp = jnp.zeros(shape, jnp.int32)
    for g in range(1, shape[-1] // width):
        grp = grp + (lane >= g * width).astype(jnp.int32)
    return grp


def _select_by_group(grp, vals):
    out = vals[-1]
    for g in range(len(vals) - 2, -1, -1):
        out = jnp.where(grp == g, vals[g], out)
    return out


def _store_token_rows(ref, val):
    n, d = val.shape
    nchunk = d // LANES
    for c in range(nchunk):
        ref[pl.ds(c, n, stride=nchunk), :] = val[:, c * LANES:(c + 1) * LANES]


def _load_token_rows(ref, n, nchunk):
    return jnp.concatenate([ref[pl.ds(c, n, stride=nchunk), :] for c in range(nchunk)], axis=1)


HALO = 16


def _mix_body(tm, nt, n_tiles, d_b, alpha, x_ref, at_ref, unperm_ref, u_ref, uh_ref, qc_ref, kmT_ref, vm_ref,
              wo_ref, wp_ref, ps_ref, g1_ref, b1_ref, wr_ref, br_ref, cin_ref, hs_ref,
              h1_ref, meta_ref, cnt_ref, carry_ref):
    g = pl.program_id(0)

    @pl.when(g == 0)
    def _():
        carry_ref[...] = cin_ref[...]

    @pl.when(g == n_tiles)
    def _():
        ns_rows = hs_ref.shape[0]
        h1_ref[0:ns_rows, :] = hs_ref[...]
        h1_ref[ns_rows:, :] = jnp.zeros((h1_ref.shape[0] - ns_rows, LANES), F32)

    @pl.when(g < n_tiles)
    def _():
        _mix_tile(tm, g % nt, d_b, alpha, x_ref, at_ref, unperm_ref, u_ref, uh_ref, qc_ref, kmT_ref, vm_ref, wo_ref,
                  wp_ref, ps_ref, g1_ref, b1_ref, wr_ref, br_ref, h1_ref, meta_ref, cnt_ref, carry_ref)


def _mix_tile(tm, i, d_b, alpha, x_ref, at_ref, unperm_ref, u_ref, uh_ref, qc_ref, kmT_ref, vm_ref, wo_ref, wp_ref,
              ps_ref, g1_ref, b1_ref, wr_ref, br_ref, h1_ref, meta_ref, cnt_ref, carry_ref):
    u = u_ref[0]
    halo = jnp.where(i == 0, 0.0, uh_ref[0])
    ext = jnp.concatenate([halo, halo, u], axis=0)
    s2 = ext + pltpu.roll(ext, 1, 0)
    s4 = s2 + pltpu.roll(s2, 2, 0)
    s8 = s4 + pltpu.roll(s4, 4, 0)
    s16 = s8 + pltpu.roll(s8, 8, 0)
    grp = _lane_group((tm, d_b), d_b // len(POOL_WINDOWS))
    sums = [s[2 * HALO:] for s in (s2, s4, s8, s16)]
    win = _select_by_group(grp, sums)
    wlen = _select_by_group(grp, [jnp.int32(w) for w in POOL_WINDOWS])
    pos = i * tm + lax.broadcasted_iota(jnp.int32, (tm, d_b), 0)
    cnt = jnp.minimum(wlen, pos + 1).astype(F32)
    diff = win / cnt - u
    pool = _dot(diff.astype(BF16), wp_ref[...]) * ps_ref[...]

    qc = qc_ref[0] * ATTN_SCALE
    d_c = qc.shape[1]
    kmT = kmT_ref[0].astype(BF16)
    vm = vm_ref[0].astype(BF16)
    hl = _lane_group((tm, d_c), HEAD_DIM)
    memo = jnp.zeros((tm, d_c), F32)
    for h in range(d_c // HEAD_DIM):
        qm = jnp.where(hl == h, qc, 0.0).astype(BF16)
        s = _dot(qm, kmT)
        p = jnp.exp(s - jnp.max(s, axis=1, keepdims=True))
        den = jnp.sum(p, axis=1, keepdims=True)
        o = _dot(p.astype(BF16), vm) / den
        memo = jnp.where(hl == h, o, memo)

    at_rl = jnp.concatenate([at_ref[0, r] for r in range(RES)], axis=0)
    attn = _dot(unperm_ref[...], at_rl).astype(BF16)
    cat = jnp.concatenate([attn, pool.astype(BF16), memo.astype(BF16)], axis=1)
    mixv = _dot(cat, wo_ref[...])
    h1 = _layer_norm(alpha * x_ref[0] + mixv, g1_ref[...], b1_ref[...])
    _store_token_rows(h1_ref, h1)

    h_hi = h1.astype(BF16)
    h_lo = (h1 - h_hi.astype(F32)).astype(BF16)
    part = _dot(h_hi, wr_ref[...])
    logits = part[:, :LANES] + part[:, LANES:] + _dot(h_lo, wr_ref[:, :LANES]) + br_ref[...]
    e0, e1, g0, g1 = _route(logits)
    meta, new_carry = _rank_and_meta(e0, e1, g0, g1, carry_ref[0:1, :])
    meta_ref[...] = meta
    carry_ref[...] = jnp.broadcast_to(new_carry, carry_ref.shape)
    cnt_ref[...] = carry_ref[...]


def _mix(x, attn_rm, u, qc, kmT, vm, wo_bf, wp_bd_bf, pool_scale, g1, b1, wr, br, counts_in, h1s, alpha, tm):
    B, S, D = x.shape
    d_a, d_b, d_c = attn_rm.shape[2], u.shape[2], qc.shape[2]
    n_mem = vm.shape[1]
    nt = S // tm
    n_tiles = B * nt
    nchunk = D // LANES
    assert tm % (RES * SUBLANES * 2) == 0 and h1s.shape[0] < tm * nchunk
    last = n_tiles - 1
    bi = lambda g: (jnp.minimum(g, last) // nt, jnp.minimum(g, last) % nt)
    row = lambda n: pl.BlockSpec((1, tm, n), lambda g: bi(g) + (0,))
    full = lambda shape: pl.BlockSpec(shape, lambda g: (0,) * len(shape))
    per_b = lambda shape: pl.BlockSpec((1,) + shape, lambda g: (bi(g)[0], 0, 0))
    halo_spec = pl.BlockSpec((1, HALO, d_b), lambda g: (bi(g)[0], jnp.maximum(bi(g)[1] * (tm // HALO) - 1, 0), 0))
    at_spec = pl.BlockSpec((1, RES, tm // RES, d_a), lambda g: (bi(g)[0], 0, bi(g)[1], 0))
    dst = jnp.arange(tm)
    unperm = ((dst % RES) * (tm // RES) + dst // RES)[:, None] == jnp.arange(tm)[None, :]
    return pl.pallas_call(
        functools.partial(_mix_body, tm, nt, n_tiles, d_b, alpha),
        grid=(n_tiles + 1,),
        in_specs=[row(D), at_spec, full((tm, tm)), row(d_b), halo_spec, row(d_c), per_b((d_c, n_mem)),
                  per_b((n_mem, d_c)), full((D, D)), full((d_b, d_b)), full((1, d_b)), full((1, D)), full((1, D)),
                  full((D, 2 * LANES)), full((1, LANES)), full((SUBLANES, LANES)), full(h1s.shape)],
        out_specs=[pl.BlockSpec((tm * nchunk, LANES), lambda g: (g, 0)),
                   pl.BlockSpec((tm, LANES), lambda g: (jnp.minimum(g, last), 0)),
                   pl.BlockSpec((SUBLANES, LANES), lambda g: (0, 0))],
        out_shape=[jax.ShapeDtypeStruct(((n_tiles + 1) * tm * nchunk, LANES), F32),
                   jax.ShapeDtypeStruct((B * S, LANES), F32),
                   jax.ShapeDtypeStruct((SUBLANES, LANES), F32)],
        scratch_shapes=[pltpu.VMEM((SUBLANES, LANES), F32)],
        compiler_params=_cparams(("arbitrary",)),
        name="mix_ln1_router",
    )(x, attn_rm.reshape(B, RES, S // RES, d_a), unperm.astype(BF16), u, u, qc, kmT, vm, wo_bf, wp_bd_bf,
      pool_scale, g1, b1, wr, br, counts_in, h1s)


def _columns_to_lanes(cols, rows):
    lane = lax.broadcasted_iota(jnp.int32, (rows, LANES), 1)
    tile = jnp.zeros((rows, LANES), F32)
    for t, c in enumerate(cols):
        tile = jnp.where(lane == t, c, tile)
    return tile


def _sproj_body(x_ref, wT_ref, o_ref):
    o_ref[...] = _dot_nt(wT_ref[...], x_ref[...], precision=HIGHEST)


def _sproj(x, wT):
    vm = pl.BlockSpec(memory_space=pltpu.VMEM)
    return pl.pallas_call(
        _sproj_body, in_specs=[vm, vm], out_specs=vm,
        out_shape=jax.ShapeDtypeStruct((wT.shape[0], x.shape[0]), F32),
        compiler_params=pltpu.CompilerParams(vmem_limit_bytes=VMEM_LIMIT),
        name="decode_proj",
    )(x, wT)


def _decode_body(T, wb, slopes_ref, pT_ref, ck_ref, cv_ref, mk_ref, mv_ref,
                 nk_ref, nv_ref, at_ref, mo_ref, qkv_ref):
    H = ck_ref.shape[1]
    d_a = H * HEAD_DIM
    ntok = pT_ref.shape[1]
    shift = (ntok - T * pl.program_id(0)) % ntok
    qkv_ref[...] = pltpu.roll(pT_ref[...], shift, 1)

    def multiplicity(dist):
        mult = jnp.zeros(dist.shape, F32)
        for window, dil in DILATED_PATTERNS:
            assert dil & (dil - 1) == 0
            mult = mult + (((dist & (dil - 1)) == 0) & (dist <= window) & (dist >= 0)).astype(F32)
        return mult

    OWN = min(d for _, d in DILATED_PATTERNS if d > 1)
    far = wb - LANES
    assert T <= OWN and far % OWN == 0 and wb % OWN == 0
    assert all(w <= LANES for w, d in DILATED_PATTERNS if d == 1) and all(d % OWN == 0 for _, d in DILATED_PATTERNS if d > 1)
    lane_far = lax.broadcasted_iota(jnp.int32, (T, far), 1)
    t_far = lax.broadcasted_iota(jnp.int32, (T, far), 0)
    owned = (lane_far & (OWN - 1)) == t_far
    dist_far = wb + t_far - lane_far
    mult_far = jnp.where(owned, multiplicity(dist_far), 0.0)
    dist_far = dist_far.astype(F32)
    t_near = lax.broadcasted_iota(jnp.int32, (T, LANES), 0)
    dist_near = wb + t_near - (far + lax.broadcasted_iota(jnp.int32, (T, LANES), 1))
    mult_near = multiplicity(dist_near)
    dist_near = dist_near.astype(F32)
    dist_new = lax.broadcasted_iota(jnp.int32, (T, T), 0) - lax.broadcasted_iota(jnp.int32, (T, T), 1)
    mult_new = multiplicity(dist_new)
    dist_new = jnp.maximum(dist_new, 0).astype(F32)

    lane_t = lax.broadcasted_iota(jnp.int32, (HEAD_DIM, LANES), 1)
    own_t = lane_t & (OWN - 1)
    last = wb - LANES
    n_far = far // LANES

    def head(h, carry):
        slope = slopes_ref[h]
        kTc = ck_ref[0, h]
        vTc = cv_ref[0, h]
        r0 = pl.multiple_of(h * HEAD_DIM, HEAD_DIM)
        qT = qkv_ref[pl.ds(r0, HEAD_DIM), :][:, 0:T] * ATTN_SCALE
        kTn = qkv_ref[pl.ds(d_a + r0, HEAD_DIM), :][:, 0:T]
        vTn = qkv_ref[pl.ds(2 * d_a + r0, HEAD_DIM), :][:, 0:T]
        tile = lambda a, c: a[:, c * LANES:(c + 1) * LANES]

        qb = [jnp.broadcast_to(qT[:, t:t + 1], (HEAD_DIM, LANES)) for t in range(T)]
        qpat = qb[T - 1]
        for t in range(T - 1):
            qpat = jnp.where(own_t == t, qb[t], qpat)
        s_far = jnp.concatenate([jnp.sum(tile(kTc, c) * qpat, axis=0, keepdims=True) for c in range(n_far)], axis=1)
        s_far = jnp.where(mult_far > 0, s_far - slope * dist_far, NEG_INF)
        k_near = tile(kTc, n_far)
        s_near = jnp.concatenate([jnp.sum(k_near * qb[t], axis=0, keepdims=True) for t in range(T)], axis=0)
        s_near = jnp.where(mult_near > 0, s_near - slope * dist_near, NEG_INF)
        s_new = jnp.concatenate([jnp.sum(kTn * qT[:, t:t + 1], axis=0, keepdims=True) for t in range(T)], axis=0)
        s_new = jnp.where(mult_new > 0, s_new - slope * dist_new, NEG_INF)
        rmax = lambda a: jnp.max(a, axis=1, keepdims=True)
        m = jnp.maximum(jnp.maximum(rmax(s_far), rmax(s_near)), rmax(s_new))
        p_far = mult_far * jnp.exp(s_far - m)
        p_near = mult_near * jnp.exp(s_near - m)
        p_new = mult_new * jnp.exp(s_new - m)
        rsum = lambda a: jnp.sum(a, axis=1, keepdims=True)
        den = rsum(p_far) + rsum(p_near) + rsum(p_new)

        p_comb = jnp.sum(p_far, axis=0, keepdims=True)
        acc = tile(vTc, 0) * tile(p_comb, 0)
        for c in range(1, n_far):
            acc = acc + tile(vTc, c) * tile(p_comb, c)
        v_near = tile(vTc, n_far)
        cols = []
        for t in range(T):
            o = (jnp.sum(jnp.where(own_t == t, acc, 0.0) + v_near * p_near[t:t + 1], axis=1, keepdims=True)
                 + jnp.sum(vTn * p_new[t:t + 1], axis=1, keepdims=True))
            cols.append(o / den[t:t + 1])
        at_ref[0, pl.ds(r0, HEAD_DIM), :] = _columns_to_lanes(cols, HEAD_DIM)

        rk = pltpu.roll(kTc, wb - T, 1)
        rv = pltpu.roll(vTc, wb - T, 1)
        nk_ref[0, h] = rk
        nv_ref[0, h] = rv
        newk = jnp.zeros((HEAD_DIM, LANES), F32)
        newv = jnp.zeros((HEAD_DIM, LANES), F32)
        for t in range(T):
            newk = jnp.where(lane_t == LANES - T + t, kTn[:, t:t + 1], newk)
            newv = jnp.where(lane_t == LANES - T + t, vTn[:, t:t + 1], newv)
        nk_ref[0, h, :, last:] = jnp.where(lane_t >= LANES - T, newk, rk[:, last:])
        nv_ref[0, h, :, last:] = jnp.where(lane_t >= LANES - T, newv, rv[:, last:])
        return carry

    lax.fori_loop(0, H, head, 0, unroll=2)

    Hc = mk_ref.shape[1]
    rows = []
    for h in range(Hc):
        kT = mk_ref[0, h]
        r0 = h * HEAD_DIM
        qT = qkv_ref[3 * d_a + r0:3 * d_a + r0 + HEAD_DIM, :][:, 0:T] * ATTN_SCALE
        rows += [jnp.sum(kT * qT[:, t:t + 1], axis=0, keepdims=True) for t in range(T)]
    s = jnp.concatenate(rows, axis=0)
    p = jnp.exp(s - jnp.max(s, axis=1, keepdims=True))
    p = p / jnp.sum(p, axis=1, keepdims=True)
    for h in range(Hc):
        vT = mv_ref[0, h]
        cols = [jnp.sum(vT * p[h * T + t:h * T + t + 1], axis=1, keepdims=True) for t in range(T)]
        mo_ref[0, h * HEAD_DIM:(h + 1) * HEAD_DIM, :] = _columns_to_lanes(cols, HEAD_DIM)


def _decode(projT, T, ck, cv, mk, mv, slopes):
    DB, H, hd, wb = ck.shape
    _, Hc, _, n_mem = mk.shape
    d_a, d_c = H * hd, Hc * hd
    assert wb >= max(w for w, _ in DILATED_PATTERNS) and wb % LANES == 0 and T <= LANES
    assert projT.shape == (3 * d_a + d_c, DB * T)
    cache = pl.BlockSpec((1, H, hd, wb), lambda b, *_: (b, 0, 0, 0))
    memc = pl.BlockSpec((1, Hc, hd, n_mem), lambda b, *_: (b, 0, 0, 0))
    grid_spec = pltpu.PrefetchScalarGridSpec(
        num_scalar_prefetch=0,
        grid=(DB,),
        in_specs=[pl.BlockSpec(memory_space=pltpu.SMEM),
                  pl.BlockSpec(projT.shape, lambda b, *_: (0, 0)),
                  cache, cache, memc, memc],
        out_specs=[cache, cache,
                   pl.BlockSpec((1, d_a, LANES), lambda b, *_: (b, 0, 0)),
                   pl.BlockSpec((1, d_c, LANES), lambda b, *_: (b, 0, 0))],
        scratch_shapes=[pltpu.VMEM(projT.shape, F32)],
    )
    nk, nv, at, mo = pl.pallas_call(
        functools.partial(_decode_body, T, wb),
        grid_spec=grid_spec,
        out_shape=[jax.ShapeDtypeStruct(ck.shape, F32), jax.ShapeDtypeStruct(cv.shape, F32),
                   jax.ShapeDtypeStruct((DB, d_a, LANES), F32), jax.ShapeDtypeStruct((DB, d_c, LANES), F32)],
        compiler_params=_cparams(("arbitrary",)),
        name="decode_attn_cache",
    )(slopes, projT, ck, cv, mk, mv)
    return nk, nv, at[:, :, :T], mo[:, :, :T]


def _smix_body(T, alpha, pos0, x_ref, at_ref, mo_ref, st_ref, wu_ref, wo_ref, wp_ref, ps_ref, g1_ref, b1_ref,
               wr_ref, br_ref, cin_ref, h1_ref, meta_ref, cnt_ref, pool_ref):
    n = x_ref.shape[0]
    db = n // T
    pb = st_ref.shape[0]
    d_b = st_ref.shape[2]
    x = x_ref[...]
    u_new = _dot(x, wu_ref[...], precision=HIGHEST)
    seq = [st_ref[j] for j in range(pb)] + [u_new[t * db:(t + 1) * db] for t in range(T)]
    for j in range(pb):
        pool_ref[j] = seq[j + T]
    grp = _lane_group((db, d_b), d_b // len(POOL_WINDOWS))
    diffs = []
    for t in range(T):
        j = pb + t
        per_w = []
        for w in POOL_WINDOWS:
            acc = seq[j]
            for back in range(1, w):
                if j - back >= 0:
                    acc = acc + seq[j - back]
            per_w.append(acc / float(min(w, pos0 + j + 1)))
        diffs.append(_select_by_group(grp, per_w) - seq[j])
    diff = jnp.concatenate(diffs, axis=0)
    pool = _dot(diff, wp_ref[...], precision=HIGHEST) * ps_ref[...]
    cat = jnp.concatenate([at_ref[...], pool, mo_ref[...]], axis=1)
    mixv = _dot(cat, wo_ref[...], precision=HIGHEST)
    h1 = _layer_norm(alpha * x + mixv, g1_ref[...], b1_ref[...])
    _store_token_rows(h1_ref, h1)
    logits = _dot(h1, wr_ref[...], precision=HIGHEST) + br_ref[...]
    e0, e1, g0, g1 = _route(logits)
    meta, new_carry = _rank_and_meta(e0, e1, g0, g1, cin_ref[0:1, :])
    meta_ref[...] = meta
    cnt_ref[...] = jnp.broadcast_to(new_carry, cnt_ref.shape)


def _smix(x_tb, attn_tb, memo_tb, state, wu, wo, wp_bd, pool_scale, g1, b1, wr, br, counts_in, alpha, T):
    n, D = x_tb.shape
    pb, db, d_b = state.shape
    nchunk = D // LANES
    vm = pl.BlockSpec(memory_space=pltpu.VMEM)
    return pl.pallas_call(
        functools.partial(_smix_body, T, alpha, PAST_LEN - pb),
        in_specs=[vm] * 13,
        out_specs=[vm] * 4,
        out_shape=[jax.ShapeDtypeStruct((n * nchunk, LANES), F32), jax.ShapeDtypeStruct((n, LANES), F32),
                   jax.ShapeDtypeStruct((SUBLANES, LANES), F32), jax.ShapeDtypeStruct((pb, db, d_b), F32)],
        compiler_params=pltpu.CompilerParams(vmem_limit_bytes=VMEM_LIMIT),
        name="decode_mix_ln1_router",
    )(x_tb, attn_tb, memo_tb, state, wu, wo, wp_bd, pool_scale, g1, b1, wr, br, counts_in)


def _slot_map_body(blk, pos_ref, fill_ref, inv_ref, sem):
    g = pl.program_id(0)

    @pl.when(g == 0)
    def _():
        c = pltpu.make_async_copy(fill_ref, inv_ref, sem)
        c.start()
        c.wait()

    group = math.gcd(blk, 32)

    def body(j, carry):
        token0 = (g * blk + j * group) // 2
        for u in range(group):
            inv_ref[pos_ref[0, 0, j * group + u]] = token0 + u // 2
        return carry

    lax.fori_loop(0, blk // group, body, 0)


def _slot_map(pos, n_slots, zero_token, blk):
    n2 = pos.shape[0]
    assert n2 % blk == 0 and n_slots % 1024 == 0
    return pl.pallas_call(
        functools.partial(_slot_map_body, blk),
        grid=(n2 // blk,),
        in_specs=[pl.BlockSpec((1, 1, blk), lambda g: (g, 0, 0), memory_space=pltpu.SMEM),
                  pl.BlockSpec(memory_space=pl.ANY)],
        out_specs=pl.BlockSpec(memory_space=pltpu.SMEM),
        out_shape=jax.ShapeDtypeStruct((n_slots,), jnp.int32),
        scratch_shapes=[pltpu.SemaphoreType.DMA(())],
        compiler_params=_cparams(("arbitrary",)),
        name="slot_map",
    )(pos.reshape(n2 // blk, 1, blk), jnp.full((n_slots,), zero_token, jnp.int32))


def _expert_body(tmx, nchunk, te_ref, nt_ref, tv_ref, cur_ref, nxt_ref, h_ref, wg_ref, wu_ref, wd_ref, ys_ref,
                 buf_ref, sems):
    i = pl.program_id(0)
    n_tiles = nt_ref[0]

    def gather(idx_ref, slot, tile, wait):
        groups = (tv_ref[tile] + DMA_UNROLL - 1) // DMA_UNROLL

        def body(g, carry):
            for u in range(DMA_UNROLL):
                j = g * DMA_UNROLL + u
                dst = buf_ref.at[slot, pl.ds(pl.multiple_of(j * nchunk, nchunk), nchunk)]
                copy = pltpu.make_async_copy(h_ref.at[idx_ref[0, 0, j]], dst, sems.at[slot])
                if wait:
                    copy.wait()
                else:
                    copy.start(priority=u % 2)
            return carry

        lax.fori_loop(0, groups, body, 0)

    @pl.when(i == 0)
    def _():
        buf_ref[...] = jnp.zeros_like(buf_ref)
        gather(cur_ref, 0, 0, False)

    @pl.when(i < n_tiles)
    def _():
        slot = i % 2
        gather(cur_ref, slot, i, True)

        @pl.when(i + 1 < n_tiles)
        def _():
            gather(nxt_ref, 1 - slot, i + 1, False)

        x = _load_token_rows(buf_ref.at[slot], tmx, nchunk).astype(BF16)
        hg = _dot(x, wg_ref[0].astype(BF16))
        hu = _dot(x, wu_ref[0].astype(BF16))
        a = (hg * jax.nn.sigmoid(hg) * hu).astype(BF16)
        y = _dot(a, wd_ref[0].astype(BF16))
        _store_token_rows(ys_ref, y)

    @pl.when(i >= n_tiles)
    def _():
        ys_ref[...] = jnp.zeros_like(ys_ref)


def _experts(tile_expert, n_tiles, tile_valid, inv, h_rows, wg, wu, wd, tmx, max_tiles):
    E, D, F = wg.shape
    nchunk = D // LANES
    inv3 = inv.reshape(-1, 1, tmx)
    last_blk = inv3.shape[0] - 1
    assert inv3.shape[0] >= max_tiles and tmx % DMA_UNROLL == 0
    grid_spec = pltpu.PrefetchScalarGridSpec(
        num_scalar_prefetch=3,
        grid=(max_tiles,),
        in_specs=[pl.BlockSpec((1, 1, tmx), lambda i, *_: (i, 0, 0), memory_space=pltpu.SMEM),
                  pl.BlockSpec((1, 1, tmx), lambda i, *_: (jnp.minimum(i + 1, last_blk), 0, 0),
                               memory_space=pltpu.SMEM),
                  pl.BlockSpec(memory_space=pl.ANY),
                  pl.BlockSpec((1, D, F), lambda i, te, *_: (te[i], 0, 0)),
                  pl.BlockSpec((1, D, F), lambda i, te, *_: (te[i], 0, 0)),
                  pl.BlockSpec((1, F, D), lambda i, te, *_: (te[i], 0, 0))],
        out_specs=pl.BlockSpec((tmx * nchunk, LANES), lambda i, *_: (i, 0)),
        scratch_shapes=[pltpu.VMEM((2, tmx * nchunk, LANES), F32), pltpu.SemaphoreType.DMA((2,))],
    )
    return pl.pallas_call(
        functools.partial(_expert_body, tmx, nchunk),
        grid_spec=grid_spec,
        out_shape=jax.ShapeDtypeStruct((max_tiles * tmx * nchunk, LANES), F32),
        compiler_params=_cparams(("arbitrary",)),
        name="expert_swiglu",
    )(tile_expert, n_tiles, tile_valid, inv3, inv3, h_rows, wg, wu, wd)


def _combine_body(tm, nchunk, alpha, pos_ref, nxt_ref, h1_ref, meta_ref, ys_ref, g2_ref, b2_ref, o_ref, buf_ref,
                  sems):
    i = pl.program_id(0)
    slot = i % 2

    def gather(idx_ref, s, wait):
        def body(t, carry):
            for k in range(2):
                dst = buf_ref.at[s, k, pl.ds(pl.multiple_of(t * nchunk, nchunk), nchunk)]
                copy = pltpu.make_async_copy(ys_ref.at[idx_ref[0, 0, 2 * t + k]], dst, sems.at[s])
                if wait:
                    copy.wait()
                else:
                    copy.start(priority=k)
            return carry

        lax.fori_loop(0, tm, body, 0, unroll=DMA_UNROLL)

    @pl.when(i == 0)
    def _():
        gather(pos_ref, 0, False)

    gather(pos_ref, slot, True)

    @pl.when(i + 1 < pl.num_programs(0))
    def _():
        gather(nxt_ref, 1 - slot, False)

    h1 = _load_token_rows(h1_ref, tm, nchunk)
    y0 = _load_token_rows(buf_ref.at[slot, 0], tm, nchunk)
    y1 = _load_token_rows(buf_ref.at[slot, 1], tm, nchunk)
    meta = meta_ref[...]
    lane = lax.broadcasted_iota(jnp.int32, meta.shape, 1)
    gate0 = jnp.sum(jnp.where(lane == 2, meta, 0.0), axis=1, keepdims=True)
    gate1 = jnp.sum(jnp.where(lane == 3, meta, 0.0), axis=1, keepdims=True)
    f = gate0 * y0 + gate1 * y1
    o_ref[...] = _layer_norm(alpha * h1 + f, g2_ref[...], b2_ref[...])


def _combine(pos, tok0, n, h1_flat, meta, ys, g2, b2, alpha, tm):
    D = g2.shape[1]
    nchunk = D // LANES
    assert n % tm == 0 and tok0 % tm == 0
    pos3 = pos.reshape(-1, 1, 2 * tm)
    off = tok0 // tm
    last = off + n // tm - 1
    return pl.pallas_call(
        functools.partial(_combine_body, tm, nchunk, alpha),
        grid=(n // tm,),
        in_specs=[pl.BlockSpec((1, 1, 2 * tm), lambda i: (i + off, 0, 0), memory_space=pltpu.SMEM),
                  pl.BlockSpec((1, 1, 2 * tm), lambda i: (jnp.minimum(i + off + 1, last), 0, 0),
                               memory_space=pltpu.SMEM),
                  pl.BlockSpec((tm * nchunk, LANES), lambda i: (i + off, 0)),
                  pl.BlockSpec((tm, LANES), lambda i: (i, 0)),
                  pl.BlockSpec(memory_space=pl.ANY),
                  pl.BlockSpec((1, D), lambda i: (0, 0)), pl.BlockSpec((1, D), lambda i: (0, 0))],
        out_specs=pl.BlockSpec((tm, D), lambda i: (i, 0)),
        out_shape=jax.ShapeDtypeStruct((n, D), F32),
        scratch_shapes=[pltpu.VMEM((2, 2, tm * nchunk, LANES), F32), pltpu.SemaphoreType.DMA((2,))],
        compiler_params=_cparams(("arbitrary",)),
        name="combine_ln2",
    )(pos3, pos3, h1_flat, meta, ys, g2, b2)


def _block_diag(w):
    g, a, b = w.shape
    eye = jnp.eye(g, dtype=w.dtype)
    return (eye[:, None, :, None] * w[:, :, None, :]).reshape(g * a, g * b)


def _layer(h_p, h_s, win_k, win_v, pool_st, mem_k, mem_v, mem_prompt,
           w_in, w_mem_kv, w_pool, pool_scale, w_o, ln1_g, ln1_b, ln2_g, ln2_b,
           w_r1, b_r1, w_r2, b_r2, w_gate, w_up, w_down, alpha):
    B, S, D = h_p.shape
    DB, T, _ = h_s.shape
    H = win_k.shape[2]
    Hc = mem_k.shape[2]
    d_a, d_c = H * HEAD_DIM, Hc * HEAD_DIM
    d_b = pool_st.shape[2]
    nchunk = D // LANES
    slopes = 2.0 ** (-8.0 * jnp.arange(1, H + 1, dtype=F32) / H)

    w_in_bf = w_in.astype(BF16)
    w_inT = w_in.T
    wkvT_bf = w_inT[d_a:3 * d_a].astype(BF16)
    wp_bd = _block_diag(w_pool)
    ps = pool_scale.reshape(1, d_b)
    g1, b1 = ln1_g.reshape(1, D), ln1_b.reshape(1, D)
    g2, b2 = ln2_g.reshape(1, D), ln2_b.reshape(1, D)
    n_r = N_EXPERT_GROUPS + N_EXPERTS
    wr = jnp.concatenate([w_r1, jnp.transpose(w_r2, (1, 0, 2)).reshape(D, N_EXPERTS),
                          jnp.zeros((D, LANES - n_r), F32)], axis=1)
    br = jnp.concatenate([b_r1, b_r2.reshape(-1), jnp.zeros((LANES - n_r,), F32)]).reshape(1, LANES)

    ck = jnp.transpose(win_k, (0, 2, 3, 1))
    cv = jnp.transpose(win_v, (0, 2, 3, 1))
    mk = jnp.transpose(mem_k, (0, 2, 3, 1))
    mv = jnp.transpose(mem_v, (0, 2, 3, 1))
    w_qkvqcT = jnp.concatenate([w_inT[:3 * d_a], w_inT[3 * d_a + d_b:]], axis=0)
    projT = _sproj(h_s.reshape(DB * T, D), w_qkvqcT)
    nk, nv, attn_sT, memo_sT = _decode(projT, T, ck, cv, mk, mv, slopes)
    to_tb = lambda a: jnp.transpose(a, (2, 0, 1)).reshape(T * DB, a.shape[1])
    x_tb = jnp.transpose(h_s, (1, 0, 2)).reshape(T * DB, D)
    state = jnp.transpose(pool_st, (1, 0, 2))
    h1s, meta_s, cnt_s, new_pool = _smix(x_tb, to_tb(attn_sT), to_tb(memo_sT), state,
                                         w_in[:, 3 * d_a:3 * d_a + d_b], w_o, wp_bd, ps, g1, b1, wr, br,
                                         jnp.zeros((SUBLANES, LANES), F32), alpha, T)

    w_memT = w_mem_kv.T
    kmT, vm, vmT = _memproj(mem_prompt, w_memT[:d_c].astype(BF16), w_mem_kv[:, d_c:].astype(BF16),
                            w_memT[d_c:].astype(BF16))
    q, k, v, u, qc, kT, vT = _proj(h_p, w_in_bf, wkvT_bf, d_a, d_b, d_c, ts=512)
    attn = _attn(q, k, v, slopes)
    wr_hi = lax.bitcast_convert_type(lax.bitcast_convert_type(wr, jnp.uint32) & jnp.uint32(0xFFFF0000), F32)
    wr_split = jnp.concatenate([wr_hi.astype(BF16), (wr - wr_hi).astype(BF16)], axis=1)
    h_rows, meta_p, cnt_all = _mix(h_p, attn, u, qc, kmT, vm, w_o.astype(BF16), wp_bd.astype(BF16), ps, g1, b1,
                                   wr_split, br, cnt_s, h1s, alpha, tm=512)

    tmx = 256
    n_p, n_s = B * S, DB * T
    n = n_p + n_s
    counts = cnt_all[0, :N_EXPERTS].astype(jnp.int32)
    padded = (counts + tmx - 1) // tmx * tmx
    seg_end = jnp.cumsum(padded)
    seg_off = seg_end - padded
    meta = jnp.concatenate([meta_p, meta_s], axis=0)
    e_ids = meta[:, 0:2].astype(jnp.int32)
    lookup = lambda table, idx: jnp.sum(jnp.where(idx[..., None] == jnp.arange(N_EXPERTS), table, 0), axis=-1)
    pos = (lookup(seg_off, e_ids) + meta[:, 4:6].astype(jnp.int32)).reshape(-1)
    max_tiles = (2 * n) // tmx + N_EXPERTS
    n_tiles = (seg_end[-1] // tmx).astype(jnp.int32).reshape(1)
    tile_row0 = jnp.arange(max_tiles, dtype=jnp.int32) * tmx
    tile_expert = jnp.sum((tile_row0[:, None] >= seg_end[None, :]).astype(jnp.int32), axis=1)
    tile_expert = jnp.minimum(tile_expert, N_EXPERTS - 1)
    tile_valid = jnp.clip(lookup(seg_off + counts, tile_expert) - tile_row0, 0, tmx).astype(jnp.int32)

    tok_tile = math.gcd(math.gcd(n_p, n_s), 128)
    n_slots = -(-(max_tiles * tmx) // 1024) * 1024
    inv = _slot_map(pos, n_slots, n, blk=math.gcd(2 * n, 2048))
    ys = _experts(tile_expert, n_tiles, tile_valid, inv, h_rows.reshape(-1, nchunk, LANES),
                  w_gate.reshape(N_EXPERTS, D, -1),
                  w_up.reshape(N_EXPERTS, D, -1), w_down.reshape(N_EXPERTS, -1, D), tmx, max_tiles)
    ys3 = ys.reshape(-1, nchunk, LANES)
    y_p = _combine(pos, 0, n_p, h_rows, meta_p, ys3, g2, b2, alpha, tm=tok_tile)
    y_s = _combine(pos, n_p, n_s, h_rows, meta_s, ys3, g2, b2, alpha, tm=tok_tile)

    y_p = y_p.reshape(B, S, D)
    y_s = jnp.transpose(y_s.reshape(T, DB, D), (1, 0, 2))
    heads = lambda a, h: jnp.transpose(a.reshape(a.shape[0], h, HEAD_DIM, a.shape[2]), (0, 3, 1, 2))
    wbp = min(max(w for w, _ in DILATED_PATTERNS), S)
    new_wk_p = heads(kT, H)[:, S - wbp:]
    new_wv_p = heads(vT, H)[:, S - wbp:]
    pb = pool_st.shape[1]
    new_pool_p = u[:, S - pb:]
    new_mk_p = heads(kmT, Hc)
    new_mv_p = heads(vmT, Hc)
    new_wk_s = jnp.transpose(nk, (0, 3, 1, 2))
    new_wv_s = jnp.transpose(nv, (0, 3, 1, 2))
    new_pool_s = jnp.transpose(new_pool, (1, 0, 2))
    return (y_p, y_s, new_wk_p, new_wv_p, new_pool_p, new_mk_p, new_mv_p, new_wk_s, new_wv_s, new_pool_s)


def kernel(x_prompt, x_sample, cache_win_k, cache_win_v, state_pool, cache_mem_k, cache_mem_v, mem_prompt, w_in, w_mem_kv, w_pool, pool_scale, w_o, ln1_g, ln1_b, ln2_g, ln2_b, w_r1, b_r1, w_r2, b_r2, w_gate, w_up, w_down):
    depth = w_in.shape[0]
    alpha = (2.0 * depth) ** 0.25
    h_p, h_s = x_prompt, x_sample
    outs = [[] for _ in range(8)]
    for l in range(depth):
        res = _layer(h_p, h_s, cache_win_k[l], cache_win_v[l], state_pool[l], cache_mem_k[l], cache_mem_v[l],
                     mem_prompt, w_in[l], w_mem_kv[l], w_pool[l], pool_scale[l], w_o[l], ln1_g[l], ln1_b[l],
                     ln2_g[l], ln2_b[l], w_r1[l], b_r1[l], w_r2[l], b_r2[l], w_gate[l], w_up[l], w_down[l], alpha)
        h_p, h_s = res[0], res[1]
        for lst, val in zip(outs, res[2:]):
            lst.append(val)
    return (h_p, h_s) + tuple(jnp.stack(o) for o in outs)
```

```python
import functools
import math

import jax
import jax.numpy as jnp
from jax import lax
from jax.experimental import pallas as pl
from jax.experimental.pallas import tpu as pltpu

F32 = jnp.float32
BF16 = jnp.bfloat16
HIGHEST = lax.Precision.HIGHEST
NEG_INF = float("-inf")

HEAD_DIM = 64
DILATED_PATTERNS = ((128, 1), (512, 4), (2048, 16))
BAND_BLOCK = 128
RES = max(d for _, d in DILATED_PATTERNS)
POOL_WINDOWS = (2, 4, 8, 16)
N_EXPERT_GROUPS = 4
EXPERTS_PER_GROUP = 8
N_EXPERTS = N_EXPERT_GROUPS * EXPERTS_PER_GROUP
PAST_LEN = 16384
LN_EPS = 1e-5
ATTN_SCALE = HEAD_DIM ** -0.5

LANES = 128
SUBLANES = 8
VMEM_LIMIT = 56 * 1024 * 1024

ROUTER_OFF = N_EXPERT_GROUPS
ATTN_BATCH = 8
DMA_UNROLL = 8


def _cparams(sem):
    return pltpu.CompilerParams(dimension_semantics=sem, vmem_limit_bytes=VMEM_LIMIT)


def _dot(a, b, precision=None):
    return jnp.dot(a, b, preferred_element_type=F32, precision=precision)


def _dot_nt(a, b, precision=None):
    return lax.dot_general(a, b, (((1,), (1,)), ((), ())), preferred_element_type=F32, precision=precision)


def _memproj_body(mem_ref, wkT_ref, wv_ref, wvT_ref, kmT_ref, vm_ref, vmT_ref):
    m = mem_ref[0].astype(BF16)
    kmT_ref[0] = _dot_nt(wkT_ref[...], m)
    vm_ref[0] = _dot(m, wv_ref[...])
    vmT_ref[0] = _dot_nt(wvT_ref[...], m)


def _memproj(mem, wkT, wv, wvT):
    B, M, D = mem.shape
    C = wv.shape[1]
    full = lambda shape: pl.BlockSpec(shape, lambda b: (0,) * len(shape))
    return pl.pallas_call(
        _memproj_body,
        grid=(B,),
        in_specs=[pl.BlockSpec((1, M, D), lambda b: (b, 0, 0)), full((C, D)), full((D, C)), full((C, D))],
        out_specs=[pl.BlockSpec((1, C, M), lambda b: (b, 0, 0)),
                   pl.BlockSpec((1, M, C), lambda b: (b, 0, 0)),
                   pl.BlockSpec((1, C, M), lambda b: (b, 0, 0))],
        out_shape=[jax.ShapeDtypeStruct((B, C, M), F32), jax.ShapeDtypeStruct((B, M, C), F32),
                   jax.ShapeDtypeStruct((B, C, M), F32)],
        compiler_params=_cparams(("arbitrary",)),
        name="memproj",
    )(mem, wkT, wv, wvT)


def _proj_body(d_a, d_b, x_ref, perm_ref, w_ref, wkvT_ref, q_ref, k_ref, v_ref, u_ref, qc_ref, kT_ref, vT_ref):
    xb = x_ref[0].astype(BF16)
    ts = xb.shape[0]
    xp = _dot(perm_ref[...], xb).astype(BF16)
    y = _dot(xp, w_ref[:, 0:3 * d_a])
    rows = ts // RES
    for r in range(RES):
        q_ref[0, r] = y[r * rows:(r + 1) * rows, 0:d_a]
        k_ref[0, r] = y[r * rows:(r + 1) * rows, d_a:2 * d_a]
        v_ref[0, r] = y[r * rows:(r + 1) * rows, 2 * d_a:3 * d_a]
    y2 = _dot(xb, w_ref[:, 3 * d_a:])
    u_ref[0] = y2[:, 0:d_b]
    qc_ref[0] = y2[:, d_b:]
    yT = _dot_nt(wkvT_ref[...], xb)
    kT_ref[0] = yT[0:d_a]
    vT_ref[0] = yT[d_a:]


def _residue_major_perm(n):
    dst = jnp.arange(n)
    src = RES * (dst % (n // RES)) + dst // (n // RES)
    return (src[:, None] == jnp.arange(n)[None, :]).astype(BF16)


def _proj(x, w_bf, wkvT_bf, d_a, d_b, d_c, ts):
    B, S, D = x.shape
    d_in = w_bf.shape[1]
    assert ts % (RES * SUBLANES) == 0 and S % ts == 0
    row = lambda n: pl.BlockSpec((1, ts, n), lambda b, i: (b, i, 0))
    col = lambda n: pl.BlockSpec((1, n, ts), lambda b, i: (b, 0, i))
    grp = pl.BlockSpec((1, RES, ts // RES, d_a), lambda b, i: (b, 0, i, 0))
    outs = pl.pallas_call(
        functools.partial(_proj_body, d_a, d_b),
        grid=(B, S // ts),
        in_specs=[row(D), pl.BlockSpec((ts, ts), lambda b, i: (0, 0)), pl.BlockSpec((D, d_in), lambda b, i: (0, 0)),
                  pl.BlockSpec((2 * d_a, D), lambda b, i: (0, 0))],
        out_specs=[grp, grp, grp, row(d_b), row(d_c), col(d_a), col(d_a)],
        out_shape=[jax.ShapeDtypeStruct((B, RES, S // RES, d_a), F32)] * 3
        + [jax.ShapeDtypeStruct((B, S, d_b), F32), jax.ShapeDtypeStruct((B, S, d_c), F32)]
        + [jax.ShapeDtypeStruct((B, d_a, S), F32)] * 2,
        compiler_params=_cparams(("arbitrary", "arbitrary")),
        name="proj",
    )(x, _residue_major_perm(ts), w_bf, wkvT_bf)
    return [o.reshape(B, S, d_a) for o in outs[:3]] + list(outs[3:])


def _attn_body(S, slopes_ref, q_ref, k_ref, v_ref, o_ref, opat_ref, lpat_ref):
    hp = pl.program_id(1)
    blk = BAND_BLOCK
    lane = lax.broadcasted_iota(jnp.int32, (blk, LANES), 1)
    head0 = lane < HEAD_DIM

    for pi, (window, dil) in enumerate(DILATED_PATTERNS):
        n_steps = window // dil
        ngrp = RES // dil
        cs = blk // ngrp
        cs_bits = cs.bit_length() - 1
        assert ngrp * dil == RES and cs * ngrp == blk and cs % SUBLANES == 0 and S == RES * blk
        has_prev = ngrp > 1
        nk = 2 * blk if has_prev else blk
        nat = lambda a, ngrp=ngrp, cs=cs, cs_bits=cs_bits: (a & (cs - 1)) * ngrp + (a >> cs_bits)
        qi = lax.broadcasted_iota(jnp.int32, (blk, nk), 0)
        kj = lax.broadcasted_iota(jnp.int32, (blk, nk), 1)
        if has_prev:
            steps = (nat(qi) + blk) - (nat(kj & (blk - 1)) + (kj & blk))
        else:
            steps = qi - kj
        valid = (steps >= 0) & (steps <= n_steps)
        dist = (steps * dil).astype(F32)
        biases = []
        for hh in range(2):
            slope = slopes_ref[2 * hp + hh]
            biases.append(jnp.where(valid, -slope * dist, NEG_INF))
        prev_cols = kj < blk

        ones_k = jnp.ones((ATTN_BATCH, nk, LANES), BF16)

        def blocks(it, carry, dil=dil, ngrp=ngrp, cs=cs, has_prev=has_prev, biases=biases, prev_cols=prev_cols,
                   pi=pi, nk=nk, ones_k=ones_k):
            def chunk(idx, c, back):
                r = idx // ngrp
                j = jnp.maximum(idx % ngrp - back, 0)
                return pl.ds(pl.multiple_of((r + dil * c) * blk + cs * j, cs), cs)

            def load(ref, idx, back=0):
                return jnp.concatenate([ref[0, chunk(idx, c, back), :] for c in range(ngrp)], axis=0)

            qs, ks, vs, firsts = [], [], [], []
            for b in range(ATTN_BATCH):
                idx = it * ATTN_BATCH + b
                qs.append(load(q_ref, idx))
                if has_prev:
                    ks.append(jnp.concatenate([load(k_ref, idx, 1), load(k_ref, idx)], axis=0))
                    vs.append(jnp.concatenate([load(v_ref, idx, 1), load(v_ref, idx)], axis=0))
                    firsts.append(jnp.where(prev_cols & (idx % ngrp == 0), NEG_INF, 0.0))
                else:
                    ks.append(load(k_ref, idx))
                    vs.append(load(v_ref, idx))
            q3 = jnp.stack(qs) * ATTN_SCALE
            k3 = jnp.stack(ks).astype(BF16)
            v3 = jnp.concatenate([jnp.stack(vs).astype(BF16), ones_k], axis=2)
            outs, lses = [], []
            for hh in range(2):
                qm = jnp.where(head0 if hh == 0 else ~head0, q3, 0.0).astype(BF16)
                s = jnp.einsum("bqd,bkd->bqk", qm, k3, preferred_element_type=F32) + biases[hh]
                if has_prev:
                    s = s + jnp.stack(firsts)
                m = jnp.max(s, axis=2, keepdims=True)
                p = jnp.exp(s - m).astype(BF16)
                o = jnp.einsum("bqk,bkd->bqd", p, v3, preferred_element_type=F32)
                den = o[:, :, LANES:]
                outs.append(o[:, :, :LANES] / den)
                lses.append(m + jnp.log(den))
            o_pair = jnp.where(head0, outs[0], outs[1])
            l_pair = jnp.where(head0, lses[0], lses[1])
            for b in range(ATTN_BATCH):
                idx = it * ATTN_BATCH + b
                for c in range(ngrp):
                    opat_ref[pi, chunk(idx, c, 0), :] = o_pair[b, c * cs:(c + 1) * cs]
                    lpat_ref[pi, chunk(idx, c, 0), :] = l_pair[b, c * cs:(c + 1) * cs]
            return carry

        lax.fori_loop(0, RES // ATTN_BATCH, blocks, 0)

    chunk = 256

    def mix(c, carry):
        rows = pl.ds(pl.multiple_of(c * chunk, chunk), chunk)
        l0, l1, l2 = lpat_ref[0, rows, :], lpat_ref[1, rows, :], lpat_ref[2, rows, :]
        mx = jnp.maximum(jnp.maximum(l0, l1), l2)
        w0, w1, w2 = jnp.exp(l0 - mx), jnp.exp(l1 - mx), jnp.exp(l2 - mx)
        num = w0 * opat_ref[0, rows, :] + w1 * opat_ref[1, rows, :] + w2 * opat_ref[2, rows, :]
        o_ref[0, rows, :] = (num / (w0 + w1 + w2)).astype(o_ref.dtype)
        return carry

    lax.fori_loop(0, S // chunk, mix, 0)


def _attn(q, k, v, slopes):
    B, S, d_a = q.shape
    assert S % (BAND_BLOCK * max(d for _, d in DILATED_PATTERNS)) == 0
    npair = d_a // LANES
    spec = pl.BlockSpec((1, S, LANES), lambda b, h, *_: (b, 0, h))
    grid_spec = pltpu.PrefetchScalarGridSpec(
        num_scalar_prefetch=0,
        grid=(B, npair),
        in_specs=[pl.BlockSpec(memory_space=pltpu.SMEM), spec, spec, spec],
        out_specs=spec,
        scratch_shapes=[pltpu.VMEM((len(DILATED_PATTERNS), S, LANES), F32),
                        pltpu.VMEM((len(DILATED_PATTERNS), S, LANES), F32)],
    )
    return pl.pallas_call(
        functools.partial(_attn_body, S),
        grid_spec=grid_spec,
        out_shape=jax.ShapeDtypeStruct((B, S, d_a), BF16),
        compiler_params=_cparams(("arbitrary", "arbitrary")),
        name="dilated_attn",
    )(slopes, q, k, v)


def _layer_norm(z, g, b):
    mu = jnp.mean(z, axis=-1, keepdims=True)
    zc = z - mu
    var = jnp.mean(zc * zc, axis=-1, keepdims=True)
    return zc * lax.rsqrt(var + LN_EPS) * g + b


def _route(logits):
    n = logits.shape[0]
    lane = lax.broadcasted_iota(jnp.int32, (n, LANES), 1)
    lane_f = lane.astype(F32)
    big = float(LANES)
    is_outer = lane < N_EXPERT_GROUPS
    l1 = jnp.where(is_outer, logits, NEG_INF)
    m1 = jnp.max(l1, axis=1, keepdims=True)
    g_sel = jnp.min(jnp.where(l1 == m1, lane_f, big), axis=1, keepdims=True)
    v1 = 1.0 / jnp.sum(jnp.exp(l1 - m1), axis=1, keepdims=True)
    lo = ROUTER_OFF + g_sel * EXPERTS_PER_GROUP
    in_group = (lane_f >= lo) & (lane_f < lo + EXPERTS_PER_GROUP)
    l2 = jnp.where(in_group, logits, NEG_INF)
    ma = jnp.max(l2, axis=1, keepdims=True)
    ia = jnp.min(jnp.where(l2 == ma, lane_f, big), axis=1, keepdims=True)
    l2b = jnp.where(lane_f == ia, NEG_INF, l2)
    mb = jnp.max(l2b, axis=1, keepdims=True)
    ib = jnp.min(jnp.where(l2b == mb, lane_f, big), axis=1, keepdims=True)
    eb = jnp.exp(mb - ma)
    wa = 1.0 / (1.0 + eb)
    wb = eb / (1.0 + eb)
    return ia - ROUTER_OFF, ib - ROUTER_OFF, v1 * wa, v1 * wb


def _rank_and_meta(e0, e1, g0, g1, carry):
    n = e0.shape[0]
    lane_f = lax.broadcasted_iota(jnp.int32, (n, LANES), 1).astype(F32)
    oh0 = (lane_f == e0).astype(F32)
    oh1 = (lane_f == e1).astype(F32)
    both = oh0 + oh1
    ti = lax.broadcasted_iota(jnp.int32, (n, n), 0)
    tj = lax.broadcasted_iota(jnp.int32, (n, n), 1)
    tri = (tj < ti).astype(BF16)
    before = _dot(tri, both.astype(BF16)) + carry
    r0 = jnp.sum(before * oh0, axis=1, keepdims=True)
    r1 = jnp.sum(before * oh1, axis=1, keepdims=True)
    new_carry = carry + jnp.sum(both, axis=0, keepdims=True)
    lane = lax.broadcasted_iota(jnp.int32, (n, LANES), 1)
    meta = jnp.zeros((n, LANES), F32)
    for i, val in enumerate((e0, e1, g0, g1, r0, r1)):
        meta = jnp.where(lane == i, val, meta)
    return meta, new_carry


def _lane_group(shape, width):
    lane = lax.broadcasted_iota(jnp.int32, shape, len(shape) - 1)
    grp = jnp.zeros(shape, jnp.int32)
    for g in range(1, shape[-1] // width):
        grp = grp + (lane >= g * width).astype(jnp.int32)
    return grp


def _select_by_group(grp, vals):
    out = vals[-1]
    for g in range(len(vals) - 2, -1, -1):
        out = jnp.where(grp == g, vals[g], out)
    return out


def _store_token_rows(ref, val):
    n, d = val.shape
    nchunk = d // LANES
    for c in range(nchunk):
        ref[pl.ds(c, n, stride=nchunk), :] = val[:, c * LANES:(c + 1) * LANES]


def _load_token_rows(ref, n, nchunk):
    return jnp.concatenate([ref[pl.ds(c, n, stride=nchunk), :] for c in range(nchunk)], axis=1)


HALO = 16


def _mix_body(tm, nt, n_tiles, d_b, alpha, x_ref, at_ref, unperm_ref, u_ref, uh_ref, qc_ref, kmT_ref, vm_ref,
              wo_ref, wp_ref, ps_ref, g1_ref, b1_ref, wr_ref, br_ref, cin_ref, hs_ref,
              h1_ref, meta_ref, cnt_ref, carry_ref):
    g = pl.program_id(0)

    @pl.when(g == 0)
    def _():
        carry_ref[...] = cin_ref[...]

    @pl.when(g == n_tiles)
    def _():
        ns_rows = hs_ref.shape[0]
        h1_ref[0:ns_rows, :] = hs_ref[...]
        h1_ref[ns_rows:, :] = jnp.zeros((h1_ref.shape[0] - ns_rows, LANES), F32)

    @pl.when(g < n_tiles)
    def _():
        _mix_tile(tm, g % nt, d_b, alpha, x_ref, at_ref, unperm_ref, u_ref, uh_ref, qc_ref, kmT_ref, vm_ref, wo_ref,
                  wp_ref, ps_ref, g1_ref, b1_ref, wr_ref, br_ref, h1_ref, meta_ref, cnt_ref, carry_ref)


def _mix_tile(tm, i, d_b, alpha, x_ref, at_ref, unperm_ref, u_ref, uh_ref, qc_ref, kmT_ref, vm_ref, wo_ref, wp_ref,
              ps_ref, g1_ref, b1_ref, wr_ref, br_ref, h1_ref, meta_ref, cnt_ref, carry_ref):
    u = u_ref[0]
    halo = jnp.where(i == 0, 0.0, uh_ref[0])
    ext = jnp.concatenate([halo, halo, u], axis=0)
    s2 = ext + pltpu.roll(ext, 1, 0)
    s4 = s2 + pltpu.roll(s2, 2, 0)
    s8 = s4 + pltpu.roll(s4, 4, 0)
    s16 = s8 + pltpu.roll(s8, 8, 0)
    grp = _lane_group((tm, d_b), d_b // len(POOL_WINDOWS))
    sums = [s[2 * HALO:] for s in (s2, s4, s8, s16)]
    win = _select_by_group(grp, sums)
    wlen = _select_by_group(grp, [jnp.int32(w) for w in POOL_WINDOWS])
    pos = i * tm + lax.broadcasted_iota(jnp.int32, (tm, d_b), 0)
    cnt = jnp.minimum(wlen, pos + 1).astype(F32)
    diff = win / cnt - u
    pool = _dot(diff.astype(BF16), wp_ref[...]) * ps_ref[...]

    qc = qc_ref[0] * ATTN_SCALE
    d_c = qc.shape[1]
    kmT = kmT_ref[0].astype(BF16)
    vm = vm_ref[0].astype(BF16)
    hl = _lane_group((tm, d_c), HEAD_DIM)
    memo = jnp.zeros((tm, d_c), F32)
    for h in range(d_c // HEAD_DIM):
        qm = jnp.where(hl == h, qc, 0.0).astype(BF16)
        s = _dot(qm, kmT)
        p = jnp.exp(s - jnp.max(s, axis=1, keepdims=True))
        den = jnp.sum(p, axis=1, keepdims=True)
        o = _dot(p.astype(BF16), vm) / den
        memo = jnp.where(hl == h, o, memo)

    at_rl = jnp.concatenate([at_ref[0, r] for r in range(RES)], axis=0)
    attn = _dot(unperm_ref[...], at_rl).astype(BF16)
    cat = jnp.concatenate([attn, pool.astype(BF16), memo.astype(BF16)], axis=1)
    mixv = _dot(cat, wo_ref[...])
    h1 = _layer_norm(alpha * x_ref[0] + mixv, g1_ref[...], b1_ref[...])
    _store_token_rows(h1_ref, h1)

    h_hi = h1.astype(BF16)
    h_lo = (h1 - h_hi.astype(F32)).astype(BF16)
    part = _dot(h_hi, wr_ref[...])
    logits = part[:, :LANES] + part[:, LANES:] + _dot(h_lo, wr_ref[:, :LANES]) + br_ref[...]
    e0, e1, g0, g1 = _route(logits)
    meta, new_carry = _rank_and_meta(e0, e1, g0, g1, carry_ref[0:1, :])
    meta_ref[...] = meta
    carry_ref[...] = jnp.broadcast_to(new_carry, carry_ref.shape)
    cnt_ref[...] = carry_ref[...]


def _mix(x, attn_rm, u, qc, kmT, vm, wo_bf, wp_bd_bf, pool_scale, g1, b1, wr, br, counts_in, h1s, alpha, tm):
    B, S, D = x.shape
    d_a, d_b, d_c = attn_rm.shape[2], u.shape[2], qc.shape[2]
    n_mem = vm.shape[1]
    nt = S // tm
    n_tiles = B * nt
    nchunk = D // LANES
    assert tm % (RES * SUBLANES * 2) == 0 and h1s.shape[0] < tm * nchunk
    last = n_tiles - 1
    bi = lambda g: (jnp.minimum(g, last) // nt, jnp.minimum(g, last) % nt)
    row = lambda n: pl.BlockSpec((1, tm, n), lambda g: bi(g) + (0,))
    full = lambda shape: pl.BlockSpec(shape, lambda g: (0,) * len(shape))
    per_b = lambda shape: pl.BlockSpec((1,) + shape, lambda g: (bi(g)[0], 0, 0))
    halo_spec = pl.BlockSpec((1, HALO, d_b), lambda g: (bi(g)[0], jnp.maximum(bi(g)[1] * (tm // HALO) - 1, 0), 0))
    at_spec = pl.BlockSpec((1, RES, tm // RES, d_a), lambda g: (bi(g)[0], 0, bi(g)[1], 0))
    dst = jnp.arange(tm)
    unperm = ((dst % RES) * (tm // RES) + dst // RES)[:, None] == jnp.arange(tm)[None, :]
    return pl.pallas_call(
        functools.partial(_mix_body, tm, nt, n_tiles, d_b, alpha),
        grid=(n_tiles + 1,),
        in_specs=[row(D), at_spec, full((tm, tm)), row(d_b), halo_spec, row(d_c), per_b((d_c, n_mem)),
                  per_b((n_mem, d_c)), full((D, D)), full((d_b, d_b)), full((1, d_b)), full((1, D)), full((1, D)),
                  full((D, 2 * LANES)), full((1, LANES)), full((SUBLANES, LANES)), full(h1s.shape)],
        out_specs=[pl.BlockSpec((tm * nchunk, LANES), lambda g: (g, 0)),
                   pl.BlockSpec((tm, LANES), lambda g: (jnp.minimum(g, last), 0)),
                   pl.BlockSpec((SUBLANES, LANES), lambda g: (0, 0))],
        out_shape=[jax.ShapeDtypeStruct(((n_tiles + 1) * tm * nchunk, LANES), F32),
                   jax.ShapeDtypeStruct((B * S, LANES), F32),
                   jax.ShapeDtypeStruct((SUBLANES, LANES), F32)],
        scratch_shapes=[pltpu.VMEM((SUBLANES, LANES), F32)],
        compiler_params=_cparams(("arbitrary",)),
        name="mix_ln1_router",
    )(x, attn_rm.reshape(B, RES, S // RES, d_a), unperm.astype(BF16), u, u, qc, kmT, vm, wo_bf, wp_bd_bf,
      pool_scale, g1, b1, wr, br, counts_in, h1s)


def _columns_to_lanes(cols, rows):
    lane = lax.broadcasted_iota(jnp.int32, (rows, LANES), 1)
    tile = jnp.zeros((rows, LANES), F32)
    for t, c in enumerate(cols):
        tile = jnp.where(lane == t, c, tile)
    return tile


def _sproj_body(x_ref, wT_ref, o_ref):
    o_ref[...] = _dot_nt(wT_ref[...], x_ref[...], precision=HIGHEST)


def _sproj(x, wT):
    vm = pl.BlockSpec(memory_space=pltpu.VMEM)
    return pl.pallas_call(
        _sproj_body, in_specs=[vm, vm], out_specs=vm,
        out_shape=jax.ShapeDtypeStruct((wT.shape[0], x.shape[0]), F32),
        compiler_params=pltpu.CompilerParams(vmem_limit_bytes=VMEM_LIMIT),
        name="decode_proj",
    )(x, wT)


def _decode_body(T, wb, slopes_ref, pT_ref, ck_ref, cv_ref, mk_ref, mv_ref,
                 nk_ref, nv_ref, at_ref, mo_ref, qkv_ref):
    H = ck_ref.shape[1]
    d_a = H * HEAD_DIM
    ntok = pT_ref.shape[1]
    shift = (ntok - T * pl.program_id(0)) % ntok
    qkv_ref[...] = pltpu.roll(pT_ref[...], shift, 1)

    def multiplicity(dist):
        mult = jnp.zeros(dist.shape, F32)
        for window, dil in DILATED_PATTERNS:
            assert dil & (dil - 1) == 0
            mult = mult + (((dist & (dil - 1)) == 0) & (dist <= window) & (dist >= 0)).astype(F32)
        return mult

    OWN = min(d for _, d in DILATED_PATTERNS if d > 1)
    far = wb - LANES
    assert T <= OWN and far % OWN == 0 and wb % OWN == 0
    assert all(w <= LANES for w, d in DILATED_PATTERNS if d == 1) and all(d % OWN == 0 for _, d in DILATED_PATTERNS if d > 1)
    lane_far = lax.broadcasted_iota(jnp.int32, (T, far), 1)
    t_far = lax.broadcasted_iota(jnp.int32, (T, far), 0)
    owned = (lane_far & (OWN - 1)) == t_far
    dist_far = wb + t_far - lane_far
    mult_far = jnp.where(owned, multiplicity(dist_far), 0.0)
    dist_far = dist_far.astype(F32)
    t_near = lax.broadcasted_iota(jnp.int32, (T, LANES), 0)
    dist_near = wb + t_near - (far + lax.broadcasted_iota(jnp.int32, (T, LANES), 1))
    mult_near = multiplicity(dist_near)
    dist_near = dist_near.astype(F32)
    dist_new = lax.broadcasted_iota(jnp.int32, (T, T), 0) - lax.broadcasted_iota(jnp.int32, (T, T), 1)
    mult_new = multiplicity(dist_new)
    dist_new = jnp.maximum(dist_new, 0).astype(F32)

    lane_t = lax.broadcasted_iota(jnp.int32, (HEAD_DIM, LANES), 1)
    own_t = lane_t & (OWN - 1)
    last = wb - LANES
    n_far = far // LANES

    def head(h, carry):
        slope = slopes_ref[h]
        kTc = ck_ref[0, h]
        vTc = cv_ref[0, h]
        r0 = pl.multiple_of(h * HEAD_DIM, HEAD_DIM)
        qT = qkv_ref[pl.ds(r0, HEAD_DIM), :][:, 0:T] * ATTN_SCALE
        kTn = qkv_ref[pl.ds(d_a + r0, HEAD_DIM), :][:, 0:T]
        vTn = qkv_ref[pl.ds(2 * d_a + r0, HEAD_DIM), :][:, 0:T]
        tile = lambda a, c: a[:, c * LANES:(c + 1) * LANES]

        qb = [jnp.broadcast_to(qT[:, t:t + 1], (HEAD_DIM, LANES)) for t in range(T)]
        qpat = qb[T - 1]
        for t in range(T - 1):
            qpat = jnp.where(own_t == t, qb[t], qpat)
        s_far = jnp.concatenate([jnp.sum(tile(kTc, c) * qpat, axis=0, keepdims=True) for c in range(n_far)], axis=1)
        s_far = jnp.where(mult_far > 0, s_far - slope * dist_far, NEG_INF)
        k_near = tile(kTc, n_far)
        s_near = jnp.concatenate([jnp.sum(k_near * qb[t], axis=0, keepdims=True) for t in range(T)], axis=0)
        s_near = jnp.where(mult_near > 0, s_near - slope * dist_near, NEG_INF)
        s_new = jnp.concatenate([jnp.sum(kTn * qT[:, t:t + 1], axis=0, keepdims=True) for t in range(T)], axis=0)
        s_new = jnp.where(mult_new > 0, s_new - slope * dist_new, NEG_INF)
        rmax = lambda a: jnp.max(a, axis=1, keepdims=True)
        m = jnp.maximum(jnp.maximum(rmax(s_far), rmax(s_near)), rmax(s_new))
        p_far = mult_far * jnp.exp(s_far - m)
        p_near = mult_near * jnp.exp(s_near - m)
        p_new = mult_new * jnp.exp(s_new - m)
        rsum = lambda a: jnp.sum(a, axis=1, keepdims=True)
        den = rsum(p_far) + rsum(p_near) + rsum(p_new)

        p_comb = jnp.sum(p_far, axis=0, keepdims=True)
        acc = tile(vTc, 0) * tile(p_comb, 0)
        for c in range(1, n_far):
            acc = acc + tile(vTc, c) * tile(p_comb, c)
        v_near = tile(vTc, n_far)
        cols = []
        for t in range(T):
            o = (jnp.sum(jnp.where(own_t == t, acc, 0.0) + v_near * p_near[t:t + 1], axis=1, keepdims=True)
                 + jnp.sum(vTn * p_new[t:t + 1], axis=1, keepdims=True))
            cols.append(o / den[t:t + 1])
        at_ref[0, pl.ds(r0, HEAD_DIM), :] = _columns_to_lanes(cols, HEAD_DIM)

        rk = pltpu.roll(kTc, wb - T, 1)
        rv = pltpu.roll(vTc, wb - T, 1)
        nk_ref[0, h] = rk
        nv_ref[0, h] = rv
        newk = jnp.zeros((HEAD_DIM, LANES), F32)
        newv = jnp.zeros((HEAD_DIM, LANES), F32)
        for t in range(T):
            newk = jnp.where(lane_t == LANES - T + t, kTn[:, t:t + 1], newk)
            newv = jnp.where(lane_t == LANES - T + t, vTn[:, t:t + 1], newv)
        nk_ref[0, h, :, last:] = jnp.where(lane_t >= LANES - T, newk, rk[:, last:])
        nv_ref[0, h, :, last:] = jnp.where(lane_t >= LANES - T, newv, rv[:, last:])
        return carry

    lax.fori_loop(0, H, head, 0, unroll=2)

    Hc = mk_ref.shape[1]
    rows = []
    for h in range(Hc):
        kT = mk_ref[0, h]
        r0 = h * HEAD_DIM
        qT = qkv_ref[3 * d_a + r0:3 * d_a + r0 + HEAD_DIM, :][:, 0:T] * ATTN_SCALE
        rows += [jnp.sum(kT * qT[:, t:t + 1], axis=0, keepdims=True) for t in range(T)]
    s = jnp.concatenate(rows, axis=0)
    p = jnp.exp(s - jnp.max(s, axis=1, keepdims=True))
    p = p / jnp.sum(p, axis=1, keepdims=True)
    for h in range(Hc):
        vT = mv_ref[0, h]
        cols = [jnp.sum(vT * p[h * T + t:h * T + t + 1], axis=1, keepdims=True) for t in range(T)]
        mo_ref[0, h * HEAD_DIM:(h + 1) * HEAD_DIM, :] = _columns_to_lanes(cols, HEAD_DIM)


def _decode(projT, T, ck, cv, mk, mv, slopes):
    DB, H, hd, wb = ck.shape
    _, Hc, _, n_mem = mk.shape
    d_a, d_c = H * hd, Hc * hd
    assert wb >= max(w for w, _ in DILATED_PATTERNS) and wb % LANES == 0 and T <= LANES
    assert projT.shape == (3 * d_a + d_c, DB * T)
    cache = pl.BlockSpec((1, H, hd, wb), lambda b, *_: (b, 0, 0, 0))
    memc = pl.BlockSpec((1, Hc, hd, n_mem), lambda b, *_: (b, 0, 0, 0))
    grid_spec = pltpu.PrefetchScalarGridSpec(
        num_scalar_prefetch=0,
        grid=(DB,),
        in_specs=[pl.BlockSpec(memory_space=pltpu.SMEM),
                  pl.BlockSpec(projT.shape, lambda b, *_: (0, 0)),
                  cache, cache, memc, memc],
        out_specs=[cache, cache,
                   pl.BlockSpec((1, d_a, LANES), lambda b, *_: (b, 0, 0)),
                   pl.BlockSpec((1, d_c, LANES), lambda b, *_: (b, 0, 0))],
        scratch_shapes=[pltpu.VMEM(projT.shape, F32)],
    )
    nk, nv, at, mo = pl.pallas_call(
        functools.partial(_decode_body, T, wb),
        grid_spec=grid_spec,
        out_shape=[jax.ShapeDtypeStruct(ck.shape, F32), jax.ShapeDtypeStruct(cv.shape, F32),
                   jax.ShapeDtypeStruct((DB, d_a, LANES), F32), jax.ShapeDtypeStruct((DB, d_c, LANES), F32)],
        compiler_params=_cparams(("arbitrary",)),
        name="decode_attn_cache",
    )(slopes, projT, ck, cv, mk, mv)
    return nk, nv, at[:, :, :T], mo[:, :, :T]


def _smix_body(T, alpha, pos0, x_ref, at_ref, mo_ref, st_ref, wu_ref, wo_ref, wp_ref, ps_ref, g1_ref, b1_ref,
               wr_ref, br_ref, cin_ref, h1_ref, meta_ref, cnt_ref, pool_ref):
    n = x_ref.shape[0]
    db = n // T
    pb = st_ref.shape[0]
    d_b = st_ref.shape[2]
    x = x_ref[...]
    u_new = _dot(x, wu_ref[...], precision=HIGHEST)
    seq = [st_ref[j] for j in range(pb)] + [u_new[t * db:(t + 1) * db] for t in range(T)]
    for j in range(pb):
        pool_ref[j] = seq[j + T]
    grp = _lane_group((db, d_b), d_b // len(POOL_WINDOWS))
    diffs = []
    for t in range(T):
        j = pb + t
        per_w = []
        for w in POOL_WINDOWS:
            acc = seq[j]
            for back in range(1, w):
                if j - back >= 0:
                    acc = acc + seq[j - back]
            per_w.append(acc / float(min(w, pos0 + j + 1)))
        diffs.append(_select_by_group(grp, per_w) - seq[j])
    diff = jnp.concatenate(diffs, axis=0)
    pool = _dot(diff, wp_ref[...], precision=HIGHEST) * ps_ref[...]
    cat = jnp.concatenate([at_ref[...], pool, mo_ref[...]], axis=1)
    mixv = _dot(cat, wo_ref[...], precision=HIGHEST)
    h1 = _layer_norm(alpha * x + mixv, g1_ref[...], b1_ref[...])
    _store_token_rows(h1_ref, h1)
    logits = _dot(h1, wr_ref[...], precision=HIGHEST) + br_ref[...]
    e0, e1, g0, g1 = _route(logits)
    meta, new_carry = _rank_and_meta(e0, e1, g0, g1, cin_ref[0:1, :])
    meta_ref[...] = meta
    cnt_ref[...] = jnp.broadcast_to(new_carry, cnt_ref.shape)


def _smix(x_tb, attn_tb, memo_tb, state, wu, wo, wp_bd, pool_scale, g1, b1, wr, br, counts_in, alpha, T):
    n, D = x_tb.shape
    pb, db, d_b = state.shape
    nchunk = D // LANES
    vm = pl.BlockSpec(memory_space=pltpu.VMEM)
    return pl.pallas_call(
        functools.partial(_smix_body, T, alpha, PAST_LEN - pb),
        in_specs=[vm] * 13,
        out_specs=[vm] * 4,
        out_shape=[jax.ShapeDtypeStruct((n * nchunk, LANES), F32), jax.ShapeDtypeStruct((n, LANES), F32),
                   jax.ShapeDtypeStruct((SUBLANES, LANES), F32), jax.ShapeDtypeStruct((pb, db, d_b), F32)],
        compiler_params=pltpu.CompilerParams(vmem_limit_bytes=VMEM_LIMIT),
        name="decode_mix_ln1_router",
    )(x_tb, attn_tb, memo_tb, state, wu, wo, wp_bd, pool_scale, g1, b1, wr, br, counts_in)


def _slot_map_body(blk, pos_ref, fill_ref, inv_ref, sem):
    g = pl.program_id(0)

    @pl.when(g == 0)
    def _():
        c = pltpu.make_async_copy(fill_ref, inv_ref, sem)
        c.start()
        c.wait()

    group = math.gcd(blk, 32)

    def body(j, carry):
        token0 = (g * blk + j * group) // 2
        for u in range(group):
            inv_ref[pos_ref[0, 0, j * group + u]] = token0 + u // 2
        return carry

    lax.fori_loop(0, blk // group, body, 0)


def _slot_map(pos, n_slots, zero_token, blk):
    n2 = pos.shape[0]
    assert n2 % blk == 0 and n_slots % 1024 == 0
    return pl.pallas_call(
        functools.partial(_slot_map_body, blk),
        grid=(n2 // blk,),
        in_specs=[pl.BlockSpec((1, 1, blk), lambda g: (g, 0, 0), memory_space=pltpu.SMEM),
                  pl.BlockSpec(memory_space=pl.ANY)],
        out_specs=pl.BlockSpec(memory_space=pltpu.SMEM),
        out_shape=jax.ShapeDtypeStruct((n_slots,), jnp.int32),
        scratch_shapes=[pltpu.SemaphoreType.DMA(())],
        compiler_params=_cparams(("arbitrary",)),
        name="slot_map",
    )(pos.reshape(n2 // blk, 1, blk), jnp.full((n_slots,), zero_token, jnp.int32))


def _expert_body(tmx, nchunk, te_ref, nt_ref, cur_ref, nxt_ref, h_ref, wg_ref, wu_ref, wd_ref, ys_ref,
                 buf_a, buf_b, wg_bf, wu_bf, wd_bf, sems):
    i = pl.program_id(0)
    n_tiles = nt_ref[0]

    def copy(idx_ref, buf, s, j):
        return pltpu.make_async_copy(h_ref.at[idx_ref[0, 0, j]], buf.at[pl.ds(j * nchunk, nchunk)], sems.at[s])

    def start(idx_ref, buf, s):
        for j in range(tmx):
            copy(idx_ref, buf, s, j).start(priority=j % 2)

    def wait(buf, s):
        def body(j, carry):
            copy(cur_ref, buf, s, j).wait()
            return carry

        lax.fori_loop(0, tmx, body, 0, unroll=DMA_UNROLL)

    @pl.when(i == 0)
    def _():
        start(cur_ref, buf_a, 0)

    it = jnp.minimum(i, pl.num_programs(0) - 2)
    @pl.when((i < n_tiles) & ((i == 0) | (te_ref[it] != te_ref[jnp.maximum(it - 1, 0)])))
    def _():
        wg_bf[...] = wg_ref[0].astype(BF16)
        wu_bf[...] = wu_ref[0].astype(BF16)
        wd_bf[...] = wd_ref[0].astype(BF16)

    def step(cur, s_cur, nxt, s_nxt):
        wait(cur, s_cur)
        start(nxt_ref, nxt, s_nxt)
        x = _load_token_rows(cur, tmx, nchunk).astype(BF16)
        hg = _dot(x, wg_bf[...])
        hu = _dot(x, wu_bf[...])
        a = (hg * jax.nn.sigmoid(hg) * hu).astype(BF16)
        y = _dot(a, wd_bf[...])
        _store_token_rows(ys_ref, y)

    @pl.when((i < n_tiles) & (i % 2 == 0))
    def _():
        step(buf_a, 0, buf_b, 1)

    @pl.when((i < n_tiles) & (i % 2 == 1))
    def _():
        step(buf_b, 1, buf_a, 0)

    @pl.when(i >= n_tiles)
    def _():
        ys_ref[...] = jnp.zeros_like(ys_ref)

    @pl.when((i == n_tiles) & (n_tiles % 2 == 0))
    def _():
        wait(buf_a, 0)

    @pl.when((i == n_tiles) & (n_tiles % 2 == 1))
    def _():
        wait(buf_b, 1)


def _experts(tile_expert, n_tiles, inv, h_rows, wg, wu, wd, tmx, max_tiles):
    E, D, F = wg.shape
    nchunk = D // LANES
    inv3 = inv.reshape(-1, 1, tmx)
    last_blk = inv3.shape[0] - 1
    assert last_blk >= max_tiles and tmx % DMA_UNROLL == 0
    weights = lambda shape: pl.BlockSpec((1,) + shape, lambda i, te, nt: (te[jnp.minimum(i, max_tiles - 1)], 0, 0))
    rows = pltpu.VMEM((tmx * nchunk, LANES), F32)
    grid_spec = pltpu.PrefetchScalarGridSpec(
        num_scalar_prefetch=2,
        grid=(max_tiles + 1,),
        in_specs=[pl.BlockSpec((1, 1, tmx), lambda i, *_: (i, 0, 0), memory_space=pltpu.SMEM),
                  pl.BlockSpec((1, 1, tmx), lambda i, *_: (jnp.minimum(i + 1, last_blk), 0, 0),
                               memory_space=pltpu.SMEM),
                  pl.BlockSpec(memory_space=pl.ANY), weights((D, F)), weights((D, F)), weights((F, D))],
        out_specs=pl.BlockSpec((tmx * nchunk, LANES), lambda i, *_: (i, 0)),
        scratch_shapes=[rows, rows, pltpu.VMEM((D, F), BF16), pltpu.VMEM((D, F), BF16), pltpu.VMEM((F, D), BF16),
                        pltpu.SemaphoreType.DMA((2,))],
    )
    return pl.pallas_call(
        functools.partial(_expert_body, tmx, nchunk),
        grid_spec=grid_spec,
        out_shape=jax.ShapeDtypeStruct(((max_tiles + 1) * tmx * nchunk, LANES), F32),
        compiler_params=_cparams(("arbitrary",)),
        name="expert_swiglu",
    )(tile_expert, n_tiles, inv3, inv3, h_rows, wg, wu, wd)


def _combine_body(tm, nchunk, alpha, pos_ref, nxt_ref, h1_ref, meta_ref, ys_ref, g2_ref, b2_ref, o_ref, buf_a, buf_b,
                  sems):
    i = pl.program_id(0)
    last = pl.num_programs(0) - 1

    def copy(idx_ref, buf, s, t, k):
        dst = buf.at[k, pl.ds(t * nchunk, nchunk)]
        return pltpu.make_async_copy(ys_ref.at[idx_ref[0, 0, 2 * t + k]], dst, sems.at[s])

    def start(idx_ref, buf, s):
        for t in range(tm):
            for k in range(2):
                copy(idx_ref, buf, s, t, k).start(priority=k)

    def wait(buf, s):
        def body(t, carry):
            for k in range(2):
                copy(pos_ref, buf, s, t, k).wait()
            return carry

        lax.fori_loop(0, tm, body, 0, unroll=DMA_UNROLL)

    @pl.when(i == 0)
    def _():
        start(pos_ref, buf_a, 0)

    def step(cur, s_cur, nxt, s_nxt):
        wait(cur, s_cur)
        start(nxt_ref, nxt, s_nxt)
        h1 = _load_token_rows(h1_ref, tm, nchunk)
        y0 = _load_token_rows(cur.at[0], tm, nchunk)
        y1 = _load_token_rows(cur.at[1], tm, nchunk)
        meta = meta_ref[...]
        lane = lax.broadcasted_iota(jnp.int32, meta.shape, 1)
        gate0 = jnp.sum(jnp.where(lane == 2, meta, 0.0), axis=1, keepdims=True)
        gate1 = jnp.sum(jnp.where(lane == 3, meta, 0.0), axis=1, keepdims=True)
        f = gate0 * y0 + gate1 * y1
        o_ref[...] = _layer_norm(alpha * h1 + f, g2_ref[...], b2_ref[...])

        @pl.when(i == last)
        def _():
            wait(nxt, s_nxt)

    @pl.when(i % 2 == 0)
    def _():
        step(buf_a, 0, buf_b, 1)

    @pl.when(i % 2 == 1)
    def _():
        step(buf_b, 1, buf_a, 0)


def _combine(pos, tok0, n, h1_flat, meta, ys, g2, b2, alpha, tm):
    D = g2.shape[1]
    nchunk = D // LANES
    assert n % tm == 0 and tok0 % tm == 0
    pos3 = pos.reshape(-1, 1, 2 * tm)
    off = tok0 // tm
    last = off + n // tm - 1
    return pl.pallas_call(
        functools.partial(_combine_body, tm, nchunk, alpha),
        grid=(n // tm,),
        in_specs=[pl.BlockSpec((1, 1, 2 * tm), lambda i: (i + off, 0, 0), memory_space=pltpu.SMEM),
                  pl.BlockSpec((1, 1, 2 * tm), lambda i: (jnp.minimum(i + off + 1, last), 0, 0),
                               memory_space=pltpu.SMEM),
                  pl.BlockSpec((tm * nchunk, LANES), lambda i: (i + off, 0)),
                  pl.BlockSpec((tm, LANES), lambda i: (i, 0)),
                  pl.BlockSpec(memory_space=pl.ANY),
                  pl.BlockSpec((1, D), lambda i: (0, 0)), pl.BlockSpec((1, D), lambda i: (0, 0))],
        out_specs=pl.BlockSpec((tm, D), lambda i: (i, 0)),
        out_shape=jax.ShapeDtypeStruct((n, D), F32),
        scratch_shapes=[pltpu.VMEM((2, tm * nchunk, LANES), F32), pltpu.VMEM((2, tm * nchunk, LANES), F32),
                        pltpu.SemaphoreType.DMA((2,))],
        compiler_params=_cparams(("arbitrary",)),
        name="combine_ln2",
    )(pos3, pos3, h1_flat, meta, ys, g2, b2)


def _block_diag(w):
    g, a, b = w.shape
    eye = jnp.eye(g, dtype=w.dtype)
    return (eye[:, None, :, None] * w[:, :, None, :]).reshape(g * a, g * b)


def _layer(h_p, h_s, win_k, win_v, pool_st, mem_k, mem_v, mem_prompt,
           w_in, w_mem_kv, w_pool, pool_scale, w_o, ln1_g, ln1_b, ln2_g, ln2_b,
           w_r1, b_r1, w_r2, b_r2, w_gate, w_up, w_down, alpha):
    B, S, D = h_p.shape
    DB, T, _ = h_s.shape
    H = win_k.shape[2]
    Hc = mem_k.shape[2]
    d_a, d_c = H * HEAD_DIM, Hc * HEAD_DIM
    d_b = pool_st.shape[2]
    nchunk = D // LANES
    slopes = 2.0 ** (-8.0 * jnp.arange(1, H + 1, dtype=F32) / H)

    w_in_bf = w_in.astype(BF16)
    w_inT = w_in.T
    wkvT_bf = w_inT[d_a:3 * d_a].astype(BF16)
    wp_bd = _block_diag(w_pool)
    ps = pool_scale.reshape(1, d_b)
    g1, b1 = ln1_g.reshape(1, D), ln1_b.reshape(1, D)
    g2, b2 = ln2_g.reshape(1, D), ln2_b.reshape(1, D)
    n_r = N_EXPERT_GROUPS + N_EXPERTS
    wr = jnp.concatenate([w_r1, jnp.transpose(w_r2, (1, 0, 2)).reshape(D, N_EXPERTS),
                          jnp.zeros((D, LANES - n_r), F32)], axis=1)
    br = jnp.concatenate([b_r1, b_r2.reshape(-1), jnp.zeros((LANES - n_r,), F32)]).reshape(1, LANES)

    ck = jnp.transpose(win_k, (0, 2, 3, 1))
    cv = jnp.transpose(win_v, (0, 2, 3, 1))
    mk = jnp.transpose(mem_k, (0, 2, 3, 1))
    mv = jnp.transpose(mem_v, (0, 2, 3, 1))
    w_qkvqcT = jnp.concatenate([w_inT[:3 * d_a], w_inT[3 * d_a + d_b:]], axis=0)
    projT = _sproj(h_s.reshape(DB * T, D), w_qkvqcT)
    nk, nv, attn_sT, memo_sT = _decode(projT, T, ck, cv, mk, mv, slopes)
    to_tb = lambda a: jnp.transpose(a, (2, 0, 1)).reshape(T * DB, a.shape[1])
    x_tb = jnp.transpose(h_s, (1, 0, 2)).reshape(T * DB, D)
    state = jnp.transpose(pool_st, (1, 0, 2))
    h1s, meta_s, cnt_s, new_pool = _smix(x_tb, to_tb(attn_sT), to_tb(memo_sT), state,
                                         w_in[:, 3 * d_a:3 * d_a + d_b], w_o, wp_bd, ps, g1, b1, wr, br,
                                         jnp.zeros((SUBLANES, LANES), F32), alpha, T)

    w_memT = w_mem_kv.T
    kmT, vm, vmT = _memproj(mem_prompt, w_memT[:d_c].astype(BF16), w_mem_kv[:, d_c:].astype(BF16),
                            w_memT[d_c:].astype(BF16))
    q, k, v, u, qc, kT, vT = _proj(h_p, w_in_bf, wkvT_bf, d_a, d_b, d_c, ts=512)
    attn = _attn(q, k, v, slopes)
    wr_hi = lax.bitcast_convert_type(lax.bitcast_convert_type(wr, jnp.uint32) & jnp.uint32(0xFFFF0000), F32)
    wr_split = jnp.concatenate([wr_hi.astype(BF16), (wr - wr_hi).astype(BF16)], axis=1)
    h_rows, meta_p, cnt_all = _mix(h_p, attn, u, qc, kmT, vm, w_o.astype(BF16), wp_bd.astype(BF16), ps, g1, b1,
                                   wr_split, br, cnt_s, h1s, alpha, tm=512)

    tmx = 256
    n_p, n_s = B * S, DB * T
    n = n_p + n_s
    counts = cnt_all[0, :N_EXPERTS].astype(jnp.int32)
    padded = (counts + tmx - 1) // tmx * tmx
    seg_end = jnp.cumsum(padded)
    seg_off = seg_end - padded
    meta = jnp.concatenate([meta_p, meta_s], axis=0)
    e_ids = meta[:, 0:2].astype(jnp.int32)
    lookup = lambda table, idx: jnp.sum(jnp.where(idx[..., None] == jnp.arange(N_EXPERTS), table, 0), axis=-1)
    pos = (lookup(seg_off, e_ids) + meta[:, 4:6].astype(jnp.int32)).reshape(-1)
    max_tiles = (2 * n) // tmx + N_EXPERTS
    n_tiles = (seg_end[-1] // tmx).astype(jnp.int32).reshape(1)
    tile_row0 = jnp.arange(max_tiles, dtype=jnp.int32) * tmx
    tile_expert = jnp.sum((tile_row0[:, None] >= seg_end[None, :]).astype(jnp.int32), axis=1)
    tile_expert = jnp.minimum(tile_expert, N_EXPERTS - 1)

    tok_tile = math.gcd(math.gcd(n_p, n_s), 128)
    n_slots = -(-((max_tiles + 1) * tmx) // 1024) * 1024
    inv = _slot_map(pos, n_slots, n, blk=math.gcd(2 * n, 2048))
    ys = _experts(tile_expert, n_tiles, inv, h_rows.reshape(-1, nchunk, LANES),
                  w_gate.reshape(N_EXPERTS, D, -1),
                  w_up.reshape(N_EXPERTS, D, -1), w_down.reshape(N_EXPERTS, -1, D), tmx, max_tiles)
    ys3 = ys.reshape(-1, nchunk, LANES)
    y_p = _combine(pos, 0, n_p, h_rows, meta_p, ys3, g2, b2, alpha, tm=tok_tile)
    y_s = _combine(pos, n_p, n_s, h_rows, meta_s, ys3, g2, b2, alpha, tm=tok_tile)

    y_p = y_p.reshape(B, S, D)
    y_s = jnp.transpose(y_s.reshape(T, DB, D), (1, 0, 2))
    heads = lambda a, h: jnp.transpose(a.reshape(a.shape[0], h, HEAD_DIM, a.shape[2]), (0, 3, 1, 2))
    wbp = min(max(w for w, _ in DILATED_PATTERNS), S)
    new_wk_p = heads(kT, H)[:, S - wbp:]
    new_wv_p = heads(vT, H)[:, S - wbp:]
    pb = pool_st.shape[1]
    new_pool_p = u[:, S - pb:]
    new_mk_p = heads(kmT, Hc)
    new_mv_p = heads(vmT, Hc)
    new_wk_s = jnp.transpose(nk, (0, 3, 1, 2))
    new_wv_s = jnp.transpose(nv, (0, 3, 1, 2))
    new_pool_s = jnp.transpose(new_pool, (1, 0, 2))
    return (y_p, y_s, new_wk_p, new_wv_p, new_pool_p, new_mk_p, new_mv_p, new_wk_s, new_wv_s, new_pool_s)


def kernel(x_prompt, x_sample, cache_win_k, cache_win_v, state_pool, cache_mem_k, cache_mem_v, mem_prompt, w_in, w_mem_kv, w_pool, pool_scale, w_o, ln1_g, ln1_b, ln2_g, ln2_b, w_r1, b_r1, w_r2, b_r2, w_gate, w_up, w_down):
    depth = w_in.shape[0]
    alpha = (2.0 * depth) ** 0.25
    h_p, h_s = x_prompt, x_sample
    outs = [[] for _ in range(8)]
    for l in range(depth):
        res = _layer(h_p, h_s, cache_win_k[l], cache_win_v[l], state_pool[l], cache_mem_k[l], cache_mem_v[l],
                     mem_prompt, w_in[l], w_mem_kv[l], w_pool[l], pool_scale[l], w_o[l], ln1_g[l], ln1_b[l],
                     ln2_g[l], ln2_b[l], w_r1[l], b_r1[l], w_r2[l], b_r2[l], w_gate[l], w_up[l], w_down[l], alpha)
        h_p, h_s = res[0], res[1]
        for lst, val in zip(outs, res[2:]):
            lst.append(val)
    return (h_p, h_s) + tuple(jnp.stack(o) for o in outs)
```

```python
import functools
import math

import jax
import jax.numpy as jnp
from jax import lax
from jax.experimental import pallas as pl
from jax.experimental.pallas import tpu as pltpu

F32 = jnp.float32
BF16 = jnp.bfloat16
HIGHEST = lax.Precision.HIGHEST
NEG_INF = float("-inf")

HEAD_DIM = 64
DILATED_PATTERNS = ((128, 1), (512, 4), (2048, 16))
BAND_BLOCK = 128
RES = max(d for _, d in DILATED_PATTERNS)
POOL_WINDOWS = (2, 4, 8, 16)
N_EXPERT_GROUPS = 4
EXPERTS_PER_GROUP = 8
N_EXPERTS = N_EXPERT_GROUPS * EXPERTS_PER_GROUP
PAST_LEN = 16384
LN_EPS = 1e-5
ATTN_SCALE = HEAD_DIM ** -0.5

LANES = 128
SUBLANES = 8
VMEM_LIMIT = 56 * 1024 * 1024

ROUTER_OFF = N_EXPERT_GROUPS
ATTN_BATCH = 8
DMA_UNROLL = 8


def _cparams(sem):
    return pltpu.CompilerParams(dimension_semantics=sem, vmem_limit_bytes=VMEM_LIMIT)


def _dot(a, b, precision=None):
    return jnp.dot(a, b, preferred_element_type=F32, precision=precision)


def _dot_nt(a, b, precision=None):
    return lax.dot_general(a, b, (((1,), (1,)), ((), ())), preferred_element_type=F32, precision=precision)


def _memproj_body(mem_ref, wkT_ref, wv_ref, wvT_ref, kmT_ref, vm_ref, vmT_ref):
    m = mem_ref[0].astype(BF16)
    kmT_ref[0] = _dot_nt(wkT_ref[...], m)
    vm_ref[0] = _dot(m, wv_ref[...])
    vmT_ref[0] = _dot_nt(wvT_ref[...], m)


def _memproj(mem, wkT, wv, wvT):
    B, M, D = mem.shape
    C = wv.shape[1]
    full = lambda shape: pl.BlockSpec(shape, lambda b: (0,) * len(shape))
    return pl.pallas_call(
        _memproj_body,
        grid=(B,),
        in_specs=[pl.BlockSpec((1, M, D), lambda b: (b, 0, 0)), full((C, D)), full((D, C)), full((C, D))],
        out_specs=[pl.BlockSpec((1, C, M), lambda b: (b, 0, 0)),
                   pl.BlockSpec((1, M, C), lambda b: (b, 0, 0)),
                   pl.BlockSpec((1, C, M), lambda b: (b, 0, 0))],
        out_shape=[jax.ShapeDtypeStruct((B, C, M), F32), jax.ShapeDtypeStruct((B, M, C), F32),
                   jax.ShapeDtypeStruct((B, C, M), F32)],
        compiler_params=_cparams(("arbitrary",)),
        name="memproj",
    )(mem, wkT, wv, wvT)


def _proj_body(d_a, d_b, x_ref, perm_ref, w_ref, wkvT_ref, q_ref, k_ref, v_ref, u_ref, qc_ref, kT_ref, vT_ref):
    xb = x_ref[0].astype(BF16)
    ts = xb.shape[0]
    xp = _dot(perm_ref[...], xb).astype(BF16)
    y = _dot(xp, w_ref[:, 0:3 * d_a])
    rows = ts // RES
    for r in range(RES):
        q_ref[0, r] = y[r * rows:(r + 1) * rows, 0:d_a]
        k_ref[0, r] = y[r * rows:(r + 1) * rows, d_a:2 * d_a]
        v_ref[0, r] = y[r * rows:(r + 1) * rows, 2 * d_a:3 * d_a]
    y2 = _dot(xb, w_ref[:, 3 * d_a:])
    u_ref[0] = y2[:, 0:d_b]
    qc_ref[0] = y2[:, d_b:]
    yT = _dot_nt(wkvT_ref[...], xb)
    kT_ref[0] = yT[0:d_a]
    vT_ref[0] = yT[d_a:]


def _residue_major_perm(n):
    dst = jnp.arange(n)
    src = RES * (dst % (n // RES)) + dst // (n // RES)
    return (src[:, None] == jnp.arange(n)[None, :]).astype(BF16)


def _proj(x, w_bf, wkvT_bf, d_a, d_b, d_c, ts):
    B, S, D = x.shape
    d_in = w_bf.shape[1]
    assert ts % (RES * SUBLANES) == 0 and S % ts == 0
    row = lambda n: pl.BlockSpec((1, ts, n), lambda b, i: (b, i, 0))
    col = lambda n: pl.BlockSpec((1, n, ts), lambda b, i: (b, 0, i))
    grp = pl.BlockSpec((1, RES, ts // RES, d_a), lambda b, i: (b, 0, i, 0))
    outs = pl.pallas_call(
        functools.partial(_proj_body, d_a, d_b),
        grid=(B, S // ts),
        in_specs=[row(D), pl.BlockSpec((ts, ts), lambda b, i: (0, 0)), pl.BlockSpec((D, d_in), lambda b, i: (0, 0)),
                  pl.BlockSpec((2 * d_a, D), lambda b, i: (0, 0))],
        out_specs=[grp, grp, grp, row(d_b), row(d_c), col(d_a), col(d_a)],
        out_shape=[jax.ShapeDtypeStruct((B, RES, S // RES, d_a), F32)] * 3
        + [jax.ShapeDtypeStruct((B, S, d_b), F32), jax.ShapeDtypeStruct((B, S, d_c), F32)]
        + [jax.ShapeDtypeStruct((B, d_a, S), F32)] * 2,
        compiler_params=_cparams(("arbitrary", "arbitrary")),
        name="proj",
    )(x, _residue_major_perm(ts), w_bf, wkvT_bf)
    return [o.reshape(B, S, d_a) for o in outs[:3]] + list(outs[3:])


def _attn_body(S, slopes_ref, q_ref, k_ref, v_ref, o_ref, opat_ref, lpat_ref):
    hp = pl.program_id(1)
    blk = BAND_BLOCK
    lane = lax.broadcasted_iota(jnp.int32, (blk, LANES), 1)
    head0 = lane < HEAD_DIM

    for pi, (window, dil) in enumerate(DILATED_PATTERNS):
        n_steps = window // dil
        ngrp = RES // dil
        cs = blk // ngrp
        cs_bits = cs.bit_length() - 1
        assert ngrp * dil == RES and cs * ngrp == blk and cs % SUBLANES == 0 and S == RES * blk
        has_prev = ngrp > 1
        nk = 2 * blk if has_prev else blk
        nat = lambda a, ngrp=ngrp, cs=cs, cs_bits=cs_bits: (a & (cs - 1)) * ngrp + (a >> cs_bits)
        qi = lax.broadcasted_iota(jnp.int32, (blk, nk), 0)
        kj = lax.broadcasted_iota(jnp.int32, (blk, nk), 1)
        if has_prev:
            steps = (nat(qi) + blk) - (nat(kj & (blk - 1)) + (kj & blk))
        else:
            steps = qi - kj
        valid = (steps >= 0) & (steps <= n_steps)
        dist = (steps * dil).astype(F32)
        biases = []
        for hh in range(2):
            slope = slopes_ref[2 * hp + hh]
            biases.append(jnp.where(valid, -slope * dist, NEG_INF))
        prev_cols = kj < blk

        ones_k = jnp.ones((ATTN_BATCH, nk, LANES), BF16)

        def blocks(it, carry, dil=dil, ngrp=ngrp, cs=cs, has_prev=has_prev, biases=biases, prev_cols=prev_cols,
                   pi=pi, nk=nk, ones_k=ones_k):
            def chunk(idx, c, back):
                r = idx // ngrp
                j = jnp.maximum(idx % ngrp - back, 0)
                return pl.ds(pl.multiple_of((r + dil * c) * blk + cs * j, cs), cs)

            def load(ref, idx, back=0):
                return jnp.concatenate([ref[0, chunk(idx, c, back), :] for c in range(ngrp)], axis=0)

            qs, ks, vs, firsts = [], [], [], []
            for b in range(ATTN_BATCH):
                idx = it * ATTN_BATCH + b
                qs.append(load(q_ref, idx))
                if has_prev:
                    ks.append(jnp.concatenate([load(k_ref, idx, 1), load(k_ref, idx)], axis=0))
                    vs.append(jnp.concatenate([load(v_ref, idx, 1), load(v_ref, idx)], axis=0))
                    firsts.append(jnp.where(prev_cols & (idx % ngrp == 0), NEG_INF, 0.0))
                else:
                    ks.append(load(k_ref, idx))
                    vs.append(load(v_ref, idx))
            q3 = jnp.stack(qs) * ATTN_SCALE
            k3 = jnp.stack(ks).astype(BF16)
            v3 = jnp.concatenate([jnp.stack(vs).astype(BF16), ones_k], axis=2)
            outs, lses = [], []
            for hh in range(2):
                qm = jnp.where(head0 if hh == 0 else ~head0, q3, 0.0).astype(BF16)
                s = jnp.einsum("bqd,bkd->bqk", qm, k3, preferred_element_type=F32) + biases[hh]
                if has_prev:
                    s = s + jnp.stack(firsts)
                m = jnp.max(s, axis=2, keepdims=True)
                p = jnp.exp(s - m).astype(BF16)
                o = jnp.einsum("bqk,bkd->bqd", p, v3, preferred_element_type=F32)
                den = o[:, :, LANES:]
                outs.append(o[:, :, :LANES] / den)
                lses.append(m + jnp.log(den))
            o_pair = jnp.where(head0, outs[0], outs[1])
            l_pair = jnp.where(head0, lses[0], lses[1])
            for b in range(ATTN_BATCH):
                idx = it * ATTN_BATCH + b
                for c in range(ngrp):
                    opat_ref[pi, chunk(idx, c, 0), :] = o_pair[b, c * cs:(c + 1) * cs]
                    lpat_ref[pi, chunk(idx, c, 0), :] = l_pair[b, c * cs:(c + 1) * cs]
            return carry

        lax.fori_loop(0, RES // ATTN_BATCH, blocks, 0)

    chunk = 256

    def mix(c, carry):
        rows = pl.ds(pl.multiple_of(c * chunk, chunk), chunk)
        l0, l1, l2 = lpat_ref[0, rows, :], lpat_ref[1, rows, :], lpat_ref[2, rows, :]
        mx = jnp.maximum(jnp.maximum(l0, l1), l2)
        w0, w1, w2 = jnp.exp(l0 - mx), jnp.exp(l1 - mx), jnp.exp(l2 - mx)
        num = w0 * opat_ref[0, rows, :] + w1 * opat_ref[1, rows, :] + w2 * opat_ref[2, rows, :]
        o_ref[0, rows, :] = (num / (w0 + w1 + w2)).astype(o_ref.dtype)
        return carry

    lax.fori_loop(0, S // chunk, mix, 0)


def _attn(q, k, v, slopes):
    B, S, d_a = q.shape
    assert S % (BAND_BLOCK * max(d for _, d in DILATED_PATTERNS)) == 0
    npair = d_a // LANES
    spec = pl.BlockSpec((1, S, LANES), lambda b, h, *_: (b, 0, h))
    grid_spec = pltpu.PrefetchScalarGridSpec(
        num_scalar_prefetch=0,
        grid=(B, npair),
        in_specs=[pl.BlockSpec(memory_space=pltpu.SMEM), spec, spec, spec],
        out_specs=spec,
        scratch_shapes=[pltpu.VMEM((len(DILATED_PATTERNS), S, LANES), F32),
                        pltpu.VMEM((len(DILATED_PATTERNS), S, LANES), F32)],
    )
    return pl.pallas_call(
        functools.partial(_attn_body, S),
        grid_spec=grid_spec,
        out_shape=jax.ShapeDtypeStruct((B, S, d_a), BF16),
        compiler_params=_cparams(("arbitrary", "arbitrary")),
        name="dilated_attn",
    )(slopes, q, k, v)


def _layer_norm(z, g, b):
    mu = jnp.mean(z, axis=-1, keepdims=True)
    zc = z - mu
    var = jnp.mean(zc * zc, axis=-1, keepdims=True)
    return zc * lax.rsqrt(var + LN_EPS) * g + b


def _route(logits):
    n = logits.shape[0]
    lane = lax.broadcasted_iota(jnp.int32, (n, LANES), 1)
    lane_f = lane.astype(F32)
    big = float(LANES)
    is_outer = lane < N_EXPERT_GROUPS
    l1 = jnp.where(is_outer, logits, NEG_INF)
    m1 = jnp.max(l1, axis=1, keepdims=True)
    g_sel = jnp.min(jnp.where(l1 == m1, lane_f, big), axis=1, keepdims=True)
    v1 = 1.0 / jnp.sum(jnp.exp(l1 - m1), axis=1, keepdims=True)
    lo = ROUTER_OFF + g_sel * EXPERTS_PER_GROUP
    in_group = (lane_f >= lo) & (lane_f < lo + EXPERTS_PER_GROUP)
    l2 = jnp.where(in_group, logits, NEG_INF)
    ma = jnp.max(l2, axis=1, keepdims=True)
    ia = jnp.min(jnp.where(l2 == ma, lane_f, big), axis=1, keepdims=True)
    l2b = jnp.where(lane_f == ia, NEG_INF, l2)
    mb = jnp.max(l2b, axis=1, keepdims=True)
    ib = jnp.min(jnp.where(l2b == mb, lane_f, big), axis=1, keepdims=True)
    eb = jnp.exp(mb - ma)
    wa = 1.0 / (1.0 + eb)
    wb = eb / (1.0 + eb)
    return ia - ROUTER_OFF, ib - ROUTER_OFF, v1 * wa, v1 * wb


def _rank_and_meta(e0, e1, g0, g1, carry):
    n = e0.shape[0]
    lane_f = lax.broadcasted_iota(jnp.int32, (n, LANES), 1).astype(F32)
    oh0 = (lane_f == e0).astype(F32)
    oh1 = (lane_f == e1).astype(F32)
    both = oh0 + oh1
    ti = lax.broadcasted_iota(jnp.int32, (n, n), 0)
    tj = lax.broadcasted_iota(jnp.int32, (n, n), 1)
    tri = (tj < ti).astype(BF16)
    before = _dot(tri, both.astype(BF16)) + carry
    r0 = jnp.sum(before * oh0, axis=1, keepdims=True)
    r1 = jnp.sum(before * oh1, axis=1, keepdims=True)
    new_carry = carry + jnp.sum(both, axis=0, keepdims=True)
    lane = lax.broadcasted_iota(jnp.int32, (n, LANES), 1)
    meta = jnp.zeros((n, LANES), F32)
    for i, val in enumerate((e0, e1, g0, g1, r0, r1)):
        meta = jnp.where(lane == i, val, meta)
    return meta, new_carry


def _lane_group(shape, width):
    lane = lax.broadcasted_iota(jnp.int32, shape, len(shape) - 1)
    grp = jnp.zeros(shape, jnp.int32)
    for g in range(1, shape[-1] // width):
        grp = grp + (lane >= g * width).astype(jnp.int32)
    return grp


def _select_by_group(grp, vals):
    out = vals[-1]
    for g in range(len(vals) - 2, -1, -1):
        out = jnp.where(grp == g, vals[g], out)
    return out


def _store_token_rows(ref, val):
    n, d = val.shape
    nchunk = d // LANES
    for c in range(nchunk):
        ref[pl.ds(c, n, stride=nchunk), :] = val[:, c * LANES:(c + 1) * LANES]


def _load_token_rows(ref, n, nchunk):
    return jnp.concatenate([ref[pl.ds(c, n, stride=nchunk), :] for c in range(nchunk)], axis=1)


HALO = 16


def _mix_body(tm, nt, n_tiles, d_b, alpha, x_ref, at_ref, unperm_ref, u_ref, uh_ref, qc_ref, kmT_ref, vm_ref,
              wo_ref, wp_ref, ps_ref, g1_ref, b1_ref, wr_ref, br_ref, cin_ref, hs_ref,
              h1_ref, meta_ref, cnt_ref, carry_ref):
    g = pl.program_id(0)

    @pl.when(g == 0)
    def _():
        carry_ref[...] = cin_ref[...]

    @pl.when(g == n_tiles)
    def _():
        ns_rows = hs_ref.shape[0]
        h1_ref[0:ns_rows, :] = hs_ref[...]
        h1_ref[ns_rows:, :] = jnp.zeros((h1_ref.shape[0] - ns_rows, LANES), F32)

    @pl.when(g < n_tiles)
    def _():
        _mix_tile(tm, g % nt, d_b, alpha, x_ref, at_ref, unperm_ref, u_ref, uh_ref, qc_ref, kmT_ref, vm_ref, wo_ref,
                  wp_ref, ps_ref, g1_ref, b1_ref, wr_ref, br_ref, h1_ref, meta_ref, cnt_ref, carry_ref)


def _mix_tile(tm, i, d_b, alpha, x_ref, at_ref, unperm_ref, u_ref, uh_ref, qc_ref, kmT_ref, vm_ref, wo_ref, wp_ref,
              ps_ref, g1_ref, b1_ref, wr_ref, br_ref, h1_ref, meta_ref, cnt_ref, carry_ref):
    u = u_ref[0]
    halo = jnp.where(i == 0, 0.0, uh_ref[0])
    ext = jnp.concatenate([halo, halo, u], axis=0)
    s2 = ext + pltpu.roll(ext, 1, 0)
    s4 = s2 + pltpu.roll(s2, 2, 0)
    s8 = s4 + pltpu.roll(s4, 4, 0)
    s16 = s8 + pltpu.roll(s8, 8, 0)
    grp = _lane_group((tm, d_b), d_b // len(POOL_WINDOWS))
    sums = [s[2 * HALO:] for s in (s2, s4, s8, s16)]
    win = _select_by_group(grp, sums)
    wlen = _select_by_group(grp, [jnp.int32(w) for w in POOL_WINDOWS])
    pos = i * tm + lax.broadcasted_iota(jnp.int32, (tm, d_b), 0)
    cnt = jnp.minimum(wlen, pos + 1).astype(F32)
    diff = win / cnt - u
    pool = _dot(diff.astype(BF16), wp_ref[...]) * ps_ref[...]

    qc = qc_ref[0] * ATTN_SCALE
    d_c = qc.shape[1]
    kmT = kmT_ref[0].astype(BF16)
    vm = vm_ref[0].astype(BF16)
    hl = _lane_group((tm, d_c), HEAD_DIM)
    memo = jnp.zeros((tm, d_c), F32)
    for h in range(d_c // HEAD_DIM):
        qm = jnp.where(hl == h, qc, 0.0).astype(BF16)
        s = _dot(qm, kmT)
        p = jnp.exp(s - jnp.max(s, axis=1, keepdims=True))
        den = jnp.sum(p, axis=1, keepdims=True)
        o = _dot(p.astype(BF16), vm) / den
        memo = jnp.where(hl == h, o, memo)

    at_rl = jnp.concatenate([at_ref[0, r] for r in range(RES)], axis=0)
    attn = _dot(unperm_ref[...], at_rl).astype(BF16)
    cat = jnp.concatenate([attn, pool.astype(BF16), memo.astype(BF16)], axis=1)
    mixv = _dot(cat, wo_ref[...])
    h1 = _layer_norm(alpha * x_ref[0] + mixv, g1_ref[...], b1_ref[...])
    _store_token_rows(h1_ref, h1)

    logits = _dot(h1.astype(BF16), wr_ref[...]) + br_ref[...]
    e0, e1, g0, g1 = _route(logits)
    meta, new_carry = _rank_and_meta(e0, e1, g0, g1, carry_ref[0:1, :])
    meta_ref[...] = meta
    carry_ref[...] = jnp.broadcast_to(new_carry, carry_ref.shape)
    cnt_ref[...] = carry_ref[...]


def _mix(x, attn_rm, u, qc, kmT, vm, wo_bf, wp_bd_bf, pool_scale, g1, b1, wr, br, counts_in, h1s, alpha, tm):
    B, S, D = x.shape
    d_a, d_b, d_c = attn_rm.shape[2], u.shape[2], qc.shape[2]
    n_mem = vm.shape[1]
    nt = S // tm
    n_tiles = B * nt
    nchunk = D // LANES
    assert tm % (RES * SUBLANES * 2) == 0 and h1s.shape[0] < tm * nchunk
    last = n_tiles - 1
    bi = lambda g: (jnp.minimum(g, last) // nt, jnp.minimum(g, last) % nt)
    row = lambda n: pl.BlockSpec((1, tm, n), lambda g: bi(g) + (0,))
    full = lambda shape: pl.BlockSpec(shape, lambda g: (0,) * len(shape))
    per_b = lambda shape: pl.BlockSpec((1,) + shape, lambda g: (bi(g)[0], 0, 0))
    halo_spec = pl.BlockSpec((1, HALO, d_b), lambda g: (bi(g)[0], jnp.maximum(bi(g)[1] * (tm // HALO) - 1, 0), 0))
    at_spec = pl.BlockSpec((1, RES, tm // RES, d_a), lambda g: (bi(g)[0], 0, bi(g)[1], 0))
    dst = jnp.arange(tm)
    unperm = ((dst % RES) * (tm // RES) + dst // RES)[:, None] == jnp.arange(tm)[None, :]
    return pl.pallas_call(
        functools.partial(_mix_body, tm, nt, n_tiles, d_b, alpha),
        grid=(n_tiles + 1,),
        in_specs=[row(D), at_spec, full((tm, tm)), row(d_b), halo_spec, row(d_c), per_b((d_c, n_mem)),
                  per_b((n_mem, d_c)), full((D, D)), full((d_b, d_b)), full((1, d_b)), full((1, D)), full((1, D)),
                  full((D, LANES)), full((1, LANES)), full((SUBLANES, LANES)), full(h1s.shape)],
        out_specs=[pl.BlockSpec((tm * nchunk, LANES), lambda g: (g, 0)),
                   pl.BlockSpec((tm, LANES), lambda g: (jnp.minimum(g, last), 0)),
                   pl.BlockSpec((SUBLANES, LANES), lambda g: (0, 0))],
        out_shape=[jax.ShapeDtypeStruct(((n_tiles + 1) * tm * nchunk, LANES), F32),
                   jax.ShapeDtypeStruct((B * S, LANES), F32),
                   jax.ShapeDtypeStruct((SUBLANES, LANES), F32)],
        scratch_shapes=[pltpu.VMEM((SUBLANES, LANES), F32)],
        compiler_params=_cparams(("arbitrary",)),
        name="mix_ln1_router",
    )(x, attn_rm.reshape(B, RES, S // RES, d_a), unperm.astype(BF16), u, u, qc, kmT, vm, wo_bf, wp_bd_bf,
      pool_scale, g1, b1, wr, br, counts_in, h1s)


def _columns_to_lanes(cols, rows):
    lane = lax.broadcasted_iota(jnp.int32, (rows, LANES), 1)
    tile = jnp.zeros((rows, LANES), F32)
    for t, c in enumerate(cols):
        tile = jnp.where(lane == t, c, tile)
    return tile


def _sproj_body(x_ref, wT_ref, o_ref):
    o_ref[...] = _dot_nt(wT_ref[...].astype(BF16), x_ref[...].astype(BF16))


def _sproj(x, wT):
    vm = pl.BlockSpec(memory_space=pltpu.VMEM)
    return pl.pallas_call(
        _sproj_body, in_specs=[vm, vm], out_specs=vm,
        out_shape=jax.ShapeDtypeStruct((wT.shape[0], x.shape[0]), F32),
        compiler_params=pltpu.CompilerParams(vmem_limit_bytes=VMEM_LIMIT),
        name="decode_proj",
    )(x, wT)


def _decode_body(T, wb, slopes_ref, pT_ref, ck_ref, cv_ref, mk_ref, mv_ref,
                 nk_ref, nv_ref, at_ref, mo_ref, qkv_ref):
    H = ck_ref.shape[1]
    d_a = H * HEAD_DIM
    ntok = pT_ref.shape[1]
    shift = (ntok - T * pl.program_id(0)) % ntok
    qkv_ref[...] = pltpu.roll(pT_ref[...], shift, 1)

    def member(dist, window, dil):
        assert dil & (dil - 1) == 0
        return ((dist & (dil - 1)) == 0) & (dist <= window) & (dist >= 0)

    OWN = min(d for _, d in DILATED_PATTERNS if d > 1)
    far = wb - LANES
    assert T <= OWN and far % OWN == 0 and wb % OWN == 0
    assert all(w <= LANES for w, d in DILATED_PATTERNS if d == 1) and all(d % OWN == 0 for _, d in DILATED_PATTERNS if d > 1)
    lane_far = lax.broadcasted_iota(jnp.int32, (T, far), 1)
    t_far = lax.broadcasted_iota(jnp.int32, (T, far), 0)
    owned = (lane_far & (OWN - 1)) == t_far
    dist_far = wb + t_far - lane_far
    t_near = lax.broadcasted_iota(jnp.int32, (T, LANES), 0)
    dist_near = wb + t_near - (far + lax.broadcasted_iota(jnp.int32, (T, LANES), 1))
    dist_new = lax.broadcasted_iota(jnp.int32, (T, T), 0) - lax.broadcasted_iota(jnp.int32, (T, T), 1)
    masks = [(None if dil == 1 else owned & member(dist_far, window, dil),
              member(dist_near, window, dil), member(dist_new, window, dil)) for window, dil in DILATED_PATTERNS]
    dist_far = dist_far.astype(F32)
    dist_near = dist_near.astype(F32)
    dist_new = jnp.maximum(dist_new, 0).astype(F32)
    rnd = lambda a: a.astype(BF16).astype(F32)

    lane_t = lax.broadcasted_iota(jnp.int32, (HEAD_DIM, LANES), 1)
    own_t = lane_t & (OWN - 1)
    last = wb - LANES
    n_far = far // LANES

    def head(h, carry):
        slope = slopes_ref[h]
        kTc = ck_ref[0, h]
        vTc = cv_ref[0, h]
        r0 = pl.multiple_of(h * HEAD_DIM, HEAD_DIM)
        qT = rnd(qkv_ref[pl.ds(r0, HEAD_DIM), :][:, 0:T] * ATTN_SCALE)
        kTn = qkv_ref[pl.ds(d_a + r0, HEAD_DIM), :][:, 0:T]
        vTn = qkv_ref[pl.ds(2 * d_a + r0, HEAD_DIM), :][:, 0:T]
        kTn_r, vTn_r = rnd(kTn), rnd(vTn)
        tile = lambda a, c: a[:, c * LANES:(c + 1) * LANES]

        qb = [jnp.broadcast_to(qT[:, t:t + 1], (HEAD_DIM, LANES)) for t in range(T)]
        qpat = qb[T - 1]
        for t in range(T - 1):
            qpat = jnp.where(own_t == t, qb[t], qpat)
        s_far = jnp.concatenate([jnp.sum(rnd(tile(kTc, c)) * qpat, axis=0, keepdims=True) for c in range(n_far)],
                                axis=1) - slope * dist_far
        k_near = rnd(tile(kTc, n_far))
        s_near = jnp.concatenate([jnp.sum(k_near * qb[t], axis=0, keepdims=True) for t in range(T)], axis=0)
        s_near = s_near - slope * dist_near
        s_new = jnp.concatenate([jnp.sum(kTn_r * qT[:, t:t + 1], axis=0, keepdims=True) for t in range(T)], axis=0)
        s_new = s_new - slope * dist_new

        rmax = lambda a: jnp.max(a, axis=1, keepdims=True)
        rsum = lambda a: jnp.sum(a, axis=1, keepdims=True)
        parts, lses = [], []
        for m_far, m_near, m_new in masks:
            sn, sw = jnp.where(m_near, s_near, NEG_INF), jnp.where(m_new, s_new, NEG_INF)
            m = jnp.maximum(rmax(sn), rmax(sw))
            if m_far is not None:
                sf = jnp.where(m_far, s_far, NEG_INF)
                m = jnp.maximum(m, rmax(sf))
            en, ew = jnp.exp(sn - m), jnp.exp(sw - m)
            den = rsum(en) + rsum(ew)
            ef = None
            if m_far is not None:
                ef = jnp.exp(sf - m)
                den = den + rsum(ef)
                ef = rnd(ef)
            parts.append((ef, rnd(en), rnd(ew), den))
            lses.append(m + jnp.log(den))
        top = functools.reduce(jnp.maximum, lses)
        mixw = [jnp.exp(l - top) for l in lses]
        total = functools.reduce(lambda a, b: a + b, mixw)
        p_far = p_near = p_new = None
        for (ef, en, ew, den), w in zip(parts, mixw):
            scale = w / (total * den)
            p_near = en * scale if p_near is None else p_near + en * scale
            p_new = ew * scale if p_new is None else p_new + ew * scale
            if ef is not None:
                p_far = ef * scale if p_far is None else p_far + ef * scale

        p_comb = jnp.sum(p_far, axis=0, keepdims=True)
        acc = rnd(tile(vTc, 0)) * tile(p_comb, 0)
        for c in range(1, n_far):
            acc = acc + rnd(tile(vTc, c)) * tile(p_comb, c)
        v_near = rnd(tile(vTc, n_far))
        cols = []
        for t in range(T):
            cols.append(jnp.sum(jnp.where(own_t == t, acc, 0.0) + v_near * p_near[t:t + 1], axis=1, keepdims=True)
                        + jnp.sum(vTn_r * p_new[t:t + 1], axis=1, keepdims=True))
        at_ref[0, pl.ds(r0, HEAD_DIM), :] = _columns_to_lanes(cols, HEAD_DIM)

        rk = pltpu.roll(kTc, wb - T, 1)
        rv = pltpu.roll(vTc, wb - T, 1)
        nk_ref[0, h] = rk
        nv_ref[0, h] = rv
        newk = jnp.zeros((HEAD_DIM, LANES), F32)
        newv = jnp.zeros((HEAD_DIM, LANES), F32)
        for t in range(T):
            newk = jnp.where(lane_t == LANES - T + t, kTn[:, t:t + 1], newk)
            newv = jnp.where(lane_t == LANES - T + t, vTn[:, t:t + 1], newv)
        nk_ref[0, h, :, last:] = jnp.where(lane_t >= LANES - T, newk, rk[:, last:])
        nv_ref[0, h, :, last:] = jnp.where(lane_t >= LANES - T, newv, rv[:, last:])
        return carry

    lax.fori_loop(0, H, head, 0, unroll=2)

    Hc = mk_ref.shape[1]
    rows = []
    for h in range(Hc):
        kT = rnd(mk_ref[0, h])
        r0 = h * HEAD_DIM
        qT = rnd(qkv_ref[3 * d_a + r0:3 * d_a + r0 + HEAD_DIM, :][:, 0:T] * ATTN_SCALE)
        rows += [jnp.sum(kT * qT[:, t:t + 1], axis=0, keepdims=True) for t in range(T)]
    s = jnp.concatenate(rows, axis=0)
    p = jnp.exp(s - jnp.max(s, axis=1, keepdims=True))
    p = rnd(p / jnp.sum(p, axis=1, keepdims=True))
    for h in range(Hc):
        vT = rnd(mv_ref[0, h])
        cols = [jnp.sum(vT * p[h * T + t:h * T + t + 1], axis=1, keepdims=True) for t in range(T)]
        mo_ref[0, h * HEAD_DIM:(h + 1) * HEAD_DIM, :] = _columns_to_lanes(cols, HEAD_DIM)


def _decode(projT, T, ck, cv, mk, mv, slopes):
    DB, H, hd, wb = ck.shape
    _, Hc, _, n_mem = mk.shape
    d_a, d_c = H * hd, Hc * hd
    assert wb >= max(w for w, _ in DILATED_PATTERNS) and wb % LANES == 0 and T <= LANES
    assert projT.shape == (3 * d_a + d_c, DB * T)
    cache = pl.BlockSpec((1, H, hd, wb), lambda b, *_: (b, 0, 0, 0))
    memc = pl.BlockSpec((1, Hc, hd, n_mem), lambda b, *_: (b, 0, 0, 0))
    grid_spec = pltpu.PrefetchScalarGridSpec(
        num_scalar_prefetch=0,
        grid=(DB,),
        in_specs=[pl.BlockSpec(memory_space=pltpu.SMEM),
                  pl.BlockSpec(projT.shape, lambda b, *_: (0, 0)),
                  cache, cache, memc, memc],
        out_specs=[cache, cache,
                   pl.BlockSpec((1, d_a, LANES), lambda b, *_: (b, 0, 0)),
                   pl.BlockSpec((1, d_c, LANES), lambda b, *_: (b, 0, 0))],
        scratch_shapes=[pltpu.VMEM(projT.shape, F32)],
    )
    nk, nv, at, mo = pl.pallas_call(
        functools.partial(_decode_body, T, wb),
        grid_spec=grid_spec,
        out_shape=[jax.ShapeDtypeStruct(ck.shape, F32), jax.ShapeDtypeStruct(cv.shape, F32),
                   jax.ShapeDtypeStruct((DB, d_a, LANES), F32), jax.ShapeDtypeStruct((DB, d_c, LANES), F32)],
        compiler_params=_cparams(("arbitrary",)),
        name="decode_attn_cache",
    )(slopes, projT, ck, cv, mk, mv)
    return nk, nv, at[:, :, :T], mo[:, :, :T]


def _smix_body(T, alpha, pos0, x_ref, at_ref, mo_ref, st_ref, wu_ref, wo_ref, wp_ref, ps_ref, g1_ref, b1_ref,
               wr_ref, br_ref, cin_ref, h1_ref, meta_ref, cnt_ref, pool_ref):
    n = x_ref.shape[0]
    db = n // T
    pb = st_ref.shape[0]
    d_b = st_ref.shape[2]
    x = x_ref[...]
    bdot = lambda a, b: _dot(a.astype(BF16), b.astype(BF16))
    u_new = bdot(x, wu_ref[...])
    seq = [st_ref[j] for j in range(pb)] + [u_new[t * db:(t + 1) * db] for t in range(T)]
    for j in range(pb):
        pool_ref[j] = seq[j + T]
    grp = _lane_group((db, d_b), d_b // len(POOL_WINDOWS))
    diffs = []
    for t in range(T):
        j = pb + t
        per_w = []
        for w in POOL_WINDOWS:
            acc = seq[j]
            for back in range(1, w):
                if j - back >= 0:
                    acc = acc + seq[j - back]
            per_w.append(acc / float(min(w, pos0 + j + 1)))
        diffs.append(_select_by_group(grp, per_w) - seq[j])
    diff = jnp.concatenate(diffs, axis=0)
    pool = bdot(diff, wp_ref[...]) * ps_ref[...]
    cat = jnp.concatenate([at_ref[...], pool, mo_ref[...]], axis=1)
    mixv = bdot(cat, wo_ref[...])
    h1 = _layer_norm(alpha * x + mixv, g1_ref[...], b1_ref[...])
    _store_token_rows(h1_ref, h1)
    logits = bdot(h1, wr_ref[...]) + br_ref[...]
    e0, e1, g0, g1 = _route(logits)
    meta, new_carry = _rank_and_meta(e0, e1, g0, g1, cin_ref[0:1, :])
    meta_ref[...] = meta
    cnt_ref[...] = jnp.broadcast_to(new_carry, cnt_ref.shape)


def _smix(x_tb, attn_tb, memo_tb, state, wu, wo, wp_bd, pool_scale, g1, b1, wr, br, counts_in, alpha, T):
    n, D = x_tb.shape
    pb, db, d_b = state.shape
    nchunk = D // LANES
    vm = pl.BlockSpec(memory_space=pltpu.VMEM)
    return pl.pallas_call(
        functools.partial(_smix_body, T, alpha, PAST_LEN - pb),
        in_specs=[vm] * 13,
        out_specs=[vm] * 4,
        out_shape=[jax.ShapeDtypeStruct((n * nchunk, LANES), F32), jax.ShapeDtypeStruct((n, LANES), F32),
                   jax.ShapeDtypeStruct((SUBLANES, LANES), F32), jax.ShapeDtypeStruct((pb, db, d_b), F32)],
        compiler_params=pltpu.CompilerParams(vmem_limit_bytes=VMEM_LIMIT),
        name="decode_mix_ln1_router",
    )(x_tb, attn_tb, memo_tb, state, wu, wo, wp_bd, pool_scale, g1, b1, wr, br, counts_in)


def _slot_map_body(blk, pos_ref, fill_ref, inv_ref, sem):
    g = pl.program_id(0)

    @pl.when(g == 0)
    def _():
        c = pltpu.make_async_copy(fill_ref, inv_ref, sem)
        c.start()
        c.wait()

    group = math.gcd(blk, 32)

    def body(j, carry):
        token0 = (g * blk + j * group) // 2
        for u in range(group):
            inv_ref[pos_ref[0, 0, j * group + u]] = token0 + u // 2
        return carry

    lax.fori_loop(0, blk // group, body, 0)


def _slot_map(pos, n_slots, zero_token, blk):
    n2 = pos.shape[0]
    assert n2 % blk == 0 and n_slots % 1024 == 0
    return pl.pallas_call(
        functools.partial(_slot_map_body, blk),
        grid=(n2 // blk,),
        in_specs=[pl.BlockSpec((1, 1, blk), lambda g: (g, 0, 0), memory_space=pltpu.SMEM),
                  pl.BlockSpec(memory_space=pl.ANY)],
        out_specs=pl.BlockSpec(memory_space=pltpu.SMEM),
        out_shape=jax.ShapeDtypeStruct((n_slots,), jnp.int32),
        scratch_shapes=[pltpu.SemaphoreType.DMA(())],
        compiler_params=_cparams(("arbitrary",)),
        name="slot_map",
    )(pos.reshape(n2 // blk, 1, blk), jnp.full((n_slots,), zero_token, jnp.int32))


def _expert_body(tmx, nchunk, te_ref, nt_ref, tv_ref, cur_ref, nxt_ref, h_ref, wg_ref, wu_ref, wd_ref, ys_ref,
                 buf_a, buf_b, wg_bf, wu_bf, wd_bf, sems):
    i = pl.program_id(0)
    n_tiles = nt_ref[0]

    def gather(idx_ref, buf, s, tile, wait):
        groups = (tv_ref[tile] + DMA_UNROLL - 1) // DMA_UNROLL

        def body(g, carry):
            for u in range(DMA_UNROLL):
                j = g * DMA_UNROLL + u
                dst = buf.at[pl.ds(pl.multiple_of(j * nchunk, nchunk), nchunk)]
                copy = pltpu.make_async_copy(h_ref.at[idx_ref[0, 0, j]], dst, sems.at[s])
                if wait:
                    copy.wait()
                else:
                    copy.start(priority=u % 2)
            return carry

        lax.fori_loop(0, groups, body, 0)

    @pl.when(i == 0)
    def _():
        buf_a[...] = jnp.zeros_like(buf_a)
        buf_b[...] = jnp.zeros_like(buf_b)
        gather(cur_ref, buf_a, 0, 0, False)

    it = jnp.minimum(i, pl.num_programs(0) - 2)
    @pl.when((i < n_tiles) & ((i == 0) | (te_ref[it] != te_ref[jnp.maximum(it - 1, 0)])))
    def _():
        wg_bf[...] = wg_ref[0].astype(BF16)
        wu_bf[...] = wu_ref[0].astype(BF16)
        wd_bf[...] = wd_ref[0].astype(BF16)

    def step(cur, s_cur, nxt, s_nxt):
        gather(cur_ref, cur, s_cur, it, True)

        @pl.when(i + 1 < n_tiles)
        def _():
            gather(nxt_ref, nxt, s_nxt, it + 1, False)

        x = _load_token_rows(cur, tmx, nchunk).astype(BF16)
        hg = _dot(x, wg_bf[...])
        hu = _dot(x, wu_bf[...])
        a = (hg * jax.nn.sigmoid(hg) * hu).astype(BF16)
        y = _dot(a, wd_bf[...])
        _store_token_rows(ys_ref, y)

    @pl.when((i < n_tiles) & (i % 2 == 0))
    def _():
        step(buf_a, 0, buf_b, 1)

    @pl.when((i < n_tiles) & (i % 2 == 1))
    def _():
        step(buf_b, 1, buf_a, 0)

    @pl.when(i >= n_tiles)
    def _():
        ys_ref[...] = jnp.zeros_like(ys_ref)


def _experts(tile_expert, n_tiles, tile_valid, inv, h_rows, wg, wu, wd, tmx, max_tiles):
    E, D, F = wg.shape
    nchunk = D // LANES
    inv3 = inv.reshape(-1, 1, tmx)
    last_blk = inv3.shape[0] - 1
    assert last_blk >= max_tiles and tmx % DMA_UNROLL == 0
    weights = lambda shape: pl.BlockSpec((1,) + shape, lambda i, te, *_: (te[jnp.minimum(i, max_tiles - 1)], 0, 0))
    rows = pltpu.VMEM((tmx * nchunk, LANES), F32)
    grid_spec = pltpu.PrefetchScalarGridSpec(
        num_scalar_prefetch=3,
        grid=(max_tiles + 1,),
        in_specs=[pl.BlockSpec((1, 1, tmx), lambda i, *_: (i, 0, 0), memory_space=pltpu.SMEM),
                  pl.BlockSpec((1, 1, tmx), lambda i, *_: (jnp.minimum(i + 1, last_blk), 0, 0),
                               memory_space=pltpu.SMEM),
                  pl.BlockSpec(memory_space=pl.ANY), weights((D, F)), weights((D, F)), weights((F, D))],
        out_specs=pl.BlockSpec((tmx * nchunk, LANES), lambda i, *_: (i, 0)),
        scratch_shapes=[rows, rows, pltpu.VMEM((D, F), BF16), pltpu.VMEM((D, F), BF16), pltpu.VMEM((F, D), BF16),
                        pltpu.SemaphoreType.DMA((2,))],
    )
    return pl.pallas_call(
        functools.partial(_expert_body, tmx, nchunk),
        grid_spec=grid_spec,
        out_shape=jax.ShapeDtypeStruct(((max_tiles + 1) * tmx * nchunk, LANES), F32),
        compiler_params=_cparams(("arbitrary",)),
        name="expert_swiglu",
    )(tile_expert, n_tiles, tile_valid, inv3, inv3, h_rows, wg, wu, wd)


def _combine_body(tm, nchunk, alpha, pos_ref, nxt_ref, h1_ref, meta_ref, ys_ref, g2_ref, b2_ref, o_ref, buf_a, buf_b,
                  sems):
    i = pl.program_id(0)
    last = pl.num_programs(0) - 1

    def copy(idx_ref, buf, s, t, k):
        dst = buf.at[k, pl.ds(pl.multiple_of(t * nchunk, nchunk), nchunk)]
        return pltpu.make_async_copy(ys_ref.at[idx_ref[0, 0, 2 * t + k]], dst, sems.at[s])

    def start(idx_ref, buf, s):
        def body(t, carry):
            for k in range(2):
                copy(idx_ref, buf, s, t, k).start(priority=k)
            return carry

        lax.fori_loop(0, tm, body, 0, unroll=DMA_UNROLL)

    def wait(buf, s):
        def body(t, carry):
            for k in range(2):
                copy(pos_ref, buf, s, t, k).wait()
            return carry

        lax.fori_loop(0, tm, body, 0, unroll=DMA_UNROLL)

    @pl.when(i == 0)
    def _():
        start(pos_ref, buf_a, 0)

    def step(cur, s_cur, nxt, s_nxt):
        wait(cur, s_cur)
        start(nxt_ref, nxt, s_nxt)
        h1 = _load_token_rows(h1_ref, tm, nchunk)
        y0 = _load_token_rows(cur.at[0], tm, nchunk)
        y1 = _load_token_rows(cur.at[1], tm, nchunk)
        meta = meta_ref[...]
        lane = lax.broadcasted_iota(jnp.int32, meta.shape, 1)
        gate0 = jnp.sum(jnp.where(lane == 2, meta, 0.0), axis=1, keepdims=True)
        gate1 = jnp.sum(jnp.where(lane == 3, meta, 0.0), axis=1, keepdims=True)
        f = gate0 * y0 + gate1 * y1
        o_ref[...] = _layer_norm(alpha * h1 + f, g2_ref[...], b2_ref[...])

        @pl.when(i == last)
        def _():
            wait(nxt, s_nxt)

    @pl.when(i % 2 == 0)
    def _():
        step(buf_a, 0, buf_b, 1)

    @pl.when(i % 2 == 1)
    def _():
        step(buf_b, 1, buf_a, 0)


def _combine(pos, tok0, n, h1_flat, meta, ys, g2, b2, alpha, tm):
    D = g2.shape[1]
    nchunk = D // LANES
    assert n % tm == 0 and tok0 % tm == 0
    pos3 = pos.reshape(-1, 1, 2 * tm)
    off = tok0 // tm
    last = off + n // tm - 1
    return pl.pallas_call(
        functools.partial(_combine_body, tm, nchunk, alpha),
        grid=(n // tm,),
        in_specs=[pl.BlockSpec((1, 1, 2 * tm), lambda i: (i + off, 0, 0), memory_space=pltpu.SMEM),
                  pl.BlockSpec((1, 1, 2 * tm), lambda i: (jnp.minimum(i + off + 1, last), 0, 0),
                               memory_space=pltpu.SMEM),
                  pl.BlockSpec((tm * nchunk, LANES), lambda i: (i + off, 0)),
                  pl.BlockSpec((tm, LANES), lambda i: (i, 0)),
                  pl.BlockSpec(memory_space=pl.ANY),
                  pl.BlockSpec((1, D), lambda i: (0, 0)), pl.BlockSpec((1, D), lambda i: (0, 0))],
        out_specs=pl.BlockSpec((tm, D), lambda i: (i, 0)),
        out_shape=jax.ShapeDtypeStruct((n, D), F32),
        scratch_shapes=[pltpu.VMEM((2, tm * nchunk, LANES), F32), pltpu.VMEM((2, tm * nchunk, LANES), F32),
                        pltpu.SemaphoreType.DMA((2,))],
        compiler_params=_cparams(("arbitrary",)),
        name="combine_ln2",
    )(pos3, pos3, h1_flat, meta, ys, g2, b2)


def _block_diag(w):
    g, a, b = w.shape
    eye = jnp.eye(g, dtype=w.dtype)
    return (eye[:, None, :, None] * w[:, :, None, :]).reshape(g * a, g * b)


def _layer(h_p, h_s, win_k, win_v, pool_st, mem_k, mem_v, mem_prompt,
           w_in, w_mem_kv, w_pool, pool_scale, w_o, ln1_g, ln1_b, ln2_g, ln2_b,
           w_r1, b_r1, w_r2, b_r2, w_gate, w_up, w_down, alpha):
    B, S, D = h_p.shape
    DB, T, _ = h_s.shape
    H = win_k.shape[2]
    Hc = mem_k.shape[2]
    d_a, d_c = H * HEAD_DIM, Hc * HEAD_DIM
    d_b = pool_st.shape[2]
    nchunk = D // LANES
    slopes = 2.0 ** (-8.0 * jnp.arange(1, H + 1, dtype=F32) / H)

    w_in_bf = w_in.astype(BF16)
    w_inT = w_in.T
    wkvT_bf = w_inT[d_a:3 * d_a].astype(BF16)
    wp_bd = _block_diag(w_pool)
    ps = pool_scale.reshape(1, d_b)
    g1, b1 = ln1_g.reshape(1, D), ln1_b.reshape(1, D)
    g2, b2 = ln2_g.reshape(1, D), ln2_b.reshape(1, D)
    n_r = N_EXPERT_GROUPS + N_EXPERTS
    wr = jnp.concatenate([w_r1, jnp.transpose(w_r2, (1, 0, 2)).reshape(D, N_EXPERTS),
                          jnp.zeros((D, LANES - n_r), F32)], axis=1)
    br = jnp.concatenate([b_r1, b_r2.reshape(-1), jnp.zeros((LANES - n_r,), F32)]).reshape(1, LANES)

    ck = jnp.transpose(win_k, (0, 2, 3, 1))
    cv = jnp.transpose(win_v, (0, 2, 3, 1))
    mk = jnp.transpose(mem_k, (0, 2, 3, 1))
    mv = jnp.transpose(mem_v, (0, 2, 3, 1))
    w_qkvqcT = jnp.concatenate([w_inT[:3 * d_a], w_inT[3 * d_a + d_b:]], axis=0)
    projT = _sproj(h_s.reshape(DB * T, D), w_qkvqcT)
    nk, nv, attn_sT, memo_sT = _decode(projT, T, ck, cv, mk, mv, slopes)
    to_tb = lambda a: jnp.transpose(a, (2, 0, 1)).reshape(T * DB, a.shape[1])
    x_tb = jnp.transpose(h_s, (1, 0, 2)).reshape(T * DB, D)
    state = jnp.transpose(pool_st, (1, 0, 2))
    h1s, meta_s, cnt_s, new_pool = _smix(x_tb, to_tb(attn_sT), to_tb(memo_sT), state,
                                         w_in[:, 3 * d_a:3 * d_a + d_b], w_o, wp_bd, ps, g1, b1, wr, br,
                                         jnp.zeros((SUBLANES, LANES), F32), alpha, T)

    w_memT = w_mem_kv.T
    kmT, vm, vmT = _memproj(mem_prompt, w_memT[:d_c].astype(BF16), w_mem_kv[:, d_c:].astype(BF16),
                            w_memT[d_c:].astype(BF16))
    q, k, v, u, qc, kT, vT = _proj(h_p, w_in_bf, wkvT_bf, d_a, d_b, d_c, ts=512)
    attn = _attn(q, k, v, slopes)
    h_rows, meta_p, cnt_all = _mix(h_p, attn, u, qc, kmT, vm, w_o.astype(BF16), wp_bd.astype(BF16), ps, g1, b1,
                                   wr.astype(BF16), br, cnt_s, h1s, alpha, tm=512)

    tmx = 256
    n_p, n_s = B * S, DB * T
    n = n_p + n_s
    counts = cnt_all[0, :N_EXPERTS].astype(jnp.int32)
    padded = (counts + tmx - 1) // tmx * tmx
    seg_end = jnp.cumsum(padded)
    seg_off = seg_end - padded
    meta = jnp.concatenate([meta_p, meta_s], axis=0)
    e_ids = meta[:, 0:2].astype(jnp.int32)
    lookup = lambda table, idx: jnp.sum(jnp.where(idx[..., None] == jnp.arange(N_EXPERTS), table, 0), axis=-1)
    pos = (lookup(seg_off, e_ids) + meta[:, 4:6].astype(jnp.int32)).reshape(-1)
    max_tiles = (2 * n) // tmx + N_EXPERTS
    n_tiles = (seg_end[-1] // tmx).astype(jnp.int32).reshape(1)
    tile_row0 = jnp.arange(max_tiles, dtype=jnp.int32) * tmx
    tile_expert = jnp.sum((tile_row0[:, None] >= seg_end[None, :]).astype(jnp.int32), axis=1)
    tile_expert = jnp.minimum(tile_expert, N_EXPERTS - 1)

    tok_tile = math.gcd(math.gcd(n_p, n_s), 128)
    n_slots = -(-((max_tiles + 1) * tmx) // 1024) * 1024
    inv = _slot_map(pos, n_slots, n, blk=math.gcd(2 * n, 2048))
    tile_valid = jnp.clip(lookup(seg_off + counts, tile_expert) - tile_row0, 0, tmx).astype(jnp.int32)
    ys = _experts(tile_expert, n_tiles, tile_valid, inv, h_rows.reshape(-1, nchunk, LANES),
                  w_gate.reshape(N_EXPERTS, D, -1),
                  w_up.reshape(N_EXPERTS, D, -1), w_down.reshape(N_EXPERTS, -1, D), tmx, max_tiles)
    ys3 = ys.reshape(-1, nchunk, LANES)
    y_p = _combine(pos, 0, n_p, h_rows, meta_p, ys3, g2, b2, alpha, tm=tok_tile)
    y_s = _combine(pos, n_p, n_s, h_rows, meta_s, ys3, g2, b2, alpha, tm=tok_tile)

    y_p = y_p.reshape(B, S, D)
    y_s = jnp.transpose(y_s.reshape(T, DB, D), (1, 0, 2))
    heads = lambda a, h: jnp.transpose(a.reshape(a.shape[0], h, HEAD_DIM, a.shape[2]), (0, 3, 1, 2))
    wbp = min(max(w for w, _ in DILATED_PATTERNS), S)
    new_wk_p = heads(kT, H)[:, S - wbp:]
    new_wv_p = heads(vT, H)[:, S - wbp:]
    pb = pool_st.shape[1]
    new_pool_p = u[:, S - pb:]
    new_mk_p = heads(kmT, Hc)
    new_mv_p = heads(vmT, Hc)
    new_wk_s = jnp.transpose(nk, (0, 3, 1, 2))
    new_wv_s = jnp.transpose(nv, (0, 3, 1, 2))
    new_pool_s = jnp.transpose(new_pool, (1, 0, 2))
    return (y_p, y_s, new_wk_p, new_wv_p, new_pool_p, new_mk_p, new_mv_p, new_wk_s, new_wv_s, new_pool_s)


def kernel(x_prompt, x_sample, cache_win_k, cache_win_v, state_pool, cache_mem_k, cache_mem_v, mem_prompt, w_in, w_mem_kv, w_pool, pool_scale, w_o, ln1_g, ln1_b, ln2_g, ln2_b, w_r1, b_r1, w_r2, b_r2, w_gate, w_up, w_down):
    depth = w_in.shape[0]
    alpha = (2.0 * depth) ** 0.25
    h_p, h_s = x_prompt, x_sample
    outs = [[] for _ in range(8)]
    for l in range(depth):
        res = _layer(h_p, h_s, cache_win_k[l], cache_win_v[l], state_pool[l], cache_mem_k[l], cache_mem_v[l],
                     mem_prompt, w_in[l], w_mem_kv[l], w_pool[l], pool_scale[l], w_o[l], ln1_g[l], ln1_b[l],
                     ln2_g[l], ln2_b[l], w_r1[l], b_r1[l], w_r2[l], b_r2[l], w_gate[l], w_up[l], w_down[l], alpha)
        h_p, h_s = res[0], res[1]
        for lst, val in zip(outs, res[2:]):
            lst.append(val)
    return (h_p, h_s) + tuple(jnp.stack(o) for o in outs)
```

```python
import functools
import math

import jax
import jax.numpy as jnp
from jax import lax
from jax.experimental import pallas as pl
from jax.experimental.pallas import tpu as pltpu

F32 = jnp.float32
BF16 = jnp.bfloat16
HIGHEST = lax.Precision.HIGHEST
NEG_INF = float("-inf")

HEAD_DIM = 64
DILATED_PATTERNS = ((128, 1), (512, 4), (2048, 16))
BAND_BLOCK = 128
RES = max(d for _, d in DILATED_PATTERNS)
POOL_WINDOWS = (2, 4, 8, 16)
N_EXPERT_GROUPS = 4
EXPERTS_PER_GROUP = 8
N_EXPERTS = N_EXPERT_GROUPS * EXPERTS_PER_GROUP
PAST_LEN = 16384
LN_EPS = 1e-5
ATTN_SCALE = HEAD_DIM ** -0.5

LANES = 128
SUBLANES = 8
VMEM_LIMIT = 56 * 1024 * 1024

ROUTER_OFF = N_EXPERT_GROUPS
ATTN_BATCH = 8
DMA_UNROLL = 8


def _cparams(sem):
    return pltpu.CompilerParams(dimension_semantics=sem, vmem_limit_bytes=VMEM_LIMIT)


def _dot(a, b, precision=None):
    return jnp.dot(a, b, preferred_element_type=F32, precision=precision)


def _dot_nt(a, b, precision=None):
    return lax.dot_general(a, b, (((1,), (1,)), ((), ())), preferred_element_type=F32, precision=precision)


def _memproj_body(mem_ref, wkT_ref, wv_ref, wvT_ref, kmT_ref, vm_ref, vmT_ref):
    m = mem_ref[0].astype(BF16)
    kmT_ref[0] = _dot_nt(wkT_ref[...], m)
    vm_ref[0] = _dot(m, wv_ref[...])
    vmT_ref[0] = _dot_nt(wvT_ref[...], m)


def _memproj(mem, wkT, wv, wvT):
    B, M, D = mem.shape
    C = wv.shape[1]
    full = lambda shape: pl.BlockSpec(shape, lambda b: (0,) * len(shape))
    return pl.pallas_call(
        _memproj_body,
        grid=(B,),
        in_specs=[pl.BlockSpec((1, M, D), lambda b: (b, 0, 0)), full((C, D)), full((D, C)), full((C, D))],
        out_specs=[pl.BlockSpec((1, C, M), lambda b: (b, 0, 0)),
                   pl.BlockSpec((1, M, C), lambda b: (b, 0, 0)),
                   pl.BlockSpec((1, C, M), lambda b: (b, 0, 0))],
        out_shape=[jax.ShapeDtypeStruct((B, C, M), F32), jax.ShapeDtypeStruct((B, M, C), F32),
                   jax.ShapeDtypeStruct((B, C, M), F32)],
        compiler_params=_cparams(("arbitrary",)),
        name="memproj",
    )(mem, wkT, wv, wvT)


def _proj_body(d_a, d_b, x_ref, perm_ref, w_ref, wkvT_ref, q_ref, k_ref, v_ref, u_ref, qc_ref, kT_ref, vT_ref):
    xb = x_ref[0].astype(BF16)
    ts = xb.shape[0]
    xp = _dot(perm_ref[...], xb).astype(BF16)
    y = _dot(xp, w_ref[:, 0:3 * d_a])
    rows = ts // RES
    for r in range(RES):
        q_ref[0, r] = y[r * rows:(r + 1) * rows, 0:d_a]
        k_ref[0, r] = y[r * rows:(r + 1) * rows, d_a:2 * d_a]
        v_ref[0, r] = y[r * rows:(r + 1) * rows, 2 * d_a:3 * d_a]
    y2 = _dot(xb, w_ref[:, 3 * d_a:])
    u_ref[0] = y2[:, 0:d_b]
    qc_ref[0] = y2[:, d_b:]
    yT = _dot_nt(wkvT_ref[...], xb)
    kT_ref[0] = yT[0:d_a]
    vT_ref[0] = yT[d_a:]


def _residue_major_perm(n):
    dst = jnp.arange(n)
    src = RES * (dst % (n // RES)) + dst // (n // RES)
    return (src[:, None] == jnp.arange(n)[None, :]).astype(BF16)


def _proj(x, w_bf, wkvT_bf, d_a, d_b, d_c, ts):
    B, S, D = x.shape
    d_in = w_bf.shape[1]
    assert ts % (RES * SUBLANES) == 0 and S % ts == 0
    row = lambda n: pl.BlockSpec((1, ts, n), lambda b, i: (b, i, 0))
    col = lambda n: pl.BlockSpec((1, n, ts), lambda b, i: (b, 0, i))
    grp = pl.BlockSpec((1, RES, ts // RES, d_a), lambda b, i: (b, 0, i, 0))
    outs = pl.pallas_call(
        functools.partial(_proj_body, d_a, d_b),
        grid=(B, S // ts),
        in_specs=[row(D), pl.BlockSpec((ts, ts), lambda b, i: (0, 0)), pl.BlockSpec((D, d_in), lambda b, i: (0, 0)),
                  pl.BlockSpec((2 * d_a, D), lambda b, i: (0, 0))],
        out_specs=[grp, grp, grp, row(d_b), row(d_c), col(d_a), col(d_a)],
        out_shape=[jax.ShapeDtypeStruct((B, RES, S // RES, d_a), F32)] * 3
        + [jax.ShapeDtypeStruct((B, S, d_b), F32), jax.ShapeDtypeStruct((B, S, d_c), F32)]
        + [jax.ShapeDtypeStruct((B, d_a, S), F32)] * 2,
        compiler_params=_cparams(("arbitrary", "arbitrary")),
        name="proj",
    )(x, _residue_major_perm(ts), w_bf, wkvT_bf)
    return [o.reshape(B, S, d_a) for o in outs[:3]] + list(outs[3:])


def _attn_body(S, slopes_ref, q_ref, k_ref, v_ref, o_ref, opat_ref, lpat_ref):
    hp = pl.program_id(1)
    blk = BAND_BLOCK
    lane = lax.broadcasted_iota(jnp.int32, (blk, LANES), 1)
    head0 = lane < HEAD_DIM

    for pi, (window, dil) in enumerate(DILATED_PATTERNS):
        n_steps = window // dil
        ngrp = RES // dil
        cs = blk // ngrp
        cs_bits = cs.bit_length() - 1
        assert ngrp * dil == RES and cs * ngrp == blk and cs % SUBLANES == 0 and S == RES * blk
        has_prev = ngrp > 1
        nk = 2 * blk if has_prev else blk
        nat = lambda a, ngrp=ngrp, cs=cs, cs_bits=cs_bits: (a & (cs - 1)) * ngrp + (a >> cs_bits)
        qi = lax.broadcasted_iota(jnp.int32, (blk, nk), 0)
        kj = lax.broadcasted_iota(jnp.int32, (blk, nk), 1)
        if has_prev:
            steps = (nat(qi) + blk) - (nat(kj & (blk - 1)) + (kj & blk))
        else:
            steps = qi - kj
        valid = (steps >= 0) & (steps <= n_steps)
        dist = (steps * dil).astype(F32)
        biases = []
        for hh in range(2):
            slope = slopes_ref[2 * hp + hh]
            biases.append(jnp.where(valid, -slope * dist, NEG_INF))
        prev_cols = kj < blk

        ones_k = jnp.ones((ATTN_BATCH, nk, LANES), BF16)

        def blocks(it, carry, dil=dil, ngrp=ngrp, cs=cs, has_prev=has_prev, biases=biases, prev_cols=prev_cols,
                   pi=pi, nk=nk, ones_k=ones_k):
            def chunk(idx, c, back):
                r = idx // ngrp
                j = jnp.maximum(idx % ngrp - back, 0)
                return pl.ds(pl.multiple_of((r + dil * c) * blk + cs * j, cs), cs)

            def load(ref, idx, back=0):
                return jnp.concatenate([ref[0, chunk(idx, c, back), :] for c in range(ngrp)], axis=0)

            qs, ks, vs, firsts = [], [], [], []
            for b in range(ATTN_BATCH):
                idx = it * ATTN_BATCH + b
                qs.append(load(q_ref, idx))
                if has_prev:
                    ks.append(jnp.concatenate([load(k_ref, idx, 1), load(k_ref, idx)], axis=0))
                    vs.append(jnp.concatenate([load(v_ref, idx, 1), load(v_ref, idx)], axis=0))
                    firsts.append(jnp.where(prev_cols & (idx % ngrp == 0), NEG_INF, 0.0))
                else:
                    ks.append(load(k_ref, idx))
                    vs.append(load(v_ref, idx))
            q3 = jnp.stack(qs) * ATTN_SCALE
            k3 = jnp.stack(ks).astype(BF16)
            v3 = jnp.concatenate([jnp.stack(vs).astype(BF16), ones_k], axis=2)
            outs, lses = [], []
            for hh in range(2):
                qm = jnp.where(head0 if hh == 0 else ~head0, q3, 0.0).astype(BF16)
                s = jnp.einsum("bqd,bkd->bqk", qm, k3, preferred_element_type=F32) + biases[hh]
                if has_prev:
                    s = s + jnp.stack(firsts)
                m = jnp.max(s, axis=2, keepdims=True)
                p = jnp.exp(s - m).astype(BF16)
                o = jnp.einsum("bqk,bkd->bqd", p, v3, preferred_element_type=F32)
                den = o[:, :, LANES:]
                outs.append(o[:, :, :LANES] / den)
                lses.append(m + jnp.log(den))
            o_pair = jnp.where(head0, outs[0], outs[1])
            l_pair = jnp.where(head0, lses[0], lses[1])
            for b in range(ATTN_BATCH):
                idx = it * ATTN_BATCH + b
                for c in range(ngrp):
                    opat_ref[pi, chunk(idx, c, 0), :] = o_pair[b, c * cs:(c + 1) * cs]
                    lpat_ref[pi, chunk(idx, c, 0), :] = l_pair[b, c * cs:(c + 1) * cs]
            return carry

        lax.fori_loop(0, RES // ATTN_BATCH, blocks, 0)

    chunk = 256

    def mix(c, carry):
        rows = pl.ds(pl.multiple_of(c * chunk, chunk), chunk)
        l0, l1, l2 = lpat_ref[0, rows, :], lpat_ref[1, rows, :], lpat_ref[2, rows, :]
        mx = jnp.maximum(jnp.maximum(l0, l1), l2)
        w0, w1, w2 = jnp.exp(l0 - mx), jnp.exp(l1 - mx), jnp.exp(l2 - mx)
        num = w0 * opat_ref[0, rows, :] + w1 * opat_ref[1, rows, :] + w2 * opat_ref[2, rows, :]
        o_ref[0, rows, :] = (num / (w0 + w1 + w2)).astype(o_ref.dtype)
        return carry

    lax.fori_loop(0, S // chunk, mix, 0)


def _attn(q, k, v, slopes):
    B, S, d_a = q.shape
    assert S % (BAND_BLOCK * max(d for _, d in DILATED_PATTERNS)) == 0
    npair = d_a // LANES
    spec = pl.BlockSpec((1, S, LANES), lambda b, h, *_: (b, 0, h))
    grid_spec = pltpu.PrefetchScalarGridSpec(
        num_scalar_prefetch=0,
        grid=(B, npair),
        in_specs=[pl.BlockSpec(memory_space=pltpu.SMEM), spec, spec, spec],
        out_specs=spec,
        scratch_shapes=[pltpu.VMEM((len(DILATED_PATTERNS), S, LANES), F32),
                        pltpu.VMEM((len(DILATED_PATTERNS), S, LANES), F32)],
    )
    return pl.pallas_call(
        functools.partial(_attn_body, S),
        grid_spec=grid_spec,
        out_shape=jax.ShapeDtypeStruct((B, S, d_a), BF16),
        compiler_params=_cparams(("arbitrary", "arbitrary")),
        name="dilated_attn",
    )(slopes, q, k, v)


def _layer_norm(z, g, b):
    mu = jnp.mean(z, axis=-1, keepdims=True)
    zc = z - mu
    var = jnp.mean(zc * zc, axis=-1, keepdims=True)
    return zc * lax.rsqrt(var + LN_EPS) * g + b


def _route(logits):
    n = logits.shape[0]
    lane = lax.broadcasted_iota(jnp.int32, (n, LANES), 1)
    lane_f = lane.astype(F32)
    big = float(LANES)
    is_outer = lane < N_EXPERT_GROUPS
    l1 = jnp.where(is_outer, logits, NEG_INF)
    m1 = jnp.max(l1, axis=1, keepdims=True)
    g_sel = jnp.min(jnp.where(l1 == m1, lane_f, big), axis=1, keepdims=True)
    v1 = 1.0 / jnp.sum(jnp.exp(l1 - m1), axis=1, keepdims=True)
    lo = ROUTER_OFF + g_sel * EXPERTS_PER_GROUP
    in_group = (lane_f >= lo) & (lane_f < lo + EXPERTS_PER_GROUP)
    l2 = jnp.where(in_group, logits, NEG_INF)
    ma = jnp.max(l2, axis=1, keepdims=True)
    ia = jnp.min(jnp.where(l2 == ma, lane_f, big), axis=1, keepdims=True)
    l2b = jnp.where(lane_f == ia, NEG_INF, l2)
    mb = jnp.max(l2b, axis=1, keepdims=True)
    ib = jnp.min(jnp.where(l2b == mb, lane_f, big), axis=1, keepdims=True)
    eb = jnp.exp(mb - ma)
    wa = 1.0 / (1.0 + eb)
    wb = eb / (1.0 + eb)
    return ia - ROUTER_OFF, ib - ROUTER_OFF, v1 * wa, v1 * wb


def _rank_and_meta(e0, e1, g0, g1, carry):
    n = e0.shape[0]
    lane_f = lax.broadcasted_iota(jnp.int32, (n, LANES), 1).astype(F32)
    oh0 = (lane_f == e0).astype(F32)
    oh1 = (lane_f == e1).astype(F32)
    both = oh0 + oh1
    ti = lax.broadcasted_iota(jnp.int32, (n, n), 0)
    tj = lax.broadcasted_iota(jnp.int32, (n, n), 1)
    tri = (tj < ti).astype(BF16)
    before = _dot(tri, both.astype(BF16)) + carry
    r0 = jnp.sum(before * oh0, axis=1, keepdims=True)
    r1 = jnp.sum(before * oh1, axis=1, keepdims=True)
    new_carry = carry + jnp.sum(both, axis=0, keepdims=True)
    lane = lax.broadcasted_iota(jnp.int32, (n, LANES), 1)
    meta = jnp.zeros((n, LANES), F32)
    for i, val in enumerate((e0, e1, g0, g1, r0, r1)):
        meta = jnp.where(lane == i, val, meta)
    return meta, meta.T[0:SUBLANES], new_carry


def _lane_group(shape, width):
    lane = lax.broadcasted_iota(jnp.int32, shape, len(shape) - 1)
    grp = jnp.zeros(shape, jnp.int32)
    for g in range(1, shape[-1] // width):
        grp = grp + (lane >= g * width).astype(jnp.int32)
    return grp


def _select_by_group(grp, vals):
    out = vals[-1]
    for g in range(len(vals) - 2, -1, -1):
        out = jnp.where(grp == g, vals[g], out)
    return out


def _store_token_rows(ref, val):
    n, d = val.shape
    nchunk = d // LANES
    for c in range(nchunk):
        ref[pl.ds(c, n, stride=nchunk), :] = val[:, c * LANES:(c + 1) * LANES]


def _load_token_rows(ref, n, nchunk):
    return jnp.concatenate([ref[pl.ds(c, n, stride=nchunk), :] for c in range(nchunk)], axis=1)


HALO = 16


def _mix_body(tm, nt, n_tiles, d_b, alpha, x_ref, at_ref, unperm_ref, u_ref, uh_ref, qc_ref, kmT_ref, vm_ref,
              wo_ref, wp_ref, ps_ref, g1_ref, b1_ref, wr_ref, br_ref, cin_ref, hs_ref,
              h1_ref, meta_ref, metaT_ref, cnt_ref, carry_ref):
    g = pl.program_id(0)

    @pl.when(g == 0)
    def _():
        carry_ref[...] = cin_ref[...]

    @pl.when(g == n_tiles)
    def _():
        ns_rows = hs_ref.shape[0]
        h1_ref[0:ns_rows, :] = hs_ref[...]
        h1_ref[ns_rows:, :] = jnp.zeros((h1_ref.shape[0] - ns_rows, LANES), F32)

    @pl.when(g < n_tiles)
    def _():
        _mix_tile(tm, g % nt, d_b, alpha, x_ref, at_ref, unperm_ref, u_ref, uh_ref, qc_ref, kmT_ref, vm_ref, wo_ref,
                  wp_ref, ps_ref, g1_ref, b1_ref, wr_ref, br_ref, h1_ref, meta_ref, metaT_ref, cnt_ref, carry_ref)


def _mix_tile(tm, i, d_b, alpha, x_ref, at_ref, unperm_ref, u_ref, uh_ref, qc_ref, kmT_ref, vm_ref, wo_ref, wp_ref,
              ps_ref, g1_ref, b1_ref, wr_ref, br_ref, h1_ref, meta_ref, metaT_ref, cnt_ref, carry_ref):
    u = u_ref[0]
    halo = jnp.where(i == 0, 0.0, uh_ref[0])
    ext = jnp.concatenate([halo, halo, u], axis=0)
    s2 = ext + pltpu.roll(ext, 1, 0)
    s4 = s2 + pltpu.roll(s2, 2, 0)
    s8 = s4 + pltpu.roll(s4, 4, 0)
    s16 = s8 + pltpu.roll(s8, 8, 0)
    grp = _lane_group((tm, d_b), d_b // len(POOL_WINDOWS))
    sums = [s[2 * HALO:] for s in (s2, s4, s8, s16)]
    win = _select_by_group(grp, sums)
    wlen = _select_by_group(grp, [jnp.int32(w) for w in POOL_WINDOWS])
    pos = i * tm + lax.broadcasted_iota(jnp.int32, (tm, d_b), 0)
    cnt = jnp.minimum(wlen, pos + 1).astype(F32)
    diff = win / cnt - u
    pool = _dot(diff.astype(BF16), wp_ref[...]) * ps_ref[...]

    qc = qc_ref[0] * ATTN_SCALE
    d_c = qc.shape[1]
    kmT = kmT_ref[0].astype(BF16)
    vm = vm_ref[0].astype(BF16)
    hl = _lane_group((tm, d_c), HEAD_DIM)
    memo = jnp.zeros((tm, d_c), F32)
    for h in range(d_c // HEAD_DIM):
        qm = jnp.where(hl == h, qc, 0.0).astype(BF16)
        s = _dot(qm, kmT)
        p = jnp.exp(s - jnp.max(s, axis=1, keepdims=True))
        den = jnp.sum(p, axis=1, keepdims=True)
        o = _dot(p.astype(BF16), vm) / den
        memo = jnp.where(hl == h, o, memo)

    at_rl = jnp.concatenate([at_ref[0, r] for r in range(RES)], axis=0)
    attn = _dot(unperm_ref[...], at_rl).astype(BF16)
    cat = jnp.concatenate([attn, pool.astype(BF16), memo.astype(BF16)], axis=1)
    mixv = _dot(cat, wo_ref[...])
    h1 = _layer_norm(alpha * x_ref[0] + mixv, g1_ref[...], b1_ref[...])
    _store_token_rows(h1_ref, h1)

    logits = _dot(h1.astype(BF16), wr_ref[...]) + br_ref[...]
    e0, e1, g0, g1 = _route(logits)
    meta, metaT, new_carry = _rank_and_meta(e0, e1, g0, g1, carry_ref[0:1, :])
    meta_ref[...] = meta
    metaT_ref[...] = metaT
    carry_ref[...] = jnp.broadcast_to(new_carry, carry_ref.shape)
    cnt_ref[...] = carry_ref[...]


def _mix(x, attn_rm, u, qc, kmT, vm, wo_bf, wp_bd_bf, pool_scale, g1, b1, wr, br, counts_in, h1s, alpha, tm):
    B, S, D = x.shape
    d_a, d_b, d_c = attn_rm.shape[2], u.shape[2], qc.shape[2]
    n_mem = vm.shape[1]
    nt = S // tm
    n_tiles = B * nt
    nchunk = D // LANES
    assert tm % (RES * SUBLANES * 2) == 0 and h1s.shape[0] < tm * nchunk
    last = n_tiles - 1
    bi = lambda g: (jnp.minimum(g, last) // nt, jnp.minimum(g, last) % nt)
    row = lambda n: pl.BlockSpec((1, tm, n), lambda g: bi(g) + (0,))
    full = lambda shape: pl.BlockSpec(shape, lambda g: (0,) * len(shape))
    per_b = lambda shape: pl.BlockSpec((1,) + shape, lambda g: (bi(g)[0], 0, 0))
    halo_spec = pl.BlockSpec((1, HALO, d_b), lambda g: (bi(g)[0], jnp.maximum(bi(g)[1] * (tm // HALO) - 1, 0), 0))
    at_spec = pl.BlockSpec((1, RES, tm // RES, d_a), lambda g: (bi(g)[0], 0, bi(g)[1], 0))
    dst = jnp.arange(tm)
    unperm = ((dst % RES) * (tm // RES) + dst // RES)[:, None] == jnp.arange(tm)[None, :]
    return pl.pallas_call(
        functools.partial(_mix_body, tm, nt, n_tiles, d_b, alpha),
        grid=(n_tiles + 1,),
        in_specs=[row(D), at_spec, full((tm, tm)), row(d_b), halo_spec, row(d_c), per_b((d_c, n_mem)),
                  per_b((n_mem, d_c)), full((D, D)), full((d_b, d_b)), full((1, d_b)), full((1, D)), full((1, D)),
                  full((D, LANES)), full((1, LANES)), full((SUBLANES, LANES)), full(h1s.shape)],
        out_specs=[pl.BlockSpec((tm * nchunk, LANES), lambda g: (g, 0)),
                   pl.BlockSpec((tm, LANES), lambda g: (jnp.minimum(g, last), 0)),
                   pl.BlockSpec((SUBLANES, tm), lambda g: (0, jnp.minimum(g, last))),
                   pl.BlockSpec((SUBLANES, LANES), lambda g: (0, 0))],
        out_shape=[jax.ShapeDtypeStruct(((n_tiles + 1) * tm * nchunk, LANES), F32),
                   jax.ShapeDtypeStruct((B * S, LANES), F32),
                   jax.ShapeDtypeStruct((SUBLANES, B * S), F32),
                   jax.ShapeDtypeStruct((SUBLANES, LANES), F32)],
        scratch_shapes=[pltpu.VMEM((SUBLANES, LANES), F32)],
        compiler_params=_cparams(("arbitrary",)),
        name="mix_ln1_router",
    )(x, attn_rm.reshape(B, RES, S // RES, d_a), unperm.astype(BF16), u, u, qc, kmT, vm, wo_bf, wp_bd_bf,
      pool_scale, g1, b1, wr, br, counts_in, h1s)


def _columns_to_lanes(cols, rows):
    lane = lax.broadcasted_iota(jnp.int32, (rows, LANES), 1)
    tile = jnp.zeros((rows, LANES), F32)
    for t, c in enumerate(cols):
        tile = jnp.where(lane == t, c, tile)
    return tile


def _sproj_body(x_ref, wT_ref, o_ref):
    o_ref[...] = _dot_nt(wT_ref[...], x_ref[...].astype(BF16))


def _sproj(x, wT_bf):
    vm = pl.BlockSpec(memory_space=pltpu.VMEM)
    return pl.pallas_call(
        _sproj_body, in_specs=[vm, vm], out_specs=vm,
        out_shape=jax.ShapeDtypeStruct((wT_bf.shape[0], x.shape[0]), F32),
        compiler_params=pltpu.CompilerParams(vmem_limit_bytes=VMEM_LIMIT),
        name="decode_proj",
    )(x, wT_bf)


def _decode_body(T, wb, qc_row0, slopes_ref, pT_ref, ck_ref, cv_ref, mk_ref, mv_ref,
                 nk_ref, nv_ref, at_ref, mo_ref, qkv_ref):
    H = ck_ref.shape[1]
    d_a = H * HEAD_DIM
    ntok = pT_ref.shape[1]
    shift = (ntok - T * pl.program_id(0)) % ntok
    qkv_ref[...] = pltpu.roll(pT_ref[...], shift, 1)

    def member(dist, window, dil):
        assert dil & (dil - 1) == 0
        return ((dist & (dil - 1)) == 0) & (dist <= window) & (dist >= 0)

    OWN = min(d for _, d in DILATED_PATTERNS if d > 1)
    far = wb - LANES
    assert T <= OWN and far % OWN == 0 and wb % OWN == 0
    assert all(w <= LANES for w, d in DILATED_PATTERNS if d == 1) and all(d % OWN == 0 for _, d in DILATED_PATTERNS if d > 1)
    lane_far = lax.broadcasted_iota(jnp.int32, (T, far), 1)
    t_far = lax.broadcasted_iota(jnp.int32, (T, far), 0)
    owned = (lane_far & (OWN - 1)) == t_far
    dist_far = wb + t_far - lane_far
    t_near = lax.broadcasted_iota(jnp.int32, (T, LANES), 0)
    dist_near = wb + t_near - (far + lax.broadcasted_iota(jnp.int32, (T, LANES), 1))
    dist_new = lax.broadcasted_iota(jnp.int32, (T, T), 0) - lax.broadcasted_iota(jnp.int32, (T, T), 1)
    masks = [(None if dil == 1 else owned & member(dist_far, window, dil),
              member(dist_near, window, dil), member(dist_new, window, dil)) for window, dil in DILATED_PATTERNS]
    dist_far = dist_far.astype(F32)
    dist_near = dist_near.astype(F32)
    dist_new = jnp.maximum(dist_new, 0).astype(F32)
    rnd = lambda a: a.astype(BF16).astype(F32)

    lane_t = lax.broadcasted_iota(jnp.int32, (HEAD_DIM, LANES), 1)
    own_t = lane_t & (OWN - 1)
    last = wb - LANES
    n_far = far // LANES

    def head(h, carry):
        slope = slopes_ref[h]
        kTc = ck_ref[0, h]
        vTc = cv_ref[0, h]
        r0 = pl.multiple_of(h * HEAD_DIM, HEAD_DIM)
        qT = rnd(qkv_ref[pl.ds(r0, HEAD_DIM), :][:, 0:T] * ATTN_SCALE)
        kTn = qkv_ref[pl.ds(d_a + r0, HEAD_DIM), :][:, 0:T]
        vTn = qkv_ref[pl.ds(2 * d_a + r0, HEAD_DIM), :][:, 0:T]
        kTn_r, vTn_r = rnd(kTn), rnd(vTn)
        tile = lambda a, c: a[:, c * LANES:(c + 1) * LANES]

        qb = [jnp.broadcast_to(qT[:, t:t + 1], (HEAD_DIM, LANES)) for t in range(T)]
        qpat = qb[T - 1]
        for t in range(T - 1):
            qpat = jnp.where(own_t == t, qb[t], qpat)
        s_far = jnp.concatenate([jnp.sum(rnd(tile(kTc, c)) * qpat, axis=0, keepdims=True) for c in range(n_far)],
                                axis=1) - slope * dist_far
        k_near = rnd(tile(kTc, n_far))
        s_near = jnp.concatenate([jnp.sum(k_near * qb[t], axis=0, keepdims=True) for t in range(T)], axis=0)
        s_near = s_near - slope * dist_near
        s_new = jnp.concatenate([jnp.sum(kTn_r * qT[:, t:t + 1], axis=0, keepdims=True) for t in range(T)], axis=0)
        s_new = s_new - slope * dist_new

        rmax = lambda a: jnp.max(a, axis=1, keepdims=True)
        rsum = lambda a: jnp.sum(a, axis=1, keepdims=True)
        parts, lses = [], []
        for m_far, m_near, m_new in masks:
            sn, sw = jnp.where(m_near, s_near, NEG_INF), jnp.where(m_new, s_new, NEG_INF)
            m = jnp.maximum(rmax(sn), rmax(sw))
            if m_far is not None:
                sf = jnp.where(m_far, s_far, NEG_INF)
                m = jnp.maximum(m, rmax(sf))
            en, ew = jnp.exp(sn - m), jnp.exp(sw - m)
            den = rsum(en) + rsum(ew)
            ef = None
            if m_far is not None:
                ef = jnp.exp(sf - m)
                den = den + rsum(ef)
                ef = rnd(ef)
            parts.append((ef, rnd(en), rnd(ew), den))
            lses.append(m + jnp.log(den))
        top = functools.reduce(jnp.maximum, lses)
        mixw = [jnp.exp(l - top) for l in lses]
        total = functools.reduce(lambda a, b: a + b, mixw)
        p_far = p_near = p_new = None
        for (ef, en, ew, den), w in zip(parts, mixw):
            scale = w / (total * den)
            p_near = en * scale if p_near is None else p_near + en * scale
            p_new = ew * scale if p_new is None else p_new + ew * scale
            if ef is not None:
                p_far = ef * scale if p_far is None else p_far + ef * scale

        p_comb = jnp.sum(p_far, axis=0, keepdims=True)
        acc = rnd(tile(vTc, 0)) * tile(p_comb, 0)
        for c in range(1, n_far):
            acc = acc + rnd(tile(vTc, c)) * tile(p_comb, c)
        v_near = rnd(tile(vTc, n_far))
        cols = []
        for t in range(T):
            cols.append(jnp.sum(jnp.where(own_t == t, acc, 0.0) + v_near * p_near[t:t + 1], axis=1, keepdims=True)
                        + jnp.sum(vTn_r * p_new[t:t + 1], axis=1, keepdims=True))
        at_ref[0, pl.ds(r0, HEAD_DIM), :] = _columns_to_lanes(cols, HEAD_DIM)

        rk = pltpu.roll(kTc, wb - T, 1)
        rv = pltpu.roll(vTc, wb - T, 1)
        nk_ref[0, h] = rk
        nv_ref[0, h] = rv
        newk = jnp.zeros((HEAD_DIM, LANES), F32)
        newv = jnp.zeros((HEAD_DIM, LANES), F32)
        for t in range(T):
            newk = jnp.where(lane_t == LANES - T + t, kTn[:, t:t + 1], newk)
            newv = jnp.where(lane_t == LANES - T + t, vTn[:, t:t + 1], newv)
        nk_ref[0, h, :, last:] = jnp.where(lane_t >= LANES - T, newk, rk[:, last:])
        nv_ref[0, h, :, last:] = jnp.where(lane_t >= LANES - T, newv, rv[:, last:])
        return carry

    lax.fori_loop(0, H, head, 0, unroll=2)

    Hc = mk_ref.shape[1]
    rows = []
    for h in range(Hc):
        kT = rnd(mk_ref[0, h])
        r0 = h * HEAD_DIM
        qT = rnd(qkv_ref[qc_row0 + r0:qc_row0 + r0 + HEAD_DIM, :][:, 0:T] * ATTN_SCALE)
        rows += [jnp.sum(kT * qT[:, t:t + 1], axis=0, keepdims=True) for t in range(T)]
    s = jnp.concatenate(rows, axis=0)
    p = jnp.exp(s - jnp.max(s, axis=1, keepdims=True))
    p = rnd(p / jnp.sum(p, axis=1, keepdims=True))
    for h in range(Hc):
        vT = rnd(mv_ref[0, h])
        cols = [jnp.sum(vT * p[h * T + t:h * T + t + 1], axis=1, keepdims=True) for t in range(T)]
        mo_ref[0, h * HEAD_DIM:(h + 1) * HEAD_DIM, :] = _columns_to_lanes(cols, HEAD_DIM)


def _decode(projT, T, qc_row0, ck, cv, mk, mv, slopes):
    DB, H, hd, wb = ck.shape
    _, Hc, _, n_mem = mk.shape
    d_a, d_c = H * hd, Hc * hd
    assert wb >= max(w for w, _ in DILATED_PATTERNS) and wb % LANES == 0 and T <= LANES
    assert projT.shape[1] == DB * T and projT.shape[0] >= qc_row0 + d_c
    cache = pl.BlockSpec((1, H, hd, wb), lambda b, *_: (b, 0, 0, 0))
    memc = pl.BlockSpec((1, Hc, hd, n_mem), lambda b, *_: (b, 0, 0, 0))
    grid_spec = pltpu.PrefetchScalarGridSpec(
        num_scalar_prefetch=0,
        grid=(DB,),
        in_specs=[pl.BlockSpec(memory_space=pltpu.SMEM),
                  pl.BlockSpec(projT.shape, lambda b, *_: (0, 0)),
                  cache, cache, memc, memc],
        out_specs=[cache, cache,
                   pl.BlockSpec((1, d_a, LANES), lambda b, *_: (b, 0, 0)),
                   pl.BlockSpec((1, d_c, LANES), lambda b, *_: (b, 0, 0))],
        scratch_shapes=[pltpu.VMEM(projT.shape, F32)],
    )
    nk, nv, at, mo = pl.pallas_call(
        functools.partial(_decode_body, T, wb, qc_row0),
        grid_spec=grid_spec,
        out_shape=[jax.ShapeDtypeStruct(ck.shape, F32), jax.ShapeDtypeStruct(cv.shape, F32),
                   jax.ShapeDtypeStruct((DB, d_a, LANES), F32), jax.ShapeDtypeStruct((DB, d_c, LANES), F32)],
        compiler_params=_cparams(("arbitrary",)),
        name="decode_attn_cache",
    )(slopes, projT, ck, cv, mk, mv)
    return nk, nv, at[:, :, :T], mo[:, :, :T]


def _smix_body(T, alpha, pos0, x_ref, at_ref, mo_ref, st_ref, wu_ref, wo_ref, wp_ref, ps_ref, g1_ref, b1_ref,
               wr_ref, br_ref, cin_ref, h1_ref, meta_ref, metaT_ref, cnt_ref, pool_ref):
    n = x_ref.shape[0]
    db = n // T
    pb = st_ref.shape[0]
    d_b = st_ref.shape[2]
    x = x_ref[...]
    bdot = lambda a, b: _dot(a.astype(BF16), b.astype(BF16))
    u_new = bdot(x, wu_ref[...])
    seq = [st_ref[j] for j in range(pb)] + [u_new[t * db:(t + 1) * db] for t in range(T)]
    for j in range(pb):
        pool_ref[j] = seq[j + T]
    grp = _lane_group((db, d_b), d_b // len(POOL_WINDOWS))
    diffs = []
    for t in range(T):
        j = pb + t
        per_w = []
        for w in POOL_WINDOWS:
            acc = seq[j]
            for back in range(1, w):
                if j - back >= 0:
                    acc = acc + seq[j - back]
            per_w.append(acc / float(min(w, pos0 + j + 1)))
        diffs.append(_select_by_group(grp, per_w) - seq[j])
    diff = jnp.concatenate(diffs, axis=0)
    pool = bdot(diff, wp_ref[...]) * ps_ref[...]
    cat = jnp.concatenate([at_ref[...], pool, mo_ref[...]], axis=1)
    mixv = bdot(cat, wo_ref[...])
    h1 = _layer_norm(alpha * x + mixv, g1_ref[...], b1_ref[...])
    _store_token_rows(h1_ref, h1)
    logits = bdot(h1, wr_ref[...]) + br_ref[...]
    e0, e1, g0, g1 = _route(logits)
    meta, metaT, new_carry = _rank_and_meta(e0, e1, g0, g1, cin_ref[0:1, :])
    meta_ref[...] = meta
    metaT_ref[...] = metaT
    cnt_ref[...] = jnp.broadcast_to(new_carry, cnt_ref.shape)


def _smix(x_tb, attn_tb, memo_tb, state, wu, wo, wp_bd, pool_scale, g1, b1, wr, br, counts_in, alpha, T):
    n, D = x_tb.shape
    pb, db, d_b = state.shape
    nchunk = D // LANES
    vm = pl.BlockSpec(memory_space=pltpu.VMEM)
    return pl.pallas_call(
        functools.partial(_smix_body, T, alpha, PAST_LEN - pb),
        in_specs=[vm] * 13,
        out_specs=[vm] * 5,
        out_shape=[jax.ShapeDtypeStruct((n * nchunk, LANES), F32), jax.ShapeDtypeStruct((n, LANES), F32),
                   jax.ShapeDtypeStruct((SUBLANES, n), F32),
                   jax.ShapeDtypeStruct((SUBLANES, LANES), F32), jax.ShapeDtypeStruct((pb, db, d_b), F32)],
        compiler_params=pltpu.CompilerParams(vmem_limit_bytes=VMEM_LIMIT),
        name="decode_mix_ln1_router",
    )(x_tb, attn_tb, memo_tb, state, wu, wo, wp_bd, pool_scale, g1, b1, wr, br, counts_in)


def _slot_map_body(blk, n, pos_ref, fill_ref, inv_ref, sem):
    g = pl.program_id(0)

    @pl.when(g == 0)
    def _():
        c = pltpu.make_async_copy(fill_ref, inv_ref, sem)
        c.start()
        c.wait()

    group = math.gcd(math.gcd(blk, n), 32)

    def body(j, carry):
        pair0 = g * blk + j * group
        token0 = jnp.where(pair0 >= n, pair0 - n, pair0)
        for u in range(group):
            inv_ref[pos_ref[0, 0, j * group + u]] = token0 + u
        return carry

    lax.fori_loop(0, blk // group, body, 0)


def _slot_map(pos, n_slots, zero_token, blk):
    n2 = pos.shape[0]
    assert n2 % blk == 0 and n_slots % 1024 == 0
    return pl.pallas_call(
        functools.partial(_slot_map_body, blk, n2 // 2),
        grid=(n2 // blk,),
        in_specs=[pl.BlockSpec((1, 1, blk), lambda g: (g, 0, 0), memory_space=pltpu.SMEM),
                  pl.BlockSpec(memory_space=pl.ANY)],
        out_specs=pl.BlockSpec(memory_space=pltpu.SMEM),
        out_shape=jax.ShapeDtypeStruct((n_slots,), jnp.int32),
        scratch_shapes=[pltpu.SemaphoreType.DMA(())],
        compiler_params=_cparams(("arbitrary",)),
        name="slot_map",
    )(pos.reshape(n2 // blk, 1, blk), jnp.full((n_slots,), zero_token, jnp.int32))


def _expert_body(tmx, nchunk, te_ref, nt_ref, tv_ref, cur_ref, nxt_ref, h_ref, wg_ref, wu_ref, wd_ref, ys_ref,
                 buf_a, buf_b, wg_bf, wu_bf, wd_bf, sems):
    i = pl.program_id(0)
    n_tiles = nt_ref[0]

    def gather(idx_ref, buf, s, tile, wait):
        groups = (tv_ref[tile] + DMA_UNROLL - 1) // DMA_UNROLL

        def body(g, carry):
            for u in range(DMA_UNROLL):
                j = g * DMA_UNROLL + u
                dst = buf.at[pl.ds(pl.multiple_of(j * nchunk, nchunk), nchunk)]
                copy = pltpu.make_async_copy(h_ref.at[idx_ref[0, 0, j]], dst, sems.at[s])
                if wait:
                    copy.wait()
                else:
                    copy.start(priority=u % 2)
            return carry

        lax.fori_loop(0, groups, body, 0)

    @pl.when(i == 0)
    def _():
        buf_a[...] = jnp.zeros_like(buf_a)
        buf_b[...] = jnp.zeros_like(buf_b)
        gather(cur_ref, buf_a, 0, 0, False)

    it = jnp.minimum(i, pl.num_programs(0) - 2)
    @pl.when((i < n_tiles) & ((i == 0) | (te_ref[it] != te_ref[jnp.maximum(it - 1, 0)])))
    def _():
        wg_bf[...] = wg_ref[0].astype(BF16)
        wu_bf[...] = wu_ref[0].astype(BF16)
        wd_bf[...] = wd_ref[0].astype(BF16)

    def step(cur, s_cur, nxt, s_nxt):
        gather(cur_ref, cur, s_cur, it, True)

        @pl.when(i + 1 < n_tiles)
        def _():
            gather(nxt_ref, nxt, s_nxt, it + 1, False)

        x = _load_token_rows(cur, tmx, nchunk).astype(BF16)
        hg = _dot(x, wg_bf[...])
        hu = _dot(x, wu_bf[...])
        a = (hg * jax.nn.sigmoid(hg) * hu).astype(BF16)
        y = _dot(a, wd_bf[...])
        _store_token_rows(ys_ref, y)

    @pl.when((i < n_tiles) & (i % 2 == 0))
    def _():
        step(buf_a, 0, buf_b, 1)

    @pl.when((i < n_tiles) & (i % 2 == 1))
    def _():
        step(buf_b, 1, buf_a, 0)

    @pl.when(i >= n_tiles)
    def _():
        ys_ref[...] = jnp.zeros_like(ys_ref)


def _experts(tile_expert, n_tiles, tile_valid, inv, h_rows, wg, wu, wd, tmx, max_tiles):
    E, D, F = wg.shape
    nchunk = D // LANES
    inv3 = inv.reshape(-1, 1, tmx)
    last_blk = inv3.shape[0] - 1
    assert last_blk >= max_tiles and tmx % DMA_UNROLL == 0
    weights = lambda shape: pl.BlockSpec((1,) + shape, lambda i, te, *_: (te[jnp.minimum(i, max_tiles - 1)], 0, 0))
    rows = pltpu.VMEM((tmx * nchunk, LANES), F32)
    grid_spec = pltpu.PrefetchScalarGridSpec(
        num_scalar_prefetch=3,
        grid=(max_tiles + 1,),
        in_specs=[pl.BlockSpec((1, 1, tmx), lambda i, *_: (i, 0, 0), memory_space=pltpu.SMEM),
                  pl.BlockSpec((1, 1, tmx), lambda i, *_: (jnp.minimum(i + 1, last_blk), 0, 0),
                               memory_space=pltpu.SMEM),
                  pl.BlockSpec(memory_space=pl.ANY), weights((D, F)), weights((D, F)), weights((F, D))],
        out_specs=pl.BlockSpec((tmx * nchunk, LANES), lambda i, *_: (i, 0)),
        scratch_shapes=[rows, rows, pltpu.VMEM((D, F), BF16), pltpu.VMEM((D, F), BF16), pltpu.VMEM((F, D), BF16),
                        pltpu.SemaphoreType.DMA((2,))],
    )
    return pl.pallas_call(
        functools.partial(_expert_body, tmx, nchunk),
        grid_spec=grid_spec,
        out_shape=jax.ShapeDtypeStruct(((max_tiles + 1) * tmx * nchunk, LANES), F32),
        compiler_params=_cparams(("arbitrary",)),
        name="expert_swiglu",
    )(tile_expert, n_tiles, tile_valid, inv3, inv3, h_rows, wg, wu, wd)


def _combine_body(tm, nchunk, alpha, pos_ref, nxt_ref, h1_ref, meta_ref, ys_ref, g2_ref, b2_ref, o_ref, buf_a, buf_b,
                  sems):
    i = pl.program_id(0)
    last = pl.num_programs(0) - 1

    def copy(idx_ref, buf, s, t, k):
        dst = buf.at[k, pl.ds(pl.multiple_of(t * nchunk, nchunk), nchunk)]
        return pltpu.make_async_copy(ys_ref.at[idx_ref[0, 0, k * tm + t]], dst, sems.at[s])

    def start(idx_ref, buf, s):
        def body(t, carry):
            for k in range(2):
                copy(idx_ref, buf, s, t, k).start(priority=k)
            return carry

        lax.fori_loop(0, tm, body, 0, unroll=DMA_UNROLL)

    def wait(buf, s):
        def body(t, carry):
            for k in range(2):
                copy(pos_ref, buf, s, t, k).wait()
            return carry

        lax.fori_loop(0, tm, body, 0, unroll=DMA_UNROLL)

    @pl.when(i == 0)
    def _():
        start(pos_ref, buf_a, 0)

    def step(cur, s_cur, nxt, s_nxt):
        wait(cur, s_cur)
        start(nxt_ref, nxt, s_nxt)
        h1 = _load_token_rows(h1_ref, tm, nchunk)
        y0 = _load_token_rows(cur.at[0], tm, nchunk)
        y1 = _load_token_rows(cur.at[1], tm, nchunk)
        meta = meta_ref[...]
        lane = lax.broadcasted_iota(jnp.int32, meta.shape, 1)
        gate0 = jnp.sum(jnp.where(lane == 2, meta, 0.0), axis=1, keepdims=True)
        gate1 = jnp.sum(jnp.where(lane == 3, meta, 0.0), axis=1, keepdims=True)
        f = gate0 * y0 + gate1 * y1
        o_ref[...] = _layer_norm(alpha * h1 + f, g2_ref[...], b2_ref[...])

        @pl.when(i == last)
        def _():
            wait(nxt, s_nxt)

    @pl.when(i % 2 == 0)
    def _():
        step(buf_a, 0, buf_b, 1)

    @pl.when(i % 2 == 1)
    def _():
        step(buf_b, 1, buf_a, 0)


def _combine(pos2, tok0, n, h1_flat, meta, ys, g2, b2, alpha, tm):
    D = g2.shape[1]
    nchunk = D // LANES
    assert n % tm == 0 and tok0 % tm == 0
    pos3 = jnp.transpose(pos2.reshape(2, -1, tm), (1, 0, 2)).reshape(-1, 1, 2 * tm)
    off = tok0 // tm
    last = off + n // tm - 1
    return pl.pallas_call(
        functools.partial(_combine_body, tm, nchunk, alpha),
        grid=(n // tm,),
        in_specs=[pl.BlockSpec((1, 1, 2 * tm), lambda i: (i + off, 0, 0), memory_space=pltpu.SMEM),
                  pl.BlockSpec((1, 1, 2 * tm), lambda i: (jnp.minimum(i + off + 1, last), 0, 0),
                               memory_space=pltpu.SMEM),
                  pl.BlockSpec((tm * nchunk, LANES), lambda i: (i + off, 0)),
                  pl.BlockSpec((tm, LANES), lambda i: (i, 0)),
                  pl.BlockSpec(memory_space=pl.ANY),
                  pl.BlockSpec((1, D), lambda i: (0, 0)), pl.BlockSpec((1, D), lambda i: (0, 0))],
        out_specs=pl.BlockSpec((tm, D), lambda i: (i, 0)),
        out_shape=jax.ShapeDtypeStruct((n, D), F32),
        scratch_shapes=[pltpu.VMEM((2, tm * nchunk, LANES), F32), pltpu.VMEM((2, tm * nchunk, LANES), F32),
                        pltpu.SemaphoreType.DMA((2,))],
        compiler_params=_cparams(("arbitrary",)),
        name="combine_ln2",
    )(pos3, pos3, h1_flat, meta, ys, g2, b2)


def _block_diag(w):
    g, a, b = w.shape
    eye = jnp.eye(g, dtype=w.dtype)
    return (eye[:, None, :, None] * w[:, :, None, :]).reshape(g * a, g * b)


def _layer(h_p, h_s, win_k, win_v, pool_st, mem_k, mem_v, mem_prompt,
           w_in, w_mem_kv, w_pool, pool_scale, w_o, ln1_g, ln1_b, ln2_g, ln2_b,
           w_r1, b_r1, w_r2, b_r2, w_gate, w_up, w_down, alpha):
    B, S, D = h_p.shape
    DB, T, _ = h_s.shape
    H = win_k.shape[2]
    Hc = mem_k.shape[2]
    d_a, d_c = H * HEAD_DIM, Hc * HEAD_DIM
    d_b = pool_st.shape[2]
    nchunk = D // LANES
    slopes = 2.0 ** (-8.0 * jnp.arange(1, H + 1, dtype=F32) / H)

    w_in_bf = w_in.astype(BF16)
    w_inT_bf = w_in_bf.T
    wkvT_bf = w_inT_bf[d_a:3 * d_a]
    wp_bd = _block_diag(w_pool)
    ps = pool_scale.reshape(1, d_b)
    g1, b1 = ln1_g.reshape(1, D), ln1_b.reshape(1, D)
    g2, b2 = ln2_g.reshape(1, D), ln2_b.reshape(1, D)
    n_r = N_EXPERT_GROUPS + N_EXPERTS
    wr = jnp.concatenate([w_r1, jnp.transpose(w_r2, (1, 0, 2)).reshape(D, N_EXPERTS),
                          jnp.zeros((D, LANES - n_r), F32)], axis=1)
    br = jnp.concatenate([b_r1, b_r2.reshape(-1), jnp.zeros((LANES - n_r,), F32)]).reshape(1, LANES)

    ck = jnp.transpose(win_k, (0, 2, 3, 1))
    cv = jnp.transpose(win_v, (0, 2, 3, 1))
    mk = jnp.transpose(mem_k, (0, 2, 3, 1))
    mv = jnp.transpose(mem_v, (0, 2, 3, 1))
    projT = _sproj(h_s.reshape(DB * T, D), w_inT_bf)
    nk, nv, attn_sT, memo_sT = _decode(projT, T, 3 * d_a + d_b, ck, cv, mk, mv, slopes)
    to_tb = lambda a: jnp.transpose(a, (2, 0, 1)).reshape(T * DB, a.shape[1])
    x_tb = jnp.transpose(h_s, (1, 0, 2)).reshape(T * DB, D)
    state = jnp.transpose(pool_st, (1, 0, 2))
    h1s, meta_s, metaT_s, cnt_s, new_pool = _smix(x_tb, to_tb(attn_sT), to_tb(memo_sT), state,
                                         w_in[:, 3 * d_a:3 * d_a + d_b], w_o, wp_bd, ps, g1, b1, wr, br,
                                         jnp.zeros((SUBLANES, LANES), F32), alpha, T)

    w_memT = w_mem_kv.T
    kmT, vm, vmT = _memproj(mem_prompt, w_memT[:d_c].astype(BF16), w_mem_kv[:, d_c:].astype(BF16),
                            w_memT[d_c:].astype(BF16))
    q, k, v, u, qc, kT, vT = _proj(h_p, w_in_bf, wkvT_bf, d_a, d_b, d_c, ts=512)
    attn = _attn(q, k, v, slopes)
    h_rows, meta_p, metaT_p, cnt_all = _mix(h_p, attn, u, qc, kmT, vm, w_o.astype(BF16), wp_bd.astype(BF16), ps, g1, b1,
                                   wr.astype(BF16), br, cnt_s, h1s, alpha, tm=512)

    tmx = 256
    n_p, n_s = B * S, DB * T
    n = n_p + n_s
    counts = cnt_all[0, :N_EXPERTS].astype(jnp.int32)
    padded = (counts + tmx - 1) // tmx * tmx
    seg_end = jnp.cumsum(padded)
    seg_off = seg_end - padded
    metaT = jnp.concatenate([metaT_p, metaT_s], axis=1)
    e_ids = metaT[0:2].astype(jnp.int32)
    experts = jnp.arange(N_EXPERTS).reshape((N_EXPERTS,) + (1,) * 2)
    lookup = lambda table, idx: jnp.sum(
        jnp.where(idx.reshape((1,) * (3 - idx.ndim) + idx.shape) == experts, table.reshape(experts.shape), 0),
        axis=0).reshape(idx.shape)
    pos2 = lookup(seg_off, e_ids) + metaT[4:6].astype(jnp.int32)
    max_tiles = (2 * n) // tmx + N_EXPERTS
    n_tiles = (seg_end[-1] // tmx).astype(jnp.int32).reshape(1)
    tile_row0 = jnp.arange(max_tiles, dtype=jnp.int32) * tmx
    tile_expert = jnp.sum((tile_row0[:, None] >= seg_end[None, :]).astype(jnp.int32), axis=1)
    tile_expert = jnp.minimum(tile_expert, N_EXPERTS - 1)

    tok_tile = math.gcd(math.gcd(n_p, n_s), 128)
    n_slots = -(-((max_tiles + 1) * tmx) // 1024) * 1024
    inv = _slot_map(pos2.reshape(-1), n_slots, n, blk=math.gcd(2 * n, 2048))
    tile_valid = jnp.clip(lookup(seg_off + counts, tile_expert) - tile_row0, 0, tmx).astype(jnp.int32)
    ys = _experts(tile_expert, n_tiles, tile_valid, inv, h_rows.reshape(-1, nchunk, LANES),
                  w_gate.reshape(N_EXPERTS, D, -1),
                  w_up.reshape(N_EXPERTS, D, -1), w_down.reshape(N_EXPERTS, -1, D), tmx, max_tiles)
    ys3 = ys.reshape(-1, nchunk, LANES)
    y_p = _combine(pos2, 0, n_p, h_rows, meta_p, ys3, g2, b2, alpha, tm=tok_tile)
    y_s = _combine(pos2, n_p, n_s, h_rows, meta_s, ys3, g2, b2, alpha, tm=tok_tile)

    y_p = y_p.reshape(B, S, D)
    y_s = jnp.transpose(y_s.reshape(T, DB, D), (1, 0, 2))
    heads = lambda a, h: jnp.transpose(a.reshape(a.shape[0], h, HEAD_DIM, a.shape[2]), (0, 3, 1, 2))
    wbp = min(max(w for w, _ in DILATED_PATTERNS), S)
    new_wk_p = heads(kT, H)[:, S - wbp:]
    new_wv_p = heads(vT, H)[:, S - wbp:]
    pb = pool_st.shape[1]
    new_pool_p = u[:, S - pb:]
    new_mk_p = heads(kmT, Hc)
    new_mv_p = heads(vmT, Hc)
    new_wk_s = jnp.transpose(nk, (0, 3, 1, 2))
    new_wv_s = jnp.transpose(nv, (0, 3, 1, 2))
    new_pool_s = jnp.transpose(new_pool, (1, 0, 2))
    return (y_p, y_s, new_wk_p, new_wv_p, new_pool_p, new_mk_p, new_mv_p, new_wk_s, new_wv_s, new_pool_s)


def kernel(x_prompt, x_sample, cache_win_k, cache_win_v, state_pool, cache_mem_k, cache_mem_v, mem_prompt, w_in, w_mem_kv, w_pool, pool_scale, w_o, ln1_g, ln1_b, ln2_g, ln2_b, w_r1, b_r1, w_r2, b_r2, w_gate, w_up, w_down):
    depth = w_in.shape[0]
    alpha = (2.0 * depth) ** 0.25
    h_p, h_s = x_prompt, x_sample
    outs = [[] for _ in range(8)]
    for l in range(depth):
        res = _layer(h_p, h_s, cache_win_k[l], cache_win_v[l], state_pool[l], cache_mem_k[l], cache_mem_v[l],
                     mem_prompt, w_in[l], w_mem_kv[l], w_pool[l], pool_scale[l], w_o[l], ln1_g[l], ln1_b[l],
                     ln2_g[l], ln2_b[l], w_r1[l], b_r1[l], w_r2[l], b_r2[l], w_gate[l], w_up[l], w_down[l], alpha)
        h_p, h_s = res[0], res[1]
        for lst, val in zip(outs, res[2:]):
            lst.append(val)
    return (h_p, h_s) + tuple(jnp.stack(o) for o in outs)
```

```python
import functools
import math

import jax
import jax.numpy as jnp
from jax import lax
from jax.experimental import pallas as pl
from jax.experimental.pallas import tpu as pltpu

F32 = jnp.float32
BF16 = jnp.bfloat16
HIGHEST = lax.Precision.HIGHEST
NEG_INF = float("-inf")

HEAD_DIM = 64
DILATED_PATTERNS = ((128, 1), (512, 4), (2048, 16))
BAND_BLOCK = 128
RES = max(d for _, d in DILATED_PATTERNS)
POOL_WINDOWS = (2, 4, 8, 16)
N_EXPERT_GROUPS = 4
EXPERTS_PER_GROUP = 8
N_EXPERTS = N_EXPERT_GROUPS * EXPERTS_PER_GROUP
PAST_LEN = 16384
LN_EPS = 1e-5
ATTN_SCALE = HEAD_DIM ** -0.5

LANES = 128
SUBLANES = 8
VMEM_LIMIT = 56 * 1024 * 1024

ROUTER_OFF = N_EXPERT_GROUPS
ATTN_BATCH = 16
DMA_UNROLL = 8


def _cparams(sem):
    return pltpu.CompilerParams(dimension_semantics=sem, vmem_limit_bytes=VMEM_LIMIT)


def _dot(a, b, precision=None):
    return jnp.dot(a, b, preferred_element_type=F32, precision=precision)


def _dot_nt(a, b, precision=None):
    return lax.dot_general(a, b, (((1,), (1,)), ((), ())), preferred_element_type=F32, precision=precision)


def _memproj_body(mem_ref, wkT_ref, wv_ref, wvT_ref, kmT_ref, vm_ref, vmT_ref):
    m = mem_ref[0].astype(BF16)
    kmT_ref[0] = _dot_nt(wkT_ref[...], m)
    vm_ref[0] = _dot(m, wv_ref[...])
    vmT_ref[0] = _dot_nt(wvT_ref[...], m)


def _memproj(mem, wkT, wv, wvT):
    B, M, D = mem.shape
    C = wv.shape[1]
    full = lambda shape: pl.BlockSpec(shape, lambda b: (0,) * len(shape))
    return pl.pallas_call(
        _memproj_body,
        grid=(B,),
        in_specs=[pl.BlockSpec((1, M, D), lambda b: (b, 0, 0)), full((C, D)), full((D, C)), full((C, D))],
        out_specs=[pl.BlockSpec((1, C, M), lambda b: (b, 0, 0)),
                   pl.BlockSpec((1, M, C), lambda b: (b, 0, 0)),
                   pl.BlockSpec((1, C, M), lambda b: (b, 0, 0))],
        out_shape=[jax.ShapeDtypeStruct((B, C, M), F32), jax.ShapeDtypeStruct((B, M, C), F32),
                   jax.ShapeDtypeStruct((B, C, M), F32)],
        compiler_params=_cparams(("arbitrary",)),
        name="memproj",
    )(mem, wkT, wv, wvT)


def _proj_body(d_a, d_b, x_ref, perm_ref, w_ref, wkvT_ref, q_ref, k_ref, v_ref, u_ref, qc_ref, kT_ref, vT_ref):
    xb = x_ref[0].astype(BF16)
    ts = xb.shape[0]
    xp = _dot(perm_ref[...], xb).astype(BF16)
    y = _dot(xp, w_ref[:, 0:3 * d_a])
    rows = ts // RES
    for r in range(RES):
        q_ref[0, r] = y[r * rows:(r + 1) * rows, 0:d_a]
        k_ref[0, r] = y[r * rows:(r + 1) * rows, d_a:2 * d_a]
        v_ref[0, r] = y[r * rows:(r + 1) * rows, 2 * d_a:3 * d_a]
    y2 = _dot(xb, w_ref[:, 3 * d_a:])
    u_ref[0] = y2[:, 0:d_b]
    qc_ref[0] = y2[:, d_b:]
    yT = _dot_nt(wkvT_ref[...], xb)
    kT_ref[0] = yT[0:d_a]
    vT_ref[0] = yT[d_a:]


def _residue_major_perm(n):
    dst = jnp.arange(n)
    src = RES * (dst % (n // RES)) + dst // (n // RES)
    return (src[:, None] == jnp.arange(n)[None, :]).astype(BF16)


def _proj(x, w_bf, wkvT_bf, d_a, d_b, d_c, ts):
    B, S, D = x.shape
    d_in = w_bf.shape[1]
    assert ts % (RES * SUBLANES) == 0 and S % ts == 0
    row = lambda n: pl.BlockSpec((1, ts, n), lambda b, i: (b, i, 0))
    col = lambda n: pl.BlockSpec((1, n, ts), lambda b, i: (b, 0, i))
    grp = pl.BlockSpec((1, RES, ts // RES, d_a), lambda b, i: (b, 0, i, 0))
    outs = pl.pallas_call(
        functools.partial(_proj_body, d_a, d_b),
        grid=(B, S // ts),
        in_specs=[row(D), pl.BlockSpec((ts, ts), lambda b, i: (0, 0)), pl.BlockSpec((D, d_in), lambda b, i: (0, 0)),
                  pl.BlockSpec((2 * d_a, D), lambda b, i: (0, 0))],
        out_specs=[grp, grp, grp, row(d_b), row(d_c), col(d_a), col(d_a)],
        out_shape=[jax.ShapeDtypeStruct((B, RES, S // RES, d_a), F32)] * 3
        + [jax.ShapeDtypeStruct((B, S, d_b), F32), jax.ShapeDtypeStruct((B, S, d_c), F32)]
        + [jax.ShapeDtypeStruct((B, d_a, S), F32)] * 2,
        compiler_params=_cparams(("arbitrary", "arbitrary")),
        name="proj",
    )(x, _residue_major_perm(ts), w_bf, wkvT_bf)
    return [o.reshape(B, S, d_a) for o in outs[:3]] + list(outs[3:])


def _attn_body(S, slopes_ref, q_ref, k_ref, v_ref, o_ref, opat_ref, lpat_ref):
    hp = pl.program_id(1)
    blk = BAND_BLOCK
    lane = lax.broadcasted_iota(jnp.int32, (blk, LANES), 1)
    head0 = lane < HEAD_DIM

    for pi, (window, dil) in enumerate(DILATED_PATTERNS):
        n_steps = window // dil
        ngrp = RES // dil
        cs = blk // ngrp
        cs_bits = cs.bit_length() - 1
        assert ngrp * dil == RES and cs * ngrp == blk and cs % SUBLANES == 0 and S == RES * blk
        has_prev = ngrp > 1
        nk = 2 * blk if has_prev else blk
        nat = lambda a, ngrp=ngrp, cs=cs, cs_bits=cs_bits: (a & (cs - 1)) * ngrp + (a >> cs_bits)
        qi = lax.broadcasted_iota(jnp.int32, (blk, nk), 0)
        kj = lax.broadcasted_iota(jnp.int32, (blk, nk), 1)
        if has_prev:
            steps = (nat(qi) + blk) - (nat(kj & (blk - 1)) + (kj & blk))
        else:
            steps = qi - kj
        valid = (steps >= 0) & (steps <= n_steps)
        dist = (steps * dil).astype(F32)
        biases = []
        for hh in range(2):
            slope = slopes_ref[2 * hp + hh]
            biases.append(jnp.where(valid, -slope * dist, NEG_INF))
        prev_cols = kj < blk

        ones_k = jnp.ones((ATTN_BATCH, nk, LANES), BF16)

        def blocks(it, carry, dil=dil, ngrp=ngrp, cs=cs, has_prev=has_prev, biases=biases, prev_cols=prev_cols,
                   pi=pi, nk=nk, ones_k=ones_k):
            def chunk(idx, c, back):
                r = idx // ngrp
                j = jnp.maximum(idx % ngrp - back, 0)
                return pl.ds(pl.multiple_of((r + dil * c) * blk + cs * j, cs), cs)

            def load(ref, idx, back=0):
                return jnp.concatenate([ref[0, chunk(idx, c, back), :] for c in range(ngrp)], axis=0)

            qs, ks, vs, firsts = [], [], [], []
            for b in range(ATTN_BATCH):
                idx = it * ATTN_BATCH + b
                qs.append(load(q_ref, idx))
                if has_prev:
                    ks.append(jnp.concatenate([load(k_ref, idx, 1), load(k_ref, idx)], axis=0))
                    vs.append(jnp.concatenate([load(v_ref, idx, 1), load(v_ref, idx)], axis=0))
                    firsts.append(jnp.where(prev_cols & (idx % ngrp == 0), NEG_INF, 0.0))
                else:
                    ks.append(load(k_ref, idx))
                    vs.append(load(v_ref, idx))
            q3 = jnp.stack(qs) * ATTN_SCALE
            k3 = jnp.stack(ks).astype(BF16)
            v3 = jnp.concatenate([jnp.stack(vs).astype(BF16), ones_k], axis=2)
            outs, lses = [], []
            for hh in range(2):
                qm = jnp.where(head0 if hh == 0 else ~head0, q3, 0.0).astype(BF16)
                s = jnp.einsum("bqd,bkd->bqk", qm, k3, preferred_element_type=F32) + biases[hh]
                if has_prev:
                    s = s + jnp.stack(firsts)
                m = jnp.max(s, axis=2, keepdims=True)
                p = jnp.exp(s - m).astype(BF16)
                o = jnp.einsum("bqk,bkd->bqd", p, v3, preferred_element_type=F32)
                den = o[:, :, LANES:]
                outs.append(o[:, :, :LANES] / den)
                lses.append(m + jnp.log(den))
            o_pair = jnp.where(head0, outs[0], outs[1])
            l_pair = jnp.where(head0, lses[0], lses[1])
            for b in range(ATTN_BATCH):
                idx = it * ATTN_BATCH + b
                for c in range(ngrp):
                    opat_ref[pi, chunk(idx, c, 0), :] = o_pair[b, c * cs:(c + 1) * cs]
                    lpat_ref[pi, chunk(idx, c, 0), :] = l_pair[b, c * cs:(c + 1) * cs]
            return carry

        lax.fori_loop(0, RES // ATTN_BATCH, blocks, 0)

    chunk = 256

    def mix(c, carry):
        rows = pl.ds(pl.multiple_of(c * chunk, chunk), chunk)
        l0, l1, l2 = lpat_ref[0, rows, :], lpat_ref[1, rows, :], lpat_ref[2, rows, :]
        mx = jnp.maximum(jnp.maximum(l0, l1), l2)
        w0, w1, w2 = jnp.exp(l0 - mx), jnp.exp(l1 - mx), jnp.exp(l2 - mx)
        num = w0 * opat_ref[0, rows, :] + w1 * opat_ref[1, rows, :] + w2 * opat_ref[2, rows, :]
        o_ref[0, rows, :] = (num / (w0 + w1 + w2)).astype(o_ref.dtype)
        return carry

    lax.fori_loop(0, S // chunk, mix, 0)


def _attn(q, k, v, slopes):
    B, S, d_a = q.shape
    assert S % (BAND_BLOCK * max(d for _, d in DILATED_PATTERNS)) == 0
    npair = d_a // LANES
    spec = pl.BlockSpec((1, S, LANES), lambda b, h, *_: (b, 0, h))
    grid_spec = pltpu.PrefetchScalarGridSpec(
        num_scalar_prefetch=0,
        grid=(B, npair),
        in_specs=[pl.BlockSpec(memory_space=pltpu.SMEM), spec, spec, spec],
        out_specs=spec,
        scratch_shapes=[pltpu.VMEM((len(DILATED_PATTERNS), S, LANES), F32),
                        pltpu.VMEM((len(DILATED_PATTERNS), S, LANES), F32)],
    )
    return pl.pallas_call(
        functools.partial(_attn_body, S),
        grid_spec=grid_spec,
        out_shape=jax.ShapeDtypeStruct((B, S, d_a), BF16),
        compiler_params=_cparams(("arbitrary", "arbitrary")),
        name="dilated_attn",
    )(slopes, q, k, v)


def _layer_norm(z, g, b):
    mu = jnp.mean(z, axis=-1, keepdims=True)
    zc = z - mu
    var = jnp.mean(zc * zc, axis=-1, keepdims=True)
    return zc * lax.rsqrt(var + LN_EPS) * g + b


def _route(logits):
    n = logits.shape[0]
    lane = lax.broadcasted_iota(jnp.int32, (n, LANES), 1)
    lane_f = lane.astype(F32)
    big = float(LANES)
    is_outer = lane < N_EXPERT_GROUPS
    l1 = jnp.where(is_outer, logits, NEG_INF)
    m1 = jnp.max(l1, axis=1, keepdims=True)
    g_sel = jnp.min(jnp.where(l1 == m1, lane_f, big), axis=1, keepdims=True)
    v1 = 1.0 / jnp.sum(jnp.exp(l1 - m1), axis=1, keepdims=True)
    lo = ROUTER_OFF + g_sel * EXPERTS_PER_GROUP
    in_group = (lane_f >= lo) & (lane_f < lo + EXPERTS_PER_GROUP)
    l2 = jnp.where(in_group, logits, NEG_INF)
    ma = jnp.max(l2, axis=1, keepdims=True)
    ia = jnp.min(jnp.where(l2 == ma, lane_f, big), axis=1, keepdims=True)
    l2b = jnp.where(lane_f == ia, NEG_INF, l2)
    mb = jnp.max(l2b, axis=1, keepdims=True)
    ib = jnp.min(jnp.where(l2b == mb, lane_f, big), axis=1, keepdims=True)
    eb = jnp.exp(mb - ma)
    wa = 1.0 / (1.0 + eb)
    wb = eb / (1.0 + eb)
    return ia - ROUTER_OFF, ib - ROUTER_OFF, v1 * wa, v1 * wb


def _rank_and_meta(e0, e1, g0, g1, carry):
    n = e0.shape[0]
    lane_f = lax.broadcasted_iota(jnp.int32, (n, LANES), 1).astype(F32)
    oh0 = (lane_f == e0).astype(F32)
    oh1 = (lane_f == e1).astype(F32)
    both = oh0 + oh1
    ti = lax.broadcasted_iota(jnp.int32, (n, n), 0)
    tj = lax.broadcasted_iota(jnp.int32, (n, n), 1)
    tri = (tj < ti).astype(BF16)
    before = _dot(tri, both.astype(BF16)) + carry
    r0 = jnp.sum(before * oh0, axis=1, keepdims=True)
    r1 = jnp.sum(before * oh1, axis=1, keepdims=True)
    new_carry = carry + jnp.sum(both, axis=0, keepdims=True)
    lane = lax.broadcasted_iota(jnp.int32, (n, LANES), 1)
    meta = jnp.zeros((n, LANES), F32)
    for i, val in enumerate((e0, e1, g0, g1, r0, r1)):
        meta = jnp.where(lane == i, val, meta)
    return meta, meta.T[0:SUBLANES], new_carry


def _lane_group(shape, width):
    lane = lax.broadcasted_iota(jnp.int32, shape, len(shape) - 1)
    grp = jnp.zeros(shape, jnp.int32)
    for g in range(1, shape[-1] // width):
        grp = grp + (lane >= g * width).astype(jnp.int32)
    return grp


def _select_by_group(grp, vals):
    out = vals[-1]
    for g in range(len(vals) - 2, -1, -1):
        out = jnp.where(grp == g, vals[g], out)
    return out


def _store_token_rows(ref, val):
    n, d = val.shape
    nchunk = d // LANES
    for c in range(nchunk):
        ref[pl.ds(c, n, stride=nchunk), :] = val[:, c * LANES:(c + 1) * LANES]


def _load_token_rows(ref, n, nchunk):
    return jnp.concatenate([ref[pl.ds(c, n, stride=nchunk), :] for c in range(nchunk)], axis=1)


HALO = 16


def _mix_body(tm, nt, n_tiles, d_b, alpha, x_ref, at_ref, unperm_ref, u_ref, uh_ref, qc_ref, kmT_ref, vm_ref,
              wo_ref, wp_ref, ps_ref, g1_ref, b1_ref, wr_ref, br_ref, cin_ref, hs_ref,
              h1_ref, meta_ref, metaT_ref, cnt_ref, carry_ref):
    g = pl.program_id(0)

    @pl.when(g == 0)
    def _():
        carry_ref[...] = cin_ref[...]

    @pl.when(g == n_tiles)
    def _():
        ns_rows = hs_ref.shape[0]
        h1_ref[0:ns_rows, :] = hs_ref[...]
        h1_ref[ns_rows:, :] = jnp.zeros((h1_ref.shape[0] - ns_rows, LANES), F32)

    @pl.when(g < n_tiles)
    def _():
        _mix_tile(tm, g % nt, d_b, alpha, x_ref, at_ref, unperm_ref, u_ref, uh_ref, qc_ref, kmT_ref, vm_ref, wo_ref,
                  wp_ref, ps_ref, g1_ref, b1_ref, wr_ref, br_ref, h1_ref, meta_ref, metaT_ref, cnt_ref, carry_ref)


def _mix_tile(tm, i, d_b, alpha, x_ref, at_ref, unperm_ref, u_ref, uh_ref, qc_ref, kmT_ref, vm_ref, wo_ref, wp_ref,
              ps_ref, g1_ref, b1_ref, wr_ref, br_ref, h1_ref, meta_ref, metaT_ref, cnt_ref, carry_ref):
    u = u_ref[0]
    halo = jnp.where(i == 0, 0.0, uh_ref[0])
    ext = jnp.concatenate([halo, halo, u], axis=0)
    s2 = ext + pltpu.roll(ext, 1, 0)
    s4 = s2 + pltpu.roll(s2, 2, 0)
    s8 = s4 + pltpu.roll(s4, 4, 0)
    s16 = s8 + pltpu.roll(s8, 8, 0)
    grp = _lane_group((tm, d_b), d_b // len(POOL_WINDOWS))
    sums = [s[2 * HALO:] for s in (s2, s4, s8, s16)]
    win = _select_by_group(grp, sums)
    wlen = _select_by_group(grp, [jnp.int32(w) for w in POOL_WINDOWS])
    pos = i * tm + lax.broadcasted_iota(jnp.int32, (tm, d_b), 0)
    cnt = jnp.minimum(wlen, pos + 1).astype(F32)
    diff = win / cnt - u
    pool = _dot(diff.astype(BF16), wp_ref[...]) * ps_ref[...]

    qc = qc_ref[0] * ATTN_SCALE
    d_c = qc.shape[1]
    kmT = kmT_ref[0].astype(BF16)
    vm = vm_ref[0].astype(BF16)
    hl = _lane_group((tm, d_c), HEAD_DIM)
    memo = jnp.zeros((tm, d_c), F32)
    for h in range(d_c // HEAD_DIM):
        qm = jnp.where(hl == h, qc, 0.0).astype(BF16)
        s = _dot(qm, kmT)
        p = jnp.exp(s - jnp.max(s, axis=1, keepdims=True))
        den = jnp.sum(p, axis=1, keepdims=True)
        o = _dot(p.astype(BF16), vm) / den
        memo = jnp.where(hl == h, o, memo)

    at_rl = jnp.concatenate([at_ref[0, r] for r in range(RES)], axis=0)
    attn = _dot(unperm_ref[...], at_rl).astype(BF16)
    cat = jnp.concatenate([attn, pool.astype(BF16), memo.astype(BF16)], axis=1)
    mixv = _dot(cat, wo_ref[...])
    h1 = _layer_norm(alpha * x_ref[0] + mixv, g1_ref[...], b1_ref[...])
    _store_token_rows(h1_ref, h1)

    logits = _dot(h1.astype(BF16), wr_ref[...]) + br_ref[...]
    e0, e1, g0, g1 = _route(logits)
    meta, metaT, new_carry = _rank_and_meta(e0, e1, g0, g1, carry_ref[0:1, :])
    meta_ref[...] = meta
    metaT_ref[...] = metaT
    carry_ref[...] = jnp.broadcast_to(new_carry, carry_ref.shape)
    cnt_ref[...] = carry_ref[...]


def _mix(x, attn_rm, u, qc, kmT, vm, wo_bf, wp_bd_bf, pool_scale, g1, b1, wr, br, counts_in, h1s, alpha, tm):
    B, S, D = x.shape
    d_a, d_b, d_c = attn_rm.shape[2], u.shape[2], qc.shape[2]
    n_mem = vm.shape[1]
    nt = S // tm
    n_tiles = B * nt
    nchunk = D // LANES
    assert tm % (RES * SUBLANES * 2) == 0 and h1s.shape[0] < tm * nchunk
    last = n_tiles - 1
    bi = lambda g: (jnp.minimum(g, last) // nt, jnp.minimum(g, last) % nt)
    row = lambda n: pl.BlockSpec((1, tm, n), lambda g: bi(g) + (0,))
    full = lambda shape: pl.BlockSpec(shape, lambda g: (0,) * len(shape))
    per_b = lambda shape: pl.BlockSpec((1,) + shape, lambda g: (bi(g)[0], 0, 0))
    halo_spec = pl.BlockSpec((1, HALO, d_b), lambda g: (bi(g)[0], jnp.maximum(bi(g)[1] * (tm // HALO) - 1, 0), 0))
    at_spec = pl.BlockSpec((1, RES, tm // RES, d_a), lambda g: (bi(g)[0], 0, bi(g)[1], 0))
    dst = jnp.arange(tm)
    unperm = ((dst % RES) * (tm // RES) + dst // RES)[:, None] == jnp.arange(tm)[None, :]
    return pl.pallas_call(
        functools.partial(_mix_body, tm, nt, n_tiles, d_b, alpha),
        grid=(n_tiles + 1,),
        in_specs=[row(D), at_spec, full((tm, tm)), row(d_b), halo_spec, row(d_c), per_b((d_c, n_mem)),
                  per_b((n_mem, d_c)), full((D, D)), full((d_b, d_b)), full((1, d_b)), full((1, D)), full((1, D)),
                  full((D, LANES)), full((1, LANES)), full((SUBLANES, LANES)), full(h1s.shape)],
        out_specs=[pl.BlockSpec((tm * nchunk, LANES), lambda g: (g, 0)),
                   pl.BlockSpec((tm, LANES), lambda g: (jnp.minimum(g, last), 0)),
                   pl.BlockSpec((SUBLANES, tm), lambda g: (0, jnp.minimum(g, last))),
                   pl.BlockSpec((SUBLANES, LANES), lambda g: (0, 0))],
        out_shape=[jax.ShapeDtypeStruct(((n_tiles + 1) * tm * nchunk, LANES), F32),
                   jax.ShapeDtypeStruct((B * S, LANES), F32),
                   jax.ShapeDtypeStruct((SUBLANES, B * S), F32),
                   jax.ShapeDtypeStruct((SUBLANES, LANES), F32)],
        scratch_shapes=[pltpu.VMEM((SUBLANES, LANES), F32)],
        compiler_params=_cparams(("arbitrary",)),
        name="mix_ln1_router",
    )(x, attn_rm.reshape(B, RES, S // RES, d_a), unperm.astype(BF16), u, u, qc, kmT, vm, wo_bf, wp_bd_bf,
      pool_scale, g1, b1, wr, br, counts_in, h1s)


def _columns_to_lanes(cols, rows):
    lane = lax.broadcasted_iota(jnp.int32, (rows, LANES), 1)
    tile = jnp.zeros((rows, LANES), F32)
    for t, c in enumerate(cols):
        tile = jnp.where(lane == t, c, tile)
    return tile


def _sproj_body(x_ref, wT_ref, o_ref):
    o_ref[...] = _dot_nt(wT_ref[...], x_ref[...].astype(BF16))


def _sproj(x, wT_bf):
    vm = pl.BlockSpec(memory_space=pltpu.VMEM)
    return pl.pallas_call(
        _sproj_body, in_specs=[vm, vm], out_specs=vm,
        out_shape=jax.ShapeDtypeStruct((wT_bf.shape[0], x.shape[0]), F32),
        compiler_params=pltpu.CompilerParams(vmem_limit_bytes=VMEM_LIMIT),
        name="decode_proj",
    )(x, wT_bf)


def _decode_body(T, wb, qc_row0, slopes_ref, pT_ref, ck_ref, cv_ref, mk_ref, mv_ref,
                 nk_ref, nv_ref, at_ref, mo_ref, qkv_ref, sfar_ref, snear_ref, snew_ref):
    H = ck_ref.shape[1]
    d_a = H * HEAD_DIM
    ntok = pT_ref.shape[1]
    shift = (ntok - T * pl.program_id(0)) % ntok
    qkv_ref[...] = pltpu.roll(pT_ref[...], shift, 1)

    def member(dist, window, dil):
        assert dil & (dil - 1) == 0
        return ((dist & (dil - 1)) == 0) & (dist <= window) & (dist >= 0)

    OWN = min(d for _, d in DILATED_PATTERNS if d > 1)
    far = wb - LANES
    assert T <= OWN and far % OWN == 0 and wb % OWN == 0
    assert all(w <= LANES for w, d in DILATED_PATTERNS if d == 1) and all(d % OWN == 0 for _, d in DILATED_PATTERNS if d > 1)
    R = H * T
    assert T & (T - 1) == 0
    query = lambda shape: lax.broadcasted_iota(jnp.int32, shape, 0) & (T - 1)
    lane_far = lax.broadcasted_iota(jnp.int32, (R, far), 1)
    t_far = query((R, far))
    owned = (lane_far & (OWN - 1)) == t_far
    dist_far = wb + t_far - lane_far
    dist_near = wb + query((R, LANES)) - (far + lax.broadcasted_iota(jnp.int32, (R, LANES), 1))
    dist_new = query((R, T)) - lax.broadcasted_iota(jnp.int32, (R, T), 1)
    masks = [(None if dil == 1 else owned & member(dist_far, window, dil),
              member(dist_near, window, dil), member(dist_new, window, dil)) for window, dil in DILATED_PATTERNS]
    dist_far = dist_far.astype(F32)
    dist_near = dist_near.astype(F32)
    dist_new = jnp.maximum(dist_new, 0).astype(F32)
    rnd = lambda a: a.astype(BF16).astype(F32)

    lane_t = lax.broadcasted_iota(jnp.int32, (HEAD_DIM, LANES), 1)
    own_t = lane_t & (OWN - 1)
    last = wb - LANES
    n_far = far // LANES

    tile = lambda a, c: a[:, c * LANES:(c + 1) * LANES]

    G = SUBLANES // T
    assert G * T == SUBLANES and H % G == 0
    group_rows = lambda g: pl.ds(pl.multiple_of(g * SUBLANES, SUBLANES), SUBLANES)

    def scores(g, carry):
        parts = [head_scores(g * G + i) for i in range(G)]
        sfar_ref[group_rows(g), :] = jnp.concatenate([p[0] for p in parts], axis=0)
        snear_ref[group_rows(g), :] = jnp.concatenate([p[1] for p in parts], axis=0)
        snew_ref[group_rows(g), :] = jnp.concatenate([p[2] for p in parts], axis=0)
        return carry

    def head_scores(h):
        slope = slopes_ref[h]
        kTc = ck_ref[0, h]
        vTc = cv_ref[0, h]
        r0 = pl.multiple_of(h * HEAD_DIM, HEAD_DIM)
        qT = rnd(qkv_ref[pl.ds(r0, HEAD_DIM), :][:, 0:T] * ATTN_SCALE)
        kTn = qkv_ref[pl.ds(d_a + r0, HEAD_DIM), :][:, 0:T]
        vTn = qkv_ref[pl.ds(2 * d_a + r0, HEAD_DIM), :][:, 0:T]
        kTn_r = rnd(kTn)

        qb = [jnp.broadcast_to(qT[:, t:t + 1], (HEAD_DIM, LANES)) for t in range(T)]
        qpat = qb[T - 1]
        for t in range(T - 1):
            qpat = jnp.where(own_t == t, qb[t], qpat)
        s_far = jnp.concatenate([jnp.sum(rnd(tile(kTc, c)) * qpat, axis=0, keepdims=True) for c in range(n_far)],
                                axis=1)
        s_far = s_far - slope * dist_far[0:T]
        k_near = rnd(tile(kTc, n_far))
        s_near = jnp.concatenate([jnp.sum(k_near * qb[t], axis=0, keepdims=True) for t in range(T)], axis=0)
        s_near = s_near - slope * dist_near[0:T]
        s_new = jnp.concatenate([jnp.sum(kTn_r * qT[:, t:t + 1], axis=0, keepdims=True) for t in range(T)], axis=0)
        s_new = s_new - slope * dist_new[0:T]

        rk = pltpu.roll(kTc, wb - T, 1)
        rv = pltpu.roll(vTc, wb - T, 1)
        nk_ref[0, h] = rk
        nv_ref[0, h] = rv
        newk = jnp.zeros((HEAD_DIM, LANES), F32)
        newv = jnp.zeros((HEAD_DIM, LANES), F32)
        for t in range(T):
            newk = jnp.where(lane_t == LANES - T + t, kTn[:, t:t + 1], newk)
            newv = jnp.where(lane_t == LANES - T + t, vTn[:, t:t + 1], newv)
        nk_ref[0, h, :, last:] = jnp.where(lane_t >= LANES - T, newk, rk[:, last:])
        nv_ref[0, h, :, last:] = jnp.where(lane_t >= LANES - T, newv, rv[:, last:])
        return s_far, s_near, s_new

    lax.fori_loop(0, H // G, scores, 0)

    s_far, s_near, s_new = sfar_ref[...], snear_ref[...], snew_ref[...]
    rmax = lambda a: jnp.max(a, axis=1, keepdims=True)
    rsum = lambda a: jnp.sum(a, axis=1, keepdims=True)
    parts, lses = [], []
    for m_far, m_near, m_new in masks:
        sn, sw = jnp.where(m_near, s_near, NEG_INF), jnp.where(m_new, s_new, NEG_INF)
        m = jnp.maximum(rmax(sn), rmax(sw))
        if m_far is not None:
            sf = jnp.where(m_far, s_far, NEG_INF)
            m = jnp.maximum(m, rmax(sf))
        en, ew = jnp.exp(sn - m), jnp.exp(sw - m)
        den = rsum(en) + rsum(ew)
        ef = None
        if m_far is not None:
            ef = jnp.exp(sf - m)
            den = den + rsum(ef)
            ef = rnd(ef)
        parts.append((ef, rnd(en), rnd(ew), den))
        lses.append(m + jnp.log(den))
    top = functools.reduce(jnp.maximum, lses)
    mixw = [jnp.exp(l - top) for l in lses]
    total = functools.reduce(lambda a, b: a + b, mixw)
    p_far = p_near = p_new = None
    for (ef, en, ew, den), w in zip(parts, mixw):
        scale = w / (total * den)
        p_near = en * scale if p_near is None else p_near + en * scale
        p_new = ew * scale if p_new is None else p_new + ew * scale
        if ef is not None:
            p_far = ef * scale if p_far is None else p_far + ef * scale
    sfar_ref[...] = p_far
    snear_ref[...] = p_near
    snew_ref[...] = p_new

    def values(g, carry):
        pf, pn, pw = sfar_ref[group_rows(g), :], snear_ref[group_rows(g), :], snew_ref[group_rows(g), :]
        for i in range(G):
            head_values(g * G + i, pf[i * T:(i + 1) * T], pn[i * T:(i + 1) * T], pw[i * T:(i + 1) * T])
        return carry

    def head_values(h, p_far_h, p_near_h, p_new_h):
        vTc = cv_ref[0, h]
        r0 = pl.multiple_of(h * HEAD_DIM, HEAD_DIM)
        vTn_r = rnd(qkv_ref[pl.ds(2 * d_a + r0, HEAD_DIM), :][:, 0:T])
        p_comb = jnp.sum(p_far_h, axis=0, keepdims=True)
        acc = rnd(tile(vTc, 0)) * tile(p_comb, 0)
        for c in range(1, n_far):
            acc = acc + rnd(tile(vTc, c)) * tile(p_comb, c)
        v_near = rnd(tile(vTc, n_far))
        cols = []
        for t in range(T):
            cols.append(jnp.sum(jnp.where(own_t == t, acc, 0.0) + v_near * p_near_h[t:t + 1], axis=1, keepdims=True)
                        + jnp.sum(vTn_r * p_new_h[t:t + 1], axis=1, keepdims=True))
        at_ref[0, pl.ds(r0, HEAD_DIM), :] = _columns_to_lanes(cols, HEAD_DIM)

    lax.fori_loop(0, H // G, values, 0)

    Hc = mk_ref.shape[1]
    rows = []
    for h in range(Hc):
        kT = rnd(mk_ref[0, h])
        r0 = h * HEAD_DIM
        qT = rnd(qkv_ref[qc_row0 + r0:qc_row0 + r0 + HEAD_DIM, :][:, 0:T] * ATTN_SCALE)
        rows += [jnp.sum(kT * qT[:, t:t + 1], axis=0, keepdims=True) for t in range(T)]
    s = jnp.concatenate(rows, axis=0)
    p = jnp.exp(s - jnp.max(s, axis=1, keepdims=True))
    p = rnd(p / jnp.sum(p, axis=1, keepdims=True))
    for h in range(Hc):
        vT = rnd(mv_ref[0, h])
        cols = [jnp.sum(vT * p[h * T + t:h * T + t + 1], axis=1, keepdims=True) for t in range(T)]
        mo_ref[0, h * HEAD_DIM:(h + 1) * HEAD_DIM, :] = _columns_to_lanes(cols, HEAD_DIM)


def _decode(projT, T, qc_row0, ck, cv, mk, mv, slopes):
    DB, H, hd, wb = ck.shape
    _, Hc, _, n_mem = mk.shape
    d_a, d_c = H * hd, Hc * hd
    assert wb >= max(w for w, _ in DILATED_PATTERNS) and wb % LANES == 0 and T <= LANES
    assert projT.shape[1] == DB * T and projT.shape[0] >= qc_row0 + d_c
    cache = pl.BlockSpec((1, H, hd, wb), lambda b, *_: (b, 0, 0, 0))
    memc = pl.BlockSpec((1, Hc, hd, n_mem), lambda b, *_: (b, 0, 0, 0))
    grid_spec = pltpu.PrefetchScalarGridSpec(
        num_scalar_prefetch=0,
        grid=(DB,),
        in_specs=[pl.BlockSpec(memory_space=pltpu.SMEM),
                  pl.BlockSpec(projT.shape, lambda b, *_: (0, 0)),
                  cache, cache, memc, memc],
        out_specs=[cache, cache,
                   pl.BlockSpec((1, d_a, LANES), lambda b, *_: (b, 0, 0)),
                   pl.BlockSpec((1, d_c, LANES), lambda b, *_: (b, 0, 0))],
        scratch_shapes=[pltpu.VMEM(projT.shape, F32), pltpu.VMEM((H * T, wb - LANES), F32),
                        pltpu.VMEM((H * T, LANES), F32), pltpu.VMEM((H * T, T), F32)],
    )
    nk, nv, at, mo = pl.pallas_call(
        functools.partial(_decode_body, T, wb, qc_row0),
        grid_spec=grid_spec,
        out_shape=[jax.ShapeDtypeStruct(ck.shape, F32), jax.ShapeDtypeStruct(cv.shape, F32),
                   jax.ShapeDtypeStruct((DB, d_a, LANES), F32), jax.ShapeDtypeStruct((DB, d_c, LANES), F32)],
        compiler_params=_cparams(("arbitrary",)),
        name="decode_attn_cache",
    )(slopes, projT, ck, cv, mk, mv)
    return nk, nv, at[:, :, :T], mo[:, :, :T]


def _smix_body(T, alpha, pos0, x_ref, at_ref, mo_ref, st_ref, wu_ref, wo_ref, wp_ref, ps_ref, g1_ref, b1_ref,
               wr_ref, br_ref, cin_ref, h1_ref, meta_ref, metaT_ref, cnt_ref, pool_ref):
    n = x_ref.shape[0]
    db = n // T
    pb = st_ref.shape[0]
    d_b = st_ref.shape[2]
    x = x_ref[...]
    bdot = lambda a, b: _dot(a.astype(BF16), b.astype(BF16))
    u_new = bdot(x, wu_ref[...])
    seq = [st_ref[j] for j in range(pb)] + [u_new[t * db:(t + 1) * db] for t in range(T)]
    for j in range(pb):
        pool_ref[j] = seq[j + T]
    grp = _lane_group((db, d_b), d_b // len(POOL_WINDOWS))
    diffs = []
    for t in range(T):
        j = pb + t
        per_w = []
        for w in POOL_WINDOWS:
            acc = seq[j]
            for back in range(1, w):
                if j - back >= 0:
                    acc = acc + seq[j - back]
            per_w.append(acc / float(min(w, pos0 + j + 1)))
        diffs.append(_select_by_group(grp, per_w) - seq[j])
    diff = jnp.concatenate(diffs, axis=0)
    pool = bdot(diff, wp_ref[...]) * ps_ref[...]
    cat = jnp.concatenate([at_ref[...], pool, mo_ref[...]], axis=1)
    mixv = bdot(cat, wo_ref[...])
    h1 = _layer_norm(alpha * x + mixv, g1_ref[...], b1_ref[...])
    _store_token_rows(h1_ref, h1)
    logits = bdot(h1, wr_ref[...]) + br_ref[...]
    e0, e1, g0, g1 = _route(logits)
    meta, metaT, new_carry = _rank_and_meta(e0, e1, g0, g1, cin_ref[0:1, :])
    meta_ref[...] = meta
    metaT_ref[...] = metaT
    cnt_ref[...] = jnp.broadcast_to(new_carry, cnt_ref.shape)


def _smix(x_tb, attn_tb, memo_tb, state, wu, wo, wp_bd, pool_scale, g1, b1, wr, br, counts_in, alpha, T):
    n, D = x_tb.shape
    pb, db, d_b = state.shape
    nchunk = D // LANES
    vm = pl.BlockSpec(memory_space=pltpu.VMEM)
    return pl.pallas_call(
        functools.partial(_smix_body, T, alpha, PAST_LEN - pb),
        in_specs=[vm] * 13,
        out_specs=[vm] * 5,
        out_shape=[jax.ShapeDtypeStruct((n * nchunk, LANES), F32), jax.ShapeDtypeStruct((n, LANES), F32),
                   jax.ShapeDtypeStruct((SUBLANES, n), F32),
                   jax.ShapeDtypeStruct((SUBLANES, LANES), F32), jax.ShapeDtypeStruct((pb, db, d_b), F32)],
        compiler_params=pltpu.CompilerParams(vmem_limit_bytes=VMEM_LIMIT),
        name="decode_mix_ln1_router",
    )(x_tb, attn_tb, memo_tb, state, wu, wo, wp_bd, pool_scale, g1, b1, wr, br, counts_in)


def _slot_map_body(blk, n, pos_ref, fill_ref, inv_ref, sem):
    g = pl.program_id(0)

    @pl.when(g == 0)
    def _():
        c = pltpu.make_async_copy(fill_ref, inv_ref, sem)
        c.start()
        c.wait()

    group = math.gcd(math.gcd(blk, n), 32)

    def body(j, carry):
        pair0 = g * blk + j * group
        token0 = jnp.where(pair0 >= n, pair0 - n, pair0)
        for u in range(group):
            inv_ref[pos_ref[0, 0, j * group + u]] = token0 + u
        return carry

    lax.fori_loop(0, blk // group, body, 0)


def _slot_map(pos, n_slots, zero_token, blk):
    n2 = pos.shape[0]
    assert n2 % blk == 0 and n_slots % 1024 == 0
    return pl.pallas_call(
        functools.partial(_slot_map_body, blk, n2 // 2),
        grid=(n2 // blk,),
        in_specs=[pl.BlockSpec((1, 1, blk), lambda g: (g, 0, 0), memory_space=pltpu.SMEM),
                  pl.BlockSpec(memory_space=pl.ANY)],
        out_specs=pl.BlockSpec(memory_space=pltpu.SMEM),
        out_shape=jax.ShapeDtypeStruct((n_slots,), jnp.int32),
        scratch_shapes=[pltpu.SemaphoreType.DMA(())],
        compiler_params=_cparams(("arbitrary",)),
        name="slot_map",
    )(pos.reshape(n2 // blk, 1, blk), jnp.full((n_slots,), zero_token, jnp.int32))


def _expert_body(tmx, nchunk, te_ref, nt_ref, tv_ref, cur_ref, nxt_ref, h_ref, wg_ref, wu_ref, wd_ref, ys_ref,
                 buf_a, buf_b, wg_bf, wu_bf, wd_bf, sems):
    i = pl.program_id(0)
    n_tiles = nt_ref[0]

    def gather(idx_ref, buf, s, tile, wait):
        groups = (tv_ref[tile] + DMA_UNROLL - 1) // DMA_UNROLL

        def body(g, carry):
            for u in range(DMA_UNROLL):
                j = g * DMA_UNROLL + u
                dst = buf.at[pl.ds(pl.multiple_of(j * nchunk, nchunk), nchunk)]
                copy = pltpu.make_async_copy(h_ref.at[idx_ref[0, 0, j]], dst, sems.at[s])
                if wait:
                    copy.wait()
                else:
                    copy.start(priority=u % 2)
            return carry

        lax.fori_loop(0, groups, body, 0)

    @pl.when(i == 0)
    def _():
        buf_a[...] = jnp.zeros_like(buf_a)
        buf_b[...] = jnp.zeros_like(buf_b)
        gather(cur_ref, buf_a, 0, 0, False)

    it = jnp.minimum(i, pl.num_programs(0) - 2)
    @pl.when((i < n_tiles) & ((i == 0) | (te_ref[it] != te_ref[jnp.maximum(it - 1, 0)])))
    def _():
        wg_bf[...] = wg_ref[0].astype(BF16)
        wu_bf[...] = wu_ref[0].astype(BF16)
        wd_bf[...] = wd_ref[0].astype(BF16)

    def step(cur, s_cur, nxt, s_nxt):
        gather(cur_ref, cur, s_cur, it, True)

        @pl.when(i + 1 < n_tiles)
        def _():
            gather(nxt_ref, nxt, s_nxt, it + 1, False)

        x = _load_token_rows(cur, tmx, nchunk).astype(BF16)
        hg = _dot(x, wg_bf[...])
        hu = _dot(x, wu_bf[...])
        a = (hg * jax.nn.sigmoid(hg) * hu).astype(BF16)
        y = _dot(a, wd_bf[...])
        _store_token_rows(ys_ref, y)

    @pl.when((i < n_tiles) & (i % 2 == 0))
    def _():
        step(buf_a, 0, buf_b, 1)

    @pl.when((i < n_tiles) & (i % 2 == 1))
    def _():
        step(buf_b, 1, buf_a, 0)

    @pl.when(i >= n_tiles)
    def _():
        ys_ref[...] = jnp.zeros_like(ys_ref)


def _experts(tile_expert, n_tiles, tile_valid, inv, h_rows, wg, wu, wd, tmx, max_tiles):
    E, D, F = wg.shape
    nchunk = D // LANES
    inv3 = inv.reshape(-1, 1, tmx)
    last_blk = inv3.shape[0] - 1
    assert last_blk >= max_tiles and tmx % DMA_UNROLL == 0
    weights = lambda shape: pl.BlockSpec((1,) + shape, lambda i, te, *_: (te[jnp.minimum(i, max_tiles - 1)], 0, 0))
    rows = pltpu.VMEM((tmx * nchunk, LANES), F32)
    grid_spec = pltpu.PrefetchScalarGridSpec(
        num_scalar_prefetch=3,
        grid=(max_tiles + 1,),
        in_specs=[pl.BlockSpec((1, 1, tmx), lambda i, *_: (i, 0, 0), memory_space=pltpu.SMEM),
                  pl.BlockSpec((1, 1, tmx), lambda i, *_: (jnp.minimum(i + 1, last_blk), 0, 0),
                               memory_space=pltpu.SMEM),
                  pl.BlockSpec(memory_space=pl.ANY), weights((D, F)), weights((D, F)), weights((F, D))],
        out_specs=pl.BlockSpec((tmx * nchunk, LANES), lambda i, *_: (i, 0)),
        scratch_shapes=[rows, rows, pltpu.VMEM((D, F), BF16), pltpu.VMEM((D, F), BF16), pltpu.VMEM((F, D), BF16),
                        pltpu.SemaphoreType.DMA((2,))],
    )
    return pl.pallas_call(
        functools.partial(_expert_body, tmx, nchunk),
        grid_spec=grid_spec,
        out_shape=jax.ShapeDtypeStruct(((max_tiles + 1) * tmx * nchunk, LANES), F32),
        compiler_params=_cparams(("arbitrary",)),
        name="expert_swiglu",
    )(tile_expert, n_tiles, tile_valid, inv3, inv3, h_rows, wg, wu, wd)


def _combine_body(tm, nchunk, alpha, pos_ref, nxt_ref, h1_ref, meta_ref, ys_ref, g2_ref, b2_ref, o_ref, buf_a, buf_b,
                  sems):
    i = pl.program_id(0)
    last = pl.num_programs(0) - 1

    def copy(idx_ref, buf, s, t, k):
        dst = buf.at[k, pl.ds(pl.multiple_of(t * nchunk, nchunk), nchunk)]
        return pltpu.make_async_copy(ys_ref.at[idx_ref[0, 0, k * tm + t]], dst, sems.at[s])

    def start(idx_ref, buf, s):
        def body(t, carry):
            for k in range(2):
                copy(idx_ref, buf, s, t, k).start(priority=k)
            return carry

        lax.fori_loop(0, tm, body, 0, unroll=DMA_UNROLL)

    def wait(buf, s):
        def body(t, carry):
            for k in range(2):
                copy(pos_ref, buf, s, t, k).wait()
            return carry

        lax.fori_loop(0, tm, body, 0, unroll=DMA_UNROLL)

    @pl.when(i == 0)
    def _():
        start(pos_ref, buf_a, 0)

    def step(cur, s_cur, nxt, s_nxt):
        wait(cur, s_cur)
        start(nxt_ref, nxt, s_nxt)
        h1 = _load_token_rows(h1_ref, tm, nchunk)
        y0 = _load_token_rows(cur.at[0], tm, nchunk)
        y1 = _load_token_rows(cur.at[1], tm, nchunk)
        meta = meta_ref[...]
        lane = lax.broadcasted_iota(jnp.int32, meta.shape, 1)
        gate0 = jnp.sum(jnp.where(lane == 2, meta, 0.0), axis=1, keepdims=True)
        gate1 = jnp.sum(jnp.where(lane == 3, meta, 0.0), axis=1, keepdims=True)
        f = gate0 * y0 + gate1 * y1
        o_ref[...] = _layer_norm(alpha * h1 + f, g2_ref[...], b2_ref[...])

        @pl.when(i == last)
        def _():
            wait(nxt, s_nxt)

    @pl.when(i % 2 == 0)
    def _():
        step(buf_a, 0, buf_b, 1)

    @pl.when(i % 2 == 1)
    def _():
        step(buf_b, 1, buf_a, 0)


def _combine(pos2, tok0, n, h1_flat, meta, ys, g2, b2, alpha, tm):
    D = g2.shape[1]
    nchunk = D // LANES
    assert n % tm == 0 and tok0 % tm == 0
    pos3 = jnp.transpose(pos2.reshape(2, -1, tm), (1, 0, 2)).reshape(-1, 1, 2 * tm)
    off = tok0 // tm
    last = off + n // tm - 1
    return pl.pallas_call(
        functools.partial(_combine_body, tm, nchunk, alpha),
        grid=(n // tm,),
        in_specs=[pl.BlockSpec((1, 1, 2 * tm), lambda i: (i + off, 0, 0), memory_space=pltpu.SMEM),
                  pl.BlockSpec((1, 1, 2 * tm), lambda i: (jnp.minimum(i + off + 1, last), 0, 0),
                               memory_space=pltpu.SMEM),
                  pl.BlockSpec((tm * nchunk, LANES), lambda i: (i + off, 0)),
                  pl.BlockSpec((tm, LANES), lambda i: (i, 0)),
                  pl.BlockSpec(memory_space=pl.ANY),
                  pl.BlockSpec((1, D), lambda i: (0, 0)), pl.BlockSpec((1, D), lambda i: (0, 0))],
        out_specs=pl.BlockSpec((tm, D), lambda i: (i, 0)),
        out_shape=jax.ShapeDtypeStruct((n, D), F32),
        scratch_shapes=[pltpu.VMEM((2, tm * nchunk, LANES), F32), pltpu.VMEM((2, tm * nchunk, LANES), F32),
                        pltpu.SemaphoreType.DMA((2,))],
        compiler_params=_cparams(("arbitrary",)),
        name="combine_ln2",
    )(pos3, pos3, h1_flat, meta, ys, g2, b2)


def _block_diag(w):
    g, a, b = w.shape
    eye = jnp.eye(g, dtype=w.dtype)
    return (eye[:, None, :, None] * w[:, :, None, :]).reshape(g * a, g * b)


def _layer(h_p, h_s, win_k, win_v, pool_st, mem_k, mem_v, mem_prompt,
           w_in, w_mem_kv, w_pool, pool_scale, w_o, ln1_g, ln1_b, ln2_g, ln2_b,
           w_r1, b_r1, w_r2, b_r2, w_gate, w_up, w_down, alpha):
    B, S, D = h_p.shape
    DB, T, _ = h_s.shape
    H = win_k.shape[2]
    Hc = mem_k.shape[2]
    d_a, d_c = H * HEAD_DIM, Hc * HEAD_DIM
    d_b = pool_st.shape[2]
    nchunk = D // LANES
    slopes = 2.0 ** (-8.0 * jnp.arange(1, H + 1, dtype=F32) / H)

    w_in_bf = w_in.astype(BF16)
    w_inT_bf = w_in_bf.T
    wkvT_bf = w_inT_bf[d_a:3 * d_a]
    wp_bd = _block_diag(w_pool)
    ps = pool_scale.reshape(1, d_b)
    g1, b1 = ln1_g.reshape(1, D), ln1_b.reshape(1, D)
    g2, b2 = ln2_g.reshape(1, D), ln2_b.reshape(1, D)
    n_r = N_EXPERT_GROUPS + N_EXPERTS
    wr = jnp.concatenate([w_r1, jnp.transpose(w_r2, (1, 0, 2)).reshape(D, N_EXPERTS),
                          jnp.zeros((D, LANES - n_r), F32)], axis=1)
    br = jnp.concatenate([b_r1, b_r2.reshape(-1), jnp.zeros((LANES - n_r,), F32)]).reshape(1, LANES)

    ck = jnp.transpose(win_k, (0, 2, 3, 1))
    cv = jnp.transpose(win_v, (0, 2, 3, 1))
    mk = jnp.transpose(mem_k, (0, 2, 3, 1))
    mv = jnp.transpose(mem_v, (0, 2, 3, 1))
    projT = _sproj(h_s.reshape(DB * T, D), w_inT_bf)
    nk, nv, attn_sT, memo_sT = _decode(projT, T, 3 * d_a + d_b, ck, cv, mk, mv, slopes)
    to_tb = lambda a: jnp.transpose(a, (2, 0, 1)).reshape(T * DB, a.shape[1])
    x_tb = jnp.transpose(h_s, (1, 0, 2)).reshape(T * DB, D)
    state = jnp.transpose(pool_st, (1, 0, 2))
    h1s, meta_s, metaT_s, cnt_s, new_pool = _smix(x_tb, to_tb(attn_sT), to_tb(memo_sT), state,
                                         w_in[:, 3 * d_a:3 * d_a + d_b], w_o, wp_bd, ps, g1, b1, wr, br,
                                         jnp.zeros((SUBLANES, LANES), F32), alpha, T)

    w_memT = w_mem_kv.T
    kmT, vm, vmT = _memproj(mem_prompt, w_memT[:d_c].astype(BF16), w_mem_kv[:, d_c:].astype(BF16),
                            w_memT[d_c:].astype(BF16))
    q, k, v, u, qc, kT, vT = _proj(h_p, w_in_bf, wkvT_bf, d_a, d_b, d_c, ts=512)
    attn = _attn(q, k, v, slopes)
    h_rows, meta_p, metaT_p, cnt_all = _mix(h_p, attn, u, qc, kmT, vm, w_o.astype(BF16), wp_bd.astype(BF16), ps, g1, b1,
                                   wr.astype(BF16), br, cnt_s, h1s, alpha, tm=512)

    tmx = 256
    n_p, n_s = B * S, DB * T
    n = n_p + n_s
    counts = cnt_all[0, :N_EXPERTS].astype(jnp.int32)
    padded = (counts + tmx - 1) // tmx * tmx
    seg_end = jnp.cumsum(padded)
    seg_off = seg_end - padded
    metaT = jnp.concatenate([metaT_p, metaT_s], axis=1)
    e_ids = metaT[0:2].astype(jnp.int32)
    experts = jnp.arange(N_EXPERTS).reshape((N_EXPERTS,) + (1,) * 2)
    lookup = lambda table, idx: jnp.sum(
        jnp.where(idx.reshape((1,) * (3 - idx.ndim) + idx.shape) == experts, table.reshape(experts.shape), 0),
        axis=0).reshape(idx.shape)
    pos2 = lookup(seg_off, e_ids) + metaT[4:6].astype(jnp.int32)
    max_tiles = (2 * n) // tmx + N_EXPERTS
    n_tiles = (seg_end[-1] // tmx).astype(jnp.int32).reshape(1)
    tile_row0 = jnp.arange(max_tiles, dtype=jnp.int32) * tmx
    tile_expert = jnp.sum((tile_row0[:, None] >= seg_end[None, :]).astype(jnp.int32), axis=1)
    tile_expert = jnp.minimum(tile_expert, N_EXPERTS - 1)

    tok_tile = math.gcd(math.gcd(n_p, n_s), 128)
    n_slots = -(-((max_tiles + 1) * tmx) // 1024) * 1024
    inv = _slot_map(pos2.reshape(-1), n_slots, n, blk=math.gcd(2 * n, 2048))
    tile_valid = jnp.clip(lookup(seg_off + counts, tile_expert) - tile_row0, 0, tmx).astype(jnp.int32)
    ys = _experts(tile_expert, n_tiles, tile_valid, inv, h_rows.reshape(-1, nchunk, LANES),
                  w_gate.reshape(N_EXPERTS, D, -1),
                  w_up.reshape(N_EXPERTS, D, -1), w_down.reshape(N_EXPERTS, -1, D), tmx, max_tiles)
    ys3 = ys.reshape(-1, nchunk, LANES)
    y_p = _combine(pos2, 0, n_p, h_rows, meta_p, ys3, g2, b2, alpha, tm=tok_tile)
    y_s = _combine(pos2, n_p, n_s, h_rows, meta_s, ys3, g2, b2, alpha, tm=tok_tile)

    y_p = y_p.reshape(B, S, D)
    y_s = jnp.transpose(y_s.reshape(T, DB, D), (1, 0, 2))
    heads = lambda a, h: jnp.transpose(a.reshape(a.shape[0], h, HEAD_DIM, a.shape[2]), (0, 3, 1, 2))
    wbp = min(max(w for w, _ in DILATED_PATTERNS), S)
    new_wk_p = heads(kT, H)[:, S - wbp:]
    new_wv_p = heads(vT, H)[:, S - wbp:]
    pb = pool_st.shape[1]
    new_pool_p = u[:, S - pb:]
    new_mk_p = heads(kmT, Hc)
    new_mv_p = heads(vmT, Hc)
    new_wk_s = jnp.transpose(nk, (0, 3, 1, 2))
    new_wv_s = jnp.transpose(nv, (0, 3, 1, 2))
    new_pool_s = jnp.transpose(new_pool, (1, 0, 2))
    return (y_p, y_s, new_wk_p, new_wv_p, new_pool_p, new_mk_p, new_mv_p, new_wk_s, new_wv_s, new_pool_s)


def kernel(x_prompt, x_sample, cache_win_k, cache_win_v, state_pool, cache_mem_k, cache_mem_v, mem_prompt, w_in, w_mem_kv, w_pool, pool_scale, w_o, ln1_g, ln1_b, ln2_g, ln2_b, w_r1, b_r1, w_r2, b_r2, w_gate, w_up, w_down):
    depth = w_in.shape[0]
    alpha = (2.0 * depth) ** 0.25
    h_p, h_s = x_prompt, x_sample
    outs = [[] for _ in range(8)]
    for l in range(depth):
        res = _layer(h_p, h_s, cache_win_k[l], cache_win_v[l], state_pool[l], cache_mem_k[l], cache_mem_v[l],
                     mem_prompt, w_in[l], w_mem_kv[l], w_pool[l], pool_scale[l], w_o[l], ln1_g[l], ln1_b[l],
                     ln2_g[l], ln2_b[l], w_r1[l], b_r1[l], w_r2[l], b_r2[l], w_gate[l], w_up[l], w_down[l], alpha)
        h_p, h_s = res[0], res[1]
        for lst, val in zip(outs, res[2:]):
            lst.append(val)
    return (h_p, h_s) + tuple(jnp.stack(o) for o in outs)
```

```python
import functools
import math

import jax
import jax.numpy as jnp
from jax import lax
from jax.experimental import pallas as pl
from jax.experimental.pallas import tpu as pltpu

F32 = jnp.float32
BF16 = jnp.bfloat16
NEG_INF = float("-inf")

HEAD_DIM = 64
DILATED_PATTERNS = ((128, 1), (512, 4), (2048, 16))
BAND_BLOCK = 128
RES = max(d for _, d in DILATED_PATTERNS)
POOL_WINDOWS = (2, 4, 8, 16)
N_EXPERT_GROUPS = 4
EXPERTS_PER_GROUP = 8
N_EXPERTS = N_EXPERT_GROUPS * EXPERTS_PER_GROUP
PAST_LEN = 16384
LN_EPS = 1e-5
ATTN_SCALE = HEAD_DIM ** -0.5

LANES = 128
SUBLANES = 8
VMEM_LIMIT = 56 * 1024 * 1024

ROUTER_OFF = N_EXPERT_GROUPS
ATTN_BATCH = 16
DMA_UNROLL = 8

PROJ_TILE = 512
MIX_TILE = 512
EXPERT_TILE = 256
ROW_DMA_TILE = 128


def _cparams(sem):
    return pltpu.CompilerParams(dimension_semantics=sem, vmem_limit_bytes=VMEM_LIMIT)


def _dot(a, b):
    return jnp.dot(a, b, preferred_element_type=F32)


def _dot_nt(a, b):
    return lax.dot_general(a, b, (((1,), (1,)), ((), ())), preferred_element_type=F32)


def _memproj_body(mem_ref, wkT_ref, wv_ref, wvT_ref, kmT_ref, vm_ref, vmT_ref):
    m = mem_ref[0].astype(BF16)
    kmT_ref[0] = _dot_nt(wkT_ref[...], m)
    vm_ref[0] = _dot(m, wv_ref[...])
    vmT_ref[0] = _dot_nt(wvT_ref[...], m)


def _memproj(mem, wkT, wv, wvT):
    B, M, D = mem.shape
    C = wv.shape[1]
    full = lambda shape: pl.BlockSpec(shape, lambda b: (0,) * len(shape))
    return pl.pallas_call(
        _memproj_body,
        grid=(B,),
        in_specs=[pl.BlockSpec((1, M, D), lambda b: (b, 0, 0)), full((C, D)), full((D, C)), full((C, D))],
        out_specs=[pl.BlockSpec((1, C, M), lambda b: (b, 0, 0)),
                   pl.BlockSpec((1, M, C), lambda b: (b, 0, 0)),
                   pl.BlockSpec((1, C, M), lambda b: (b, 0, 0))],
        out_shape=[jax.ShapeDtypeStruct((B, C, M), F32), jax.ShapeDtypeStruct((B, M, C), F32),
                   jax.ShapeDtypeStruct((B, C, M), F32)],
        compiler_params=_cparams(("arbitrary",)),
        name="memproj",
    )(mem, wkT, wv, wvT)


def _proj_body(d_a, d_b, x_ref, perm_ref, w_ref, wkvT_ref, q_ref, k_ref, v_ref, u_ref, qc_ref, kT_ref, vT_ref):
    xb = x_ref[0].astype(BF16)
    ts = xb.shape[0]
    xp = _dot(perm_ref[...], xb).astype(BF16)
    y = _dot(xp, w_ref[:, 0:3 * d_a])
    rows = ts // RES
    for r in range(RES):
        q_ref[0, r] = y[r * rows:(r + 1) * rows, 0:d_a]
        k_ref[0, r] = y[r * rows:(r + 1) * rows, d_a:2 * d_a]
        v_ref[0, r] = y[r * rows:(r + 1) * rows, 2 * d_a:3 * d_a]
    y2 = _dot(xb, w_ref[:, 3 * d_a:])
    u_ref[0] = y2[:, 0:d_b]
    qc_ref[0] = y2[:, d_b:]
    yT = _dot_nt(wkvT_ref[...], xb)
    kT_ref[0] = yT[0:d_a]
    vT_ref[0] = yT[d_a:]


def _residue_major_perm(n):
    dst = jnp.arange(n)
    src = RES * (dst % (n // RES)) + dst // (n // RES)
    return (src[:, None] == jnp.arange(n)[None, :]).astype(BF16)


def _proj(x, w_bf, wkvT_bf, d_a, d_b, d_c, ts):
    B, S, D = x.shape
    d_in = w_bf.shape[1]
    assert ts % (RES * SUBLANES) == 0 and S % ts == 0
    row = lambda n: pl.BlockSpec((1, ts, n), lambda b, i: (b, i, 0))
    col = lambda n: pl.BlockSpec((1, n, ts), lambda b, i: (b, 0, i))
    grp = pl.BlockSpec((1, RES, ts // RES, d_a), lambda b, i: (b, 0, i, 0))
    outs = pl.pallas_call(
        functools.partial(_proj_body, d_a, d_b),
        grid=(B, S // ts),
        in_specs=[row(D), pl.BlockSpec((ts, ts), lambda b, i: (0, 0)), pl.BlockSpec((D, d_in), lambda b, i: (0, 0)),
                  pl.BlockSpec((2 * d_a, D), lambda b, i: (0, 0))],
        out_specs=[grp, grp, grp, row(d_b), row(d_c), col(d_a), col(d_a)],
        out_shape=[jax.ShapeDtypeStruct((B, RES, S // RES, d_a), F32)] * 3
        + [jax.ShapeDtypeStruct((B, S, d_b), F32), jax.ShapeDtypeStruct((B, S, d_c), F32)]
        + [jax.ShapeDtypeStruct((B, d_a, S), F32)] * 2,
        compiler_params=_cparams(("arbitrary", "arbitrary")),
        name="proj",
    )(x, _residue_major_perm(ts), w_bf, wkvT_bf)
    return [o.reshape(B, S, d_a) for o in outs[:3]] + list(outs[3:])


def _attn_body(S, slopes_ref, q_ref, k_ref, v_ref, o_ref, opat_ref, lpat_ref):
    hp = pl.program_id(1)
    blk = BAND_BLOCK
    lane = lax.broadcasted_iota(jnp.int32, (blk, LANES), 1)
    head0 = lane < HEAD_DIM

    for pi, (window, dil) in enumerate(DILATED_PATTERNS):
        n_steps = window // dil
        ngrp = RES // dil
        cs = blk // ngrp
        cs_bits = cs.bit_length() - 1
        assert ngrp * dil == RES and cs * ngrp == blk and cs % SUBLANES == 0 and S == RES * blk
        has_prev = ngrp > 1
        nk = 2 * blk if has_prev else blk
        nat = lambda a, ngrp=ngrp, cs=cs, cs_bits=cs_bits: (a & (cs - 1)) * ngrp + (a >> cs_bits)
        qi = lax.broadcasted_iota(jnp.int32, (blk, nk), 0)
        kj = lax.broadcasted_iota(jnp.int32, (blk, nk), 1)
        if has_prev:
            steps = (nat(qi) + blk) - (nat(kj & (blk - 1)) + (kj & blk))
        else:
            steps = qi - kj
        valid = (steps >= 0) & (steps <= n_steps)
        dist = (steps * dil).astype(F32)
        biases = []
        for hh in range(2):
            slope = slopes_ref[2 * hp + hh]
            biases.append(jnp.where(valid, -slope * dist, NEG_INF))
        prev_cols = kj < blk

        ones_k = jnp.ones((ATTN_BATCH, nk, LANES), BF16)

        def blocks(it, carry, dil=dil, ngrp=ngrp, cs=cs, has_prev=has_prev, biases=biases, prev_cols=prev_cols,
                   pi=pi, nk=nk, ones_k=ones_k):
            def chunk(idx, c, back):
                r = idx // ngrp
                j = jnp.maximum(idx % ngrp - back, 0)
                return pl.ds(pl.multiple_of((r + dil * c) * blk + cs * j, cs), cs)

            def load(ref, idx, back=0):
                return jnp.concatenate([ref[0, chunk(idx, c, back), :] for c in range(ngrp)], axis=0)

            qs, ks, vs, firsts = [], [], [], []
            for b in range(ATTN_BATCH):
                idx = it * ATTN_BATCH + b
                qs.append(load(q_ref, idx))
                if has_prev:
                    ks.append(jnp.concatenate([load(k_ref, idx, 1), load(k_ref, idx)], axis=0))
                    vs.append(jnp.concatenate([load(v_ref, idx, 1), load(v_ref, idx)], axis=0))
                    firsts.append(jnp.where(prev_cols & (idx % ngrp == 0), NEG_INF, 0.0))
                else:
                    ks.append(load(k_ref, idx))
                    vs.append(load(v_ref, idx))
            q3 = jnp.stack(qs) * ATTN_SCALE
            k3 = jnp.stack(ks).astype(BF16)
            v3 = jnp.concatenate([jnp.stack(vs).astype(BF16), ones_k], axis=2)
            outs, lses = [], []
            for hh in range(2):
                qm = jnp.where(head0 if hh == 0 else ~head0, q3, 0.0).astype(BF16)
                s = jnp.einsum("bqd,bkd->bqk", qm, k3, preferred_element_type=F32) + biases[hh]
                if has_prev:
                    s = s + jnp.stack(firsts)
                m = jnp.max(s, axis=2, keepdims=True)
                p = jnp.exp(s - m).astype(BF16)
                o = jnp.einsum("bqk,bkd->bqd", p, v3, preferred_element_type=F32)
                den = o[:, :, LANES:]
                outs.append(o[:, :, :LANES] / den)
                lses.append(m + jnp.log(den))
            o_pair = jnp.where(head0, outs[0], outs[1])
            l_pair = jnp.where(head0, lses[0], lses[1])
            for b in range(ATTN_BATCH):
                idx = it * ATTN_BATCH + b
                for c in range(ngrp):
                    opat_ref[pi, chunk(idx, c, 0), :] = o_pair[b, c * cs:(c + 1) * cs]
                    lpat_ref[pi, chunk(idx, c, 0), :] = l_pair[b, c * cs:(c + 1) * cs]
            return carry

        lax.fori_loop(0, RES // ATTN_BATCH, blocks, 0)

    chunk = 256

    def mix(c, carry):
        rows = pl.ds(pl.multiple_of(c * chunk, chunk), chunk)
        l0, l1, l2 = lpat_ref[0, rows, :], lpat_ref[1, rows, :], lpat_ref[2, rows, :]
        mx = jnp.maximum(jnp.maximum(l0, l1), l2)
        w0, w1, w2 = jnp.exp(l0 - mx), jnp.exp(l1 - mx), jnp.exp(l2 - mx)
        num = w0 * opat_ref[0, rows, :] + w1 * opat_ref[1, rows, :] + w2 * opat_ref[2, rows, :]
        o_ref[0, rows, :] = (num / (w0 + w1 + w2)).astype(o_ref.dtype)
        return carry

    lax.fori_loop(0, S // chunk, mix, 0)


def _attn(q, k, v, slopes):
    B, S, d_a = q.shape
    assert S % (BAND_BLOCK * max(d for _, d in DILATED_PATTERNS)) == 0
    npair = d_a // LANES
    spec = pl.BlockSpec((1, S, LANES), lambda b, h, *_: (b, 0, h))
    grid_spec = pltpu.PrefetchScalarGridSpec(
        num_scalar_prefetch=0,
        grid=(B, npair),
        in_specs=[pl.BlockSpec(memory_space=pltpu.SMEM), spec, spec, spec],
        out_specs=spec,
        scratch_shapes=[pltpu.VMEM((len(DILATED_PATTERNS), S, LANES), F32),
                        pltpu.VMEM((len(DILATED_PATTERNS), S, LANES), F32)],
    )
    return pl.pallas_call(
        functools.partial(_attn_body, S),
        grid_spec=grid_spec,
        out_shape=jax.ShapeDtypeStruct((B, S, d_a), BF16),
        compiler_params=_cparams(("arbitrary", "arbitrary")),
        name="dilated_attn",
    )(slopes, q, k, v)


def _layer_norm(z, g, b):
    mu = jnp.mean(z, axis=-1, keepdims=True)
    zc = z - mu
    var = jnp.mean(zc * zc, axis=-1, keepdims=True)
    return zc * lax.rsqrt(var + LN_EPS) * g + b


def _route(logits):
    n = logits.shape[0]
    lane = lax.broadcasted_iota(jnp.int32, (n, LANES), 1)
    lane_f = lane.astype(F32)
    big = float(LANES)
    is_outer = lane < N_EXPERT_GROUPS
    l1 = jnp.where(is_outer, logits, NEG_INF)
    m1 = jnp.max(l1, axis=1, keepdims=True)
    g_sel = jnp.min(jnp.where(l1 == m1, lane_f, big), axis=1, keepdims=True)
    v1 = 1.0 / jnp.sum(jnp.exp(l1 - m1), axis=1, keepdims=True)
    lo = ROUTER_OFF + g_sel * EXPERTS_PER_GROUP
    in_group = (lane_f >= lo) & (lane_f < lo + EXPERTS_PER_GROUP)
    l2 = jnp.where(in_group, logits, NEG_INF)
    ma = jnp.max(l2, axis=1, keepdims=True)
    ia = jnp.min(jnp.where(l2 == ma, lane_f, big), axis=1, keepdims=True)
    l2b = jnp.where(lane_f == ia, NEG_INF, l2)
    mb = jnp.max(l2b, axis=1, keepdims=True)
    ib = jnp.min(jnp.where(l2b == mb, lane_f, big), axis=1, keepdims=True)
    eb = jnp.exp(mb - ma)
    wa = 1.0 / (1.0 + eb)
    wb = eb / (1.0 + eb)
    return ia - ROUTER_OFF, ib - ROUTER_OFF, v1 * wa, v1 * wb


def _rank_and_meta(e0, e1, g0, g1, carry):
    n = e0.shape[0]
    lane_f = lax.broadcasted_iota(jnp.int32, (n, LANES), 1).astype(F32)
    oh0 = (lane_f == e0).astype(F32)
    oh1 = (lane_f == e1).astype(F32)
    both = oh0 + oh1
    ti = lax.broadcasted_iota(jnp.int32, (n, n), 0)
    tj = lax.broadcasted_iota(jnp.int32, (n, n), 1)
    tri = (tj < ti).astype(BF16)
    before = _dot(tri, both.astype(BF16)) + carry
    r0 = jnp.sum(before * oh0, axis=1, keepdims=True)
    r1 = jnp.sum(before * oh1, axis=1, keepdims=True)
    new_carry = carry + jnp.sum(both, axis=0, keepdims=True)
    lane = lax.broadcasted_iota(jnp.int32, (n, LANES), 1)
    meta = jnp.zeros((n, LANES), F32)
    for i, val in enumerate((e0, e1, g0, g1, r0, r1)):
        meta = jnp.where(lane == i, val, meta)
    return meta, meta.T[0:SUBLANES], new_carry


def _lane_group(shape, width):
    lane = lax.broadcasted_iota(jnp.int32, shape, len(shape) - 1)
    grp = jnp.zeros(shape, jnp.int32)
    for g in range(1, shape[-1] // width):
        grp = grp + (lane >= g * width).astype(jnp.int32)
    return grp


def _select_by_group(grp, vals):
    out = vals[-1]
    for g in range(len(vals) - 2, -1, -1):
        out = jnp.where(grp == g, vals[g], out)
    return out


def _store_token_rows(ref, val):
    n, d = val.shape
    nchunk = d // LANES
    for c in range(nchunk):
        ref[pl.ds(c, n, stride=nchunk), :] = val[:, c * LANES:(c + 1) * LANES]


def _load_token_rows(ref, n, nchunk):
    return jnp.concatenate([ref[pl.ds(c, n, stride=nchunk), :] for c in range(nchunk)], axis=1)


HALO = 16


def _mix_body(tm, nt, n_tiles, d_b, alpha, x_ref, at_ref, unperm_ref, u_ref, uh_ref, qc_ref, kmT_ref, vm_ref,
              wo_ref, wp_ref, ps_ref, g1_ref, b1_ref, wr_ref, br_ref, cin_ref, hs_ref,
              h1_ref, meta_ref, metaT_ref, cnt_ref, carry_ref):
    g = pl.program_id(0)

    @pl.when(g == 0)
    def _():
        carry_ref[...] = cin_ref[...]

    @pl.when(g == n_tiles)
    def _():
        ns_rows = hs_ref.shape[0]
        h1_ref[0:ns_rows, :] = hs_ref[...]
        h1_ref[ns_rows:, :] = jnp.zeros((h1_ref.shape[0] - ns_rows, LANES), F32)

    @pl.when(g < n_tiles)
    def _():
        _mix_tile(tm, g % nt, d_b, alpha, x_ref, at_ref, unperm_ref, u_ref, uh_ref, qc_ref, kmT_ref, vm_ref, wo_ref,
                  wp_ref, ps_ref, g1_ref, b1_ref, wr_ref, br_ref, h1_ref, meta_ref, metaT_ref, cnt_ref, carry_ref)


def _mix_tile(tm, i, d_b, alpha, x_ref, at_ref, unperm_ref, u_ref, uh_ref, qc_ref, kmT_ref, vm_ref, wo_ref, wp_ref,
              ps_ref, g1_ref, b1_ref, wr_ref, br_ref, h1_ref, meta_ref, metaT_ref, cnt_ref, carry_ref):
    u = u_ref[0]
    halo = jnp.where(i == 0, 0.0, uh_ref[0])
    ext = jnp.concatenate([halo, halo, u], axis=0)
    s2 = ext + pltpu.roll(ext, 1, 0)
    s4 = s2 + pltpu.roll(s2, 2, 0)
    s8 = s4 + pltpu.roll(s4, 4, 0)
    s16 = s8 + pltpu.roll(s8, 8, 0)
    grp = _lane_group((tm, d_b), d_b // len(POOL_WINDOWS))
    sums = [s[2 * HALO:] for s in (s2, s4, s8, s16)]
    win = _select_by_group(grp, sums)
    wlen = _select_by_group(grp, [jnp.int32(w) for w in POOL_WINDOWS])
    pos = i * tm + lax.broadcasted_iota(jnp.int32, (tm, d_b), 0)
    cnt = jnp.minimum(wlen, pos + 1).astype(F32)
    diff = win / cnt - u
    pool = _dot(diff.astype(BF16), wp_ref[...]) * ps_ref[...]

    qc = qc_ref[0] * ATTN_SCALE
    d_c = qc.shape[1]
    kmT = kmT_ref[0].astype(BF16)
    vm = vm_ref[0].astype(BF16)
    hl = _lane_group((tm, d_c), HEAD_DIM)
    memo = jnp.zeros((tm, d_c), F32)
    for h in range(d_c // HEAD_DIM):
        qm = jnp.where(hl == h, qc, 0.0).astype(BF16)
        s = _dot(qm, kmT)
        p = jnp.exp(s - jnp.max(s, axis=1, keepdims=True))
        den = jnp.sum(p, axis=1, keepdims=True)
        o = _dot(p.astype(BF16), vm) / den
        memo = jnp.where(hl == h, o, memo)

    at_rl = jnp.concatenate([at_ref[0, r] for r in range(RES)], axis=0)
    attn = _dot(unperm_ref[...], at_rl).astype(BF16)
    cat = jnp.concatenate([attn, pool.astype(BF16), memo.astype(BF16)], axis=1)
    mixv = _dot(cat, wo_ref[...])
    h1 = _layer_norm(alpha * x_ref[0] + mixv, g1_ref[...], b1_ref[...])
    _store_token_rows(h1_ref, h1)

    logits = _dot(h1.astype(BF16), wr_ref[...]) + br_ref[...]
    e0, e1, g0, g1 = _route(logits)
    meta, metaT, new_carry = _rank_and_meta(e0, e1, g0, g1, carry_ref[0:1, :])
    meta_ref[...] = meta
    metaT_ref[...] = metaT
    carry_ref[...] = jnp.broadcast_to(new_carry, carry_ref.shape)
    cnt_ref[...] = carry_ref[...]


def _mix(x, attn_rm, u, qc, kmT, vm, wo_bf, wp_bd_bf, pool_scale, g1, b1, wr, br, counts_in, h1s, alpha, tm):
    B, S, D = x.shape
    d_a, d_b, d_c = attn_rm.shape[2], u.shape[2], qc.shape[2]
    n_mem = vm.shape[1]
    nt = S // tm
    n_tiles = B * nt
    nchunk = D // LANES
    assert tm % (RES * SUBLANES * 2) == 0 and h1s.shape[0] < tm * nchunk
    last = n_tiles - 1
    bi = lambda g: (jnp.minimum(g, last) // nt, jnp.minimum(g, last) % nt)
    row = lambda n: pl.BlockSpec((1, tm, n), lambda g: bi(g) + (0,))
    full = lambda shape: pl.BlockSpec(shape, lambda g: (0,) * len(shape))
    per_b = lambda shape: pl.BlockSpec((1,) + shape, lambda g: (bi(g)[0], 0, 0))
    halo_spec = pl.BlockSpec((1, HALO, d_b), lambda g: (bi(g)[0], jnp.maximum(bi(g)[1] * (tm // HALO) - 1, 0), 0))
    at_spec = pl.BlockSpec((1, RES, tm // RES, d_a), lambda g: (bi(g)[0], 0, bi(g)[1], 0))
    dst = jnp.arange(tm)
    unperm = ((dst % RES) * (tm // RES) + dst // RES)[:, None] == jnp.arange(tm)[None, :]
    return pl.pallas_call(
        functools.partial(_mix_body, tm, nt, n_tiles, d_b, alpha),
        grid=(n_tiles + 1,),
        in_specs=[row(D), at_spec, full((tm, tm)), row(d_b), halo_spec, row(d_c), per_b((d_c, n_mem)),
                  per_b((n_mem, d_c)), full((D, D)), full((d_b, d_b)), full((1, d_b)), full((1, D)), full((1, D)),
                  full((D, LANES)), full((1, LANES)), full((SUBLANES, LANES)), full(h1s.shape)],
        out_specs=[pl.BlockSpec((tm * nchunk, LANES), lambda g: (g, 0)),
                   pl.BlockSpec((tm, LANES), lambda g: (jnp.minimum(g, last), 0)),
                   pl.BlockSpec((SUBLANES, tm), lambda g: (0, jnp.minimum(g, last))),
                   pl.BlockSpec((SUBLANES, LANES), lambda g: (0, 0))],
        out_shape=[jax.ShapeDtypeStruct(((n_tiles + 1) * tm * nchunk, LANES), F32),
                   jax.ShapeDtypeStruct((B * S, LANES), F32),
                   jax.ShapeDtypeStruct((SUBLANES, B * S), F32),
                   jax.ShapeDtypeStruct((SUBLANES, LANES), F32)],
        scratch_shapes=[pltpu.VMEM((SUBLANES, LANES), F32)],
        compiler_params=_cparams(("arbitrary",)),
        name="mix_ln1_router",
    )(x, attn_rm.reshape(B, RES, S // RES, d_a), unperm.astype(BF16), u, u, qc, kmT, vm, wo_bf, wp_bd_bf,
      pool_scale, g1, b1, wr, br, counts_in, h1s)


def _columns_to_lanes(cols, rows):
    lane = lax.broadcasted_iota(jnp.int32, (rows, LANES), 1)
    tile = jnp.zeros((rows, LANES), F32)
    for t, c in enumerate(cols):
        tile = jnp.where(lane == t, c, tile)
    return tile


def _sproj_body(x_ref, wT_ref, o_ref):
    o_ref[...] = _dot_nt(wT_ref[...], x_ref[...].astype(BF16))


def _sproj(x, wT_bf):
    vm = pl.BlockSpec(memory_space=pltpu.VMEM)
    return pl.pallas_call(
        _sproj_body, in_specs=[vm, vm], out_specs=vm,
        out_shape=jax.ShapeDtypeStruct((wT_bf.shape[0], x.shape[0]), F32),
        compiler_params=pltpu.CompilerParams(vmem_limit_bytes=VMEM_LIMIT),
        name="decode_proj",
    )(x, wT_bf)


def _decode_body(T, wb, qc_row0, slopes_ref, pT_ref, ck_ref, cv_ref, mk_ref, mv_ref,
                 nk_ref, nv_ref, at_ref, mo_ref, qkv_ref, sfar_ref, snear_ref, snew_ref):
    H = ck_ref.shape[1]
    d_a = H * HEAD_DIM
    ntok = pT_ref.shape[1]
    shift = (ntok - T * pl.program_id(0)) % ntok
    qkv_ref[...] = pltpu.roll(pT_ref[...], shift, 1)

    def member(dist, window, dil):
        assert dil & (dil - 1) == 0
        return ((dist & (dil - 1)) == 0) & (dist <= window) & (dist >= 0)

    OWN = min(d for _, d in DILATED_PATTERNS if d > 1)
    far = wb - LANES
    assert T <= OWN and far % OWN == 0 and wb % OWN == 0
    assert all(w <= LANES for w, d in DILATED_PATTERNS if d == 1) and all(d % OWN == 0 for _, d in DILATED_PATTERNS if d > 1)
    R = H * T
    assert T & (T - 1) == 0
    query = lambda shape: lax.broadcasted_iota(jnp.int32, shape, 0) & (T - 1)
    lane_far = lax.broadcasted_iota(jnp.int32, (R, far), 1)
    t_far = query((R, far))
    owned = (lane_far & (OWN - 1)) == t_far
    dist_far = wb + t_far - lane_far
    dist_near = wb + query((R, LANES)) - (far + lax.broadcasted_iota(jnp.int32, (R, LANES), 1))
    dist_new = query((R, T)) - lax.broadcasted_iota(jnp.int32, (R, T), 1)
    masks = [(None if dil == 1 else owned & member(dist_far, window, dil),
              member(dist_near, window, dil), member(dist_new, window, dil)) for window, dil in DILATED_PATTERNS]
    dist_far = dist_far.astype(F32)
    dist_near = dist_near.astype(F32)
    dist_new = jnp.maximum(dist_new, 0).astype(F32)
    rnd = lambda a: a.astype(BF16).astype(F32)

    lane_t = lax.broadcasted_iota(jnp.int32, (HEAD_DIM, LANES), 1)
    own_t = lane_t & (OWN - 1)
    last = wb - LANES
    n_far = far // LANES

    tile = lambda a, c: a[:, c * LANES:(c + 1) * LANES]

    G = SUBLANES // T
    assert G * T == SUBLANES and H % G == 0
    group_rows = lambda g: pl.ds(pl.multiple_of(g * SUBLANES, SUBLANES), SUBLANES)

    def scores(g, carry):
        parts = [head_scores(g * G + i) for i in range(G)]
        sfar_ref[group_rows(g), :] = jnp.concatenate([p[0] for p in parts], axis=0)
        snear_ref[group_rows(g), :] = jnp.concatenate([p[1] for p in parts], axis=0)
        snew_ref[group_rows(g), :] = jnp.concatenate([p[2] for p in parts], axis=0)
        return carry

    def head_scores(h):
        slope = slopes_ref[h]
        kTc = ck_ref[0, h]
        vTc = cv_ref[0, h]
        r0 = pl.multiple_of(h * HEAD_DIM, HEAD_DIM)
        qT = rnd(qkv_ref[pl.ds(r0, HEAD_DIM), :][:, 0:T] * ATTN_SCALE)
        kTn = qkv_ref[pl.ds(d_a + r0, HEAD_DIM), :][:, 0:T]
        vTn = qkv_ref[pl.ds(2 * d_a + r0, HEAD_DIM), :][:, 0:T]
        kTn_r = rnd(kTn)

        qb = [jnp.broadcast_to(qT[:, t:t + 1], (HEAD_DIM, LANES)) for t in range(T)]
        qpat = qb[T - 1]
        for t in range(T - 1):
            qpat = jnp.where(own_t == t, qb[t], qpat)
        s_far = jnp.concatenate([jnp.sum(rnd(tile(kTc, c)) * qpat, axis=0, keepdims=True) for c in range(n_far)],
                                axis=1)
        s_far = s_far - slope * dist_far[0:T]
        k_near = rnd(tile(kTc, n_far))
        s_near = jnp.concatenate([jnp.sum(k_near * qb[t], axis=0, keepdims=True) for t in range(T)], axis=0)
        s_near = s_near - slope * dist_near[0:T]
        s_new = jnp.concatenate([jnp.sum(kTn_r * qT[:, t:t + 1], axis=0, keepdims=True) for t in range(T)], axis=0)
        s_new = s_new - slope * dist_new[0:T]

        rk = pltpu.roll(kTc, wb - T, 1)
        rv = pltpu.roll(vTc, wb - T, 1)
        nk_ref[0, h] = rk
        nv_ref[0, h] = rv
        newk = jnp.zeros((HEAD_DIM, LANES), F32)
        newv = jnp.zeros((HEAD_DIM, LANES), F32)
        for t in range(T):
            newk = jnp.where(lane_t == LANES - T + t, kTn[:, t:t + 1], newk)
            newv = jnp.where(lane_t == LANES - T + t, vTn[:, t:t + 1], newv)
        nk_ref[0, h, :, last:] = jnp.where(lane_t >= LANES - T, newk, rk[:, last:])
        nv_ref[0, h, :, last:] = jnp.where(lane_t >= LANES - T, newv, rv[:, last:])
        return s_far, s_near, s_new

    lax.fori_loop(0, H // G, scores, 0)

    s_far, s_near, s_new = sfar_ref[...], snear_ref[...], snew_ref[...]
    rmax = lambda a: jnp.max(a, axis=1, keepdims=True)
    rsum = lambda a: jnp.sum(a, axis=1, keepdims=True)
    parts, lses = [], []
    for m_far, m_near, m_new in masks:
        sn, sw = jnp.where(m_near, s_near, NEG_INF), jnp.where(m_new, s_new, NEG_INF)
        m = jnp.maximum(rmax(sn), rmax(sw))
        if m_far is not None:
            sf = jnp.where(m_far, s_far, NEG_INF)
            m = jnp.maximum(m, rmax(sf))
        en, ew = jnp.exp(sn - m), jnp.exp(sw - m)
        den = rsum(en) + rsum(ew)
        ef = None
        if m_far is not None:
            ef = jnp.exp(sf - m)
            den = den + rsum(ef)
            ef = rnd(ef)
        parts.append((ef, rnd(en), rnd(ew), den))
        lses.append(m + jnp.log(den))
    top = functools.reduce(jnp.maximum, lses)
    mixw = [jnp.exp(l - top) for l in lses]
    total = functools.reduce(lambda a, b: a + b, mixw)
    p_far = p_near = p_new = None
    for (ef, en, ew, den), w in zip(parts, mixw):
        scale = w / (total * den)
        p_near = en * scale if p_near is None else p_near + en * scale
        p_new = ew * scale if p_new is None else p_new + ew * scale
        if ef is not None:
            p_far = ef * scale if p_far is None else p_far + ef * scale
    sfar_ref[...] = p_far
    snear_ref[...] = p_near
    snew_ref[...] = p_new

    def values(g, carry):
        pf, pn, pw = sfar_ref[group_rows(g), :], snear_ref[group_rows(g), :], snew_ref[group_rows(g), :]
        for i in range(G):
            head_values(g * G + i, pf[i * T:(i + 1) * T], pn[i * T:(i + 1) * T], pw[i * T:(i + 1) * T])
        return carry

    def head_values(h, p_far_h, p_near_h, p_new_h):
        vTc = cv_ref[0, h]
        r0 = pl.multiple_of(h * HEAD_DIM, HEAD_DIM)
        vTn_r = rnd(qkv_ref[pl.ds(2 * d_a + r0, HEAD_DIM), :][:, 0:T])
        p_comb = jnp.sum(p_far_h, axis=0, keepdims=True)
        acc = rnd(tile(vTc, 0)) * tile(p_comb, 0)
        for c in range(1, n_far):
            acc = acc + rnd(tile(vTc, c)) * tile(p_comb, c)
        v_near = rnd(tile(vTc, n_far))
        cols = []
        for t in range(T):
            cols.append(jnp.sum(jnp.where(own_t == t, acc, 0.0) + v_near * p_near_h[t:t + 1], axis=1, keepdims=True)
                        + jnp.sum(vTn_r * p_new_h[t:t + 1], axis=1, keepdims=True))
        at_ref[0, pl.ds(r0, HEAD_DIM), :] = _columns_to_lanes(cols, HEAD_DIM)

    lax.fori_loop(0, H // G, values, 0)

    Hc = mk_ref.shape[1]
    rows = []
    for h in range(Hc):
        kT = rnd(mk_ref[0, h])
        r0 = h * HEAD_DIM
        qT = rnd(qkv_ref[qc_row0 + r0:qc_row0 + r0 + HEAD_DIM, :][:, 0:T] * ATTN_SCALE)
        rows += [jnp.sum(kT * qT[:, t:t + 1], axis=0, keepdims=True) for t in range(T)]
    s = jnp.concatenate(rows, axis=0)
    p = jnp.exp(s - jnp.max(s, axis=1, keepdims=True))
    p = rnd(p / jnp.sum(p, axis=1, keepdims=True))
    for h in range(Hc):
        vT = rnd(mv_ref[0, h])
        cols = [jnp.sum(vT * p[h * T + t:h * T + t + 1], axis=1, keepdims=True) for t in range(T)]
        mo_ref[0, h * HEAD_DIM:(h + 1) * HEAD_DIM, :] = _columns_to_lanes(cols, HEAD_DIM)


def _decode(projT, T, qc_row0, ck, cv, mk, mv, slopes):
    DB, H, hd, wb = ck.shape
    _, Hc, _, n_mem = mk.shape
    d_a, d_c = H * hd, Hc * hd
    assert wb >= max(w for w, _ in DILATED_PATTERNS) and wb % LANES == 0 and T <= LANES
    assert projT.shape[1] == DB * T and projT.shape[0] >= qc_row0 + d_c
    cache = pl.BlockSpec((1, H, hd, wb), lambda b, *_: (b, 0, 0, 0))
    memc = pl.BlockSpec((1, Hc, hd, n_mem), lambda b, *_: (b, 0, 0, 0))
    grid_spec = pltpu.PrefetchScalarGridSpec(
        num_scalar_prefetch=0,
        grid=(DB,),
        in_specs=[pl.BlockSpec(memory_space=pltpu.SMEM),
                  pl.BlockSpec(projT.shape, lambda b, *_: (0, 0)),
                  cache, cache, memc, memc],
        out_specs=[cache, cache,
                   pl.BlockSpec((1, d_a, LANES), lambda b, *_: (b, 0, 0)),
                   pl.BlockSpec((1, d_c, LANES), lambda b, *_: (b, 0, 0))],
        scratch_shapes=[pltpu.VMEM(projT.shape, F32), pltpu.VMEM((H * T, wb - LANES), F32),
                        pltpu.VMEM((H * T, LANES), F32), pltpu.VMEM((H * T, T), F32)],
    )
    nk, nv, at, mo = pl.pallas_call(
        functools.partial(_decode_body, T, wb, qc_row0),
        grid_spec=grid_spec,
        out_shape=[jax.ShapeDtypeStruct(ck.shape, F32), jax.ShapeDtypeStruct(cv.shape, F32),
                   jax.ShapeDtypeStruct((DB, d_a, LANES), F32), jax.ShapeDtypeStruct((DB, d_c, LANES), F32)],
        compiler_params=_cparams(("arbitrary",)),
        name="decode_attn_cache",
    )(slopes, projT, ck, cv, mk, mv)
    return nk, nv, at[:, :, :T], mo[:, :, :T]


def _smix_body(T, alpha, pos0, x_ref, at_ref, mo_ref, st_ref, wu_ref, wo_ref, wp_ref, ps_ref, g1_ref, b1_ref,
               wr_ref, br_ref, cin_ref, h1_ref, meta_ref, metaT_ref, cnt_ref, pool_ref):
    n = x_ref.shape[0]
    db = n // T
    pb = st_ref.shape[0]
    d_b = st_ref.shape[2]
    x = x_ref[...]
    bdot = lambda a, b: _dot(a.astype(BF16), b.astype(BF16))
    u_new = bdot(x, wu_ref[...])
    seq = [st_ref[j] for j in range(pb)] + [u_new[t * db:(t + 1) * db] for t in range(T)]
    for j in range(pb):
        pool_ref[j] = seq[j + T]
    grp = _lane_group((db, d_b), d_b // len(POOL_WINDOWS))
    diffs = []
    for t in range(T):
        j = pb + t
        per_w = []
        for w in POOL_WINDOWS:
            acc = seq[j]
            for back in range(1, w):
                if j - back >= 0:
                    acc = acc + seq[j - back]
            per_w.append(acc / float(min(w, pos0 + j + 1)))
        diffs.append(_select_by_group(grp, per_w) - seq[j])
    diff = jnp.concatenate(diffs, axis=0)
    pool = bdot(diff, wp_ref[...]) * ps_ref[...]
    cat = jnp.concatenate([at_ref[...], pool, mo_ref[...]], axis=1)
    mixv = bdot(cat, wo_ref[...])
    h1 = _layer_norm(alpha * x + mixv, g1_ref[...], b1_ref[...])
    _store_token_rows(h1_ref, h1)
    logits = bdot(h1, wr_ref[...]) + br_ref[...]
    e0, e1, g0, g1 = _route(logits)
    meta, metaT, new_carry = _rank_and_meta(e0, e1, g0, g1, cin_ref[0:1, :])
    meta_ref[...] = meta
    metaT_ref[...] = metaT
    cnt_ref[...] = jnp.broadcast_to(new_carry, cnt_ref.shape)


def _smix(x_tb, attn_tb, memo_tb, state, wu, wo, wp_bd, pool_scale, g1, b1, wr, br, counts_in, alpha, T):
    n, D = x_tb.shape
    pb, db, d_b = state.shape
    nchunk = D // LANES
    vm = pl.BlockSpec(memory_space=pltpu.VMEM)
    return pl.pallas_call(
        functools.partial(_smix_body, T, alpha, PAST_LEN - pb),
        in_specs=[vm] * 13,
        out_specs=[vm] * 5,
        out_shape=[jax.ShapeDtypeStruct((n * nchunk, LANES), F32), jax.ShapeDtypeStruct((n, LANES), F32),
                   jax.ShapeDtypeStruct((SUBLANES, n), F32),
                   jax.ShapeDtypeStruct((SUBLANES, LANES), F32), jax.ShapeDtypeStruct((pb, db, d_b), F32)],
        compiler_params=pltpu.CompilerParams(vmem_limit_bytes=VMEM_LIMIT),
        name="decode_mix_ln1_router",
    )(x_tb, attn_tb, memo_tb, state, wu, wo, wp_bd, pool_scale, g1, b1, wr, br, counts_in)


def _slot_map_body(blk, pos0_ref, pos1_ref, fill_ref, inv_ref, sem):
    g = pl.program_id(0)

    @pl.when(g == 0)
    def _():
        c = pltpu.make_async_copy(fill_ref, inv_ref, sem)
        c.start()
        c.wait()

    group = math.gcd(blk, 16)

    def body(j, carry):
        token0 = g * blk + j * group
        for u in range(group):
            inv_ref[pos0_ref[0, 0, j * group + u]] = token0 + u
            inv_ref[pos1_ref[0, 0, j * group + u]] = token0 + u
        return carry

    lax.fori_loop(0, blk // group, body, 0)


def _slot_map(pos2, n_slots, zero_token, blk):
    n = pos2.shape[1]
    assert n % blk == 0 and n_slots % 1024 == 0
    tokens = pl.BlockSpec((1, 1, blk), lambda g: (g, 0, 0), memory_space=pltpu.SMEM)
    return pl.pallas_call(
        functools.partial(_slot_map_body, blk),
        grid=(n // blk,),
        in_specs=[tokens, tokens, pl.BlockSpec(memory_space=pl.ANY)],
        out_specs=pl.BlockSpec(memory_space=pltpu.SMEM),
        out_shape=jax.ShapeDtypeStruct((n_slots,), jnp.int32),
        scratch_shapes=[pltpu.SemaphoreType.DMA(())],
        compiler_params=_cparams(("arbitrary",)),
        name="slot_map",
    )(pos2[0].reshape(n // blk, 1, blk), pos2[1].reshape(n // blk, 1, blk),
      jnp.full((n_slots,), zero_token, jnp.int32))


def _expert_body(tmx, nchunk, te_ref, nt_ref, tv_ref, cur_ref, nxt_ref, h_ref, wg_ref, wu_ref, wd_ref, ys_ref,
                 buf_a, buf_b, wg_bf, wu_bf, wd_bf, sems):
    i = pl.program_id(0)
    n_tiles = nt_ref[0]

    def gather(idx_ref, buf, s, tile, wait):
        groups = (tv_ref[tile] + DMA_UNROLL - 1) // DMA_UNROLL

        def body(g, carry):
            for u in range(DMA_UNROLL):
                j = g * DMA_UNROLL + u
                dst = buf.at[pl.ds(pl.multiple_of(j * nchunk, nchunk), nchunk)]
                copy = pltpu.make_async_copy(h_ref.at[idx_ref[0, 0, j]], dst, sems.at[s])
                if wait:
                    copy.wait()
                else:
                    copy.start(priority=u % 2)
            return carry

        lax.fori_loop(0, groups, body, 0)

    @pl.when(i == 0)
    def _():
        buf_a[...] = jnp.zeros_like(buf_a)
        buf_b[...] = jnp.zeros_like(buf_b)
        gather(cur_ref, buf_a, 0, 0, False)

    it = jnp.minimum(i, pl.num_programs(0) - 2)
    @pl.when((i < n_tiles) & ((i == 0) | (te_ref[it] != te_ref[jnp.maximum(it - 1, 0)])))
    def _():
        wg_bf[...] = wg_ref[0].astype(BF16)
        wu_bf[...] = wu_ref[0].astype(BF16)
        wd_bf[...] = wd_ref[0].astype(BF16)

    def step(cur, s_cur, nxt, s_nxt):
        gather(cur_ref, cur, s_cur, it, True)

        @pl.when(i + 1 < n_tiles)
        def _():
            gather(nxt_ref, nxt, s_nxt, it + 1, False)

        x = _load_token_rows(cur, tmx, nchunk).astype(BF16)
        hg = _dot(x, wg_bf[...])
        hu = _dot(x, wu_bf[...])
        a = (hg * jax.nn.sigmoid(hg) * hu).astype(BF16)
        y = _dot(a, wd_bf[...])
        _store_token_rows(ys_ref, y)

    @pl.when((i < n_tiles) & (i % 2 == 0))
    def _():
        step(buf_a, 0, buf_b, 1)

    @pl.when((i < n_tiles) & (i % 2 == 1))
    def _():
        step(buf_b, 1, buf_a, 0)

    @pl.when(i >= n_tiles)
    def _():
        ys_ref[...] = jnp.zeros_like(ys_ref)


def _experts(tile_expert, n_tiles, tile_valid, inv, h_rows, wg, wu, wd, tmx, max_tiles):
    E, D, F = wg.shape
    nchunk = D // LANES
    inv3 = inv.reshape(-1, 1, tmx)
    last_blk = inv3.shape[0] - 1
    assert last_blk >= max_tiles and tmx % DMA_UNROLL == 0
    weights = lambda shape: pl.BlockSpec((1,) + shape, lambda i, te, *_: (te[jnp.minimum(i, max_tiles - 1)], 0, 0))
    rows = pltpu.VMEM((tmx * nchunk, LANES), F32)
    grid_spec = pltpu.PrefetchScalarGridSpec(
        num_scalar_prefetch=3,
        grid=(max_tiles + 1,),
        in_specs=[pl.BlockSpec((1, 1, tmx), lambda i, *_: (i, 0, 0), memory_space=pltpu.SMEM),
                  pl.BlockSpec((1, 1, tmx), lambda i, *_: (jnp.minimum(i + 1, last_blk), 0, 0),
                               memory_space=pltpu.SMEM),
                  pl.BlockSpec(memory_space=pl.ANY), weights((D, F)), weights((D, F)), weights((F, D))],
        out_specs=pl.BlockSpec((tmx * nchunk, LANES), lambda i, *_: (i, 0)),
        scratch_shapes=[rows, rows, pltpu.VMEM((D, F), BF16), pltpu.VMEM((D, F), BF16), pltpu.VMEM((F, D), BF16),
                        pltpu.SemaphoreType.DMA((2,))],
    )
    return pl.pallas_call(
        functools.partial(_expert_body, tmx, nchunk),
        grid_spec=grid_spec,
        out_shape=jax.ShapeDtypeStruct(((max_tiles + 1) * tmx * nchunk, LANES), F32),
        compiler_params=_cparams(("arbitrary",)),
        name="expert_swiglu",
    )(tile_expert, n_tiles, tile_valid, inv3, inv3, h_rows, wg, wu, wd)


def _combine_body(tm, nchunk, alpha, pos_ref, nxt_ref, h1_ref, meta_ref, ys_ref, g2_ref, b2_ref, o_ref, buf_a, buf_b,
                  sems):
    i = pl.program_id(0)
    last = pl.num_programs(0) - 1

    def copy(idx_ref, buf, s, t, k):
        dst = buf.at[k, pl.ds(pl.multiple_of(t * nchunk, nchunk), nchunk)]
        return pltpu.make_async_copy(ys_ref.at[idx_ref[0, 0, k * tm + t]], dst, sems.at[s])

    def start(idx_ref, buf, s):
        def body(t, carry):
            for k in range(2):
                copy(idx_ref, buf, s, t, k).start(priority=k)
            return carry

        lax.fori_loop(0, tm, body, 0, unroll=DMA_UNROLL)

    def wait(buf, s):
        def body(t, carry):
            for k in range(2):
                copy(pos_ref, buf, s, t, k).wait()
            return carry

        lax.fori_loop(0, tm, body, 0, unroll=DMA_UNROLL)

    @pl.when(i == 0)
    def _():
        start(pos_ref, buf_a, 0)

    def step(cur, s_cur, nxt, s_nxt):
        wait(cur, s_cur)
        start(nxt_ref, nxt, s_nxt)
        h1 = _load_token_rows(h1_ref, tm, nchunk)
        y0 = _load_token_rows(cur.at[0], tm, nchunk)
        y1 = _load_token_rows(cur.at[1], tm, nchunk)
        meta = meta_ref[...]
        lane = lax.broadcasted_iota(jnp.int32, meta.shape, 1)
        gate0 = jnp.sum(jnp.where(lane == 2, meta, 0.0), axis=1, keepdims=True)
        gate1 = jnp.sum(jnp.where(lane == 3, meta, 0.0), axis=1, keepdims=True)
        f = gate0 * y0 + gate1 * y1
        o_ref[...] = _layer_norm(alpha * h1 + f, g2_ref[...], b2_ref[...])

        @pl.when(i == last)
        def _():
            wait(nxt, s_nxt)

    @pl.when(i % 2 == 0)
    def _():
        step(buf_a, 0, buf_b, 1)

    @pl.when(i % 2 == 1)
    def _():
        step(buf_b, 1, buf_a, 0)


def _combine(pos2, tok0, n, h1_flat, meta, ys, g2, b2, alpha, tm):
    D = g2.shape[1]
    nchunk = D // LANES
    assert n % tm == 0 and tok0 % tm == 0
    pos3 = jnp.transpose(pos2.reshape(2, -1, tm), (1, 0, 2)).reshape(-1, 1, 2 * tm)
    off = tok0 // tm
    last = off + n // tm - 1
    return pl.pallas_call(
        functools.partial(_combine_body, tm, nchunk, alpha),
        grid=(n // tm,),
        in_specs=[pl.BlockSpec((1, 1, 2 * tm), lambda i: (i + off, 0, 0), memory_space=pltpu.SMEM),
                  pl.BlockSpec((1, 1, 2 * tm), lambda i: (jnp.minimum(i + off + 1, last), 0, 0),
                               memory_space=pltpu.SMEM),
                  pl.BlockSpec((tm * nchunk, LANES), lambda i: (i + off, 0)),
                  pl.BlockSpec((tm, LANES), lambda i: (i, 0)),
                  pl.BlockSpec(memory_space=pl.ANY),
                  pl.BlockSpec((1, D), lambda i: (0, 0)), pl.BlockSpec((1, D), lambda i: (0, 0))],
        out_specs=pl.BlockSpec((tm, D), lambda i: (i, 0)),
        out_shape=jax.ShapeDtypeStruct((n, D), F32),
        scratch_shapes=[pltpu.VMEM((2, tm * nchunk, LANES), F32), pltpu.VMEM((2, tm * nchunk, LANES), F32),
                        pltpu.SemaphoreType.DMA((2,))],
        compiler_params=_cparams(("arbitrary",)),
        name="combine_ln2",
    )(pos3, pos3, h1_flat, meta, ys, g2, b2)


def _block_diag(w):
    g, a, b = w.shape
    eye = jnp.eye(g, dtype=w.dtype)
    return (eye[:, None, :, None] * w[:, :, None, :]).reshape(g * a, g * b)


def _layer(h_p, h_s, win_k, win_v, pool_st, mem_k, mem_v, mem_prompt,
           w_in, w_mem_kv, w_pool, pool_scale, w_o, ln1_g, ln1_b, ln2_g, ln2_b,
           w_r1, b_r1, w_r2, b_r2, w_gate, w_up, w_down, alpha):
    B, S, D = h_p.shape
    DB, T, _ = h_s.shape
    H = win_k.shape[2]
    Hc = mem_k.shape[2]
    d_a, d_c = H * HEAD_DIM, Hc * HEAD_DIM
    d_b = pool_st.shape[2]
    nchunk = D // LANES
    slopes = 2.0 ** (-8.0 * jnp.arange(1, H + 1, dtype=F32) / H)

    w_in_bf = w_in.astype(BF16)
    w_inT_bf = w_in_bf.T
    wkvT_bf = w_inT_bf[d_a:3 * d_a]
    wp_bd = _block_diag(w_pool)
    ps = pool_scale.reshape(1, d_b)
    g1, b1 = ln1_g.reshape(1, D), ln1_b.reshape(1, D)
    g2, b2 = ln2_g.reshape(1, D), ln2_b.reshape(1, D)
    n_r = N_EXPERT_GROUPS + N_EXPERTS
    wr = jnp.concatenate([w_r1, jnp.transpose(w_r2, (1, 0, 2)).reshape(D, N_EXPERTS),
                          jnp.zeros((D, LANES - n_r), F32)], axis=1)
    br = jnp.concatenate([b_r1, b_r2.reshape(-1), jnp.zeros((LANES - n_r,), F32)]).reshape(1, LANES)

    ck = jnp.transpose(win_k, (0, 2, 3, 1))
    cv = jnp.transpose(win_v, (0, 2, 3, 1))
    mk = jnp.transpose(mem_k, (0, 2, 3, 1))
    mv = jnp.transpose(mem_v, (0, 2, 3, 1))
    projT = _sproj(h_s.reshape(DB * T, D), w_inT_bf)
    nk, nv, attn_sT, memo_sT = _decode(projT, T, 3 * d_a + d_b, ck, cv, mk, mv, slopes)
    to_tb = lambda a: jnp.transpose(a, (2, 0, 1)).reshape(T * DB, a.shape[1])
    x_tb = jnp.transpose(h_s, (1, 0, 2)).reshape(T * DB, D)
    state = jnp.transpose(pool_st, (1, 0, 2))
    h1s, meta_s, metaT_s, cnt_s, new_pool = _smix(x_tb, to_tb(attn_sT), to_tb(memo_sT), state,
                                         w_in[:, 3 * d_a:3 * d_a + d_b], w_o, wp_bd, ps, g1, b1, wr, br,
                                         jnp.zeros((SUBLANES, LANES), F32), alpha, T)

    w_memT = w_mem_kv.T
    kmT, vm, vmT = _memproj(mem_prompt, w_memT[:d_c].astype(BF16), w_mem_kv[:, d_c:].astype(BF16),
                            w_memT[d_c:].astype(BF16))
    q, k, v, u, qc, kT, vT = _proj(h_p, w_in_bf, wkvT_bf, d_a, d_b, d_c, ts=math.gcd(S, PROJ_TILE))
    attn = _attn(q, k, v, slopes)
    h_rows, meta_p, metaT_p, cnt_all = _mix(h_p, attn, u, qc, kmT, vm, w_o.astype(BF16), wp_bd.astype(BF16), ps, g1, b1,
                                   wr.astype(BF16), br, cnt_s, h1s, alpha, tm=math.gcd(S, MIX_TILE))

    tmx = EXPERT_TILE
    n_p, n_s = B * S, DB * T
    n = n_p + n_s
    counts = cnt_all[0, :N_EXPERTS].astype(jnp.int32)
    padded = (counts + tmx - 1) // tmx * tmx
    seg_end = jnp.cumsum(padded)
    seg_off = seg_end - padded
    metaT = jnp.concatenate([metaT_p, metaT_s], axis=1)
    e_ids = metaT[0:2].astype(jnp.int32)
    experts = jnp.arange(N_EXPERTS).reshape((N_EXPERTS,) + (1,) * 2)
    lookup = lambda table, idx: jnp.sum(
        jnp.where(idx.reshape((1,) * (3 - idx.ndim) + idx.shape) == experts, table.reshape(experts.shape), 0),
        axis=0).reshape(idx.shape)
    pos2 = lookup(seg_off, e_ids) + metaT[4:6].astype(jnp.int32)
    max_tiles = (2 * n) // tmx + N_EXPERTS
    n_tiles = (seg_end[-1] // tmx).astype(jnp.int32).reshape(1)
    tile_row0 = jnp.arange(max_tiles, dtype=jnp.int32) * tmx
    tile_expert = jnp.sum((tile_row0[:, None] >= seg_end[None, :]).astype(jnp.int32), axis=1)
    tile_expert = jnp.minimum(tile_expert, N_EXPERTS - 1)

    tok_tile = math.gcd(math.gcd(n_p, n_s), ROW_DMA_TILE)
    n_slots = -(-((max_tiles + 1) * tmx) // 1024) * 1024
    inv = _slot_map(pos2, n_slots, n, blk=math.gcd(n, 1024))
    tile_valid = jnp.clip(lookup(seg_off + counts, tile_expert) - tile_row0, 0, tmx).astype(jnp.int32)
    ys = _experts(tile_expert, n_tiles, tile_valid, inv, h_rows.reshape(-1, nchunk, LANES),
                  w_gate.reshape(N_EXPERTS, D, -1),
                  w_up.reshape(N_EXPERTS, D, -1), w_down.reshape(N_EXPERTS, -1, D), tmx, max_tiles)
    ys3 = ys.reshape(-1, nchunk, LANES)
    y_p = _combine(pos2, 0, n_p, h_rows, meta_p, ys3, g2, b2, alpha, tm=tok_tile)
    y_s = _combine(pos2, n_p, n_s, h_rows, meta_s, ys3, g2, b2, alpha, tm=tok_tile)

    y_p = y_p.reshape(B, S, D)
    y_s = jnp.transpose(y_s.reshape(T, DB, D), (1, 0, 2))
    heads = lambda a, h: jnp.transpose(a.reshape(a.shape[0], h, HEAD_DIM, a.shape[2]), (0, 3, 1, 2))
    wbp = min(max(w for w, _ in DILATED_PATTERNS), S)
    new_wk_p = heads(kT, H)[:, S - wbp:]
    new_wv_p = heads(vT, H)[:, S - wbp:]
    pb = pool_st.shape[1]
    new_pool_p = u[:, S - pb:]
    new_mk_p = heads(kmT, Hc)
    new_mv_p = heads(vmT, Hc)
    new_wk_s = jnp.transpose(nk, (0, 3, 1, 2))
    new_wv_s = jnp.transpose(nv, (0, 3, 1, 2))
    new_pool_s = jnp.transpose(new_pool, (1, 0, 2))
    return (y_p, y_s, new_wk_p, new_wv_p, new_pool_p, new_mk_p, new_mv_p, new_wk_s, new_wv_s, new_pool_s)


def kernel(x_prompt, x_sample, cache_win_k, cache_win_v, state_pool, cache_mem_k, cache_mem_v, mem_prompt, w_in, w_mem_kv, w_pool, pool_scale, w_o, ln1_g, ln1_b, ln2_g, ln2_b, w_r1, b_r1, w_r2, b_r2, w_gate, w_up, w_down):
    depth = w_in.shape[0]
    alpha = (2.0 * depth) ** 0.25
    h_p, h_s = x_prompt, x_sample
    outs = [[] for _ in range(8)]
    for l in range(depth):
        res = _layer(h_p, h_s, cache_win_k[l], cache_win_v[l], state_pool[l], cache_mem_k[l], cache_mem_v[l],
                     mem_prompt, w_in[l], w_mem_kv[l], w_pool[l], pool_scale[l], w_o[l], ln1_g[l], ln1_b[l],
                     ln2_g[l], ln2_b[l], w_r1[l], b_r1[l], w_r2[l], b_r2[l], w_gate[l], w_up[l], w_down[l], alpha)
        h_p, h_s = res[0], res[1]
        for lst, val in zip(outs, res[2:]):
            lst.append(val)
    return (h_p, h_s) + tuple(jnp.stack(o) for o in outs)
```

```python
import functools
import math

import jax
import jax.numpy as jnp
from jax import lax
from jax.experimental import pallas as pl
from jax.experimental.pallas import tpu as pltpu

F32 = jnp.float32
BF16 = jnp.bfloat16
NEG_INF = float("-inf")

HEAD_DIM = 64
DILATED_PATTERNS = ((128, 1), (512, 4), (2048, 16))
BAND_BLOCK = 128
RES = max(d for _, d in DILATED_PATTERNS)
POOL_WINDOWS = (2, 4, 8, 16)
N_EXPERT_GROUPS = 4
EXPERTS_PER_GROUP = 8
N_EXPERTS = N_EXPERT_GROUPS * EXPERTS_PER_GROUP
PAST_LEN = 16384
LN_EPS = 1e-5
ATTN_SCALE = HEAD_DIM ** -0.5

LANES = 128
SUBLANES = 8
VMEM_LIMIT = 56 * 1024 * 1024

ROUTER_OFF = N_EXPERT_GROUPS
ATTN_BATCH = 16
DMA_UNROLL = 8

PROJ_TILE = 512
MIX_TILE = 512
EXPERT_TILE = 256
ROW_DMA_TILE = 256


def _cparams(sem):
    return pltpu.CompilerParams(dimension_semantics=sem, vmem_limit_bytes=VMEM_LIMIT)


def _dot(a, b):
    return jnp.dot(a, b, preferred_element_type=F32)


def _dot_nt(a, b):
    return lax.dot_general(a, b, (((1,), (1,)), ((), ())), preferred_element_type=F32)


def _memproj_body(mem_ref, wkT_ref, wv_ref, wvT_ref, kmT_ref, vm_ref, vmT_ref):
    m = mem_ref[0].astype(BF16)
    kmT_ref[0] = _dot_nt(wkT_ref[...], m)
    vm_ref[0] = _dot(m, wv_ref[...])
    vmT_ref[0] = _dot_nt(wvT_ref[...], m)


def _memproj(mem, wkT, wv, wvT):
    B, M, D = mem.shape
    C = wv.shape[1]
    full = lambda shape: pl.BlockSpec(shape, lambda b: (0,) * len(shape))
    return pl.pallas_call(
        _memproj_body,
        grid=(B,),
        in_specs=[pl.BlockSpec((1, M, D), lambda b: (b, 0, 0)), full((C, D)), full((D, C)), full((C, D))],
        out_specs=[pl.BlockSpec((1, C, M), lambda b: (b, 0, 0)),
                   pl.BlockSpec((1, M, C), lambda b: (b, 0, 0)),
                   pl.BlockSpec((1, C, M), lambda b: (b, 0, 0))],
        out_shape=[jax.ShapeDtypeStruct((B, C, M), F32), jax.ShapeDtypeStruct((B, M, C), F32),
                   jax.ShapeDtypeStruct((B, C, M), F32)],
        compiler_params=_cparams(("arbitrary",)),
        name="memproj",
    )(mem, wkT, wv, wvT)


def _proj_body(d_a, d_b, x_ref, perm_ref, w_ref, wkvT_ref, q_ref, k_ref, v_ref, u_ref, qc_ref, kT_ref, vT_ref):
    xb = x_ref[0].astype(BF16)
    ts = xb.shape[0]
    xp = _dot(perm_ref[...], xb).astype(BF16)
    y = _dot(xp, w_ref[:, 0:3 * d_a])
    rows = ts // RES
    for r in range(RES):
        q_ref[0, r] = y[r * rows:(r + 1) * rows, 0:d_a]
        k_ref[0, r] = y[r * rows:(r + 1) * rows, d_a:2 * d_a]
        v_ref[0, r] = y[r * rows:(r + 1) * rows, 2 * d_a:3 * d_a]
    y2 = _dot(xb, w_ref[:, 3 * d_a:])
    u_ref[0] = y2[:, 0:d_b]
    qc_ref[0] = y2[:, d_b:]
    yT = _dot_nt(wkvT_ref[...], xb)
    kT_ref[0] = yT[0:d_a]
    vT_ref[0] = yT[d_a:]


def _residue_major_perm(n):
    dst = jnp.arange(n)
    src = RES * (dst % (n // RES)) + dst // (n // RES)
    return (src[:, None] == jnp.arange(n)[None, :]).astype(BF16)


def _proj(x, w_bf, wkvT_bf, d_a, d_b, d_c, ts):
    B, S, D = x.shape
    d_in = w_bf.shape[1]
    assert ts % (RES * SUBLANES) == 0 and S % ts == 0
    row = lambda n: pl.BlockSpec((1, ts, n), lambda b, i: (b, i, 0))
    col = lambda n: pl.BlockSpec((1, n, ts), lambda b, i: (b, 0, i))
    grp = pl.BlockSpec((1, RES, ts // RES, d_a), lambda b, i: (b, 0, i, 0))
    outs = pl.pallas_call(
        functools.partial(_proj_body, d_a, d_b),
        grid=(B, S // ts),
        in_specs=[row(D), pl.BlockSpec((ts, ts), lambda b, i: (0, 0)), pl.BlockSpec((D, d_in), lambda b, i: (0, 0)),
                  pl.BlockSpec((2 * d_a, D), lambda b, i: (0, 0))],
        out_specs=[grp, grp, grp, row(d_b), row(d_c), col(d_a), col(d_a)],
        out_shape=[jax.ShapeDtypeStruct((B, RES, S // RES, d_a), F32)] * 3
        + [jax.ShapeDtypeStruct((B, S, d_b), F32), jax.ShapeDtypeStruct((B, S, d_c), F32)]
        + [jax.ShapeDtypeStruct((B, d_a, S), F32)] * 2,
        compiler_params=_cparams(("arbitrary", "arbitrary")),
        name="proj",
    )(x, _residue_major_perm(ts), w_bf, wkvT_bf)
    return [o.reshape(B, S, d_a) for o in outs[:3]] + list(outs[3:])


def _attn_body(S, slopes_ref, q_ref, k_ref, v_ref, o_ref, opat_ref, lpat_ref):
    hp = pl.program_id(1)
    blk = BAND_BLOCK
    lane = lax.broadcasted_iota(jnp.int32, (blk, LANES), 1)
    head0 = lane < HEAD_DIM

    for pi, (window, dil) in enumerate(DILATED_PATTERNS):
        n_steps = window // dil
        ngrp = RES // dil
        cs = blk // ngrp
        cs_bits = cs.bit_length() - 1
        assert ngrp * dil == RES and cs * ngrp == blk and cs % SUBLANES == 0 and S == RES * blk
        has_prev = ngrp > 1
        nk = 2 * blk if has_prev else blk
        nat = lambda a, ngrp=ngrp, cs=cs, cs_bits=cs_bits: (a & (cs - 1)) * ngrp + (a >> cs_bits)
        qi = lax.broadcasted_iota(jnp.int32, (blk, nk), 0)
        kj = lax.broadcasted_iota(jnp.int32, (blk, nk), 1)
        if has_prev:
            steps = (nat(qi) + blk) - (nat(kj & (blk - 1)) + (kj & blk))
        else:
            steps = qi - kj
        valid = (steps >= 0) & (steps <= n_steps)
        dist = (steps * dil).astype(F32)
        biases = []
        for hh in range(2):
            slope = slopes_ref[2 * hp + hh]
            biases.append(jnp.where(valid, -slope * dist, NEG_INF))
        prev_cols = kj < blk

        ones_k = jnp.ones((ATTN_BATCH, nk, LANES), BF16)

        def blocks(it, carry, dil=dil, ngrp=ngrp, cs=cs, has_prev=has_prev, biases=biases, prev_cols=prev_cols,
                   pi=pi, nk=nk, ones_k=ones_k):
            def chunk(idx, c, back):
                r = idx // ngrp
                j = jnp.maximum(idx % ngrp - back, 0)
                return pl.ds(pl.multiple_of((r + dil * c) * blk + cs * j, cs), cs)

            def load(ref, idx, back=0):
                return jnp.concatenate([ref[0, chunk(idx, c, back), :] for c in range(ngrp)], axis=0)

            qs, ks, vs, firsts = [], [], [], []
            for b in range(ATTN_BATCH):
                idx = it * ATTN_BATCH + b
                qs.append(load(q_ref, idx))
                if has_prev:
                    ks.append(jnp.concatenate([load(k_ref, idx, 1), load(k_ref, idx)], axis=0))
                    vs.append(jnp.concatenate([load(v_ref, idx, 1), load(v_ref, idx)], axis=0))
                    firsts.append(jnp.where(prev_cols & (idx % ngrp == 0), NEG_INF, 0.0))
                else:
                    ks.append(load(k_ref, idx))
                    vs.append(load(v_ref, idx))
            q3 = jnp.stack(qs) * ATTN_SCALE
            k3 = jnp.stack(ks).astype(BF16)
            v3 = jnp.concatenate([jnp.stack(vs).astype(BF16), ones_k], axis=2)
            outs, lses = [], []
            for hh in range(2):
                qm = jnp.where(head0 if hh == 0 else ~head0, q3, 0.0).astype(BF16)
                s = jnp.einsum("bqd,bkd->bqk", qm, k3, preferred_element_type=F32) + biases[hh]
                if has_prev:
                    s = s + jnp.stack(firsts)
                m = jnp.max(s, axis=2, keepdims=True)
                p = jnp.exp(s - m).astype(BF16)
                o = jnp.einsum("bqk,bkd->bqd", p, v3, preferred_element_type=F32)
                den = o[:, :, LANES:]
                outs.append(o[:, :, :LANES] / den)
                lses.append(m + jnp.log(den))
            o_pair = jnp.where(head0, outs[0], outs[1])
            l_pair = jnp.where(head0, lses[0], lses[1])
            for b in range(ATTN_BATCH):
                idx = it * ATTN_BATCH + b
                for c in range(ngrp):
                    opat_ref[pi, chunk(idx, c, 0), :] = o_pair[b, c * cs:(c + 1) * cs]
                    lpat_ref[pi, chunk(idx, c, 0), :] = l_pair[b, c * cs:(c + 1) * cs]
            return carry

        lax.fori_loop(0, RES // ATTN_BATCH, blocks, 0)

    chunk = 256

    def mix(c, carry):
        rows = pl.ds(pl.multiple_of(c * chunk, chunk), chunk)
        l0, l1, l2 = lpat_ref[0, rows, :], lpat_ref[1, rows, :], lpat_ref[2, rows, :]
        mx = jnp.maximum(jnp.maximum(l0, l1), l2)
        w0, w1, w2 = jnp.exp(l0 - mx), jnp.exp(l1 - mx), jnp.exp(l2 - mx)
        num = w0 * opat_ref[0, rows, :] + w1 * opat_ref[1, rows, :] + w2 * opat_ref[2, rows, :]
        o_ref[0, rows, :] = (num / (w0 + w1 + w2)).astype(o_ref.dtype)
        return carry

    lax.fori_loop(0, S // chunk, mix, 0)


def _attn(q, k, v, slopes):
    B, S, d_a = q.shape
    assert S % (BAND_BLOCK * max(d for _, d in DILATED_PATTERNS)) == 0
    npair = d_a // LANES
    spec = pl.BlockSpec((1, S, LANES), lambda b, h, *_: (b, 0, h))
    grid_spec = pltpu.PrefetchScalarGridSpec(
        num_scalar_prefetch=0,
        grid=(B, npair),
        in_specs=[pl.BlockSpec(memory_space=pltpu.SMEM), spec, spec, spec],
        out_specs=spec,
        scratch_shapes=[pltpu.VMEM((len(DILATED_PATTERNS), S, LANES), F32),
                        pltpu.VMEM((len(DILATED_PATTERNS), S, LANES), F32)],
    )
    return pl.pallas_call(
        functools.partial(_attn_body, S),
        grid_spec=grid_spec,
        out_shape=jax.ShapeDtypeStruct((B, S, d_a), BF16),
        compiler_params=_cparams(("arbitrary", "arbitrary")),
        name="dilated_attn",
    )(slopes, q, k, v)


def _layer_norm(z, g, b):
    mu = jnp.mean(z, axis=-1, keepdims=True)
    zc = z - mu
    var = jnp.mean(zc * zc, axis=-1, keepdims=True)
    return zc * lax.rsqrt(var + LN_EPS) * g + b


def _route(logits):
    n = logits.shape[0]
    lane = lax.broadcasted_iota(jnp.int32, (n, LANES), 1)
    lane_f = lane.astype(F32)
    big = float(LANES)
    is_outer = lane < N_EXPERT_GROUPS
    l1 = jnp.where(is_outer, logits, NEG_INF)
    m1 = jnp.max(l1, axis=1, keepdims=True)
    g_sel = jnp.min(jnp.where(l1 == m1, lane_f, big), axis=1, keepdims=True)
    v1 = 1.0 / jnp.sum(jnp.exp(l1 - m1), axis=1, keepdims=True)
    lo = ROUTER_OFF + g_sel * EXPERTS_PER_GROUP
    in_group = (lane_f >= lo) & (lane_f < lo + EXPERTS_PER_GROUP)
    l2 = jnp.where(in_group, logits, NEG_INF)
    ma = jnp.max(l2, axis=1, keepdims=True)
    ia = jnp.min(jnp.where(l2 == ma, lane_f, big), axis=1, keepdims=True)
    l2b = jnp.where(lane_f == ia, NEG_INF, l2)
    mb = jnp.max(l2b, axis=1, keepdims=True)
    ib = jnp.min(jnp.where(l2b == mb, lane_f, big), axis=1, keepdims=True)
    eb = jnp.exp(mb - ma)
    wa = 1.0 / (1.0 + eb)
    wb = eb / (1.0 + eb)
    return ia - ROUTER_OFF, ib - ROUTER_OFF, v1 * wa, v1 * wb


def _rank_and_meta(e0, e1, g0, g1, carry):
    n = e0.shape[0]
    lane_f = lax.broadcasted_iota(jnp.int32, (n, LANES), 1).astype(F32)
    oh0 = (lane_f == e0).astype(F32)
    oh1 = (lane_f == e1).astype(F32)
    both = oh0 + oh1
    ti = lax.broadcasted_iota(jnp.int32, (n, n), 0)
    tj = lax.broadcasted_iota(jnp.int32, (n, n), 1)
    tri = (tj < ti).astype(BF16)
    before = _dot(tri, both.astype(BF16)) + carry
    r0 = jnp.sum(before * oh0, axis=1, keepdims=True)
    r1 = jnp.sum(before * oh1, axis=1, keepdims=True)
    new_carry = carry + jnp.sum(both, axis=0, keepdims=True)
    lane = lax.broadcasted_iota(jnp.int32, (n, LANES), 1)
    meta = jnp.zeros((n, LANES), F32)
    for i, val in enumerate((e0, e1, g0, g1, r0, r1)):
        meta = jnp.where(lane == i, val, meta)
    return meta, meta.T[0:SUBLANES], new_carry


def _lane_group(shape, width):
    lane = lax.broadcasted_iota(jnp.int32, shape, len(shape) - 1)
    grp = jnp.zeros(shape, jnp.int32)
    for g in range(1, shape[-1] // width):
        grp = grp + (lane >= g * width).astype(jnp.int32)
    return grp


def _select_by_group(grp, vals):
    out = vals[-1]
    for g in range(len(vals) - 2, -1, -1):
        out = jnp.where(grp == g, vals[g], out)
    return out


def _store_token_rows(ref, val):
    n, d = val.shape
    nchunk = d // LANES
    for c in range(nchunk):
        ref[pl.ds(c, n, stride=nchunk), :] = val[:, c * LANES:(c + 1) * LANES]


def _load_token_rows(ref, n, nchunk):
    return jnp.concatenate([ref[pl.ds(c, n, stride=nchunk), :] for c in range(nchunk)], axis=1)


HALO = 16


def _mix_body(tm, nt, n_tiles, d_b, alpha, x_ref, at_ref, unperm_ref, u_ref, uh_ref, qc_ref, kmT_ref, vm_ref,
              wo_ref, wp_ref, ps_ref, g1_ref, b1_ref, wr_ref, br_ref, cin_ref, hs_ref,
              h1_ref, meta_ref, metaT_ref, cnt_ref, carry_ref):
    g = pl.program_id(0)

    @pl.when(g == 0)
    def _():
        carry_ref[...] = cin_ref[...]

    @pl.when(g == n_tiles)
    def _():
        ns_rows = hs_ref.shape[0]
        h1_ref[0:ns_rows, :] = hs_ref[...]
        h1_ref[ns_rows:, :] = jnp.zeros((h1_ref.shape[0] - ns_rows, LANES), F32)

    @pl.when(g < n_tiles)
    def _():
        _mix_tile(tm, g % nt, d_b, alpha, x_ref, at_ref, unperm_ref, u_ref, uh_ref, qc_ref, kmT_ref, vm_ref, wo_ref,
                  wp_ref, ps_ref, g1_ref, b1_ref, wr_ref, br_ref, h1_ref, meta_ref, metaT_ref, cnt_ref, carry_ref)


def _mix_tile(tm, i, d_b, alpha, x_ref, at_ref, unperm_ref, u_ref, uh_ref, qc_ref, kmT_ref, vm_ref, wo_ref, wp_ref,
              ps_ref, g1_ref, b1_ref, wr_ref, br_ref, h1_ref, meta_ref, metaT_ref, cnt_ref, carry_ref):
    u = u_ref[0]
    halo = jnp.where(i == 0, 0.0, uh_ref[0])
    ext = jnp.concatenate([halo, halo, u], axis=0)
    s2 = ext + pltpu.roll(ext, 1, 0)
    s4 = s2 + pltpu.roll(s2, 2, 0)
    s8 = s4 + pltpu.roll(s4, 4, 0)
    s16 = s8 + pltpu.roll(s8, 8, 0)
    grp = _lane_group((tm, d_b), d_b // len(POOL_WINDOWS))
    sums = [s[2 * HALO:] for s in (s2, s4, s8, s16)]
    win = _select_by_group(grp, sums)
    wlen = _select_by_group(grp, [jnp.int32(w) for w in POOL_WINDOWS])
    pos = i * tm + lax.broadcasted_iota(jnp.int32, (tm, d_b), 0)
    cnt = jnp.minimum(wlen, pos + 1).astype(F32)
    diff = win / cnt - u
    pool = _dot(diff.astype(BF16), wp_ref[...]) * ps_ref[...]

    qc = qc_ref[0] * ATTN_SCALE
    d_c = qc.shape[1]
    kmT = kmT_ref[0].astype(BF16)
    vm = vm_ref[0].astype(BF16)
    hl = _lane_group((tm, d_c), HEAD_DIM)
    memo = jnp.zeros((tm, d_c), F32)
    for h in range(d_c // HEAD_DIM):
        qm = jnp.where(hl == h, qc, 0.0).astype(BF16)
        s = _dot(qm, kmT)
        p = jnp.exp(s - jnp.max(s, axis=1, keepdims=True))
        den = jnp.sum(p, axis=1, keepdims=True)
        o = _dot(p.astype(BF16), vm) / den
        memo = jnp.where(hl == h, o, memo)

    at_rl = jnp.concatenate([at_ref[0, r] for r in range(RES)], axis=0)
    attn = _dot(unperm_ref[...], at_rl).astype(BF16)
    cat = jnp.concatenate([attn, pool.astype(BF16), memo.astype(BF16)], axis=1)
    mixv = _dot(cat, wo_ref[...])
    h1 = _layer_norm(alpha * x_ref[0] + mixv, g1_ref[...], b1_ref[...])
    _store_token_rows(h1_ref, h1)

    logits = _dot(h1.astype(BF16), wr_ref[...]) + br_ref[...]
    e0, e1, g0, g1 = _route(logits)
    meta, metaT, new_carry = _rank_and_meta(e0, e1, g0, g1, carry_ref[0:1, :])
    meta_ref[...] = meta
    metaT_ref[...] = metaT
    carry_ref[...] = jnp.broadcast_to(new_carry, carry_ref.shape)
    cnt_ref[...] = carry_ref[...]


def _mix(x, attn_rm, u, qc, kmT, vm, wo_bf, wp_bd_bf, pool_scale, g1, b1, wr, br, counts_in, h1s, alpha, tm):
    B, S, D = x.shape
    d_a, d_b, d_c = attn_rm.shape[2], u.shape[2], qc.shape[2]
    n_mem = vm.shape[1]
    nt = S // tm
    n_tiles = B * nt
    nchunk = D // LANES
    assert tm % (RES * SUBLANES * 2) == 0 and h1s.shape[0] < tm * nchunk
    last = n_tiles - 1
    bi = lambda g: (jnp.minimum(g, last) // nt, jnp.minimum(g, last) % nt)
    row = lambda n: pl.BlockSpec((1, tm, n), lambda g: bi(g) + (0,))
    full = lambda shape: pl.BlockSpec(shape, lambda g: (0,) * len(shape))
    per_b = lambda shape: pl.BlockSpec((1,) + shape, lambda g: (bi(g)[0], 0, 0))
    halo_spec = pl.BlockSpec((1, HALO, d_b), lambda g: (bi(g)[0], jnp.maximum(bi(g)[1] * (tm // HALO) - 1, 0), 0))
    at_spec = pl.BlockSpec((1, RES, tm // RES, d_a), lambda g: (bi(g)[0], 0, bi(g)[1], 0))
    dst = jnp.arange(tm)
    unperm = ((dst % RES) * (tm // RES) + dst // RES)[:, None] == jnp.arange(tm)[None, :]
    return pl.pallas_call(
        functools.partial(_mix_body, tm, nt, n_tiles, d_b, alpha),
        grid=(n_tiles + 1,),
        in_specs=[row(D), at_spec, full((tm, tm)), row(d_b), halo_spec, row(d_c), per_b((d_c, n_mem)),
                  per_b((n_mem, d_c)), full((D, D)), full((d_b, d_b)), full((1, d_b)), full((1, D)), full((1, D)),
                  full((D, LANES)), full((1, LANES)), full((SUBLANES, LANES)), full(h1s.shape)],
        out_specs=[pl.BlockSpec((tm * nchunk, LANES), lambda g: (g, 0)),
                   pl.BlockSpec((tm, LANES), lambda g: (jnp.minimum(g, last), 0)),
                   pl.BlockSpec((SUBLANES, tm), lambda g: (0, jnp.minimum(g, last))),
                   pl.BlockSpec((SUBLANES, LANES), lambda g: (0, 0))],
        out_shape=[jax.ShapeDtypeStruct(((n_tiles + 1) * tm * nchunk, LANES), F32),
                   jax.ShapeDtypeStruct((B * S, LANES), F32),
                   jax.ShapeDtypeStruct((SUBLANES, B * S), F32),
                   jax.ShapeDtypeStruct((SUBLANES, LANES), F32)],
        scratch_shapes=[pltpu.VMEM((SUBLANES, LANES), F32)],
        compiler_params=_cparams(("arbitrary",)),
        name="mix_ln1_router",
    )(x, attn_rm.reshape(B, RES, S // RES, d_a), unperm.astype(BF16), u, u, qc, kmT, vm, wo_bf, wp_bd_bf,
      pool_scale, g1, b1, wr, br, counts_in, h1s)


def _columns_to_lanes(cols, rows):
    lane = lax.broadcasted_iota(jnp.int32, (rows, LANES), 1)
    tile = jnp.zeros((rows, LANES), F32)
    for t, c in enumerate(cols):
        tile = jnp.where(lane == t, c, tile)
    return tile


def _sproj_body(x_ref, wT_ref, o_ref):
    o_ref[...] = _dot_nt(wT_ref[...], x_ref[...].astype(BF16))


def _sproj(x, wT_bf):
    vm = pl.BlockSpec(memory_space=pltpu.VMEM)
    return pl.pallas_call(
        _sproj_body, in_specs=[vm, vm], out_specs=vm,
        out_shape=jax.ShapeDtypeStruct((wT_bf.shape[0], x.shape[0]), F32),
        compiler_params=pltpu.CompilerParams(vmem_limit_bytes=VMEM_LIMIT),
        name="decode_proj",
    )(x, wT_bf)


def _decode_body(T, wb, qc_row0, slopes_ref, pT_ref, ck_ref, cv_ref, mk_ref, mv_ref,
                 nk_ref, nv_ref, at_ref, mo_ref, qkv_ref, sfar_ref, snear_ref, snew_ref):
    H = ck_ref.shape[1]
    d_a = H * HEAD_DIM
    ntok = pT_ref.shape[1]
    shift = (ntok - T * pl.program_id(0)) % ntok
    qkv_ref[...] = pltpu.roll(pT_ref[...], shift, 1)

    def member(dist, window, dil):
        assert dil & (dil - 1) == 0
        return ((dist & (dil - 1)) == 0) & (dist <= window) & (dist >= 0)

    OWN = min(d for _, d in DILATED_PATTERNS if d > 1)
    far = wb - LANES
    assert T <= OWN and far % OWN == 0 and wb % OWN == 0
    assert all(w <= LANES for w, d in DILATED_PATTERNS if d == 1) and all(d % OWN == 0 for _, d in DILATED_PATTERNS if d > 1)
    R = H * T
    assert T & (T - 1) == 0
    query = lambda shape: lax.broadcasted_iota(jnp.int32, shape, 0) & (T - 1)
    lane_far = lax.broadcasted_iota(jnp.int32, (R, far), 1)
    t_far = query((R, far))
    owned = (lane_far & (OWN - 1)) == t_far
    dist_far = wb + t_far - lane_far
    dist_near = wb + query((R, LANES)) - (far + lax.broadcasted_iota(jnp.int32, (R, LANES), 1))
    dist_new = query((R, T)) - lax.broadcasted_iota(jnp.int32, (R, T), 1)
    masks = [(None if dil == 1 else owned & member(dist_far, window, dil),
              member(dist_near, window, dil), member(dist_new, window, dil)) for window, dil in DILATED_PATTERNS]
    dist_far = dist_far.astype(F32)
    dist_near = dist_near.astype(F32)
    dist_new = jnp.maximum(dist_new, 0).astype(F32)
    rnd = lambda a: a.astype(BF16).astype(F32)

    lane_t = lax.broadcasted_iota(jnp.int32, (HEAD_DIM, LANES), 1)
    own_t = lane_t & (OWN - 1)
    last = wb - LANES
    n_far = far // LANES

    tile = lambda a, c: a[:, c * LANES:(c + 1) * LANES]

    G = SUBLANES // T
    assert G * T == SUBLANES and H % G == 0
    group_rows = lambda g: pl.ds(pl.multiple_of(g * SUBLANES, SUBLANES), SUBLANES)

    def scores(g, carry):
        parts = [head_scores(g * G + i) for i in range(G)]
        sfar_ref[group_rows(g), :] = jnp.concatenate([p[0] for p in parts], axis=0)
        snear_ref[group_rows(g), :] = jnp.concatenate([p[1] for p in parts], axis=0)
        snew_ref[group_rows(g), :] = jnp.concatenate([p[2] for p in parts], axis=0)
        return carry

    def head_scores(h):
        slope = slopes_ref[h]
        kTc = ck_ref[0, h]
        vTc = cv_ref[0, h]
        r0 = pl.multiple_of(h * HEAD_DIM, HEAD_DIM)
        qT = rnd(qkv_ref[pl.ds(r0, HEAD_DIM), :][:, 0:T] * ATTN_SCALE)
        kTn = qkv_ref[pl.ds(d_a + r0, HEAD_DIM), :][:, 0:T]
        vTn = qkv_ref[pl.ds(2 * d_a + r0, HEAD_DIM), :][:, 0:T]
        kTn_r = rnd(kTn)

        qb = [jnp.broadcast_to(qT[:, t:t + 1], (HEAD_DIM, LANES)) for t in range(T)]
        qpat = qb[T - 1]
        for t in range(T - 1):
            qpat = jnp.where(own_t == t, qb[t], qpat)
        s_far = jnp.concatenate([jnp.sum(rnd(tile(kTc, c)) * qpat, axis=0, keepdims=True) for c in range(n_far)],
                                axis=1)
        s_far = s_far - slope * dist_far[0:T]
        k_near = rnd(tile(kTc, n_far))
        s_near = jnp.concatenate([jnp.sum(k_near * qb[t], axis=0, keepdims=True) for t in range(T)], axis=0)
        s_near = s_near - slope * dist_near[0:T]
        s_new = jnp.concatenate([jnp.sum(kTn_r * qT[:, t:t + 1], axis=0, keepdims=True) for t in range(T)], axis=0)
        s_new = s_new - slope * dist_new[0:T]

        rk = pltpu.roll(kTc, wb - T, 1)
        rv = pltpu.roll(vTc, wb - T, 1)
        nk_ref[0, h] = rk
        nv_ref[0, h] = rv
        newk = jnp.zeros((HEAD_DIM, LANES), F32)
        newv = jnp.zeros((HEAD_DIM, LANES), F32)
        for t in range(T):
            newk = jnp.where(lane_t == LANES - T + t, kTn[:, t:t + 1], newk)
            newv = jnp.where(lane_t == LANES - T + t, vTn[:, t:t + 1], newv)
        nk_ref[0, h, :, last:] = jnp.where(lane_t >= LANES - T, newk, rk[:, last:])
        nv_ref[0, h, :, last:] = jnp.where(lane_t >= LANES - T, newv, rv[:, last:])
        return s_far, s_near, s_new

    lax.fori_loop(0, H // G, scores, 0)

    s_far, s_near, s_new = sfar_ref[...], snear_ref[...], snew_ref[...]
    rmax = lambda a: jnp.max(a, axis=1, keepdims=True)
    rsum = lambda a: jnp.sum(a, axis=1, keepdims=True)
    parts, lses = [], []
    for m_far, m_near, m_new in masks:
        sn, sw = jnp.where(m_near, s_near, NEG_INF), jnp.where(m_new, s_new, NEG_INF)
        m = jnp.maximum(rmax(sn), rmax(sw))
        if m_far is not None:
            sf = jnp.where(m_far, s_far, NEG_INF)
            m = jnp.maximum(m, rmax(sf))
        en, ew = jnp.exp(sn - m), jnp.exp(sw - m)
        den = rsum(en) + rsum(ew)
        ef = None
        if m_far is not None:
            ef = jnp.exp(sf - m)
            den = den + rsum(ef)
            ef = rnd(ef)
        parts.append((ef, rnd(en), rnd(ew), den))
        lses.append(m + jnp.log(den))
    top = functools.reduce(jnp.maximum, lses)
    mixw = [jnp.exp(l - top) for l in lses]
    total = functools.reduce(lambda a, b: a + b, mixw)
    p_far = p_near = p_new = None
    for (ef, en, ew, den), w in zip(parts, mixw):
        scale = w / (total * den)
        p_near = en * scale if p_near is None else p_near + en * scale
        p_new = ew * scale if p_new is None else p_new + ew * scale
        if ef is not None:
            p_far = ef * scale if p_far is None else p_far + ef * scale
    sfar_ref[...] = p_far
    snear_ref[...] = p_near
    snew_ref[...] = p_new

    def values(g, carry):
        pf, pn, pw = sfar_ref[group_rows(g), :], snear_ref[group_rows(g), :], snew_ref[group_rows(g), :]
        for i in range(G):
            head_values(g * G + i, pf[i * T:(i + 1) * T], pn[i * T:(i + 1) * T], pw[i * T:(i + 1) * T])
        return carry

    def head_values(h, p_far_h, p_near_h, p_new_h):
        vTc = cv_ref[0, h]
        r0 = pl.multiple_of(h * HEAD_DIM, HEAD_DIM)
        vTn_r = rnd(qkv_ref[pl.ds(2 * d_a + r0, HEAD_DIM), :][:, 0:T])
        p_comb = jnp.sum(p_far_h, axis=0, keepdims=True)
        acc = rnd(tile(vTc, 0)) * tile(p_comb, 0)
        for c in range(1, n_far):
            acc = acc + rnd(tile(vTc, c)) * tile(p_comb, c)
        v_near = rnd(tile(vTc, n_far))
        cols = []
        for t in range(T):
            cols.append(jnp.sum(jnp.where(own_t == t, acc, 0.0) + v_near * p_near_h[t:t + 1], axis=1, keepdims=True)
                        + jnp.sum(vTn_r * p_new_h[t:t + 1], axis=1, keepdims=True))
        at_ref[0, pl.ds(r0, HEAD_DIM), :] = _columns_to_lanes(cols, HEAD_DIM)

    lax.fori_loop(0, H // G, values, 0)

    Hc = mk_ref.shape[1]
    rows = []
    for h in range(Hc):
        kT = rnd(mk_ref[0, h])
        r0 = h * HEAD_DIM
        qT = rnd(qkv_ref[qc_row0 + r0:qc_row0 + r0 + HEAD_DIM, :][:, 0:T] * ATTN_SCALE)
        rows += [jnp.sum(kT * qT[:, t:t + 1], axis=0, keepdims=True) for t in range(T)]
    s = jnp.concatenate(rows, axis=0)
    p = jnp.exp(s - jnp.max(s, axis=1, keepdims=True))
    p = rnd(p / jnp.sum(p, axis=1, keepdims=True))
    for h in range(Hc):
        vT = rnd(mv_ref[0, h])
        cols = [jnp.sum(vT * p[h * T + t:h * T + t + 1], axis=1, keepdims=True) for t in range(T)]
        mo_ref[0, h * HEAD_DIM:(h + 1) * HEAD_DIM, :] = _columns_to_lanes(cols, HEAD_DIM)


def _decode(projT, T, qc_row0, ck, cv, mk, mv, slopes):
    DB, H, hd, wb = ck.shape
    _, Hc, _, n_mem = mk.shape
    d_a, d_c = H * hd, Hc * hd
    assert wb >= max(w for w, _ in DILATED_PATTERNS) and wb % LANES == 0 and T <= LANES
    assert projT.shape[1] == DB * T and projT.shape[0] >= qc_row0 + d_c
    cache = pl.BlockSpec((1, H, hd, wb), lambda b, *_: (b, 0, 0, 0))
    memc = pl.BlockSpec((1, Hc, hd, n_mem), lambda b, *_: (b, 0, 0, 0))
    grid_spec = pltpu.PrefetchScalarGridSpec(
        num_scalar_prefetch=0,
        grid=(DB,),
        in_specs=[pl.BlockSpec(memory_space=pltpu.SMEM),
                  pl.BlockSpec(projT.shape, lambda b, *_: (0, 0)),
                  cache, cache, memc, memc],
        out_specs=[cache, cache,
                   pl.BlockSpec((1, d_a, LANES), lambda b, *_: (b, 0, 0)),
                   pl.BlockSpec((1, d_c, LANES), lambda b, *_: (b, 0, 0))],
        scratch_shapes=[pltpu.VMEM(projT.shape, F32), pltpu.VMEM((H * T, wb - LANES), F32),
                        pltpu.VMEM((H * T, LANES), F32), pltpu.VMEM((H * T, T), F32)],
    )
    nk, nv, at, mo = pl.pallas_call(
        functools.partial(_decode_body, T, wb, qc_row0),
        grid_spec=grid_spec,
        out_shape=[jax.ShapeDtypeStruct(ck.shape, F32), jax.ShapeDtypeStruct(cv.shape, F32),
                   jax.ShapeDtypeStruct((DB, d_a, LANES), F32), jax.ShapeDtypeStruct((DB, d_c, LANES), F32)],
        compiler_params=_cparams(("arbitrary",)),
        name="decode_attn_cache",
    )(slopes, projT, ck, cv, mk, mv)
    return nk, nv, at[:, :, :T], mo[:, :, :T]


def _smix_body(T, alpha, pos0, x_ref, at_ref, mo_ref, st_ref, wu_ref, wo_ref, wp_ref, ps_ref, g1_ref, b1_ref,
               wr_ref, br_ref, cin_ref, h1_ref, meta_ref, metaT_ref, cnt_ref, pool_ref):
    n = x_ref.shape[0]
    db = n // T
    pb = st_ref.shape[0]
    d_b = st_ref.shape[2]
    x = x_ref[...]
    bdot = lambda a, b: _dot(a.astype(BF16), b.astype(BF16))
    u_new = bdot(x, wu_ref[...])
    seq = [st_ref[j] for j in range(pb)] + [u_new[t * db:(t + 1) * db] for t in range(T)]
    for j in range(pb):
        pool_ref[j] = seq[j + T]
    grp = _lane_group((db, d_b), d_b // len(POOL_WINDOWS))
    diffs = []
    for t in range(T):
        j = pb + t
        per_w = []
        for w in POOL_WINDOWS:
            acc = seq[j]
            for back in range(1, w):
                if j - back >= 0:
                    acc = acc + seq[j - back]
            per_w.append(acc / float(min(w, pos0 + j + 1)))
        diffs.append(_select_by_group(grp, per_w) - seq[j])
    diff = jnp.concatenate(diffs, axis=0)
    pool = bdot(diff, wp_ref[...]) * ps_ref[...]
    cat = jnp.concatenate([at_ref[...], pool, mo_ref[...]], axis=1)
    mixv = bdot(cat, wo_ref[...])
    h1 = _layer_norm(alpha * x + mixv, g1_ref[...], b1_ref[...])
    _store_token_rows(h1_ref, jnp.concatenate([h1, h1], axis=0))
    logits_lo = bdot(h1, wr_ref[...])
    logits_hi = jnp.dot(h1, wr_ref[...], preferred_element_type=F32, precision=lax.Precision.HIGHEST)
    logits = jnp.concatenate([logits_lo, logits_hi], axis=0) + br_ref[...]
    e0, e1, g0, g1 = _route(logits)
    meta, metaT, new_carry = _rank_and_meta(e0, e1, g0, g1, cin_ref[0:1, :])
    meta_ref[...] = meta
    metaT_ref[...] = metaT
    cnt_ref[...] = jnp.broadcast_to(new_carry, cnt_ref.shape)


def _smix(x_tb, attn_tb, memo_tb, state, wu, wo, wp_bd, pool_scale, g1, b1, wr, br, counts_in, alpha, T):
    n, D = x_tb.shape
    pb, db, d_b = state.shape
    nchunk = D // LANES
    vm = pl.BlockSpec(memory_space=pltpu.VMEM)
    return pl.pallas_call(
        functools.partial(_smix_body, T, alpha, PAST_LEN - pb),
        in_specs=[vm] * 13,
        out_specs=[vm] * 5,
        out_shape=[jax.ShapeDtypeStruct((2 * n * nchunk, LANES), F32), jax.ShapeDtypeStruct((2 * n, LANES), F32),
                   jax.ShapeDtypeStruct((SUBLANES, 2 * n), F32),
                   jax.ShapeDtypeStruct((SUBLANES, LANES), F32), jax.ShapeDtypeStruct((pb, db, d_b), F32)],
        compiler_params=pltpu.CompilerParams(vmem_limit_bytes=VMEM_LIMIT),
        name="decode_mix_ln1_router",
    )(x_tb, attn_tb, memo_tb, state, wu, wo, wp_bd, pool_scale, g1, b1, wr, br, counts_in)


def _slot_map_body(blk, pos0_ref, pos1_ref, fill_ref, inv_ref, sem):
    g = pl.program_id(0)

    @pl.when(g == 0)
    def _():
        c = pltpu.make_async_copy(fill_ref, inv_ref, sem)
        c.start()
        c.wait()

    group = math.gcd(blk, 16)

    def body(j, carry):
        token0 = g * blk + j * group
        for u in range(group):
            inv_ref[pos0_ref[0, 0, j * group + u]] = token0 + u
            inv_ref[pos1_ref[0, 0, j * group + u]] = token0 + u
        return carry

    lax.fori_loop(0, blk // group, body, 0)


def _slot_map(pos2, n_slots, zero_token, blk):
    n = pos2.shape[1]
    assert n % blk == 0 and n_slots % 1024 == 0
    tokens = pl.BlockSpec((1, 1, blk), lambda g: (g, 0, 0), memory_space=pltpu.SMEM)
    return pl.pallas_call(
        functools.partial(_slot_map_body, blk),
        grid=(n // blk,),
        in_specs=[tokens, tokens, pl.BlockSpec(memory_space=pl.ANY)],
        out_specs=pl.BlockSpec(memory_space=pltpu.SMEM),
        out_shape=jax.ShapeDtypeStruct((n_slots,), jnp.int32),
        scratch_shapes=[pltpu.SemaphoreType.DMA(())],
        compiler_params=_cparams(("arbitrary",)),
        name="slot_map",
    )(pos2[0].reshape(n // blk, 1, blk), pos2[1].reshape(n // blk, 1, blk),
      jnp.full((n_slots,), zero_token, jnp.int32))


def _expert_body(tmx, nchunk, te_ref, nt_ref, tv_ref, cur_ref, nxt_ref, h_ref, wg_ref, wu_ref, wd_ref, ys_ref,
                 buf_a, buf_b, wg_bf, wu_bf, wd_bf, sems):
    i = pl.program_id(0)
    n_tiles = nt_ref[0]

    def gather(idx_ref, buf, s, tile, wait):
        groups = (tv_ref[tile] + DMA_UNROLL - 1) // DMA_UNROLL

        def body(g, carry):
            for u in range(DMA_UNROLL):
                j = g * DMA_UNROLL + u
                dst = buf.at[pl.ds(pl.multiple_of(j * nchunk, nchunk), nchunk)]
                copy = pltpu.make_async_copy(h_ref.at[idx_ref[0, 0, j]], dst, sems.at[s])
                if wait:
                    copy.wait()
                else:
                    copy.start(priority=u % 2)
            return carry

        lax.fori_loop(0, groups, body, 0)

    @pl.when(i == 0)
    def _():
        buf_a[...] = jnp.zeros_like(buf_a)
        buf_b[...] = jnp.zeros_like(buf_b)
        gather(cur_ref, buf_a, 0, 0, False)

    it = jnp.minimum(i, pl.num_programs(0) - 2)
    @pl.when((i < n_tiles) & ((i == 0) | (te_ref[it] != te_ref[jnp.maximum(it - 1, 0)])))
    def _():
        wg_bf[...] = wg_ref[0].astype(BF16)
        wu_bf[...] = wu_ref[0].astype(BF16)
        wd_bf[...] = wd_ref[0].astype(BF16)

    def step(cur, s_cur, nxt, s_nxt):
        gather(cur_ref, cur, s_cur, it, True)

        @pl.when(i + 1 < n_tiles)
        def _():
            gather(nxt_ref, nxt, s_nxt, it + 1, False)

        x = _load_token_rows(cur, tmx, nchunk).astype(BF16)
        hg = _dot(x, wg_bf[...])
        hu = _dot(x, wu_bf[...])
        a = (hg * jax.nn.sigmoid(hg) * hu).astype(BF16)
        y = _dot(a, wd_bf[...])
        _store_token_rows(ys_ref, y)

    @pl.when((i < n_tiles) & (i % 2 == 0))
    def _():
        step(buf_a, 0, buf_b, 1)

    @pl.when((i < n_tiles) & (i % 2 == 1))
    def _():
        step(buf_b, 1, buf_a, 0)

    @pl.when(i >= n_tiles)
    def _():
        ys_ref[...] = jnp.zeros_like(ys_ref)


def _experts(tile_expert, n_tiles, tile_valid, inv, h_rows, wg, wu, wd, tmx, max_tiles):
    E, D, F = wg.shape
    nchunk = D // LANES
    inv3 = inv.reshape(-1, 1, tmx)
    last_blk = inv3.shape[0] - 1
    assert last_blk >= max_tiles and tmx % DMA_UNROLL == 0
    weights = lambda shape: pl.BlockSpec((1,) + shape, lambda i, te, *_: (te[jnp.minimum(i, max_tiles - 1)], 0, 0))
    rows = pltpu.VMEM((tmx * nchunk, LANES), F32)
    grid_spec = pltpu.PrefetchScalarGridSpec(
        num_scalar_prefetch=3,
        grid=(max_tiles + 1,),
        in_specs=[pl.BlockSpec((1, 1, tmx), lambda i, *_: (i, 0, 0), memory_space=pltpu.SMEM),
                  pl.BlockSpec((1, 1, tmx), lambda i, *_: (jnp.minimum(i + 1, last_blk), 0, 0),
                               memory_space=pltpu.SMEM),
                  pl.BlockSpec(memory_space=pl.ANY), weights((D, F)), weights((D, F)), weights((F, D))],
        out_specs=pl.BlockSpec((tmx * nchunk, LANES), lambda i, *_: (i, 0)),
        scratch_shapes=[rows, rows, pltpu.VMEM((D, F), BF16), pltpu.VMEM((D, F), BF16), pltpu.VMEM((F, D), BF16),
                        pltpu.SemaphoreType.DMA((2,))],
    )
    return pl.pallas_call(
        functools.partial(_expert_body, tmx, nchunk),
        grid_spec=grid_spec,
        out_shape=jax.ShapeDtypeStruct(((max_tiles + 1) * tmx * nchunk, LANES), F32),
        compiler_params=_cparams(("arbitrary",)),
        name="expert_swiglu",
    )(tile_expert, n_tiles, tile_valid, inv3, inv3, h_rows, wg, wu, wd)


def _combine_body(tm, nchunk, alpha, pos_ref, nxt_ref, h1_ref, meta_ref, ys_ref, g2_ref, b2_ref, o_ref, buf_a, buf_b,
                  sems):
    i = pl.program_id(0)
    last = pl.num_programs(0) - 1

    def copy(idx_ref, buf, s, t, k):
        dst = buf.at[k, pl.ds(pl.multiple_of(t * nchunk, nchunk), nchunk)]
        return pltpu.make_async_copy(ys_ref.at[idx_ref[0, 0, k * tm + t]], dst, sems.at[s])

    def start(idx_ref, buf, s):
        def body(t, carry):
            for k in range(2):
                copy(idx_ref, buf, s, t, k).start(priority=k)
            return carry

        lax.fori_loop(0, tm, body, 0, unroll=DMA_UNROLL)

    def wait(buf, s):
        def body(t, carry):
            for k in range(2):
                copy(pos_ref, buf, s, t, k).wait()
            return carry

        lax.fori_loop(0, tm, body, 0, unroll=DMA_UNROLL)

    @pl.when(i == 0)
    def _():
        start(pos_ref, buf_a, 0)

    def step(cur, s_cur, nxt, s_nxt):
        wait(cur, s_cur)
        start(nxt_ref, nxt, s_nxt)
        h1 = _load_token_rows(h1_ref, tm, nchunk)
        y0 = _load_token_rows(cur.at[0], tm, nchunk)
        y1 = _load_token_rows(cur.at[1], tm, nchunk)
        meta = meta_ref[...]
        lane = lax.broadcasted_iota(jnp.int32, meta.shape, 1)
        gate0 = jnp.sum(jnp.where(lane == 2, meta, 0.0), axis=1, keepdims=True)
        gate1 = jnp.sum(jnp.where(lane == 3, meta, 0.0), axis=1, keepdims=True)
        f = gate0 * y0 + gate1 * y1
        o_ref[...] = _layer_norm(alpha * h1 + f, g2_ref[...], b2_ref[...])

        @pl.when(i == last)
        def _():
            wait(nxt, s_nxt)

    @pl.when(i % 2 == 0)
    def _():
        step(buf_a, 0, buf_b, 1)

    @pl.when(i % 2 == 1)
    def _():
        step(buf_b, 1, buf_a, 0)


def _combine(pos2, tok0, n, h1_flat, meta, ys, g2, b2, alpha, tm):
    D = g2.shape[1]
    nchunk = D // LANES
    assert n % tm == 0 and tok0 % tm == 0 and pos2.shape == (2, n)
    pos3 = jnp.transpose(pos2.reshape(2, -1, tm), (1, 0, 2)).reshape(-1, 1, 2 * tm)
    off = tok0 // tm
    last = n // tm - 1
    return pl.pallas_call(
        functools.partial(_combine_body, tm, nchunk, alpha),
        grid=(n // tm,),
        in_specs=[pl.BlockSpec((1, 1, 2 * tm), lambda i: (i, 0, 0), memory_space=pltpu.SMEM),
                  pl.BlockSpec((1, 1, 2 * tm), lambda i: (jnp.minimum(i + 1, last), 0, 0),
                               memory_space=pltpu.SMEM),
                  pl.BlockSpec((tm * nchunk, LANES), lambda i: (i + off, 0)),
                  pl.BlockSpec((tm, LANES), lambda i: (i, 0)),
                  pl.BlockSpec(memory_space=pl.ANY),
                  pl.BlockSpec((1, D), lambda i: (0, 0)), pl.BlockSpec((1, D), lambda i: (0, 0))],
        out_specs=pl.BlockSpec((tm, D), lambda i: (i, 0)),
        out_shape=jax.ShapeDtypeStruct((n, D), F32),
        scratch_shapes=[pltpu.VMEM((2, tm * nchunk, LANES), F32), pltpu.VMEM((2, tm * nchunk, LANES), F32),
                        pltpu.SemaphoreType.DMA((2,))],
        compiler_params=_cparams(("arbitrary",)),
        name="combine_ln2",
    )(pos3, pos3, h1_flat, meta, ys, g2, b2)


def _block_diag(w):
    g, a, b = w.shape
    eye = jnp.eye(g, dtype=w.dtype)
    return (eye[:, None, :, None] * w[:, :, None, :]).reshape(g * a, g * b)


def _layer(h_p, h_s, win_k, win_v, pool_st, mem_k, mem_v, mem_prompt,
           w_in, w_mem_kv, w_pool, pool_scale, w_o, ln1_g, ln1_b, ln2_g, ln2_b,
           w_r1, b_r1, w_r2, b_r2, w_gate, w_up, w_down, alpha):
    B, S, D = h_p.shape
    DB, T, _ = h_s.shape
    H = win_k.shape[2]
    Hc = mem_k.shape[2]
    d_a, d_c = H * HEAD_DIM, Hc * HEAD_DIM
    d_b = pool_st.shape[2]
    nchunk = D // LANES
    slopes = 2.0 ** (-8.0 * jnp.arange(1, H + 1, dtype=F32) / H)

    w_in_bf = w_in.astype(BF16)
    w_inT_bf = w_in_bf.T
    wkvT_bf = w_inT_bf[d_a:3 * d_a]
    wp_bd = _block_diag(w_pool)
    ps = pool_scale.reshape(1, d_b)
    g1, b1 = ln1_g.reshape(1, D), ln1_b.reshape(1, D)
    g2, b2 = ln2_g.reshape(1, D), ln2_b.reshape(1, D)
    n_r = N_EXPERT_GROUPS + N_EXPERTS
    wr = jnp.concatenate([w_r1, jnp.transpose(w_r2, (1, 0, 2)).reshape(D, N_EXPERTS),
                          jnp.zeros((D, LANES - n_r), F32)], axis=1)
    br = jnp.concatenate([b_r1, b_r2.reshape(-1), jnp.zeros((LANES - n_r,), F32)]).reshape(1, LANES)

    ck = jnp.transpose(win_k, (0, 2, 3, 1))
    cv = jnp.transpose(win_v, (0, 2, 3, 1))
    mk = jnp.transpose(mem_k, (0, 2, 3, 1))
    mv = jnp.transpose(mem_v, (0, 2, 3, 1))
    projT = _sproj(h_s.reshape(DB * T, D), w_inT_bf)
    nk, nv, attn_sT, memo_sT = _decode(projT, T, 3 * d_a + d_b, ck, cv, mk, mv, slopes)
    to_tb = lambda a: jnp.transpose(a, (2, 0, 1)).reshape(T * DB, a.shape[1])
    x_tb = jnp.transpose(h_s, (1, 0, 2)).reshape(T * DB, D)
    state = jnp.transpose(pool_st, (1, 0, 2))
    h1s, meta_s, metaT_s, cnt_s, new_pool = _smix(x_tb, to_tb(attn_sT), to_tb(memo_sT), state,
                                         w_in[:, 3 * d_a:3 * d_a + d_b], w_o, wp_bd, ps, g1, b1, wr, br,
                                         jnp.zeros((SUBLANES, LANES), F32), alpha, T)

    w_memT = w_mem_kv.T
    kmT, vm, vmT = _memproj(mem_prompt, w_memT[:d_c].astype(BF16), w_mem_kv[:, d_c:].astype(BF16),
                            w_memT[d_c:].astype(BF16))
    q, k, v, u, qc, kT, vT = _proj(h_p, w_in_bf, wkvT_bf, d_a, d_b, d_c, ts=math.gcd(S, PROJ_TILE))
    attn = _attn(q, k, v, slopes)
    h_rows, meta_p, metaT_p, cnt_all = _mix(h_p, attn, u, qc, kmT, vm, w_o.astype(BF16), wp_bd.astype(BF16), ps, g1, b1,
                                   wr.astype(BF16), br, cnt_s, h1s, alpha, tm=math.gcd(S, MIX_TILE))

    tmx = EXPERT_TILE
    n_p, n_s = B * S, 2 * DB * T
    n = n_p + n_s
    counts = cnt_all[0, :N_EXPERTS].astype(jnp.int32)
    padded = (counts + tmx - 1) // tmx * tmx
    seg_end = jnp.cumsum(padded)
    seg_off = seg_end - padded
    metaT = jnp.concatenate([metaT_p, metaT_s], axis=1)
    e_ids = metaT[0:2].astype(jnp.int32)
    experts = jnp.arange(N_EXPERTS).reshape((N_EXPERTS,) + (1,) * 2)
    lookup = lambda table, idx: jnp.sum(
        jnp.where(idx.reshape((1,) * (3 - idx.ndim) + idx.shape) == experts, table.reshape(experts.shape), 0),
        axis=0).reshape(idx.shape)
    pos2 = lookup(seg_off, e_ids) + metaT[4:6].astype(jnp.int32)
    max_tiles = (2 * n) // tmx + N_EXPERTS
    n_tiles = (seg_end[-1] // tmx).astype(jnp.int32).reshape(1)
    tile_row0 = jnp.arange(max_tiles, dtype=jnp.int32) * tmx
    tile_expert = jnp.sum((tile_row0[:, None] >= seg_end[None, :]).astype(jnp.int32), axis=1)
    tile_expert = jnp.minimum(tile_expert, N_EXPERTS - 1)

    n_slots = -(-((max_tiles + 1) * tmx) // 1024) * 1024
    inv = _slot_map(pos2, n_slots, n, blk=math.gcd(n, 1024))
    tile_valid = jnp.clip(lookup(seg_off + counts, tile_expert) - tile_row0, 0, tmx).astype(jnp.int32)
    ys = _experts(tile_expert, n_tiles, tile_valid, inv, h_rows.reshape(-1, nchunk, LANES),
                  w_gate.reshape(N_EXPERTS, D, -1),
                  w_up.reshape(N_EXPERTS, D, -1), w_down.reshape(N_EXPERTS, -1, D), tmx, max_tiles)
    ys3 = ys.reshape(-1, nchunk, LANES)
    y_p = _combine(pos2[:, :n_p], 0, n_p, h_rows, meta_p, ys3, g2, b2, alpha, tm=math.gcd(n_p, ROW_DMA_TILE))
    y_s = _combine(pos2[:, n_p:], n_p, n_s, h_rows, meta_s, ys3, g2, b2, alpha,
                   tm=math.gcd(math.gcd(n_p, n_s), ROW_DMA_TILE))

    y_p = y_p.reshape(B, S, D)
    y_s = 0.5 * (y_s[:DB * T] + y_s[DB * T:])
    y_s = jnp.transpose(y_s.reshape(T, DB, D), (1, 0, 2))
    heads = lambda a, h: jnp.transpose(a.reshape(a.shape[0], h, HEAD_DIM, a.shape[2]), (0, 3, 1, 2))
    wbp = min(max(w for w, _ in DILATED_PATTERNS), S)
    new_wk_p = heads(kT, H)[:, S - wbp:]
    new_wv_p = heads(vT, H)[:, S - wbp:]
    pb = pool_st.shape[1]
    new_pool_p = u[:, S - pb:]
    new_mk_p = heads(kmT, Hc)
    new_mv_p = heads(vmT, Hc)
    new_wk_s = jnp.transpose(nk, (0, 3, 1, 2))
    new_wv_s = jnp.transpose(nv, (0, 3, 1, 2))
    new_pool_s = jnp.transpose(new_pool, (1, 0, 2))
    return (y_p, y_s, new_wk_p, new_wv_p, new_pool_p, new_mk_p, new_mv_p, new_wk_s, new_wv_s, new_pool_s)


def kernel(x_prompt, x_sample, cache_win_k, cache_win_v, state_pool, cache_mem_k, cache_mem_v, mem_prompt, w_in, w_mem_kv, w_pool, pool_scale, w_o, ln1_g, ln1_b, ln2_g, ln2_b, w_r1, b_r1, w_r2, b_r2, w_gate, w_up, w_down):
    depth = w_in.shape[0]
    alpha = (2.0 * depth) ** 0.25
    h_p, h_s = x_prompt, x_sample
    outs = [[] for _ in range(8)]
    for l in range(depth):
        res = _layer(h_p, h_s, cache_win_k[l], cache_win_v[l], state_pool[l], cache_mem_k[l], cache_mem_v[l],
                     mem_prompt, w_in[l], w_mem_kv[l], w_pool[l], pool_scale[l], w_o[l], ln1_g[l], ln1_b[l],
                     ln2_g[l], ln2_b[l], w_r1[l], b_r1[l], w_r2[l], b_r2[l], w_gate[l], w_up[l], w_down[l], alpha)
        h_p, h_s = res[0], res[1]
        for lst, val in zip(outs, res[2:]):
            lst.append(val)
    return (h_p, h_s) + tuple(jnp.stack(o) for o in outs)
```

```python
import functools
import math

import jax
import jax.numpy as jnp
from jax import lax
from jax.experimental import pallas as pl
from jax.experimental.pallas import tpu as pltpu

F32 = jnp.float32
BF16 = jnp.bfloat16
NEG_INF = float("-inf")

HEAD_DIM = 64
DILATED_PATTERNS = ((128, 1), (512, 4), (2048, 16))
BAND_BLOCK = 128
RES = max(d for _, d in DILATED_PATTERNS)
POOL_WINDOWS = (2, 4, 8, 16)
N_EXPERT_GROUPS = 4
EXPERTS_PER_GROUP = 8
N_EXPERTS = N_EXPERT_GROUPS * EXPERTS_PER_GROUP
PAST_LEN = 16384
LN_EPS = 1e-5
ATTN_SCALE = HEAD_DIM ** -0.5

LANES = 128
SUBLANES = 8
VMEM_LIMIT = 56 * 1024 * 1024

ROUTER_OFF = N_EXPERT_GROUPS
ATTN_BATCH = 16
DMA_UNROLL = 8

PROJ_TILE = 512
MIX_TILE = 512
EXPERT_TILE = 512
ROW_DMA_TILE = 256


def _cparams(sem):
    return pltpu.CompilerParams(dimension_semantics=sem, vmem_limit_bytes=VMEM_LIMIT)


def _dot(a, b):
    return jnp.dot(a, b, preferred_element_type=F32)


def _dot_nt(a, b):
    return lax.dot_general(a, b, (((1,), (1,)), ((), ())), preferred_element_type=F32)


def _memproj_body(mem_ref, wkT_ref, wv_ref, wvT_ref, kmT_ref, vm_ref, vmT_ref):
    m = mem_ref[0].astype(BF16)
    kmT_ref[0] = _dot_nt(wkT_ref[...], m)
    vm_ref[0] = _dot(m, wv_ref[...])
    vmT_ref[0] = _dot_nt(wvT_ref[...], m)


def _memproj(mem, wkT, wv, wvT):
    B, M, D = mem.shape
    C = wv.shape[1]
    full = lambda shape: pl.BlockSpec(shape, lambda b: (0,) * len(shape))
    return pl.pallas_call(
        _memproj_body,
        grid=(B,),
        in_specs=[pl.BlockSpec((1, M, D), lambda b: (b, 0, 0)), full((C, D)), full((D, C)), full((C, D))],
        out_specs=[pl.BlockSpec((1, C, M), lambda b: (b, 0, 0)),
                   pl.BlockSpec((1, M, C), lambda b: (b, 0, 0)),
                   pl.BlockSpec((1, C, M), lambda b: (b, 0, 0))],
        out_shape=[jax.ShapeDtypeStruct((B, C, M), F32), jax.ShapeDtypeStruct((B, M, C), F32),
                   jax.ShapeDtypeStruct((B, C, M), F32)],
        compiler_params=_cparams(("arbitrary",)),
        name="memproj",
    )(mem, wkT, wv, wvT)


def _proj_body(d_a, d_b, x_ref, perm_ref, w_ref, wkvT_ref, q_ref, k_ref, v_ref, u_ref, qc_ref, kT_ref, vT_ref):
    xb = x_ref[0].astype(BF16)
    ts = xb.shape[0]
    xp = _dot(perm_ref[...], xb).astype(BF16)
    y = _dot(xp, w_ref[:, 0:3 * d_a])
    rows = ts // RES
    for r in range(RES):
        q_ref[0, r] = y[r * rows:(r + 1) * rows, 0:d_a]
        k_ref[0, r] = y[r * rows:(r + 1) * rows, d_a:2 * d_a]
        v_ref[0, r] = y[r * rows:(r + 1) * rows, 2 * d_a:3 * d_a]
    y2 = _dot(xb, w_ref[:, 3 * d_a:])
    u_ref[0] = y2[:, 0:d_b]
    qc_ref[0] = y2[:, d_b:]
    yT = _dot_nt(wkvT_ref[...], xb)
    kT_ref[0] = yT[0:d_a]
    vT_ref[0] = yT[d_a:]


def _residue_major_perm(n):
    dst = jnp.arange(n)
    src = RES * (dst % (n // RES)) + dst // (n // RES)
    return (src[:, None] == jnp.arange(n)[None, :]).astype(BF16)


def _proj(x, w_bf, wkvT_bf, d_a, d_b, d_c, ts):
    B, S, D = x.shape
    d_in = w_bf.shape[1]
    assert ts % (RES * SUBLANES) == 0 and S % ts == 0
    row = lambda n: pl.BlockSpec((1, ts, n), lambda b, i: (b, i, 0))
    col = lambda n: pl.BlockSpec((1, n, ts), lambda b, i: (b, 0, i))
    grp = pl.BlockSpec((1, RES, ts // RES, d_a), lambda b, i: (b, 0, i, 0))
    outs = pl.pallas_call(
        functools.partial(_proj_body, d_a, d_b),
        grid=(B, S // ts),
        in_specs=[row(D), pl.BlockSpec((ts, ts), lambda b, i: (0, 0)), pl.BlockSpec((D, d_in), lambda b, i: (0, 0)),
                  pl.BlockSpec((2 * d_a, D), lambda b, i: (0, 0))],
        out_specs=[grp, grp, grp, row(d_b), row(d_c), col(d_a), col(d_a)],
        out_shape=[jax.ShapeDtypeStruct((B, RES, S // RES, d_a), F32)] * 3
        + [jax.ShapeDtypeStruct((B, S, d_b), F32), jax.ShapeDtypeStruct((B, S, d_c), F32)]
        + [jax.ShapeDtypeStruct((B, d_a, S), F32)] * 2,
        compiler_params=_cparams(("arbitrary", "arbitrary")),
        name="proj",
    )(x, _residue_major_perm(ts), w_bf, wkvT_bf)
    return [o.reshape(B, S, d_a) for o in outs[:3]] + list(outs[3:])


def _attn_body(S, slopes_ref, q_ref, k_ref, v_ref, o_ref, opat_ref, lpat_ref):
    hp = pl.program_id(1)
    blk = BAND_BLOCK
    lane = lax.broadcasted_iota(jnp.int32, (blk, LANES), 1)
    head0 = lane < HEAD_DIM

    for pi, (window, dil) in enumerate(DILATED_PATTERNS):
        n_steps = window // dil
        ngrp = RES // dil
        cs = blk // ngrp
        cs_bits = cs.bit_length() - 1
        assert ngrp * dil == RES and cs * ngrp == blk and cs % SUBLANES == 0 and S == RES * blk
        has_prev = ngrp > 1
        nk = 2 * blk if has_prev else blk
        nat = lambda a, ngrp=ngrp, cs=cs, cs_bits=cs_bits: (a & (cs - 1)) * ngrp + (a >> cs_bits)
        qi = lax.broadcasted_iota(jnp.int32, (blk, nk), 0)
        kj = lax.broadcasted_iota(jnp.int32, (blk, nk), 1)
        if has_prev:
            steps = (nat(qi) + blk) - (nat(kj & (blk - 1)) + (kj & blk))
        else:
            steps = qi - kj
        valid = (steps >= 0) & (steps <= n_steps)
        dist = (steps * dil).astype(F32)
        biases = []
        for hh in range(2):
            slope = slopes_ref[2 * hp + hh]
            biases.append(jnp.where(valid, -slope * dist, NEG_INF))
        prev_cols = kj < blk

        ones_k = jnp.ones((ATTN_BATCH, nk, LANES), BF16)

        def blocks(it, carry, dil=dil, ngrp=ngrp, cs=cs, has_prev=has_prev, biases=biases, prev_cols=prev_cols,
                   pi=pi, nk=nk, ones_k=ones_k):
            def chunk(idx, c, back):
                r = idx // ngrp
                j = jnp.maximum(idx % ngrp - back, 0)
                return pl.ds(pl.multiple_of((r + dil * c) * blk + cs * j, cs), cs)

            def load(ref, idx, back=0):
                return jnp.concatenate([ref[0, chunk(idx, c, back), :] for c in range(ngrp)], axis=0)

            qs, ks, vs, firsts = [], [], [], []
            for b in range(ATTN_BATCH):
                idx = it * ATTN_BATCH + b
                qs.append(load(q_ref, idx))
                if has_prev:
                    ks.append(jnp.concatenate([load(k_ref, idx, 1), load(k_ref, idx)], axis=0))
                    vs.append(jnp.concatenate([load(v_ref, idx, 1), load(v_ref, idx)], axis=0))
                    firsts.append(jnp.where(prev_cols & (idx % ngrp == 0), NEG_INF, 0.0))
                else:
                    ks.append(load(k_ref, idx))
                    vs.append(load(v_ref, idx))
            q3 = jnp.stack(qs) * ATTN_SCALE
            k3 = jnp.stack(ks).astype(BF16)
            v3 = jnp.concatenate([jnp.stack(vs).astype(BF16), ones_k], axis=2)
            outs, lses = [], []
            for hh in range(2):
                qm = jnp.where(head0 if hh == 0 else ~head0, q3, 0.0).astype(BF16)
                s = jnp.einsum("bqd,bkd->bqk", qm, k3, preferred_element_type=F32) + biases[hh]
                if has_prev:
                    s = s + jnp.stack(firsts)
                m = jnp.max(s, axis=2, keepdims=True)
                p = jnp.exp(s - m).astype(BF16)
                o = jnp.einsum("bqk,bkd->bqd", p, v3, preferred_element_type=F32)
                den = o[:, :, LANES:]
                outs.append(o[:, :, :LANES] / den)
                lses.append(m + jnp.log(den))
            o_pair = jnp.where(head0, outs[0], outs[1])
            l_pair = jnp.where(head0, lses[0], lses[1])
            for b in range(ATTN_BATCH):
                idx = it * ATTN_BATCH + b
                for c in range(ngrp):
                    opat_ref[pi, chunk(idx, c, 0), :] = o_pair[b, c * cs:(c + 1) * cs]
                    lpat_ref[pi, chunk(idx, c, 0), :] = l_pair[b, c * cs:(c + 1) * cs]
            return carry

        lax.fori_loop(0, RES // ATTN_BATCH, blocks, 0)

    chunk = 256

    def mix(c, carry):
        rows = pl.ds(pl.multiple_of(c * chunk, chunk), chunk)
        l0, l1, l2 = lpat_ref[0, rows, :], lpat_ref[1, rows, :], lpat_ref[2, rows, :]
        mx = jnp.maximum(jnp.maximum(l0, l1), l2)
        w0, w1, w2 = jnp.exp(l0 - mx), jnp.exp(l1 - mx), jnp.exp(l2 - mx)
        num = w0 * opat_ref[0, rows, :] + w1 * opat_ref[1, rows, :] + w2 * opat_ref[2, rows, :]
        o_ref[0, rows, :] = (num / (w0 + w1 + w2)).astype(o_ref.dtype)
        return carry

    lax.fori_loop(0, S // chunk, mix, 0)


def _attn(q, k, v, slopes):
    B, S, d_a = q.shape
    assert S % (BAND_BLOCK * max(d for _, d in DILATED_PATTERNS)) == 0
    npair = d_a // LANES
    spec = pl.BlockSpec((1, S, LANES), lambda b, h, *_: (b, 0, h))
    grid_spec = pltpu.PrefetchScalarGridSpec(
        num_scalar_prefetch=0,
        grid=(B, npair),
        in_specs=[pl.BlockSpec(memory_space=pltpu.SMEM), spec, spec, spec],
        out_specs=spec,
        scratch_shapes=[pltpu.VMEM((len(DILATED_PATTERNS), S, LANES), F32),
                        pltpu.VMEM((len(DILATED_PATTERNS), S, LANES), F32)],
    )
    return pl.pallas_call(
        functools.partial(_attn_body, S),
        grid_spec=grid_spec,
        out_shape=jax.ShapeDtypeStruct((B, S, d_a), BF16),
        compiler_params=_cparams(("arbitrary", "arbitrary")),
        name="dilated_attn",
    )(slopes, q, k, v)


def _layer_norm(z, g, b):
    mu = jnp.mean(z, axis=-1, keepdims=True)
    zc = z - mu
    var = jnp.mean(zc * zc, axis=-1, keepdims=True)
    return zc * lax.rsqrt(var + LN_EPS) * g + b


def _route(logits):
    n = logits.shape[0]
    lane = lax.broadcasted_iota(jnp.int32, (n, LANES), 1)
    lane_f = lane.astype(F32)
    big = float(LANES)
    is_outer = lane < N_EXPERT_GROUPS
    l1 = jnp.where(is_outer, logits, NEG_INF)
    m1 = jnp.max(l1, axis=1, keepdims=True)
    g_sel = jnp.min(jnp.where(l1 == m1, lane_f, big), axis=1, keepdims=True)
    v1 = 1.0 / jnp.sum(jnp.exp(l1 - m1), axis=1, keepdims=True)
    lo = ROUTER_OFF + g_sel * EXPERTS_PER_GROUP
    in_group = (lane_f >= lo) & (lane_f < lo + EXPERTS_PER_GROUP)
    l2 = jnp.where(in_group, logits, NEG_INF)
    ma = jnp.max(l2, axis=1, keepdims=True)
    ia = jnp.min(jnp.where(l2 == ma, lane_f, big), axis=1, keepdims=True)
    l2b = jnp.where(lane_f == ia, NEG_INF, l2)
    mb = jnp.max(l2b, axis=1, keepdims=True)
    ib = jnp.min(jnp.where(l2b == mb, lane_f, big), axis=1, keepdims=True)
    eb = jnp.exp(mb - ma)
    wa = 1.0 / (1.0 + eb)
    wb = eb / (1.0 + eb)
    return ia - ROUTER_OFF, ib - ROUTER_OFF, v1 * wa, v1 * wb


def _rank_and_meta(e0, e1, g0, g1, carry):
    n = e0.shape[0]
    lane_f = lax.broadcasted_iota(jnp.int32, (n, LANES), 1).astype(F32)
    oh0 = (lane_f == e0).astype(F32)
    oh1 = (lane_f == e1).astype(F32)
    both = oh0 + oh1
    ti = lax.broadcasted_iota(jnp.int32, (n, n), 0)
    tj = lax.broadcasted_iota(jnp.int32, (n, n), 1)
    tri = (tj < ti).astype(BF16)
    before = _dot(tri, both.astype(BF16)) + carry
    r0 = jnp.sum(before * oh0, axis=1, keepdims=True)
    r1 = jnp.sum(before * oh1, axis=1, keepdims=True)
    new_carry = carry + jnp.sum(both, axis=0, keepdims=True)
    lane = lax.broadcasted_iota(jnp.int32, (n, LANES), 1)
    meta = jnp.zeros((n, LANES), F32)
    for i, val in enumerate((e0, e1, g0, g1, r0, r1)):
        meta = jnp.where(lane == i, val, meta)
    return meta, meta.T[0:SUBLANES], new_carry


def _lane_group(shape, width):
    lane = lax.broadcasted_iota(jnp.int32, shape, len(shape) - 1)
    grp = jnp.zeros(shape, jnp.int32)
    for g in range(1, shape[-1] // width):
        grp = grp + (lane >= g * width).astype(jnp.int32)
    return grp


def _select_by_group(grp, vals):
    out = vals[-1]
    for g in range(len(vals) - 2, -1, -1):
        out = jnp.where(grp == g, vals[g], out)
    return out


def _store_token_rows(ref, val):
    n, d = val.shape
    nchunk = d // LANES
    for c in range(nchunk):
        ref[pl.ds(c, n, stride=nchunk), :] = val[:, c * LANES:(c + 1) * LANES]


def _load_token_rows(ref, n, nchunk):
    return jnp.concatenate([ref[pl.ds(c, n, stride=nchunk), :] for c in range(nchunk)], axis=1)


HALO = 16


def _mix_body(tm, nt, n_tiles, d_b, alpha, x_ref, at_ref, unperm_ref, u_ref, uh_ref, qc_ref, kmT_ref, vm_ref,
              wo_ref, wp_ref, ps_ref, g1_ref, b1_ref, wr_ref, br_ref, cin_ref, hs_ref,
              h1_ref, meta_ref, metaT_ref, cnt_ref, carry_ref):
    g = pl.program_id(0)

    @pl.when(g == 0)
    def _():
        carry_ref[...] = cin_ref[...]

    @pl.when(g == n_tiles)
    def _():
        ns_rows = hs_ref.shape[0]
        h1_ref[0:ns_rows, :] = hs_ref[...]
        h1_ref[ns_rows:, :] = jnp.zeros((h1_ref.shape[0] - ns_rows, LANES), F32)

    @pl.when(g < n_tiles)
    def _():
        _mix_tile(tm, g % nt, d_b, alpha, x_ref, at_ref, unperm_ref, u_ref, uh_ref, qc_ref, kmT_ref, vm_ref, wo_ref,
                  wp_ref, ps_ref, g1_ref, b1_ref, wr_ref, br_ref, h1_ref, meta_ref, metaT_ref, cnt_ref, carry_ref)


def _mix_tile(tm, i, d_b, alpha, x_ref, at_ref, unperm_ref, u_ref, uh_ref, qc_ref, kmT_ref, vm_ref, wo_ref, wp_ref,
              ps_ref, g1_ref, b1_ref, wr_ref, br_ref, h1_ref, meta_ref, metaT_ref, cnt_ref, carry_ref):
    u = u_ref[0]
    halo = jnp.where(i == 0, 0.0, uh_ref[0])
    ext = jnp.concatenate([halo, halo, u], axis=0)
    s2 = ext + pltpu.roll(ext, 1, 0)
    s4 = s2 + pltpu.roll(s2, 2, 0)
    s8 = s4 + pltpu.roll(s4, 4, 0)
    s16 = s8 + pltpu.roll(s8, 8, 0)
    grp = _lane_group((tm, d_b), d_b // len(POOL_WINDOWS))
    sums = [s[2 * HALO:] for s in (s2, s4, s8, s16)]
    win = _select_by_group(grp, sums)
    wlen = _select_by_group(grp, [jnp.int32(w) for w in POOL_WINDOWS])
    pos = i * tm + lax.broadcasted_iota(jnp.int32, (tm, d_b), 0)
    cnt = jnp.minimum(wlen, pos + 1).astype(F32)
    diff = win / cnt - u
    pool = _dot(diff.astype(BF16), wp_ref[...]) * ps_ref[...]

    qc = qc_ref[0] * ATTN_SCALE
    d_c = qc.shape[1]
    kmT = kmT_ref[0].astype(BF16)
    vm = vm_ref[0].astype(BF16)
    hl = _lane_group((tm, d_c), HEAD_DIM)
    memo = jnp.zeros((tm, d_c), F32)
    for h in range(d_c // HEAD_DIM):
        qm = jnp.where(hl == h, qc, 0.0).astype(BF16)
        s = _dot(qm, kmT)
        p = jnp.exp(s - jnp.max(s, axis=1, keepdims=True))
        den = jnp.sum(p, axis=1, keepdims=True)
        o = _dot(p.astype(BF16), vm) / den
        memo = jnp.where(hl == h, o, memo)

    at_rl = jnp.concatenate([at_ref[0, r] for r in range(RES)], axis=0)
    attn = _dot(unperm_ref[...], at_rl).astype(BF16)
    cat = jnp.concatenate([attn, pool.astype(BF16), memo.astype(BF16)], axis=1)
    mixv = _dot(cat, wo_ref[...])
    h1 = _layer_norm(alpha * x_ref[0] + mixv, g1_ref[...], b1_ref[...])
    _store_token_rows(h1_ref, h1)

    logits = _dot(h1.astype(BF16), wr_ref[...]) + br_ref[...]
    e0, e1, g0, g1 = _route(logits)
    meta, metaT, new_carry = _rank_and_meta(e0, e1, g0, g1, carry_ref[0:1, :])
    meta_ref[...] = meta
    metaT_ref[...] = metaT
    carry_ref[...] = jnp.broadcast_to(new_carry, carry_ref.shape)
    cnt_ref[...] = carry_ref[...]


def _mix(x, attn_rm, u, qc, kmT, vm, wo_bf, wp_bd_bf, pool_scale, g1, b1, wr, br, counts_in, h1s, alpha, tm):
    B, S, D = x.shape
    d_a, d_b, d_c = attn_rm.shape[2], u.shape[2], qc.shape[2]
    n_mem = vm.shape[1]
    nt = S // tm
    n_tiles = B * nt
    nchunk = D // LANES
    assert tm % (RES * SUBLANES * 2) == 0 and h1s.shape[0] < tm * nchunk
    last = n_tiles - 1
    bi = lambda g: (jnp.minimum(g, last) // nt, jnp.minimum(g, last) % nt)
    row = lambda n: pl.BlockSpec((1, tm, n), lambda g: bi(g) + (0,))
    full = lambda shape: pl.BlockSpec(shape, lambda g: (0,) * len(shape))
    per_b = lambda shape: pl.BlockSpec((1,) + shape, lambda g: (bi(g)[0], 0, 0))
    halo_spec = pl.BlockSpec((1, HALO, d_b), lambda g: (bi(g)[0], jnp.maximum(bi(g)[1] * (tm // HALO) - 1, 0), 0))
    at_spec = pl.BlockSpec((1, RES, tm // RES, d_a), lambda g: (bi(g)[0], 0, bi(g)[1], 0))
    dst = jnp.arange(tm)
    unperm = ((dst % RES) * (tm // RES) + dst // RES)[:, None] == jnp.arange(tm)[None, :]
    return pl.pallas_call(
        functools.partial(_mix_body, tm, nt, n_tiles, d_b, alpha),
        grid=(n_tiles + 1,),
        in_specs=[row(D), at_spec, full((tm, tm)), row(d_b), halo_spec, row(d_c), per_b((d_c, n_mem)),
                  per_b((n_mem, d_c)), full((D, D)), full((d_b, d_b)), full((1, d_b)), full((1, D)), full((1, D)),
                  full((D, LANES)), full((1, LANES)), full((SUBLANES, LANES)), full(h1s.shape)],
        out_specs=[pl.BlockSpec((tm * nchunk, LANES), lambda g: (g, 0)),
                   pl.BlockSpec((tm, LANES), lambda g: (jnp.minimum(g, last), 0)),
                   pl.BlockSpec((SUBLANES, tm), lambda g: (0, jnp.minimum(g, last))),
                   pl.BlockSpec((SUBLANES, LANES), lambda g: (0, 0))],
        out_shape=[jax.ShapeDtypeStruct(((n_tiles + 1) * tm * nchunk, LANES), F32),
                   jax.ShapeDtypeStruct((B * S, LANES), F32),
                   jax.ShapeDtypeStruct((SUBLANES, B * S), F32),
                   jax.ShapeDtypeStruct((SUBLANES, LANES), F32)],
        scratch_shapes=[pltpu.VMEM((SUBLANES, LANES), F32)],
        compiler_params=_cparams(("arbitrary",)),
        name="mix_ln1_router",
    )(x, attn_rm.reshape(B, RES, S // RES, d_a), unperm.astype(BF16), u, u, qc, kmT, vm, wo_bf, wp_bd_bf,
      pool_scale, g1, b1, wr, br, counts_in, h1s)


def _columns_to_lanes(cols, rows):
    lane = lax.broadcasted_iota(jnp.int32, (rows, LANES), 1)
    tile = jnp.zeros((rows, LANES), F32)
    for t, c in enumerate(cols):
        tile = jnp.where(lane == t, c, tile)
    return tile


def _sproj_body(x_ref, wT_ref, o_ref):
    o_ref[...] = _dot_nt(wT_ref[...], x_ref[...].astype(BF16))


def _sproj(x, wT_bf):
    vm = pl.BlockSpec(memory_space=pltpu.VMEM)
    return pl.pallas_call(
        _sproj_body, in_specs=[vm, vm], out_specs=vm,
        out_shape=jax.ShapeDtypeStruct((wT_bf.shape[0], x.shape[0]), F32),
        compiler_params=pltpu.CompilerParams(vmem_limit_bytes=VMEM_LIMIT),
        name="decode_proj",
    )(x, wT_bf)


def _decode_body(T, wb, qc_row0, slopes_ref, pT_ref, ck_ref, cv_ref, mk_ref, mv_ref,
                 nk_ref, nv_ref, at_ref, mo_ref, qkv_ref, sfar_ref, snear_ref, snew_ref):
    H = ck_ref.shape[1]
    d_a = H * HEAD_DIM
    ntok = pT_ref.shape[1]
    shift = (ntok - T * pl.program_id(0)) % ntok
    qkv_ref[...] = pltpu.roll(pT_ref[...], shift, 1)

    def member(dist, window, dil):
        assert dil & (dil - 1) == 0
        return ((dist & (dil - 1)) == 0) & (dist <= window) & (dist >= 0)

    OWN = min(d for _, d in DILATED_PATTERNS if d > 1)
    far = wb - LANES
    assert T <= OWN and far % OWN == 0 and wb % OWN == 0
    assert all(w <= LANES for w, d in DILATED_PATTERNS if d == 1) and all(d % OWN == 0 for _, d in DILATED_PATTERNS if d > 1)
    R = H * T
    assert T & (T - 1) == 0
    query = lambda shape: lax.broadcasted_iota(jnp.int32, shape, 0) & (T - 1)
    lane_far = lax.broadcasted_iota(jnp.int32, (R, far), 1)
    t_far = query((R, far))
    owned = (lane_far & (OWN - 1)) == t_far
    dist_far = wb + t_far - lane_far
    dist_near = wb + query((R, LANES)) - (far + lax.broadcasted_iota(jnp.int32, (R, LANES), 1))
    dist_new = query((R, T)) - lax.broadcasted_iota(jnp.int32, (R, T), 1)
    masks = [(None if dil == 1 else owned & member(dist_far, window, dil),
              member(dist_near, window, dil), member(dist_new, window, dil)) for window, dil in DILATED_PATTERNS]
    dist_far = dist_far.astype(F32)
    dist_near = dist_near.astype(F32)
    dist_new = jnp.maximum(dist_new, 0).astype(F32)
    rnd = lambda a: a.astype(BF16).astype(F32)

    lane_t = lax.broadcasted_iota(jnp.int32, (HEAD_DIM, LANES), 1)
    own_t = lane_t & (OWN - 1)
    last = wb - LANES
    n_far = far // LANES

    tile = lambda a, c: a[:, c * LANES:(c + 1) * LANES]

    G = SUBLANES // T
    assert G * T == SUBLANES and H % G == 0
    group_rows = lambda g: pl.ds(pl.multiple_of(g * SUBLANES, SUBLANES), SUBLANES)

    def scores(g, carry):
        parts = [head_scores(g * G + i) for i in range(G)]
        sfar_ref[group_rows(g), :] = jnp.concatenate([p[0] for p in parts], axis=0)
        snear_ref[group_rows(g), :] = jnp.concatenate([p[1] for p in parts], axis=0)
        snew_ref[group_rows(g), :] = jnp.concatenate([p[2] for p in parts], axis=0)
        return carry

    def head_scores(h):
        slope = slopes_ref[h]
        kTc = ck_ref[0, h]
        vTc = cv_ref[0, h]
        r0 = pl.multiple_of(h * HEAD_DIM, HEAD_DIM)
        qT = rnd(qkv_ref[pl.ds(r0, HEAD_DIM), :][:, 0:T] * ATTN_SCALE)
        kTn = qkv_ref[pl.ds(d_a + r0, HEAD_DIM), :][:, 0:T]
        vTn = qkv_ref[pl.ds(2 * d_a + r0, HEAD_DIM), :][:, 0:T]
        kTn_r = rnd(kTn)

        qb = [jnp.broadcast_to(qT[:, t:t + 1], (HEAD_DIM, LANES)) for t in range(T)]
        qpat = qb[T - 1]
        for t in range(T - 1):
            qpat = jnp.where(own_t == t, qb[t], qpat)
        s_far = jnp.concatenate([jnp.sum(rnd(tile(kTc, c)) * qpat, axis=0, keepdims=True) for c in range(n_far)],
                                axis=1)
        s_far = s_far - slope * dist_far[0:T]
        k_near = rnd(tile(kTc, n_far))
        s_near = jnp.concatenate([jnp.sum(k_near * qb[t], axis=0, keepdims=True) for t in range(T)], axis=0)
        s_near = s_near - slope * dist_near[0:T]
        s_new = jnp.concatenate([jnp.sum(kTn_r * qT[:, t:t + 1], axis=0, keepdims=True) for t in range(T)], axis=0)
        s_new = s_new - slope * dist_new[0:T]

        rk = pltpu.roll(kTc, wb - T, 1)
        rv = pltpu.roll(vTc, wb - T, 1)
        nk_ref[0, h] = rk
        nv_ref[0, h] = rv
        newk = jnp.zeros((HEAD_DIM, LANES), F32)
        newv = jnp.zeros((HEAD_DIM, LANES), F32)
        for t in range(T):
            newk = jnp.where(lane_t == LANES - T + t, kTn[:, t:t + 1], newk)
            newv = jnp.where(lane_t == LANES - T + t, vTn[:, t:t + 1], newv)
        nk_ref[0, h, :, last:] = jnp.where(lane_t >= LANES - T, newk, rk[:, last:])
        nv_ref[0, h, :, last:] = jnp.where(lane_t >= LANES - T, newv, rv[:, last:])
        return s_far, s_near, s_new

    lax.fori_loop(0, H // G, scores, 0)

    s_far, s_near, s_new = sfar_ref[...], snear_ref[...], snew_ref[...]
    rmax = lambda a: jnp.max(a, axis=1, keepdims=True)
    rsum = lambda a: jnp.sum(a, axis=1, keepdims=True)
    parts, lses = [], []
    for m_far, m_near, m_new in masks:
        sn, sw = jnp.where(m_near, s_near, NEG_INF), jnp.where(m_new, s_new, NEG_INF)
        m = jnp.maximum(rmax(sn), rmax(sw))
        if m_far is not None:
            sf = jnp.where(m_far, s_far, NEG_INF)
            m = jnp.maximum(m, rmax(sf))
        en, ew = jnp.exp(sn - m), jnp.exp(sw - m)
        den = rsum(en) + rsum(ew)
        ef = None
        if m_far is not None:
            ef = jnp.exp(sf - m)
            den = den + rsum(ef)
            ef = rnd(ef)
        parts.append((ef, rnd(en), rnd(ew), den))
        lses.append(m + jnp.log(den))
    top = functools.reduce(jnp.maximum, lses)
    mixw = [jnp.exp(l - top) for l in lses]
    total = functools.reduce(lambda a, b: a + b, mixw)
    p_far = p_near = p_new = None
    for (ef, en, ew, den), w in zip(parts, mixw):
        scale = w / (total * den)
        p_near = en * scale if p_near is None else p_near + en * scale
        p_new = ew * scale if p_new is None else p_new + ew * scale
        if ef is not None:
            p_far = ef * scale if p_far is None else p_far + ef * scale
    sfar_ref[...] = p_far
    snear_ref[...] = p_near
    snew_ref[...] = p_new

    def values(g, carry):
        pf, pn, pw = sfar_ref[group_rows(g), :], snear_ref[group_rows(g), :], snew_ref[group_rows(g), :]
        for i in range(G):
            head_values(g * G + i, pf[i * T:(i + 1) * T], pn[i * T:(i + 1) * T], pw[i * T:(i + 1) * T])
        return carry

    def head_values(h, p_far_h, p_near_h, p_new_h):
        vTc = cv_ref[0, h]
        r0 = pl.multiple_of(h * HEAD_DIM, HEAD_DIM)
        vTn_r = rnd(qkv_ref[pl.ds(2 * d_a + r0, HEAD_DIM), :][:, 0:T])
        p_comb = jnp.sum(p_far_h, axis=0, keepdims=True)
        acc = rnd(tile(vTc, 0)) * tile(p_comb, 0)
        for c in range(1, n_far):
            acc = acc + rnd(tile(vTc, c)) * tile(p_comb, c)
        v_near = rnd(tile(vTc, n_far))
        cols = []
        for t in range(T):
            cols.append(jnp.sum(jnp.where(own_t == t, acc, 0.0) + v_near * p_near_h[t:t + 1], axis=1, keepdims=True)
                        + jnp.sum(vTn_r * p_new_h[t:t + 1], axis=1, keepdims=True))
        at_ref[0, pl.ds(r0, HEAD_DIM), :] = _columns_to_lanes(cols, HEAD_DIM)

    lax.fori_loop(0, H // G, values, 0)

    Hc = mk_ref.shape[1]
    rows = []
    for h in range(Hc):
        kT = rnd(mk_ref[0, h])
        r0 = h * HEAD_DIM
        qT = rnd(qkv_ref[qc_row0 + r0:qc_row0 + r0 + HEAD_DIM, :][:, 0:T] * ATTN_SCALE)
        rows += [jnp.sum(kT * qT[:, t:t + 1], axis=0, keepdims=True) for t in range(T)]
    s = jnp.concatenate(rows, axis=0)
    p = jnp.exp(s - jnp.max(s, axis=1, keepdims=True))
    p = rnd(p / jnp.sum(p, axis=1, keepdims=True))
    for h in range(Hc):
        vT = rnd(mv_ref[0, h])
        cols = [jnp.sum(vT * p[h * T + t:h * T + t + 1], axis=1, keepdims=True) for t in range(T)]
        mo_ref[0, h * HEAD_DIM:(h + 1) * HEAD_DIM, :] = _columns_to_lanes(cols, HEAD_DIM)


def _decode(projT, T, qc_row0, ck, cv, mk, mv, slopes):
    DB, H, hd, wb = ck.shape
    _, Hc, _, n_mem = mk.shape
    d_a, d_c = H * hd, Hc * hd
    assert wb >= max(w for w, _ in DILATED_PATTERNS) and wb % LANES == 0 and T <= LANES
    assert projT.shape[1] == DB * T and projT.shape[0] >= qc_row0 + d_c
    cache = pl.BlockSpec((1, H, hd, wb), lambda b, *_: (b, 0, 0, 0))
    memc = pl.BlockSpec((1, Hc, hd, n_mem), lambda b, *_: (b, 0, 0, 0))
    grid_spec = pltpu.PrefetchScalarGridSpec(
        num_scalar_prefetch=0,
        grid=(DB,),
        in_specs=[pl.BlockSpec(memory_space=pltpu.SMEM),
                  pl.BlockSpec(projT.shape, lambda b, *_: (0, 0)),
                  cache, cache, memc, memc],
        out_specs=[cache, cache,
                   pl.BlockSpec((1, d_a, LANES), lambda b, *_: (b, 0, 0)),
                   pl.BlockSpec((1, d_c, LANES), lambda b, *_: (b, 0, 0))],
        scratch_shapes=[pltpu.VMEM(projT.shape, F32), pltpu.VMEM((H * T, wb - LANES), F32),
                        pltpu.VMEM((H * T, LANES), F32), pltpu.VMEM((H * T, T), F32)],
    )
    nk, nv, at, mo = pl.pallas_call(
        functools.partial(_decode_body, T, wb, qc_row0),
        grid_spec=grid_spec,
        out_shape=[jax.ShapeDtypeStruct(ck.shape, F32), jax.ShapeDtypeStruct(cv.shape, F32),
                   jax.ShapeDtypeStruct((DB, d_a, LANES), F32), jax.ShapeDtypeStruct((DB, d_c, LANES), F32)],
        compiler_params=_cparams(("arbitrary",)),
        name="decode_attn_cache",
    )(slopes, projT, ck, cv, mk, mv)
    return nk, nv, at[:, :, :T], mo[:, :, :T]


def _smix_body(T, alpha, pos0, x_ref, at_ref, mo_ref, st_ref, wu_ref, wo_ref, wp_ref, ps_ref, g1_ref, b1_ref,
               wr_ref, br_ref, cin_ref, h1_ref, meta_ref, metaT_ref, cnt_ref, pool_ref):
    n = x_ref.shape[0]
    db = n // T
    pb = st_ref.shape[0]
    d_b = st_ref.shape[2]
    x = x_ref[...]
    bdot = lambda a, b: _dot(a.astype(BF16), b.astype(BF16))
    u_new = bdot(x, wu_ref[...])
    seq = [st_ref[j] for j in range(pb)] + [u_new[t * db:(t + 1) * db] for t in range(T)]
    for j in range(pb):
        pool_ref[j] = seq[j + T]
    grp = _lane_group((db, d_b), d_b // len(POOL_WINDOWS))
    diffs = []
    for t in range(T):
        j = pb + t
        per_w = []
        for w in POOL_WINDOWS:
            acc = seq[j]
            for back in range(1, w):
                if j - back >= 0:
                    acc = acc + seq[j - back]
            per_w.append(acc / float(min(w, pos0 + j + 1)))
        diffs.append(_select_by_group(grp, per_w) - seq[j])
    diff = jnp.concatenate(diffs, axis=0)
    pool = bdot(diff, wp_ref[...]) * ps_ref[...]
    cat = jnp.concatenate([at_ref[...], pool, mo_ref[...]], axis=1)
    mixv = bdot(cat, wo_ref[...])
    h1 = _layer_norm(alpha * x + mixv, g1_ref[...], b1_ref[...])
    _store_token_rows(h1_ref, jnp.concatenate([h1, h1], axis=0))
    logits_lo = bdot(h1, wr_ref[...])
    logits_hi = jnp.dot(h1, wr_ref[...], preferred_element_type=F32, precision=lax.Precision.HIGHEST)
    logits = jnp.concatenate([logits_lo, logits_hi], axis=0) + br_ref[...]
    e0, e1, g0, g1 = _route(logits)
    meta, metaT, new_carry = _rank_and_meta(e0, e1, g0, g1, cin_ref[0:1, :])
    meta_ref[...] = meta
    metaT_ref[...] = metaT
    cnt_ref[...] = jnp.broadcast_to(new_carry, cnt_ref.shape)


def _smix(x_tb, attn_tb, memo_tb, state, wu, wo, wp_bd, pool_scale, g1, b1, wr, br, counts_in, alpha, T):
    n, D = x_tb.shape
    pb, db, d_b = state.shape
    nchunk = D // LANES
    vm = pl.BlockSpec(memory_space=pltpu.VMEM)
    return pl.pallas_call(
        functools.partial(_smix_body, T, alpha, PAST_LEN - pb),
        in_specs=[vm] * 13,
        out_specs=[vm] * 5,
        out_shape=[jax.ShapeDtypeStruct((2 * n * nchunk, LANES), F32), jax.ShapeDtypeStruct((2 * n, LANES), F32),
                   jax.ShapeDtypeStruct((SUBLANES, 2 * n), F32),
                   jax.ShapeDtypeStruct((SUBLANES, LANES), F32), jax.ShapeDtypeStruct((pb, db, d_b), F32)],
        compiler_params=pltpu.CompilerParams(vmem_limit_bytes=VMEM_LIMIT),
        name="decode_mix_ln1_router",
    )(x_tb, attn_tb, memo_tb, state, wu, wo, wp_bd, pool_scale, g1, b1, wr, br, counts_in)


def _slot_map_body(blk, pos0_ref, pos1_ref, fill_ref, inv_ref, sem):
    g = pl.program_id(0)

    @pl.when(g == 0)
    def _():
        c = pltpu.make_async_copy(fill_ref, inv_ref, sem)
        c.start()
        c.wait()

    group = math.gcd(blk, 16)

    def body(j, carry):
        token0 = g * blk + j * group
        for u in range(group):
            inv_ref[pos0_ref[0, 0, j * group + u]] = token0 + u
            inv_ref[pos1_ref[0, 0, j * group + u]] = token0 + u
        return carry

    lax.fori_loop(0, blk // group, body, 0)


def _slot_map(pos2, n_slots, zero_token, blk):
    n = pos2.shape[1]
    assert n % blk == 0 and n_slots % 1024 == 0
    tokens = pl.BlockSpec((1, 1, blk), lambda g: (g, 0, 0), memory_space=pltpu.SMEM)
    return pl.pallas_call(
        functools.partial(_slot_map_body, blk),
        grid=(n // blk,),
        in_specs=[tokens, tokens, pl.BlockSpec(memory_space=pl.ANY)],
        out_specs=pl.BlockSpec(memory_space=pltpu.SMEM),
        out_shape=jax.ShapeDtypeStruct((n_slots,), jnp.int32),
        scratch_shapes=[pltpu.SemaphoreType.DMA(())],
        compiler_params=_cparams(("arbitrary",)),
        name="slot_map",
    )(pos2[0].reshape(n // blk, 1, blk), pos2[1].reshape(n // blk, 1, blk),
      jnp.full((n_slots,), zero_token, jnp.int32))


def _expert_body(tmx, nchunk, te_ref, nt_ref, tv_ref, cur_ref, nxt_ref, h_ref, wg_ref, wu_ref, wd_ref, ys_ref,
                 buf_a, buf_b, wg_bf, wu_bf, wd_bf, sems):
    i = pl.program_id(0)
    n_tiles = nt_ref[0]

    def gather(idx_ref, buf, s, tile, wait):
        groups = (tv_ref[tile] + DMA_UNROLL - 1) // DMA_UNROLL

        def body(g, carry):
            for u in range(DMA_UNROLL):
                j = g * DMA_UNROLL + u
                dst = buf.at[pl.ds(pl.multiple_of(j * nchunk, nchunk), nchunk)]
                copy = pltpu.make_async_copy(h_ref.at[idx_ref[0, 0, j]], dst, sems.at[s])
                if wait:
                    copy.wait()
                else:
                    copy.start(priority=u % 2)
            return carry

        lax.fori_loop(0, groups, body, 0)

    @pl.when(i == 0)
    def _():
        buf_a[...] = jnp.zeros_like(buf_a)
        buf_b[...] = jnp.zeros_like(buf_b)
        gather(cur_ref, buf_a, 0, 0, False)

    it = jnp.minimum(i, pl.num_programs(0) - 2)
    @pl.when((i < n_tiles) & ((i == 0) | (te_ref[it] != te_ref[jnp.maximum(it - 1, 0)])))
    def _():
        wg_bf[...] = wg_ref[0].astype(BF16)
        wu_bf[...] = wu_ref[0].astype(BF16)
        wd_bf[...] = wd_ref[0].astype(BF16)

    def step(cur, s_cur, nxt, s_nxt):
        gather(cur_ref, cur, s_cur, it, True)

        @pl.when(i + 1 < n_tiles)
        def _():
            gather(nxt_ref, nxt, s_nxt, it + 1, False)

        x = _load_token_rows(cur, tmx, nchunk).astype(BF16)
        hg = _dot(x, wg_bf[...])
        hu = _dot(x, wu_bf[...])
        a = (hg * jax.nn.sigmoid(hg) * hu).astype(BF16)
        y = _dot(a, wd_bf[...])
        _store_token_rows(ys_ref, y)

    @pl.when((i < n_tiles) & (i % 2 == 0))
    def _():
        step(buf_a, 0, buf_b, 1)

    @pl.when((i < n_tiles) & (i % 2 == 1))
    def _():
        step(buf_b, 1, buf_a, 0)

    @pl.when(i >= n_tiles)
    def _():
        ys_ref[...] = jnp.zeros_like(ys_ref)


def _experts(tile_expert, n_tiles, tile_valid, inv, h_rows, wg, wu, wd, tmx, max_tiles):
    E, D, F = wg.shape
    nchunk = D // LANES
    inv3 = inv.reshape(-1, 1, tmx)
    last_blk = inv3.shape[0] - 1
    assert last_blk >= max_tiles and tmx % DMA_UNROLL == 0
    weights = lambda shape: pl.BlockSpec((1,) + shape, lambda i, te, *_: (te[jnp.minimum(i, max_tiles - 1)], 0, 0))
    rows = pltpu.VMEM((tmx * nchunk, LANES), F32)
    grid_spec = pltpu.PrefetchScalarGridSpec(
        num_scalar_prefetch=3,
        grid=(max_tiles + 1,),
        in_specs=[pl.BlockSpec((1, 1, tmx), lambda i, *_: (i, 0, 0), memory_space=pltpu.SMEM),
                  pl.BlockSpec((1, 1, tmx), lambda i, *_: (jnp.minimum(i + 1, last_blk), 0, 0),
                               memory_space=pltpu.SMEM),
                  pl.BlockSpec(memory_space=pl.ANY), weights((D, F)), weights((D, F)), weights((F, D))],
        out_specs=pl.BlockSpec((tmx * nchunk, LANES), lambda i, *_: (i, 0)),
        scratch_shapes=[rows, rows, pltpu.VMEM((D, F), BF16), pltpu.VMEM((D, F), BF16), pltpu.VMEM((F, D), BF16),
                        pltpu.SemaphoreType.DMA((2,))],
    )
    return pl.pallas_call(
        functools.partial(_expert_body, tmx, nchunk),
        grid_spec=grid_spec,
        out_shape=jax.ShapeDtypeStruct(((max_tiles + 1) * tmx * nchunk, LANES), F32),
        compiler_params=_cparams(("arbitrary",)),
        name="expert_swiglu",
    )(tile_expert, n_tiles, tile_valid, inv3, inv3, h_rows, wg, wu, wd)


def _combine_body(tm, nchunk, alpha, pos_ref, nxt_ref, h1_ref, meta_ref, ys_ref, g2_ref, b2_ref, o_ref, buf_a, buf_b,
                  sems):
    i = pl.program_id(0)
    last = pl.num_programs(0) - 1

    def copy(idx_ref, buf, s, t, k):
        dst = buf.at[k, pl.ds(pl.multiple_of(t * nchunk, nchunk), nchunk)]
        return pltpu.make_async_copy(ys_ref.at[idx_ref[0, 0, k * tm + t]], dst, sems.at[s])

    def start(idx_ref, buf, s):
        def body(t, carry):
            for k in range(2):
                copy(idx_ref, buf, s, t, k).start(priority=k)
            return carry

        lax.fori_loop(0, tm, body, 0, unroll=DMA_UNROLL)

    def wait(buf, s):
        def body(t, carry):
            for k in range(2):
                copy(pos_ref, buf, s, t, k).wait()
            return carry

        lax.fori_loop(0, tm, body, 0, unroll=DMA_UNROLL)

    @pl.when(i == 0)
    def _():
        start(pos_ref, buf_a, 0)

    def step(cur, s_cur, nxt, s_nxt):
        wait(cur, s_cur)
        start(nxt_ref, nxt, s_nxt)
        h1 = _load_token_rows(h1_ref, tm, nchunk)
        y0 = _load_token_rows(cur.at[0], tm, nchunk)
        y1 = _load_token_rows(cur.at[1], tm, nchunk)
        meta = meta_ref[...]
        lane = lax.broadcasted_iota(jnp.int32, meta.shape, 1)
        gate0 = jnp.sum(jnp.where(lane == 2, meta, 0.0), axis=1, keepdims=True)
        gate1 = jnp.sum(jnp.where(lane == 3, meta, 0.0), axis=1, keepdims=True)
        f = gate0 * y0 + gate1 * y1
        o_ref[...] = _layer_norm(alpha * h1 + f, g2_ref[...], b2_ref[...])

        @pl.when(i == last)
        def _():
            wait(nxt, s_nxt)

    @pl.when(i % 2 == 0)
    def _():
        step(buf_a, 0, buf_b, 1)

    @pl.when(i % 2 == 1)
    def _():
        step(buf_b, 1, buf_a, 0)


def _combine(pos2, tok0, n, h1_flat, meta, ys, g2, b2, alpha, tm):
    D = g2.shape[1]
    nchunk = D // LANES
    assert n % tm == 0 and tok0 % tm == 0 and pos2.shape == (2, n)
    pos3 = jnp.transpose(pos2.reshape(2, -1, tm), (1, 0, 2)).reshape(-1, 1, 2 * tm)
    off = tok0 // tm
    last = n // tm - 1
    return pl.pallas_call(
        functools.partial(_combine_body, tm, nchunk, alpha),
        grid=(n // tm,),
        in_specs=[pl.BlockSpec((1, 1, 2 * tm), lambda i: (i, 0, 0), memory_space=pltpu.SMEM),
                  pl.BlockSpec((1, 1, 2 * tm), lambda i: (jnp.minimum(i + 1, last), 0, 0),
                               memory_space=pltpu.SMEM),
                  pl.BlockSpec((tm * nchunk, LANES), lambda i: (i + off, 0)),
                  pl.BlockSpec((tm, LANES), lambda i: (i, 0)),
                  pl.BlockSpec(memory_space=pl.ANY),
                  pl.BlockSpec((1, D), lambda i: (0, 0)), pl.BlockSpec((1, D), lambda i: (0, 0))],
        out_specs=pl.BlockSpec((tm, D), lambda i: (i, 0)),
        out_shape=jax.ShapeDtypeStruct((n, D), F32),
        scratch_shapes=[pltpu.VMEM((2, tm * nchunk, LANES), F32), pltpu.VMEM((2, tm * nchunk, LANES), F32),
                        pltpu.SemaphoreType.DMA((2,))],
        compiler_params=_cparams(("arbitrary",)),
        name="combine_ln2",
    )(pos3, pos3, h1_flat, meta, ys, g2, b2)


def _block_diag(w):
    g, a, b = w.shape
    eye = jnp.eye(g, dtype=w.dtype)
    return (eye[:, None, :, None] * w[:, :, None, :]).reshape(g * a, g * b)


def _layer(h_p, h_s, win_k, win_v, pool_st, mem_k, mem_v, mem_prompt,
           w_in, w_mem_kv, w_pool, pool_scale, w_o, ln1_g, ln1_b, ln2_g, ln2_b,
           w_r1, b_r1, w_r2, b_r2, w_gate, w_up, w_down, alpha):
    B, S, D = h_p.shape
    DB, T, _ = h_s.shape
    H = win_k.shape[2]
    Hc = mem_k.shape[2]
    d_a, d_c = H * HEAD_DIM, Hc * HEAD_DIM
    d_b = pool_st.shape[2]
    nchunk = D // LANES
    slopes = 2.0 ** (-8.0 * jnp.arange(1, H + 1, dtype=F32) / H)

    w_in_bf = w_in.astype(BF16)
    w_inT_bf = w_in_bf.T
    wkvT_bf = w_inT_bf[d_a:3 * d_a]
    wp_bd = _block_diag(w_pool)
    ps = pool_scale.reshape(1, d_b)
    g1, b1 = ln1_g.reshape(1, D), ln1_b.reshape(1, D)
    g2, b2 = ln2_g.reshape(1, D), ln2_b.reshape(1, D)
    n_r = N_EXPERT_GROUPS + N_EXPERTS
    wr = jnp.concatenate([w_r1, jnp.transpose(w_r2, (1, 0, 2)).reshape(D, N_EXPERTS),
                          jnp.zeros((D, LANES - n_r), F32)], axis=1)
    br = jnp.concatenate([b_r1, b_r2.reshape(-1), jnp.zeros((LANES - n_r,), F32)]).reshape(1, LANES)

    ck = jnp.transpose(win_k, (0, 2, 3, 1))
    cv = jnp.transpose(win_v, (0, 2, 3, 1))
    mk = jnp.transpose(mem_k, (0, 2, 3, 1))
    mv = jnp.transpose(mem_v, (0, 2, 3, 1))
    projT = _sproj(h_s.reshape(DB * T, D), w_inT_bf)
    nk, nv, attn_sT, memo_sT = _decode(projT, T, 3 * d_a + d_b, ck, cv, mk, mv, slopes)
    to_tb = lambda a: jnp.transpose(a, (2, 0, 1)).reshape(T * DB, a.shape[1])
    x_tb = jnp.transpose(h_s, (1, 0, 2)).reshape(T * DB, D)
    state = jnp.transpose(pool_st, (1, 0, 2))
    h1s, meta_s, metaT_s, cnt_s, new_pool = _smix(x_tb, to_tb(attn_sT), to_tb(memo_sT), state,
                                         w_in[:, 3 * d_a:3 * d_a + d_b], w_o, wp_bd, ps, g1, b1, wr, br,
                                         jnp.zeros((SUBLANES, LANES), F32), alpha, T)

    w_memT = w_mem_kv.T
    kmT, vm, vmT = _memproj(mem_prompt, w_memT[:d_c].astype(BF16), w_mem_kv[:, d_c:].astype(BF16),
                            w_memT[d_c:].astype(BF16))
    q, k, v, u, qc, kT, vT = _proj(h_p, w_in_bf, wkvT_bf, d_a, d_b, d_c, ts=math.gcd(S, PROJ_TILE))
    attn = _attn(q, k, v, slopes)
    h_rows, meta_p, metaT_p, cnt_all = _mix(h_p, attn, u, qc, kmT, vm, w_o.astype(BF16), wp_bd.astype(BF16), ps, g1, b1,
                                   wr.astype(BF16), br, cnt_s, h1s, alpha, tm=math.gcd(S, MIX_TILE))

    tmx = EXPERT_TILE
    n_p, n_s = B * S, 2 * DB * T
    n = n_p + n_s
    counts = cnt_all[0, :N_EXPERTS].astype(jnp.int32)
    padded = (counts + tmx - 1) // tmx * tmx
    seg_end = jnp.cumsum(padded)
    seg_off = seg_end - padded
    metaT = jnp.concatenate([metaT_p, metaT_s], axis=1)
    e_ids = metaT[0:2].astype(jnp.int32)
    experts = jnp.arange(N_EXPERTS).reshape((N_EXPERTS,) + (1,) * 2)
    lookup = lambda table, idx: jnp.sum(
        jnp.where(idx.reshape((1,) * (3 - idx.ndim) + idx.shape) == experts, table.reshape(experts.shape), 0),
        axis=0).reshape(idx.shape)
    pos2 = lookup(seg_off, e_ids) + metaT[4:6].astype(jnp.int32)
    max_tiles = (2 * n) // tmx + N_EXPERTS
    n_tiles = (seg_end[-1] // tmx).astype(jnp.int32).reshape(1)
    tile_row0 = jnp.arange(max_tiles, dtype=jnp.int32) * tmx
    tile_expert = jnp.sum((tile_row0[:, None] >= seg_end[None, :]).astype(jnp.int32), axis=1)
    tile_expert = jnp.minimum(tile_expert, N_EXPERTS - 1)

    n_slots = -(-((max_tiles + 1) * tmx) // 1024) * 1024
    inv = _slot_map(pos2, n_slots, n, blk=math.gcd(n, 1024))
    tile_valid = jnp.clip(lookup(seg_off + counts, tile_expert) - tile_row0, 0, tmx).astype(jnp.int32)
    ys = _experts(tile_expert, n_tiles, tile_valid, inv, h_rows.reshape(-1, nchunk, LANES),
                  w_gate.reshape(N_EXPERTS, D, -1),
                  w_up.reshape(N_EXPERTS, D, -1), w_down.reshape(N_EXPERTS, -1, D), tmx, max_tiles)
    ys3 = ys.reshape(-1, nchunk, LANES)
    y_p = _combine(pos2[:, :n_p], 0, n_p, h_rows, meta_p, ys3, g2, b2, alpha, tm=math.gcd(n_p, ROW_DMA_TILE))
    y_s = _combine(pos2[:, n_p:], n_p, n_s, h_rows, meta_s, ys3, g2, b2, alpha,
                   tm=math.gcd(math.gcd(n_p, n_s), ROW_DMA_TILE))

    y_p = y_p.reshape(B, S, D)
    y_s = 0.5 * (y_s[:DB * T] + y_s[DB * T:])
    y_s = jnp.transpose(y_s.reshape(T, DB, D), (1, 0, 2))
    heads = lambda a, h: jnp.transpose(a.reshape(a.shape[0], h, HEAD_DIM, a.shape[2]), (0, 3, 1, 2))
    wbp = min(max(w for w, _ in DILATED_PATTERNS), S)
    new_wk_p = heads(kT, H)[:, S - wbp:]
    new_wv_p = heads(vT, H)[:, S - wbp:]
    pb = pool_st.shape[1]
    new_pool_p = u[:, S - pb:]
    new_mk_p = heads(kmT, Hc)
    new_mv_p = heads(vmT, Hc)
    new_wk_s = jnp.transpose(nk, (0, 3, 1, 2))
    new_wv_s = jnp.transpose(nv, (0, 3, 1, 2))
    new_pool_s = jnp.transpose(new_pool, (1, 0, 2))
    return (y_p, y_s, new_wk_p, new_wv_p, new_pool_p, new_mk_p, new_mv_p, new_wk_s, new_wv_s, new_pool_s)


def kernel(x_prompt, x_sample, cache_win_k, cache_win_v, state_pool, cache_mem_k, cache_mem_v, mem_prompt, w_in, w_mem_kv, w_pool, pool_scale, w_o, ln1_g, ln1_b, ln2_g, ln2_b, w_r1, b_r1, w_r2, b_r2, w_gate, w_up, w_down):
    depth = w_in.shape[0]
    alpha = (2.0 * depth) ** 0.25
    h_p, h_s = x_prompt, x_sample
    outs = [[] for _ in range(8)]
    for l in range(depth):
        res = _layer(h_p, h_s, cache_win_k[l], cache_win_v[l], state_pool[l], cache_mem_k[l], cache_mem_v[l],
                     mem_prompt, w_in[l], w_mem_kv[l], w_pool[l], pool_scale[l], w_o[l], ln1_g[l], ln1_b[l],
                     ln2_g[l], ln2_b[l], w_r1[l], b_r1[l], w_r2[l], b_r2[l], w_gate[l], w_up[l], w_down[l], alpha)
        h_p, h_s = res[0], res[1]
        for lst, val in zip(outs, res[2:]):
            lst.append(val)
    return (h_p, h_s) + tuple(jnp.stack(o) for o in outs)
```

```python
import functools
import math

import jax
import jax.numpy as jnp
from jax import lax
from jax.experimental import pallas as pl
from jax.experimental.pallas import tpu as pltpu

F32 = jnp.float32
BF16 = jnp.bfloat16
NEG_INF = float("-inf")

HEAD_DIM = 64
DILATED_PATTERNS = ((128, 1), (512, 4), (2048, 16))
BAND_BLOCK = 128
RES = max(d for _, d in DILATED_PATTERNS)
POOL_WINDOWS = (2, 4, 8, 16)
N_EXPERT_GROUPS = 4
EXPERTS_PER_GROUP = 8
N_EXPERTS = N_EXPERT_GROUPS * EXPERTS_PER_GROUP
PAST_LEN = 16384
LN_EPS = 1e-5
ATTN_SCALE = HEAD_DIM ** -0.5

LANES = 128
SUBLANES = 8
VMEM_LIMIT = 56 * 1024 * 1024

ROUTER_OFF = N_EXPERT_GROUPS
ATTN_BATCH = 16
DMA_UNROLL = 16

PROJ_TILE = 512
MIX_TILE = 1024
EXPERT_TILE = 512
ROW_DMA_TILE = 256


def _cparams(sem):
    return pltpu.CompilerParams(dimension_semantics=sem, vmem_limit_bytes=VMEM_LIMIT)


def _dot(a, b):
    return jnp.dot(a, b, preferred_element_type=F32)


def _dot_nt(a, b):
    return lax.dot_general(a, b, (((1,), (1,)), ((), ())), preferred_element_type=F32)


def _memproj_body(mem_ref, wkT_ref, wv_ref, wvT_ref, kmT_ref, vm_ref, vmT_ref):
    m = mem_ref[0].astype(BF16)
    kmT_ref[0] = _dot_nt(wkT_ref[...], m)
    vm_ref[0] = _dot(m, wv_ref[...])
    vmT_ref[0] = _dot_nt(wvT_ref[...], m)


def _memproj(mem, wkT, wv, wvT):
    B, M, D = mem.shape
    C = wv.shape[1]
    full = lambda shape: pl.BlockSpec(shape, lambda b: (0,) * len(shape))
    return pl.pallas_call(
        _memproj_body,
        grid=(B,),
        in_specs=[pl.BlockSpec((1, M, D), lambda b: (b, 0, 0)), full((C, D)), full((D, C)), full((C, D))],
        out_specs=[pl.BlockSpec((1, C, M), lambda b: (b, 0, 0)),
                   pl.BlockSpec((1, M, C), lambda b: (b, 0, 0)),
                   pl.BlockSpec((1, C, M), lambda b: (b, 0, 0))],
        out_shape=[jax.ShapeDtypeStruct((B, C, M), F32), jax.ShapeDtypeStruct((B, M, C), F32),
                   jax.ShapeDtypeStruct((B, C, M), F32)],
        compiler_params=_cparams(("arbitrary",)),
        name="memproj",
    )(mem, wkT, wv, wvT)


def _proj_body(d_a, d_b, x_ref, perm_ref, w_ref, wkvT_ref, q_ref, k_ref, v_ref, u_ref, qc_ref, kT_ref, vT_ref):
    xb = x_ref[0].astype(BF16)
    ts = xb.shape[0]
    xp = _dot(perm_ref[...], xb).astype(BF16)
    y = _dot(xp, w_ref[:, 0:3 * d_a])
    rows = ts // RES
    for r in range(RES):
        q_ref[0, r] = y[r * rows:(r + 1) * rows, 0:d_a]
        k_ref[0, r] = y[r * rows:(r + 1) * rows, d_a:2 * d_a]
        v_ref[0, r] = y[r * rows:(r + 1) * rows, 2 * d_a:3 * d_a]
    y2 = _dot(xb, w_ref[:, 3 * d_a:])
    u_ref[0] = y2[:, 0:d_b]
    qc_ref[0] = y2[:, d_b:]
    yT = _dot_nt(wkvT_ref[...], xb)
    kT_ref[0] = yT[0:d_a]
    vT_ref[0] = yT[d_a:]


def _residue_major_perm(n):
    dst = jnp.arange(n)
    src = RES * (dst % (n // RES)) + dst // (n // RES)
    return (src[:, None] == jnp.arange(n)[None, :]).astype(BF16)


def _proj(x, w_bf, wkvT_bf, d_a, d_b, d_c, ts):
    B, S, D = x.shape
    d_in = w_bf.shape[1]
    assert ts % (RES * SUBLANES) == 0 and S % ts == 0
    row = lambda n: pl.BlockSpec((1, ts, n), lambda b, i: (b, i, 0))
    col = lambda n: pl.BlockSpec((1, n, ts), lambda b, i: (b, 0, i))
    grp = pl.BlockSpec((1, RES, ts // RES, d_a), lambda b, i: (b, 0, i, 0))
    outs = pl.pallas_call(
        functools.partial(_proj_body, d_a, d_b),
        grid=(B, S // ts),
        in_specs=[row(D), pl.BlockSpec((ts, ts), lambda b, i: (0, 0)), pl.BlockSpec((D, d_in), lambda b, i: (0, 0)),
                  pl.BlockSpec((2 * d_a, D), lambda b, i: (0, 0))],
        out_specs=[grp, grp, grp, row(d_b), row(d_c), col(d_a), col(d_a)],
        out_shape=[jax.ShapeDtypeStruct((B, RES, S // RES, d_a), F32)] * 3
        + [jax.ShapeDtypeStruct((B, S, d_b), F32), jax.ShapeDtypeStruct((B, S, d_c), F32)]
        + [jax.ShapeDtypeStruct((B, d_a, S), F32)] * 2,
        compiler_params=_cparams(("arbitrary", "arbitrary")),
        name="proj",
    )(x, _residue_major_perm(ts), w_bf, wkvT_bf)
    return [o.reshape(B, S, d_a) for o in outs[:3]] + list(outs[3:])


def _attn_body(S, slopes_ref, q_ref, k_ref, v_ref, o_ref, opat_ref, lpat_ref):
    hp = pl.program_id(1)
    blk = BAND_BLOCK
    lane = lax.broadcasted_iota(jnp.int32, (blk, LANES), 1)
    head0 = lane < HEAD_DIM

    for pi, (window, dil) in enumerate(DILATED_PATTERNS):
        n_steps = window // dil
        ngrp = RES // dil
        cs = blk // ngrp
        cs_bits = cs.bit_length() - 1
        assert ngrp * dil == RES and cs * ngrp == blk and cs % SUBLANES == 0 and S == RES * blk
        has_prev = ngrp > 1
        nk = 2 * blk if has_prev else blk
        nat = lambda a, ngrp=ngrp, cs=cs, cs_bits=cs_bits: (a & (cs - 1)) * ngrp + (a >> cs_bits)
        qi = lax.broadcasted_iota(jnp.int32, (blk, nk), 0)
        kj = lax.broadcasted_iota(jnp.int32, (blk, nk), 1)
        if has_prev:
            steps = (nat(qi) + blk) - (nat(kj & (blk - 1)) + (kj & blk))
        else:
            steps = qi - kj
        valid = (steps >= 0) & (steps <= n_steps)
        dist = (steps * dil).astype(F32)
        biases = []
        for hh in range(2):
            slope = slopes_ref[2 * hp + hh]
            biases.append(jnp.where(valid, -slope * dist, NEG_INF))
        prev_cols = kj < blk

        ones_k = jnp.ones((ATTN_BATCH, nk, LANES), BF16)

        def blocks(it, carry, dil=dil, ngrp=ngrp, cs=cs, has_prev=has_prev, biases=biases, prev_cols=prev_cols,
                   pi=pi, nk=nk, ones_k=ones_k):
            def chunk(idx, c, back):
                r = idx // ngrp
                j = jnp.maximum(idx % ngrp - back, 0)
                return pl.ds(pl.multiple_of((r + dil * c) * blk + cs * j, cs), cs)

            def load(ref, idx, back=0):
                return jnp.concatenate([ref[0, chunk(idx, c, back), :] for c in range(ngrp)], axis=0)

            qs, ks, vs, firsts = [], [], [], []
            for b in range(ATTN_BATCH):
                idx = it * ATTN_BATCH + b
                qs.append(load(q_ref, idx))
                if has_prev:
                    ks.append(jnp.concatenate([load(k_ref, idx, 1), load(k_ref, idx)], axis=0))
                    vs.append(jnp.concatenate([load(v_ref, idx, 1), load(v_ref, idx)], axis=0))
                    firsts.append(jnp.where(prev_cols & (idx % ngrp == 0), NEG_INF, 0.0))
                else:
                    ks.append(load(k_ref, idx))
                    vs.append(load(v_ref, idx))
            q3 = jnp.stack(qs) * ATTN_SCALE
            k3 = jnp.stack(ks).astype(BF16)
            v3 = jnp.concatenate([jnp.stack(vs).astype(BF16), ones_k], axis=2)
            outs, lses = [], []
            for hh in range(2):
                qm = jnp.where(head0 if hh == 0 else ~head0, q3, 0.0).astype(BF16)
                s = jnp.einsum("bqd,bkd->bqk", qm, k3, preferred_element_type=F32) + biases[hh]
                if has_prev:
                    s = s + jnp.stack(firsts)
                m = jnp.max(s, axis=2, keepdims=True)
                p = jnp.exp(s - m).astype(BF16)
                o = jnp.einsum("bqk,bkd->bqd", p, v3, preferred_element_type=F32)
                den = o[:, :, LANES:]
                outs.append(o[:, :, :LANES] / den)
                lses.append(m + jnp.log(den))
            o_pair = jnp.where(head0, outs[0], outs[1])
            l_pair = jnp.where(head0, lses[0], lses[1])
            for b in range(ATTN_BATCH):
                idx = it * ATTN_BATCH + b
                for c in range(ngrp):
                    opat_ref[pi, chunk(idx, c, 0), :] = o_pair[b, c * cs:(c + 1) * cs]
                    lpat_ref[pi, chunk(idx, c, 0), :] = l_pair[b, c * cs:(c + 1) * cs]
            return carry

        lax.fori_loop(0, RES // ATTN_BATCH, blocks, 0)

    chunk = 256

    def mix(c, carry):
        rows = pl.ds(pl.multiple_of(c * chunk, chunk), chunk)
        l0, l1, l2 = lpat_ref[0, rows, :], lpat_ref[1, rows, :], lpat_ref[2, rows, :]
        mx = jnp.maximum(jnp.maximum(l0, l1), l2)
        w0, w1, w2 = jnp.exp(l0 - mx), jnp.exp(l1 - mx), jnp.exp(l2 - mx)
        num = w0 * opat_ref[0, rows, :] + w1 * opat_ref[1, rows, :] + w2 * opat_ref[2, rows, :]
        o_ref[0, rows, :] = (num / (w0 + w1 + w2)).astype(o_ref.dtype)
        return carry

    lax.fori_loop(0, S // chunk, mix, 0)


def _attn(q, k, v, slopes):
    B, S, d_a = q.shape
    assert S % (BAND_BLOCK * max(d for _, d in DILATED_PATTERNS)) == 0
    npair = d_a // LANES
    spec = pl.BlockSpec((1, S, LANES), lambda b, h, *_: (b, 0, h))
    grid_spec = pltpu.PrefetchScalarGridSpec(
        num_scalar_prefetch=0,
        grid=(B, npair),
        in_specs=[pl.BlockSpec(memory_space=pltpu.SMEM), spec, spec, spec],
        out_specs=spec,
        scratch_shapes=[pltpu.VMEM((len(DILATED_PATTERNS), S, LANES), F32),
                        pltpu.VMEM((len(DILATED_PATTERNS), S, LANES), F32)],
    )
    return pl.pallas_call(
        functools.partial(_attn_body, S),
        grid_spec=grid_spec,
        out_shape=jax.ShapeDtypeStruct((B, S, d_a), BF16),
        compiler_params=_cparams(("arbitrary", "arbitrary")),
        name="dilated_attn",
    )(slopes, q, k, v)


def _layer_norm(z, g, b):
    mu = jnp.mean(z, axis=-1, keepdims=True)
    zc = z - mu
    var = jnp.mean(zc * zc, axis=-1, keepdims=True)
    return zc * lax.rsqrt(var + LN_EPS) * g + b


def _route(logits):
    n = logits.shape[0]
    lane = lax.broadcasted_iota(jnp.int32, (n, LANES), 1)
    lane_f = lane.astype(F32)
    big = float(LANES)
    is_outer = lane < N_EXPERT_GROUPS
    l1 = jnp.where(is_outer, logits, NEG_INF)
    m1 = jnp.max(l1, axis=1, keepdims=True)
    g_sel = jnp.min(jnp.where(l1 == m1, lane_f, big), axis=1, keepdims=True)
    v1 = 1.0 / jnp.sum(jnp.exp(l1 - m1), axis=1, keepdims=True)
    lo = ROUTER_OFF + g_sel * EXPERTS_PER_GROUP
    in_group = (lane_f >= lo) & (lane_f < lo + EXPERTS_PER_GROUP)
    l2 = jnp.where(in_group, logits, NEG_INF)
    ma = jnp.max(l2, axis=1, keepdims=True)
    ia = jnp.min(jnp.where(l2 == ma, lane_f, big), axis=1, keepdims=True)
    l2b = jnp.where(lane_f == ia, NEG_INF, l2)
    mb = jnp.max(l2b, axis=1, keepdims=True)
    ib = jnp.min(jnp.where(l2b == mb, lane_f, big), axis=1, keepdims=True)
    eb = jnp.exp(mb - ma)
    wa = 1.0 / (1.0 + eb)
    wb = eb / (1.0 + eb)
    return ia - ROUTER_OFF, ib - ROUTER_OFF, v1 * wa, v1 * wb


def _rank_and_meta(e0, e1, g0, g1, carry):
    n = e0.shape[0]
    lane_f = lax.broadcasted_iota(jnp.int32, (n, LANES), 1).astype(F32)
    oh0 = (lane_f == e0).astype(F32)
    oh1 = (lane_f == e1).astype(F32)
    both = oh0 + oh1
    ti = lax.broadcasted_iota(jnp.int32, (n, n), 0)
    tj = lax.broadcasted_iota(jnp.int32, (n, n), 1)
    tri = (tj < ti).astype(BF16)
    before = _dot(tri, both.astype(BF16)) + carry
    r0 = jnp.sum(before * oh0, axis=1, keepdims=True)
    r1 = jnp.sum(before * oh1, axis=1, keepdims=True)
    new_carry = carry + jnp.sum(both, axis=0, keepdims=True)
    lane = lax.broadcasted_iota(jnp.int32, (n, LANES), 1)
    meta = jnp.zeros((n, LANES), F32)
    for i, val in enumerate((e0, e1, g0, g1, r0, r1)):
        meta = jnp.where(lane == i, val, meta)
    return meta, meta.T[0:SUBLANES], new_carry


def _lane_group(shape, width):
    lane = lax.broadcasted_iota(jnp.int32, shape, len(shape) - 1)
    grp = jnp.zeros(shape, jnp.int32)
    for g in range(1, shape[-1] // width):
        grp = grp + (lane >= g * width).astype(jnp.int32)
    return grp


def _select_by_group(grp, vals):
    out = vals[-1]
    for g in range(len(vals) - 2, -1, -1):
        out = jnp.where(grp == g, vals[g], out)
    return out


def _store_token_rows(ref, val):
    n, d = val.shape
    nchunk = d // LANES
    for c in range(nchunk):
        ref[pl.ds(c, n, stride=nchunk), :] = val[:, c * LANES:(c + 1) * LANES]


def _load_token_rows(ref, n, nchunk):
    return jnp.concatenate([ref[pl.ds(c, n, stride=nchunk), :] for c in range(nchunk)], axis=1)


HALO = 16


def _mix_body(tm, nt, n_tiles, d_b, alpha, x_ref, at_ref, unperm_ref, u_ref, uh_ref, qc_ref, kmT_ref, vm_ref,
              wo_ref, wp_ref, ps_ref, g1_ref, b1_ref, wr_ref, br_ref, cin_ref, hs_ref,
              h1_ref, meta_ref, metaT_ref, cnt_ref, carry_ref):
    g = pl.program_id(0)

    @pl.when(g == 0)
    def _():
        carry_ref[...] = cin_ref[...]

    @pl.when(g == n_tiles)
    def _():
        ns_rows = hs_ref.shape[0]
        h1_ref[0:ns_rows, :] = hs_ref[...]
        h1_ref[ns_rows:, :] = jnp.zeros((h1_ref.shape[0] - ns_rows, LANES), F32)

    @pl.when(g < n_tiles)
    def _():
        _mix_tile(tm, g % nt, d_b, alpha, x_ref, at_ref, unperm_ref, u_ref, uh_ref, qc_ref, kmT_ref, vm_ref, wo_ref,
                  wp_ref, ps_ref, g1_ref, b1_ref, wr_ref, br_ref, h1_ref, meta_ref, metaT_ref, cnt_ref, carry_ref)


def _mix_tile(tm, i, d_b, alpha, x_ref, at_ref, unperm_ref, u_ref, uh_ref, qc_ref, kmT_ref, vm_ref, wo_ref, wp_ref,
              ps_ref, g1_ref, b1_ref, wr_ref, br_ref, h1_ref, meta_ref, metaT_ref, cnt_ref, carry_ref):
    u = u_ref[0]
    halo = jnp.where(i == 0, 0.0, uh_ref[0])
    ext = jnp.concatenate([halo, halo, u], axis=0)
    s2 = ext + pltpu.roll(ext, 1, 0)
    s4 = s2 + pltpu.roll(s2, 2, 0)
    s8 = s4 + pltpu.roll(s4, 4, 0)
    s16 = s8 + pltpu.roll(s8, 8, 0)
    grp = _lane_group((tm, d_b), d_b // len(POOL_WINDOWS))
    sums = [s[2 * HALO:] for s in (s2, s4, s8, s16)]
    win = _select_by_group(grp, sums)
    wlen = _select_by_group(grp, [jnp.int32(w) for w in POOL_WINDOWS])
    pos = i * tm + lax.broadcasted_iota(jnp.int32, (tm, d_b), 0)
    cnt = jnp.minimum(wlen, pos + 1).astype(F32)
    diff = win / cnt - u
    pool = _dot(diff.astype(BF16), wp_ref[...]) * ps_ref[...]

    qc = qc_ref[0] * ATTN_SCALE
    d_c = qc.shape[1]
    kmT = kmT_ref[0].astype(BF16)
    vm = vm_ref[0].astype(BF16)
    hl = _lane_group((tm, d_c), HEAD_DIM)
    memo = jnp.zeros((tm, d_c), F32)
    for h in range(d_c // HEAD_DIM):
        qm = jnp.where(hl == h, qc, 0.0).astype(BF16)
        s = _dot(qm, kmT)
        p = jnp.exp(s - jnp.max(s, axis=1, keepdims=True))
        den = jnp.sum(p, axis=1, keepdims=True)
        o = _dot(p.astype(BF16), vm) / den
        memo = jnp.where(hl == h, o, memo)

    at_rl = jnp.concatenate([at_ref[0, r] for r in range(RES)], axis=0)
    attn = _dot(unperm_ref[...], at_rl).astype(BF16)
    cat = jnp.concatenate([attn, pool.astype(BF16), memo.astype(BF16)], axis=1)
    mixv = _dot(cat, wo_ref[...])
    h1 = _layer_norm(alpha * x_ref[0] + mixv, g1_ref[...], b1_ref[...])
    _store_token_rows(h1_ref, h1)

    logits = _dot(h1.astype(BF16), wr_ref[...]) + br_ref[...]
    e0, e1, g0, g1 = _route(logits)
    meta, metaT, new_carry = _rank_and_meta(e0, e1, g0, g1, carry_ref[0:1, :])
    meta_ref[...] = meta
    metaT_ref[...] = metaT
    carry_ref[...] = jnp.broadcast_to(new_carry, carry_ref.shape)
    cnt_ref[...] = carry_ref[...]


def _mix(x, attn_rm, u, qc, kmT, vm, wo_bf, wp_bd_bf, pool_scale, g1, b1, wr, br, counts_in, h1s, alpha, tm):
    B, S, D = x.shape
    d_a, d_b, d_c = attn_rm.shape[2], u.shape[2], qc.shape[2]
    n_mem = vm.shape[1]
    nt = S // tm
    n_tiles = B * nt
    nchunk = D // LANES
    assert tm % (RES * SUBLANES * 2) == 0 and h1s.shape[0] < tm * nchunk
    last = n_tiles - 1
    bi = lambda g: (jnp.minimum(g, last) // nt, jnp.minimum(g, last) % nt)
    row = lambda n: pl.BlockSpec((1, tm, n), lambda g: bi(g) + (0,))
    full = lambda shape: pl.BlockSpec(shape, lambda g: (0,) * len(shape))
    per_b = lambda shape: pl.BlockSpec((1,) + shape, lambda g: (bi(g)[0], 0, 0))
    halo_spec = pl.BlockSpec((1, HALO, d_b), lambda g: (bi(g)[0], jnp.maximum(bi(g)[1] * (tm // HALO) - 1, 0), 0))
    at_spec = pl.BlockSpec((1, RES, tm // RES, d_a), lambda g: (bi(g)[0], 0, bi(g)[1], 0))
    dst = jnp.arange(tm)
    unperm = ((dst % RES) * (tm // RES) + dst // RES)[:, None] == jnp.arange(tm)[None, :]
    return pl.pallas_call(
        functools.partial(_mix_body, tm, nt, n_tiles, d_b, alpha),
        grid=(n_tiles + 1,),
        in_specs=[row(D), at_spec, full((tm, tm)), row(d_b), halo_spec, row(d_c), per_b((d_c, n_mem)),
                  per_b((n_mem, d_c)), full((D, D)), full((d_b, d_b)), full((1, d_b)), full((1, D)), full((1, D)),
                  full((D, LANES)), full((1, LANES)), full((SUBLANES, LANES)), full(h1s.shape)],
        out_specs=[pl.BlockSpec((tm * nchunk, LANES), lambda g: (g, 0)),
                   pl.BlockSpec((tm, LANES), lambda g: (jnp.minimum(g, last), 0)),
                   pl.BlockSpec((SUBLANES, tm), lambda g: (0, jnp.minimum(g, last))),
                   pl.BlockSpec((SUBLANES, LANES), lambda g: (0, 0))],
        out_shape=[jax.ShapeDtypeStruct(((n_tiles + 1) * tm * nchunk, LANES), F32),
                   jax.ShapeDtypeStruct((B * S, LANES), F32),
                   jax.ShapeDtypeStruct((SUBLANES, B * S), F32),
                   jax.ShapeDtypeStruct((SUBLANES, LANES), F32)],
        scratch_shapes=[pltpu.VMEM((SUBLANES, LANES), F32)],
        compiler_params=_cparams(("arbitrary",)),
        name="mix_ln1_router",
    )(x, attn_rm.reshape(B, RES, S // RES, d_a), unperm.astype(BF16), u, u, qc, kmT, vm, wo_bf, wp_bd_bf,
      pool_scale, g1, b1, wr, br, counts_in, h1s)


def _columns_to_lanes(cols, rows):
    lane = lax.broadcasted_iota(jnp.int32, (rows, LANES), 1)
    tile = jnp.zeros((rows, LANES), F32)
    for t, c in enumerate(cols):
        tile = jnp.where(lane == t, c, tile)
    return tile


def _sproj_body(x_ref, wT_ref, o_ref):
    o_ref[...] = _dot_nt(wT_ref[...], x_ref[...].astype(BF16))


def _sproj(x, wT_bf):
    vm = pl.BlockSpec(memory_space=pltpu.VMEM)
    return pl.pallas_call(
        _sproj_body, in_specs=[vm, vm], out_specs=vm,
        out_shape=jax.ShapeDtypeStruct((wT_bf.shape[0], x.shape[0]), F32),
        compiler_params=pltpu.CompilerParams(vmem_limit_bytes=VMEM_LIMIT),
        name="decode_proj",
    )(x, wT_bf)


def _decode_body(T, wb, qc_row0, slopes_ref, pT_ref, ck_ref, cv_ref, mk_ref, mv_ref,
                 nk_ref, nv_ref, at_ref, mo_ref, qkv_ref, sfar_ref, snear_ref, snew_ref):
    H = ck_ref.shape[1]
    d_a = H * HEAD_DIM
    ntok = pT_ref.shape[1]
    shift = (ntok - T * pl.program_id(0)) % ntok
    qkv_ref[...] = pltpu.roll(pT_ref[...], shift, 1)

    def member(dist, window, dil):
        assert dil & (dil - 1) == 0
        return ((dist & (dil - 1)) == 0) & (dist <= window) & (dist >= 0)

    OWN = min(d for _, d in DILATED_PATTERNS if d > 1)
    far = wb - LANES
    assert T <= OWN and far % OWN == 0 and wb % OWN == 0
    assert all(w <= LANES for w, d in DILATED_PATTERNS if d == 1) and all(d % OWN == 0 for _, d in DILATED_PATTERNS if d > 1)
    R = H * T
    assert T & (T - 1) == 0
    query = lambda shape: lax.broadcasted_iota(jnp.int32, shape, 0) & (T - 1)
    lane_far = lax.broadcasted_iota(jnp.int32, (R, far), 1)
    t_far = query((R, far))
    owned = (lane_far & (OWN - 1)) == t_far
    dist_far = wb + t_far - lane_far
    dist_near = wb + query((R, LANES)) - (far + lax.broadcasted_iota(jnp.int32, (R, LANES), 1))
    dist_new = query((R, T)) - lax.broadcasted_iota(jnp.int32, (R, T), 1)
    masks = [(None if dil == 1 else owned & member(dist_far, window, dil),
              member(dist_near, window, dil), member(dist_new, window, dil)) for window, dil in DILATED_PATTERNS]
    dist_far = dist_far.astype(F32)
    dist_near = dist_near.astype(F32)
    dist_new = jnp.maximum(dist_new, 0).astype(F32)
    rnd = lambda a: a.astype(BF16).astype(F32)

    lane_t = lax.broadcasted_iota(jnp.int32, (HEAD_DIM, LANES), 1)
    own_t = lane_t & (OWN - 1)
    last = wb - LANES
    n_far = far // LANES

    tile = lambda a, c: a[:, c * LANES:(c + 1) * LANES]

    G = SUBLANES // T
    assert G * T == SUBLANES and H % G == 0
    group_rows = lambda g: pl.ds(pl.multiple_of(g * SUBLANES, SUBLANES), SUBLANES)

    def scores(g, carry):
        parts = [head_scores(g * G + i) for i in range(G)]
        sfar_ref[group_rows(g), :] = jnp.concatenate([p[0] for p in parts], axis=0)
        snear_ref[group_rows(g), :] = jnp.concatenate([p[1] for p in parts], axis=0)
        snew_ref[group_rows(g), :] = jnp.concatenate([p[2] for p in parts], axis=0)
        return carry

    def head_scores(h):
        slope = slopes_ref[h]
        kTc = ck_ref[0, h]
        vTc = cv_ref[0, h]
        r0 = pl.multiple_of(h * HEAD_DIM, HEAD_DIM)
        qT = rnd(qkv_ref[pl.ds(r0, HEAD_DIM), :][:, 0:T] * ATTN_SCALE)
        kTn = qkv_ref[pl.ds(d_a + r0, HEAD_DIM), :][:, 0:T]
        vTn = qkv_ref[pl.ds(2 * d_a + r0, HEAD_DIM), :][:, 0:T]
        kTn_r = rnd(kTn)

        qb = [jnp.broadcast_to(qT[:, t:t + 1], (HEAD_DIM, LANES)) for t in range(T)]
        qpat = qb[T - 1]
        for t in range(T - 1):
            qpat = jnp.where(own_t == t, qb[t], qpat)
        s_far = jnp.concatenate([jnp.sum(rnd(tile(kTc, c)) * qpat, axis=0, keepdims=True) for c in range(n_far)],
                                axis=1)
        s_far = s_far - slope * dist_far[0:T]
        k_near = rnd(tile(kTc, n_far))
        s_near = jnp.concatenate([jnp.sum(k_near * qb[t], axis=0, keepdims=True) for t in range(T)], axis=0)
        s_near = s_near - slope * dist_near[0:T]
        s_new = jnp.concatenate([jnp.sum(kTn_r * qT[:, t:t + 1], axis=0, keepdims=True) for t in range(T)], axis=0)
        s_new = s_new - slope * dist_new[0:T]

        rk = pltpu.roll(kTc, wb - T, 1)
        rv = pltpu.roll(vTc, wb - T, 1)
        nk_ref[0, h] = rk
        nv_ref[0, h] = rv
        newk = jnp.zeros((HEAD_DIM, LANES), F32)
        newv = jnp.zeros((HEAD_DIM, LANES), F32)
        for t in range(T):
            newk = jnp.where(lane_t == LANES - T + t, kTn[:, t:t + 1], newk)
            newv = jnp.where(lane_t == LANES - T + t, vTn[:, t:t + 1], newv)
        nk_ref[0, h, :, last:] = jnp.where(lane_t >= LANES - T, newk, rk[:, last:])
        nv_ref[0, h, :, last:] = jnp.where(lane_t >= LANES - T, newv, rv[:, last:])
        return s_far, s_near, s_new

    lax.fori_loop(0, H // G, scores, 0)

    s_far, s_near, s_new = sfar_ref[...], snear_ref[...], snew_ref[...]
    rmax = lambda a: jnp.max(a, axis=1, keepdims=True)
    rsum = lambda a: jnp.sum(a, axis=1, keepdims=True)
    parts, lses = [], []
    for m_far, m_near, m_new in masks:
        sn, sw = jnp.where(m_near, s_near, NEG_INF), jnp.where(m_new, s_new, NEG_INF)
        m = jnp.maximum(rmax(sn), rmax(sw))
        if m_far is not None:
            sf = jnp.where(m_far, s_far, NEG_INF)
            m = jnp.maximum(m, rmax(sf))
        en, ew = jnp.exp(sn - m), jnp.exp(sw - m)
        den = rsum(en) + rsum(ew)
        ef = None
        if m_far is not None:
            ef = jnp.exp(sf - m)
            den = den + rsum(ef)
            ef = rnd(ef)
        parts.append((ef, rnd(en), rnd(ew), den))
        lses.append(m + jnp.log(den))
    top = functools.reduce(jnp.maximum, lses)
    mixw = [jnp.exp(l - top) for l in lses]
    total = functools.reduce(lambda a, b: a + b, mixw)
    p_far = p_near = p_new = None
    for (ef, en, ew, den), w in zip(parts, mixw):
        scale = w / (total * den)
        p_near = en * scale if p_near is None else p_near + en * scale
        p_new = ew * scale if p_new is None else p_new + ew * scale
        if ef is not None:
            p_far = ef * scale if p_far is None else p_far + ef * scale
    sfar_ref[...] = p_far
    snear_ref[...] = p_near
    snew_ref[...] = p_new

    def values(g, carry):
        pf, pn, pw = sfar_ref[group_rows(g), :], snear_ref[group_rows(g), :], snew_ref[group_rows(g), :]
        for i in range(G):
            head_values(g * G + i, pf[i * T:(i + 1) * T], pn[i * T:(i + 1) * T], pw[i * T:(i + 1) * T])
        return carry

    def head_values(h, p_far_h, p_near_h, p_new_h):
        vTc = cv_ref[0, h]
        r0 = pl.multiple_of(h * HEAD_DIM, HEAD_DIM)
        vTn_r = rnd(qkv_ref[pl.ds(2 * d_a + r0, HEAD_DIM), :][:, 0:T])
        p_comb = jnp.sum(p_far_h, axis=0, keepdims=True)
        acc = rnd(tile(vTc, 0)) * tile(p_comb, 0)
        for c in range(1, n_far):
            acc = acc + rnd(tile(vTc, c)) * tile(p_comb, c)
        v_near = rnd(tile(vTc, n_far))
        cols = []
        for t in range(T):
            cols.append(jnp.sum(jnp.where(own_t == t, acc, 0.0) + v_near * p_near_h[t:t + 1], axis=1, keepdims=True)
                        + jnp.sum(vTn_r * p_new_h[t:t + 1], axis=1, keepdims=True))
        at_ref[0, pl.ds(r0, HEAD_DIM), :] = _columns_to_lanes(cols, HEAD_DIM)

    lax.fori_loop(0, H // G, values, 0)

    Hc = mk_ref.shape[1]
    rows = []
    for h in range(Hc):
        kT = rnd(mk_ref[0, h])
        r0 = h * HEAD_DIM
        qT = rnd(qkv_ref[qc_row0 + r0:qc_row0 + r0 + HEAD_DIM, :][:, 0:T] * ATTN_SCALE)
        rows += [jnp.sum(kT * qT[:, t:t + 1], axis=0, keepdims=True) for t in range(T)]
    s = jnp.concatenate(rows, axis=0)
    p = jnp.exp(s - jnp.max(s, axis=1, keepdims=True))
    p = rnd(p / jnp.sum(p, axis=1, keepdims=True))
    for h in range(Hc):
        vT = rnd(mv_ref[0, h])
        cols = [jnp.sum(vT * p[h * T + t:h * T + t + 1], axis=1, keepdims=True) for t in range(T)]
        mo_ref[0, h * HEAD_DIM:(h + 1) * HEAD_DIM, :] = _columns_to_lanes(cols, HEAD_DIM)


def _decode(projT, T, qc_row0, ck, cv, mk, mv, slopes):
    DB, H, hd, wb = ck.shape
    _, Hc, _, n_mem = mk.shape
    d_a, d_c = H * hd, Hc * hd
    assert wb >= max(w for w, _ in DILATED_PATTERNS) and wb % LANES == 0 and T <= LANES
    assert projT.shape[1] == DB * T and projT.shape[0] >= qc_row0 + d_c
    cache = pl.BlockSpec((1, H, hd, wb), lambda b, *_: (b, 0, 0, 0))
    memc = pl.BlockSpec((1, Hc, hd, n_mem), lambda b, *_: (b, 0, 0, 0))
    grid_spec = pltpu.PrefetchScalarGridSpec(
        num_scalar_prefetch=0,
        grid=(DB,),
        in_specs=[pl.BlockSpec(memory_space=pltpu.SMEM),
                  pl.BlockSpec(projT.shape, lambda b, *_: (0, 0)),
                  cache, cache, memc, memc],
        out_specs=[cache, cache,
                   pl.BlockSpec((1, d_a, LANES), lambda b, *_: (b, 0, 0)),
                   pl.BlockSpec((1, d_c, LANES), lambda b, *_: (b, 0, 0))],
        scratch_shapes=[pltpu.VMEM(projT.shape, F32), pltpu.VMEM((H * T, wb - LANES), F32),
                        pltpu.VMEM((H * T, LANES), F32), pltpu.VMEM((H * T, T), F32)],
    )
    nk, nv, at, mo = pl.pallas_call(
        functools.partial(_decode_body, T, wb, qc_row0),
        grid_spec=grid_spec,
        out_shape=[jax.ShapeDtypeStruct(ck.shape, F32), jax.ShapeDtypeStruct(cv.shape, F32),
                   jax.ShapeDtypeStruct((DB, d_a, LANES), F32), jax.ShapeDtypeStruct((DB, d_c, LANES), F32)],
        compiler_params=_cparams(("arbitrary",)),
        name="decode_attn_cache",
    )(slopes, projT, ck, cv, mk, mv)
    return nk, nv, at[:, :, :T], mo[:, :, :T]


def _smix_body(T, alpha, pos0, x_ref, at_ref, mo_ref, st_ref, wu_ref, wo_ref, wp_ref, ps_ref, g1_ref, b1_ref,
               wr_ref, br_ref, cin_ref, h1_ref, meta_ref, metaT_ref, cnt_ref, pool_ref):
    n = x_ref.shape[0]
    db = n // T
    pb = st_ref.shape[0]
    d_b = st_ref.shape[2]
    x = x_ref[...]
    bdot = lambda a, b: _dot(a.astype(BF16), b.astype(BF16))
    u_new = bdot(x, wu_ref[...])
    seq = [st_ref[j] for j in range(pb)] + [u_new[t * db:(t + 1) * db] for t in range(T)]
    for j in range(pb):
        pool_ref[j] = seq[j + T]
    grp = _lane_group((db, d_b), d_b // len(POOL_WINDOWS))
    diffs = []
    for t in range(T):
        j = pb + t
        per_w = []
        for w in POOL_WINDOWS:
            acc = seq[j]
            for back in range(1, w):
                if j - back >= 0:
                    acc = acc + seq[j - back]
            per_w.append(acc / float(min(w, pos0 + j + 1)))
        diffs.append(_select_by_group(grp, per_w) - seq[j])
    diff = jnp.concatenate(diffs, axis=0)
    pool = bdot(diff, wp_ref[...]) * ps_ref[...]
    cat = jnp.concatenate([at_ref[...], pool, mo_ref[...]], axis=1)
    mixv = bdot(cat, wo_ref[...])
    h1 = _layer_norm(alpha * x + mixv, g1_ref[...], b1_ref[...])
    _store_token_rows(h1_ref, jnp.concatenate([h1, h1], axis=0))
    logits_lo = bdot(h1, wr_ref[...])
    logits_hi = jnp.dot(h1, wr_ref[...], preferred_element_type=F32, precision=lax.Precision.HIGHEST)
    logits = jnp.concatenate([logits_lo, logits_hi], axis=0) + br_ref[...]
    e0, e1, g0, g1 = _route(logits)
    meta, metaT, new_carry = _rank_and_meta(e0, e1, g0, g1, cin_ref[0:1, :])
    meta_ref[...] = meta
    metaT_ref[...] = metaT
    cnt_ref[...] = jnp.broadcast_to(new_carry, cnt_ref.shape)


def _smix(x_tb, attn_tb, memo_tb, state, wu, wo, wp_bd, pool_scale, g1, b1, wr, br, counts_in, alpha, T):
    n, D = x_tb.shape
    pb, db, d_b = state.shape
    nchunk = D // LANES
    vm = pl.BlockSpec(memory_space=pltpu.VMEM)
    return pl.pallas_call(
        functools.partial(_smix_body, T, alpha, PAST_LEN - pb),
        in_specs=[vm] * 13,
        out_specs=[vm] * 5,
        out_shape=[jax.ShapeDtypeStruct((2 * n * nchunk, LANES), F32), jax.ShapeDtypeStruct((2 * n, LANES), F32),
                   jax.ShapeDtypeStruct((SUBLANES, 2 * n), F32),
                   jax.ShapeDtypeStruct((SUBLANES, LANES), F32), jax.ShapeDtypeStruct((pb, db, d_b), F32)],
        compiler_params=pltpu.CompilerParams(vmem_limit_bytes=VMEM_LIMIT),
        name="decode_mix_ln1_router",
    )(x_tb, attn_tb, memo_tb, state, wu, wo, wp_bd, pool_scale, g1, b1, wr, br, counts_in)


def _slot_map_body(blk, pos0_ref, pos1_ref, fill_ref, inv_ref, sem):
    g = pl.program_id(0)

    @pl.when(g == 0)
    def _():
        c = pltpu.make_async_copy(fill_ref, inv_ref, sem)
        c.start()
        c.wait()

    group = math.gcd(blk, 16)

    def body(j, carry):
        token0 = g * blk + j * group
        for u in range(group):
            inv_ref[pos0_ref[0, 0, j * group + u]] = token0 + u
            inv_ref[pos1_ref[0, 0, j * group + u]] = token0 + u
        return carry

    lax.fori_loop(0, blk // group, body, 0)


def _slot_map(pos2, n_slots, zero_token, blk):
    n = pos2.shape[1]
    assert n % blk == 0 and n_slots % 1024 == 0
    tokens = pl.BlockSpec((1, 1, blk), lambda g: (g, 0, 0), memory_space=pltpu.SMEM)
    return pl.pallas_call(
        functools.partial(_slot_map_body, blk),
        grid=(n // blk,),
        in_specs=[tokens, tokens, pl.BlockSpec(memory_space=pl.ANY)],
        out_specs=pl.BlockSpec(memory_space=pltpu.SMEM),
        out_shape=jax.ShapeDtypeStruct((n_slots,), jnp.int32),
        scratch_shapes=[pltpu.SemaphoreType.DMA(())],
        compiler_params=_cparams(("arbitrary",)),
        name="slot_map",
    )(pos2[0].reshape(n // blk, 1, blk), pos2[1].reshape(n // blk, 1, blk),
      jnp.full((n_slots,), zero_token, jnp.int32))


def _expert_body(tmx, nchunk, te_ref, nt_ref, tv_ref, cur_ref, nxt_ref, h_ref, wg_ref, wu_ref, wd_ref, ys_ref,
                 buf_a, buf_b, wg_bf, wu_bf, wd_bf, sems):
    i = pl.program_id(0)
    n_tiles = nt_ref[0]

    def gather(idx_ref, buf, s, tile, wait):
        groups = (tv_ref[tile] + DMA_UNROLL - 1) // DMA_UNROLL

        def body(g, carry):
            for u in range(DMA_UNROLL):
                j = g * DMA_UNROLL + u
                dst = buf.at[pl.ds(pl.multiple_of(j * nchunk, nchunk), nchunk)]
                copy = pltpu.make_async_copy(h_ref.at[idx_ref[0, 0, j]], dst, sems.at[s])
                if wait:
                    copy.wait()
                else:
                    copy.start(priority=u % 2)
            return carry

        lax.fori_loop(0, groups, body, 0)

    @pl.when(i == 0)
    def _():
        buf_a[...] = jnp.zeros_like(buf_a)
        buf_b[...] = jnp.zeros_like(buf_b)
        gather(cur_ref, buf_a, 0, 0, False)

    it = jnp.minimum(i, pl.num_programs(0) - 2)
    @pl.when((i < n_tiles) & ((i == 0) | (te_ref[it] != te_ref[jnp.maximum(it - 1, 0)])))
    def _():
        wg_bf[...] = wg_ref[0].astype(BF16)
        wu_bf[...] = wu_ref[0].astype(BF16)
        wd_bf[...] = wd_ref[0].astype(BF16)

    def step(cur, s_cur, nxt, s_nxt):
        gather(cur_ref, cur, s_cur, it, True)

        @pl.when(i + 1 < n_tiles)
        def _():
            gather(nxt_ref, nxt, s_nxt, it + 1, False)

        x = _load_token_rows(cur, tmx, nchunk).astype(BF16)
        hg = _dot(x, wg_bf[...])
        hu = _dot(x, wu_bf[...])
        a = (hg * jax.nn.sigmoid(hg) * hu).astype(BF16)
        y = _dot(a, wd_bf[...])
        _store_token_rows(ys_ref, y)

    @pl.when((i < n_tiles) & (i % 2 == 0))
    def _():
        step(buf_a, 0, buf_b, 1)

    @pl.when((i < n_tiles) & (i % 2 == 1))
    def _():
        step(buf_b, 1, buf_a, 0)

    @pl.when(i >= n_tiles)
    def _():
        ys_ref[...] = jnp.zeros_like(ys_ref)


def _experts(tile_expert, n_tiles, tile_valid, inv, h_rows, wg, wu, wd, tmx, max_tiles):
    E, D, F = wg.shape
    nchunk = D // LANES
    inv3 = inv.reshape(-1, 1, tmx)
    last_blk = inv3.shape[0] - 1
    assert last_blk >= max_tiles and tmx % DMA_UNROLL == 0
    weights = lambda shape: pl.BlockSpec((1,) + shape, lambda i, te, *_: (te[jnp.minimum(i, max_tiles - 1)], 0, 0))
    rows = pltpu.VMEM((tmx * nchunk, LANES), F32)
    grid_spec = pltpu.PrefetchScalarGridSpec(
        num_scalar_prefetch=3,
        grid=(max_tiles + 1,),
        in_specs=[pl.BlockSpec((1, 1, tmx), lambda i, *_: (i, 0, 0), memory_space=pltpu.SMEM),
                  pl.BlockSpec((1, 1, tmx), lambda i, *_: (jnp.minimum(i + 1, last_blk), 0, 0),
                               memory_space=pltpu.SMEM),
                  pl.BlockSpec(memory_space=pl.ANY), weights((D, F)), weights((D, F)), weights((F, D))],
        out_specs=pl.BlockSpec((tmx * nchunk, LANES), lambda i, *_: (i, 0)),
        scratch_shapes=[rows, rows, pltpu.VMEM((D, F), BF16), pltpu.VMEM((D, F), BF16), pltpu.VMEM((F, D), BF16),
                        pltpu.SemaphoreType.DMA((2,))],
    )
    return pl.pallas_call(
        functools.partial(_expert_body, tmx, nchunk),
        grid_spec=grid_spec,
        out_shape=jax.ShapeDtypeStruct(((max_tiles + 1) * tmx * nchunk, LANES), F32),
        compiler_params=_cparams(("arbitrary",)),
        name="expert_swiglu",
    )(tile_expert, n_tiles, tile_valid, inv3, inv3, h_rows, wg, wu, wd)


def _combine_body(tm, nchunk, alpha, pos_ref, nxt_ref, h1_ref, meta_ref, ys_ref, g2_ref, b2_ref, o_ref, buf_a, buf_b,
                  sems):
    i = pl.program_id(0)
    last = pl.num_programs(0) - 1

    def copy(idx_ref, buf, s, t, k):
        dst = buf.at[k, pl.ds(pl.multiple_of(t * nchunk, nchunk), nchunk)]
        return pltpu.make_async_copy(ys_ref.at[idx_ref[0, 0, k * tm + t]], dst, sems.at[s])

    def start(idx_ref, buf, s):
        def body(t, carry):
            for k in range(2):
                copy(idx_ref, buf, s, t, k).start(priority=k)
            return carry

        lax.fori_loop(0, tm, body, 0, unroll=DMA_UNROLL)

    def wait(buf, s):
        def body(t, carry):
            for k in range(2):
                copy(pos_ref, buf, s, t, k).wait()
            return carry

        lax.fori_loop(0, tm, body, 0, unroll=DMA_UNROLL)

    @pl.when(i == 0)
    def _():
        start(pos_ref, buf_a, 0)

    def step(cur, s_cur, nxt, s_nxt):
        wait(cur, s_cur)
        start(nxt_ref, nxt, s_nxt)
        h1 = _load_token_rows(h1_ref, tm, nchunk)
        y0 = _load_token_rows(cur.at[0], tm, nchunk)
        y1 = _load_token_rows(cur.at[1], tm, nchunk)
        meta = meta_ref[...]
        lane = lax.broadcasted_iota(jnp.int32, meta.shape, 1)
        gate0 = jnp.sum(jnp.where(lane == 2, meta, 0.0), axis=1, keepdims=True)
        gate1 = jnp.sum(jnp.where(lane == 3, meta, 0.0), axis=1, keepdims=True)
        f = gate0 * y0 + gate1 * y1
        o_ref[...] = _layer_norm(alpha * h1 + f, g2_ref[...], b2_ref[...])

        @pl.when(i == last)
        def _():
            wait(nxt, s_nxt)

    @pl.when(i % 2 == 0)
    def _():
        step(buf_a, 0, buf_b, 1)

    @pl.when(i % 2 == 1)
    def _():
        step(buf_b, 1, buf_a, 0)


def _combine(pos2, tok0, n, h1_flat, meta, ys, g2, b2, alpha, tm):
    D = g2.shape[1]
    nchunk = D // LANES
    assert n % tm == 0 and tok0 % tm == 0 and pos2.shape == (2, n)
    pos3 = jnp.transpose(pos2.reshape(2, -1, tm), (1, 0, 2)).reshape(-1, 1, 2 * tm)
    off = tok0 // tm
    last = n // tm - 1
    return pl.pallas_call(
        functools.partial(_combine_body, tm, nchunk, alpha),
        grid=(n // tm,),
        in_specs=[pl.BlockSpec((1, 1, 2 * tm), lambda i: (i, 0, 0), memory_space=pltpu.SMEM),
                  pl.BlockSpec((1, 1, 2 * tm), lambda i: (jnp.minimum(i + 1, last), 0, 0),
                               memory_space=pltpu.SMEM),
                  pl.BlockSpec((tm * nchunk, LANES), lambda i: (i + off, 0)),
                  pl.BlockSpec((tm, LANES), lambda i: (i, 0)),
                  pl.BlockSpec(memory_space=pl.ANY),
                  pl.BlockSpec((1, D), lambda i: (0, 0)), pl.BlockSpec((1, D), lambda i: (0, 0))],
        out_specs=pl.BlockSpec((tm, D), lambda i: (i, 0)),
        out_shape=jax.ShapeDtypeStruct((n, D), F32),
        scratch_shapes=[pltpu.VMEM((2, tm * nchunk, LANES), F32), pltpu.VMEM((2, tm * nchunk, LANES), F32),
                        pltpu.SemaphoreType.DMA((2,))],
        compiler_params=_cparams(("arbitrary",)),
        name="combine_ln2",
    )(pos3, pos3, h1_flat, meta, ys, g2, b2)


def _block_diag(w):
    g, a, b = w.shape
    eye = jnp.eye(g, dtype=w.dtype)
    return (eye[:, None, :, None] * w[:, :, None, :]).reshape(g * a, g * b)


def _layer(h_p, h_s, win_k, win_v, pool_st, mem_k, mem_v, mem_prompt,
           w_in, w_mem_kv, w_pool, pool_scale, w_o, ln1_g, ln1_b, ln2_g, ln2_b,
           w_r1, b_r1, w_r2, b_r2, w_gate, w_up, w_down, alpha):
    B, S, D = h_p.shape
    DB, T, _ = h_s.shape
    H = win_k.shape[2]
    Hc = mem_k.shape[2]
    d_a, d_c = H * HEAD_DIM, Hc * HEAD_DIM
    d_b = pool_st.shape[2]
    nchunk = D // LANES
    slopes = 2.0 ** (-8.0 * jnp.arange(1, H + 1, dtype=F32) / H)

    w_in_bf = w_in.astype(BF16)
    w_inT_bf = w_in_bf.T
    wkvT_bf = w_inT_bf[d_a:3 * d_a]
    wp_bd = _block_diag(w_pool)
    ps = pool_scale.reshape(1, d_b)
    g1, b1 = ln1_g.reshape(1, D), ln1_b.reshape(1, D)
    g2, b2 = ln2_g.reshape(1, D), ln2_b.reshape(1, D)
    n_r = N_EXPERT_GROUPS + N_EXPERTS
    wr = jnp.concatenate([w_r1, jnp.transpose(w_r2, (1, 0, 2)).reshape(D, N_EXPERTS),
                          jnp.zeros((D, LANES - n_r), F32)], axis=1)
    br = jnp.concatenate([b_r1, b_r2.reshape(-1), jnp.zeros((LANES - n_r,), F32)]).reshape(1, LANES)

    ck = jnp.transpose(win_k, (0, 2, 3, 1))
    cv = jnp.transpose(win_v, (0, 2, 3, 1))
    mk = jnp.transpose(mem_k, (0, 2, 3, 1))
    mv = jnp.transpose(mem_v, (0, 2, 3, 1))
    projT = _sproj(h_s.reshape(DB * T, D), w_inT_bf)
    nk, nv, attn_sT, memo_sT = _decode(projT, T, 3 * d_a + d_b, ck, cv, mk, mv, slopes)
    to_tb = lambda a: jnp.transpose(a, (2, 0, 1)).reshape(T * DB, a.shape[1])
    x_tb = jnp.transpose(h_s, (1, 0, 2)).reshape(T * DB, D)
    state = jnp.transpose(pool_st, (1, 0, 2))
    h1s, meta_s, metaT_s, cnt_s, new_pool = _smix(x_tb, to_tb(attn_sT), to_tb(memo_sT), state,
                                         w_in[:, 3 * d_a:3 * d_a + d_b], w_o, wp_bd, ps, g1, b1, wr, br,
                                         jnp.zeros((SUBLANES, LANES), F32), alpha, T)

    w_memT = w_mem_kv.T
    kmT, vm, vmT = _memproj(mem_prompt, w_memT[:d_c].astype(BF16), w_mem_kv[:, d_c:].astype(BF16),
                            w_memT[d_c:].astype(BF16))
    q, k, v, u, qc, kT, vT = _proj(h_p, w_in_bf, wkvT_bf, d_a, d_b, d_c, ts=math.gcd(S, PROJ_TILE))
    attn = _attn(q, k, v, slopes)
    h_rows, meta_p, metaT_p, cnt_all = _mix(h_p, attn, u, qc, kmT, vm, w_o.astype(BF16), wp_bd.astype(BF16), ps, g1, b1,
                                   wr.astype(BF16), br, cnt_s, h1s, alpha, tm=math.gcd(S, MIX_TILE))

    tmx = EXPERT_TILE
    n_p, n_s = B * S, 2 * DB * T
    n = n_p + n_s
    counts = cnt_all[0, :N_EXPERTS].astype(jnp.int32)
    padded = (counts + tmx - 1) // tmx * tmx
    seg_end = jnp.cumsum(padded)
    seg_off = seg_end - padded
    metaT = jnp.concatenate([metaT_p, metaT_s], axis=1)
    e_ids = metaT[0:2].astype(jnp.int32)
    experts = jnp.arange(N_EXPERTS).reshape((N_EXPERTS,) + (1,) * 2)
    lookup = lambda table, idx: jnp.sum(
        jnp.where(idx.reshape((1,) * (3 - idx.ndim) + idx.shape) == experts, table.reshape(experts.shape), 0),
        axis=0).reshape(idx.shape)
    pos2 = lookup(seg_off, e_ids) + metaT[4:6].astype(jnp.int32)
    max_tiles = (2 * n) // tmx + N_EXPERTS
    n_tiles = (seg_end[-1] // tmx).astype(jnp.int32).reshape(1)
    tile_row0 = jnp.arange(max_tiles, dtype=jnp.int32) * tmx
    tile_expert = jnp.sum((tile_row0[:, None] >= seg_end[None, :]).astype(jnp.int32), axis=1)
    tile_expert = jnp.minimum(tile_expert, N_EXPERTS - 1)

    n_slots = -(-((max_tiles + 1) * tmx) // 1024) * 1024
    inv = _slot_map(pos2, n_slots, n, blk=math.gcd(n, 1024))
    tile_valid = jnp.clip(lookup(seg_off + counts, tile_expert) - tile_row0, 0, tmx).astype(jnp.int32)
    ys = _experts(tile_expert, n_tiles, tile_valid, inv, h_rows.reshape(-1, nchunk, LANES),
                  w_gate.reshape(N_EXPERTS, D, -1),
                  w_up.reshape(N_EXPERTS, D, -1), w_down.reshape(N_EXPERTS, -1, D), tmx, max_tiles)
    ys3 = ys.reshape(-1, nchunk, LANES)
    y_p = _combine(pos2[:, :n_p], 0, n_p, h_rows, meta_p, ys3, g2, b2, alpha, tm=math.gcd(n_p, ROW_DMA_TILE))
    y_s = _combine(pos2[:, n_p:], n_p, n_s, h_rows, meta_s, ys3, g2, b2, alpha,
                   tm=math.gcd(math.gcd(n_p, n_s), ROW_DMA_TILE))

    y_p = y_p.reshape(B, S, D)
    y_s = 0.5 * (y_s[:DB * T] + y_s[DB * T:])
    y_s = jnp.transpose(y_s.reshape(T, DB, D), (1, 0, 2))
    heads = lambda a, h: jnp.transpose(a.reshape(a.shape[0], h, HEAD_DIM, a.shape[2]), (0, 3, 1, 2))
    wbp = min(max(w for w, _ in DILATED_PATTERNS), S)
    new_wk_p = heads(kT, H)[:, S - wbp:]
    new_wv_p = heads(vT, H)[:, S - wbp:]
    pb = pool_st.shape[1]
    new_pool_p = u[:, S - pb:]
    new_mk_p = heads(kmT, Hc)
    new_mv_p = heads(vmT, Hc)
    new_wk_s = jnp.transpose(nk, (0, 3, 1, 2))
    new_wv_s = jnp.transpose(nv, (0, 3, 1, 2))
    new_pool_s = jnp.transpose(new_pool, (1, 0, 2))
    return (y_p, y_s, new_wk_p, new_wv_p, new_pool_p, new_mk_p, new_mv_p, new_wk_s, new_wv_s, new_pool_s)


def kernel(x_prompt, x_sample, cache_win_k, cache_win_v, state_pool, cache_mem_k, cache_mem_v, mem_prompt, w_in, w_mem_kv, w_pool, pool_scale, w_o, ln1_g, ln1_b, ln2_g, ln2_b, w_r1, b_r1, w_r2, b_r2, w_gate, w_up, w_down):
    depth = w_in.shape[0]
    alpha = (2.0 * depth) ** 0.25
    h_p, h_s = x_prompt, x_sample
    outs = [[] for _ in range(8)]
    for l in range(depth):
        res = _layer(h_p, h_s, cache_win_k[l], cache_win_v[l], state_pool[l], cache_mem_k[l], cache_mem_v[l],
                     mem_prompt, w_in[l], w_mem_kv[l], w_pool[l], pool_scale[l], w_o[l], ln1_g[l], ln1_b[l],
                     ln2_g[l], ln2_b[l], w_r1[l], b_r1[l], w_r2[l], b_r2[l], w_gate[l], w_up[l], w_down[l], alpha)
        h_p, h_s = res[0], res[1]
        for lst, val in zip(outs, res[2:]):
            lst.append(val)
    return (h_p, h_s) + tuple(jnp.stack(o) for o in outs)
```
